```python
import jax, jax.numpy as jnp
from jax import lax
import numpy as np

D_MODEL = 1024
BATCH = 8
SEQ = 2048
DEPTH = 4

CHUNK = 64
N_MIXERS = 2
RWKV_HEAD = 64
RWKV_HEADS = D_MODEL // RWKV_HEAD
DECAY_LORA = 64
AAA_LORA = 64
MV_LORA = 32
GATE_LORA = 160
GN_EPS = 64e-5
CONV_WIDTH = 31
FFN_CONV_WIDTH = 3
D_FF = 2816
MEM_TOKENS = 256
XATTN_HEADS = 4
XATTN_HEAD_DIM = D_MODEL // XATTN_HEADS
NORM_EPS = 1e-6
LN_EPS = 1e-5

kernel_name = 'hybrid_rwkv7_conformer_memxattn_trunk'


def rmsnorm(x, g):
    x32 = x.astype(jnp.float32)
    y = x32 * lax.rsqrt(jnp.mean(x32 * x32, axis=-1, keepdims=True) + NORM_EPS)
    return (y * g).astype(x.dtype)


def layernorm(x, g, b):
    x32 = x.astype(jnp.float32)
    mu = jnp.mean(x32, axis=-1, keepdims=True)
    var = jnp.mean(jnp.square(x32 - mu), axis=-1, keepdims=True)
    return ((x32 - mu) * lax.rsqrt(var + LN_EPS) * g + b).astype(x.dtype)


def causal_dwconv(x, w):
    k_w = w.shape[0]
    return lax.conv_general_dilated(
        x, w[:, None, :].astype(x.dtype), window_strides=(1,), padding=[(k_w - 1, 0)],
        dimension_numbers=('NWC', 'WIO', 'NWC'), feature_group_count=x.shape[-1])


def rwkv7_step(S, inp):
    r_t, dec_t, k_t, v_t, kk_t, a_t = inp
    sa = jnp.einsum('bhij,bhj->bhi', S, -kk_t)
    S = (S * dec_t[:, :, None, :]
         + sa[..., None] * (kk_t * a_t)[:, :, None, :]
         + v_t[..., None] * k_t[:, :, None, :])
    y = jnp.einsum('bhij,bhj->bhi', S, r_t)
    return S, y


def rwkv7_time_mix(h, mu, w_r, w_k, w_v, w_o, w0, w1, w2, a0, a1, a2, g1, g2,
                   k_k, k_a, r_k, ln_g, ln_b, v_first, v_res):
    B, T, D = h.shape
    H, N = RWKV_HEADS, RWKV_HEAD
    xx = jnp.pad(h, ((0, 0), (1, 0), (0, 0)))[:, :T] - h
    xr, xw, xk, xv, xa, xg = (h + xx * mu[i] for i in range(6))
    r = xr @ w_r
    k = xk @ w_k
    v = xv @ w_v
    logw = -jax.nn.softplus(-(w0 + jnp.tanh(xw @ w1) @ w2)) - 0.5
    decay = jnp.exp(-jnp.exp(logw.astype(jnp.float32)))
    a = jax.nn.sigmoid(a0 + (xa @ a1) @ a2)
    g = jax.nn.sigmoid(xg @ g1) @ g2
    kk = (k * k_k).astype(jnp.float32).reshape(B, T, H, N)
    kk = kk / jnp.maximum(jnp.sqrt(jnp.sum(kk * kk, axis=-1, keepdims=True)), 1e-12)
    k = k * (1.0 + (a - 1.0) * k_a)
    if v_res is not None:
        v0, v1, v2 = v_res
        v = v + (v_first - v) * jax.nn.sigmoid(v0 + (xv @ v1) @ v2)

    def heads_t(t):
        return t.astype(jnp.float32).reshape(B, T, H, N).transpose(1, 0, 2, 3)

    seq = (heads_t(r), heads_t(decay), heads_t(k), heads_t(v),
           kk.transpose(1, 0, 2, 3), heads_t(a))
    S0 = jnp.zeros((B, H, N, N), jnp.float32)
    _, y = lax.scan(rwkv7_step, S0, seq)
    y = y.transpose(1, 0, 2, 3)
    m = jnp.mean(y, axis=-1, keepdims=True)
    var = jnp.mean(jnp.square(y - m), axis=-1, keepdims=True)
    y = ((y - m) * lax.rsqrt(var + GN_EPS)).reshape(B, T, D) * ln_g + ln_b
    rh = r.astype(jnp.float32).reshape(B, T, H, N)
    kh = k.astype(jnp.float32).reshape(B, T, H, N)
    vh = v.astype(jnp.float32).reshape(B, T, H, N)
    bonus = (jnp.sum(rh * kh * r_k, axis=-1, keepdims=True) * vh).reshape(B, T, D)
    out = ((y + bonus) * g).astype(h.dtype) @ w_o
    return out, v


def conformer_conv(h, w_in, b_in, dw, dw_b, ln_g, ln_b, w_out, b_out):
    D = h.shape[-1]
    u = h @ w_in + b_in
    u = u[..., :D] * jax.nn.sigmoid(u[..., D:])
    u = causal_dwconv(u, dw) + dw_b
    u = jax.nn.silu(layernorm(u, ln_g, ln_b))
    return u @ w_out + b_out


def memory_cross_attention(h, memn, w_q, w_kv, w_o):
    B, T, D = h.shape
    q = (h @ w_q).reshape(B, T, XATTN_HEADS, XATTN_HEAD_DIM)
    kv = memn @ w_kv
    km = kv[..., :D].reshape(B, -1, XATTN_HEADS, XATTN_HEAD_DIM)
    vm = kv[..., D:].reshape(B, -1, XATTN_HEADS, XATTN_HEAD_DIM)
    s = jnp.einsum('bthd,bmhd->bhtm', q, km).astype(jnp.float32) * (XATTN_HEAD_DIM ** -0.5)
    p = jax.nn.softmax(s, axis=-1).astype(h.dtype)
    o = jnp.einsum('bhtm,bmhd->bthd', p, vm).reshape(B, T, D)
    return o @ w_o


def conv_ffn(h, w_in, dw, w_out):
    u = causal_dwconv(h @ w_in, dw)
    gate, val = u[..., :D_FF], u[..., D_FF:]
    return (jax.nn.silu(gate) * val) @ w_out


def _fwd_setup_inputs(seed: int = 0) -> dict:
    key = jax.random.key(seed)
    ks = iter(jax.random.split(key, 64))
    D = D_MODEL
    NA = (DEPTH + 1) // 2
    NB = DEPTH // 2
    NV = max(NA - 1, 0)

    def nrm(shape, scale):
        return jax.random.normal(next(ks), shape, jnp.float32) * scale

    def unif(shape, lo, hi):
        return jax.random.uniform(next(ks), shape, jnp.float32, lo, hi)

    def gain(shape):
        return 1.0 + nrm(shape, 0.05)

    sd = D ** -0.5
    return {
        'x': nrm((BATCH, SEQ, D), 1.0),
        'mem': nrm((BATCH, MEM_TOKENS, D), 1.0),
        'mem_norm_g': gain((D,)),
        'norm_mix_g': gain((DEPTH, D)),
        'norm_xattn_g': gain((DEPTH, D)),
        'norm_ffn_g': gain((DEPTH, D)),
        'final_norm_g': gain((D,)),
        'rwkv_mu': unif((NA, 6, D), 0.0, 1.0),
        'rwkv_w_r': nrm((NA, D, D), sd),
        'rwkv_w_k': nrm((NA, D, D), sd),
        'rwkv_w_v': nrm((NA, D, D), sd),
        'rwkv_w_o': nrm((NA, D, D), sd),
        'rwkv_w0': unif((NA, D), -6.0, 1.0),
        'rwkv_w1': nrm((NA, D, DECAY_LORA), sd),
        'rwkv_w2': nrm((NA, DECAY_LORA, D), 0.5 * DECAY_LORA ** -0.5),
        'rwkv_a0': nrm((NA, D), 0.3),
        'rwkv_a1': nrm((NA, D, AAA_LORA), sd),
        'rwkv_a2': nrm((NA, AAA_LORA, D), 0.5 * AAA_LORA ** -0.5),
        'rwkv_g1': nrm((NA, D, GATE_LORA), sd),
        'rwkv_g2': nrm((NA, GATE_LORA, D), GATE_LORA ** -0.5),
        'rwkv_k_k': 0.85 + nrm((NA, D), 0.05),
        'rwkv_k_a': 1.0 + nrm((NA, D), 0.05),
        'rwkv_r_k': nrm((NA, RWKV_HEADS, RWKV_HEAD), 0.1),
        'rwkv_ln_g': gain((NA, D)),
        'rwkv_ln_b': nrm((NA, D), 0.02),
        'rwkv_v0': nrm((NV, D), 0.3),
        'rwkv_v1': nrm((NV, D, MV_LORA), sd),
        'rwkv_v2': nrm((NV, MV_LORA, D), 0.5 * MV_LORA ** -0.5),
        'conv_w_in': nrm((NB, D, 2 * D), sd),
        'conv_b_in': nrm((NB, 2 * D), 0.02),
        'conv_dw': nrm((NB, CONV_WIDTH, D), CONV_WIDTH ** -0.5),
        'conv_dw_b': nrm((NB, D), 0.02),
        'conv_ln_g': gain((NB, D)),
        'conv_ln_b': nrm((NB, D), 0.02),
        'conv_w_out': nrm((NB, D, D), sd),
        'conv_b_out': nrm((NB, D), 0.02),
        'xattn_w_q': nrm((DEPTH, D, D), sd),
        'xattn_w_kv': nrm((DEPTH, D, 2 * D), sd),
        'xattn_w_o': nrm((DEPTH, D, D), sd),
        'ffn_w_in': nrm((DEPTH, D, 2 * D_FF), sd),
        'ffn_dw': nrm((DEPTH, FFN_CONV_WIDTH, 2 * D_FF), FFN_CONV_WIDTH ** -0.5),
        'ffn_w_out': nrm((DEPTH, D_FF, D), D_FF ** -0.5),
    }


def _fwd_reference(x, mem, mem_norm_g, norm_mix_g, norm_xattn_g, norm_ffn_g, final_norm_g,
              rwkv_mu, rwkv_w_r, rwkv_w_k, rwkv_w_v, rwkv_w_o, rwkv_w0, rwkv_w1, rwkv_w2,
              rwkv_a0, rwkv_a1, rwkv_a2, rwkv_g1, rwkv_g2, rwkv_k_k, rwkv_k_a, rwkv_r_k,
              rwkv_ln_g, rwkv_ln_b, rwkv_v0, rwkv_v1, rwkv_v2,
              conv_w_in, conv_b_in, conv_dw, conv_dw_b, conv_ln_g, conv_ln_b,
              conv_w_out, conv_b_out, xattn_w_q, xattn_w_kv, xattn_w_o,
              ffn_w_in, ffn_dw, ffn_w_out):
    memn = rmsnorm(mem, mem_norm_g)
    v_first = None
    ia = 0
    ib = 0
    for layer in range(DEPTH):
        h = rmsnorm(x, norm_mix_g[layer])
        if layer % N_MIXERS == 0:
            v_res = None if ia == 0 else (rwkv_v0[ia - 1], rwkv_v1[ia - 1], rwkv_v2[ia - 1])
            out, v = rwkv7_time_mix(
                h, rwkv_mu[ia], rwkv_w_r[ia], rwkv_w_k[ia], rwkv_w_v[ia], rwkv_w_o[ia],
                rwkv_w0[ia], rwkv_w1[ia], rwkv_w2[ia], rwkv_a0[ia], rwkv_a1[ia], rwkv_a2[ia],
                rwkv_g1[ia], rwkv_g2[ia], rwkv_k_k[ia], rwkv_k_a[ia], rwkv_r_k[ia],
                rwkv_ln_g[ia], rwkv_ln_b[ia], v_first, v_res)
            if ia == 0:
                v_first = v
            ia += 1
        else:
            out = conformer_conv(h, conv_w_in[ib], conv_b_in[ib], conv_dw[ib], conv_dw_b[ib],
                                 conv_ln_g[ib], conv_ln_b[ib], conv_w_out[ib], conv_b_out[ib])
            ib += 1
        x = x + out
        x = x + memory_cross_attention(rmsnorm(x, norm_xattn_g[layer]), memn,
                                       xattn_w_q[layer], xattn_w_kv[layer], xattn_w_o[layer])
        x = x + conv_ffn(rmsnorm(x, norm_ffn_g[layer]), ffn_w_in[layer], ffn_dw[layer],
                         ffn_w_out[layer])
    return rmsnorm(x, final_norm_g)


import jax as _jax
import jax.numpy as _jnp

TWIN_FORMAT = 'train_step'
FWD_PARAMS = ['x', 'mem', 'mem_norm_g', 'norm_mix_g', 'norm_xattn_g', 'norm_ffn_g', 'final_norm_g', 'rwkv_mu', 'rwkv_w_r', 'rwkv_w_k', 'rwkv_w_v', 'rwkv_w_o', 'rwkv_w0', 'rwkv_w1', 'rwkv_w2', 'rwkv_a0', 'rwkv_a1', 'rwkv_a2', 'rwkv_g1', 'rwkv_g2', 'rwkv_k_k', 'rwkv_k_a', 'rwkv_r_k', 'rwkv_ln_g', 'rwkv_ln_b', 'rwkv_v0', 'rwkv_v1', 'rwkv_v2', 'conv_w_in', 'conv_b_in', 'conv_dw', 'conv_dw_b', 'conv_ln_g', 'conv_ln_b', 'conv_w_out', 'conv_b_out', 'xattn_w_q', 'xattn_w_kv', 'xattn_w_o', 'ffn_w_in', 'ffn_dw', 'ffn_w_out']
TWIN_WEIGHTS = ['mem_norm_g', 'norm_mix_g', 'norm_xattn_g', 'norm_ffn_g', 'final_norm_g', 'rwkv_mu', 'rwkv_w_r', 'rwkv_w_k', 'rwkv_w_v', 'rwkv_w_o', 'rwkv_w0', 'rwkv_w1', 'rwkv_w2', 'rwkv_a0', 'rwkv_a1', 'rwkv_a2', 'rwkv_g1', 'rwkv_g2', 'rwkv_k_k', 'rwkv_k_a', 'rwkv_r_k', 'rwkv_ln_g', 'rwkv_ln_b', 'rwkv_v0', 'rwkv_v1', 'rwkv_v2', 'conv_w_in', 'conv_b_in', 'conv_dw', 'conv_dw_b', 'conv_ln_g', 'conv_ln_b', 'conv_w_out', 'conv_b_out', 'xattn_w_q', 'xattn_w_kv', 'xattn_w_o', 'ffn_w_in', 'ffn_dw', 'ffn_w_out']
TWIN_DIFF_INPUT = 'x'
TWIN_INPUTS = ['x', 'mem', 'mem_norm_g', 'norm_mix_g', 'norm_xattn_g', 'norm_ffn_g', 'final_norm_g', 'rwkv_mu', 'rwkv_w_r', 'rwkv_w_k', 'rwkv_w_v', 'rwkv_w_o', 'rwkv_w0', 'rwkv_w1', 'rwkv_w2', 'rwkv_a0', 'rwkv_a1', 'rwkv_a2', 'rwkv_g1', 'rwkv_g2', 'rwkv_k_k', 'rwkv_k_a', 'rwkv_r_k', 'rwkv_ln_g', 'rwkv_ln_b', 'rwkv_v0', 'rwkv_v1', 'rwkv_v2', 'conv_w_in', 'conv_b_in', 'conv_dw', 'conv_dw_b', 'conv_ln_g', 'conv_ln_b', 'conv_w_out', 'conv_b_out', 'xattn_w_q', 'xattn_w_kv', 'xattn_w_o', 'ffn_w_in', 'ffn_dw', 'ffn_w_out', 'loss_target', 'm_mem_norm_g', 'm_norm_mix_g', 'm_norm_xattn_g', 'm_norm_ffn_g', 'm_final_norm_g', 'm_rwkv_mu', 'm_rwkv_w_r', 'm_rwkv_w_k', 'm_rwkv_w_v', 'm_rwkv_w_o', 'm_rwkv_w0', 'm_rwkv_w1', 'm_rwkv_w2', 'm_rwkv_a0', 'm_rwkv_a1', 'm_rwkv_a2', 'm_rwkv_g1', 'm_rwkv_g2', 'm_rwkv_k_k', 'm_rwkv_k_a', 'm_rwkv_r_k', 'm_rwkv_ln_g', 'm_rwkv_ln_b', 'm_rwkv_v0', 'm_rwkv_v1', 'm_rwkv_v2', 'm_conv_w_in', 'm_conv_b_in', 'm_conv_dw', 'm_conv_dw_b', 'm_conv_ln_g', 'm_conv_ln_b', 'm_conv_w_out', 'm_conv_b_out', 'm_xattn_w_q', 'm_xattn_w_kv', 'm_xattn_w_o', 'm_ffn_w_in', 'm_ffn_dw', 'm_ffn_w_out', 'v_mem_norm_g', 'v_norm_mix_g', 'v_norm_xattn_g', 'v_norm_ffn_g', 'v_final_norm_g', 'v_rwkv_mu', 'v_rwkv_w_r', 'v_rwkv_w_k', 'v_rwkv_w_v', 'v_rwkv_w_o', 'v_rwkv_w0', 'v_rwkv_w1', 'v_rwkv_w2', 'v_rwkv_a0', 'v_rwkv_a1', 'v_rwkv_a2', 'v_rwkv_g1', 'v_rwkv_g2', 'v_rwkv_k_k', 'v_rwkv_k_a', 'v_rwkv_r_k', 'v_rwkv_ln_g', 'v_rwkv_ln_b', 'v_rwkv_v0', 'v_rwkv_v1', 'v_rwkv_v2', 'v_conv_w_in', 'v_conv_b_in', 'v_conv_dw', 'v_conv_dw_b', 'v_conv_ln_g', 'v_conv_ln_b', 'v_conv_w_out', 'v_conv_b_out', 'v_xattn_w_q', 'v_xattn_w_kv', 'v_xattn_w_o', 'v_ffn_w_in', 'v_ffn_dw', 'v_ffn_w_out']
TWIN_OUTPUTS = ['loss', 'grad_x', 'grad_mem_norm_g', 'grad_norm_mix_g', 'grad_norm_xattn_g', 'grad_norm_ffn_g', 'grad_final_norm_g', 'grad_rwkv_mu', 'grad_rwkv_w_r', 'grad_rwkv_w_k', 'grad_rwkv_w_v', 'grad_rwkv_w_o', 'grad_rwkv_w0', 'grad_rwkv_w1', 'grad_rwkv_w2', 'grad_rwkv_a0', 'grad_rwkv_a1', 'grad_rwkv_a2', 'grad_rwkv_g1', 'grad_rwkv_g2', 'grad_rwkv_k_k', 'grad_rwkv_k_a', 'grad_rwkv_r_k', 'grad_rwkv_ln_g', 'grad_rwkv_ln_b', 'grad_rwkv_v0', 'grad_rwkv_v1', 'grad_rwkv_v2', 'grad_conv_w_in', 'grad_conv_b_in', 'grad_conv_dw', 'grad_conv_dw_b', 'grad_conv_ln_g', 'grad_conv_ln_b', 'grad_conv_w_out', 'grad_conv_b_out', 'grad_xattn_w_q', 'grad_xattn_w_kv', 'grad_xattn_w_o', 'grad_ffn_w_in', 'grad_ffn_dw', 'grad_ffn_w_out', 'delta_mem_norm_g', 'delta_norm_mix_g', 'delta_norm_xattn_g', 'delta_norm_ffn_g', 'delta_final_norm_g', 'delta_rwkv_mu', 'delta_rwkv_w_r', 'delta_rwkv_w_k', 'delta_rwkv_w_v', 'delta_rwkv_w_o', 'delta_rwkv_w0', 'delta_rwkv_w1', 'delta_rwkv_w2', 'delta_rwkv_a0', 'delta_rwkv_a1', 'delta_rwkv_a2', 'delta_rwkv_g1', 'delta_rwkv_g2', 'delta_rwkv_k_k', 'delta_rwkv_k_a', 'delta_rwkv_r_k', 'delta_rwkv_ln_g', 'delta_rwkv_ln_b', 'delta_rwkv_v0', 'delta_rwkv_v1', 'delta_rwkv_v2', 'delta_conv_w_in', 'delta_conv_b_in', 'delta_conv_dw', 'delta_conv_dw_b', 'delta_conv_ln_g', 'delta_conv_ln_b', 'delta_conv_w_out', 'delta_conv_b_out', 'delta_xattn_w_q', 'delta_xattn_w_kv', 'delta_xattn_w_o', 'delta_ffn_w_in', 'delta_ffn_dw', 'delta_ffn_w_out', 'new_m_mem_norm_g', 'new_m_norm_mix_g', 'new_m_norm_xattn_g', 'new_m_norm_ffn_g', 'new_m_final_norm_g', 'new_m_rwkv_mu', 'new_m_rwkv_w_r', 'new_m_rwkv_w_k', 'new_m_rwkv_w_v', 'new_m_rwkv_w_o', 'new_m_rwkv_w0', 'new_m_rwkv_w1', 'new_m_rwkv_w2', 'new_m_rwkv_a0', 'new_m_rwkv_a1', 'new_m_rwkv_a2', 'new_m_rwkv_g1', 'new_m_rwkv_g2', 'new_m_rwkv_k_k', 'new_m_rwkv_k_a', 'new_m_rwkv_r_k', 'new_m_rwkv_ln_g', 'new_m_rwkv_ln_b', 'new_m_rwkv_v0', 'new_m_rwkv_v1', 'new_m_rwkv_v2', 'new_m_conv_w_in', 'new_m_conv_b_in', 'new_m_conv_dw', 'new_m_conv_dw_b', 'new_m_conv_ln_g', 'new_m_conv_ln_b', 'new_m_conv_w_out', 'new_m_conv_b_out', 'new_m_xattn_w_q', 'new_m_xattn_w_kv', 'new_m_xattn_w_o', 'new_m_ffn_w_in', 'new_m_ffn_dw', 'new_m_ffn_w_out', 'new_v_mem_norm_g', 'new_v_norm_mix_g', 'new_v_norm_xattn_g', 'new_v_norm_ffn_g', 'new_v_final_norm_g', 'new_v_rwkv_mu', 'new_v_rwkv_w_r', 'new_v_rwkv_w_k', 'new_v_rwkv_w_v', 'new_v_rwkv_w_o', 'new_v_rwkv_w0', 'new_v_rwkv_w1', 'new_v_rwkv_w2', 'new_v_rwkv_a0', 'new_v_rwkv_a1', 'new_v_rwkv_a2', 'new_v_rwkv_g1', 'new_v_rwkv_g2', 'new_v_rwkv_k_k', 'new_v_rwkv_k_a', 'new_v_rwkv_r_k', 'new_v_rwkv_ln_g', 'new_v_rwkv_ln_b', 'new_v_rwkv_v0', 'new_v_rwkv_v1', 'new_v_rwkv_v2', 'new_v_conv_w_in', 'new_v_conv_b_in', 'new_v_conv_dw', 'new_v_conv_dw_b', 'new_v_conv_ln_g', 'new_v_conv_ln_b', 'new_v_conv_w_out', 'new_v_conv_b_out', 'new_v_xattn_w_q', 'new_v_xattn_w_kv', 'new_v_xattn_w_o', 'new_v_ffn_w_in', 'new_v_ffn_dw', 'new_v_ffn_w_out']
TWIN_LEAF_KINDS = {'loss': 'loss', 'grad_x': 'grad_x', 'grad_mem_norm_g': 'grad_w', 'grad_norm_mix_g': 'grad_w', 'grad_norm_xattn_g': 'grad_w', 'grad_norm_ffn_g': 'grad_w', 'grad_final_norm_g': 'grad_w', 'grad_rwkv_mu': 'grad_w', 'grad_rwkv_w_r': 'grad_w', 'grad_rwkv_w_k': 'grad_w', 'grad_rwkv_w_v': 'grad_w', 'grad_rwkv_w_o': 'grad_w', 'grad_rwkv_w0': 'grad_w', 'grad_rwkv_w1': 'grad_w', 'grad_rwkv_w2': 'grad_w', 'grad_rwkv_a0': 'grad_w', 'grad_rwkv_a1': 'grad_w', 'grad_rwkv_a2': 'grad_w', 'grad_rwkv_g1': 'grad_w', 'grad_rwkv_g2': 'grad_w', 'grad_rwkv_k_k': 'grad_w', 'grad_rwkv_k_a': 'grad_w', 'grad_rwkv_r_k': 'grad_w', 'grad_rwkv_ln_g': 'grad_w', 'grad_rwkv_ln_b': 'grad_w', 'grad_rwkv_v0': 'grad_w', 'grad_rwkv_v1': 'grad_w', 'grad_rwkv_v2': 'grad_w', 'grad_conv_w_in': 'grad_w', 'grad_conv_b_in': 'grad_w', 'grad_conv_dw': 'grad_w', 'grad_conv_dw_b': 'grad_w', 'grad_conv_ln_g': 'grad_w', 'grad_conv_ln_b': 'grad_w', 'grad_conv_w_out': 'grad_w', 'grad_conv_b_out': 'grad_w', 'grad_xattn_w_q': 'grad_w', 'grad_xattn_w_kv': 'grad_w', 'grad_xattn_w_o': 'grad_w', 'grad_ffn_w_in': 'grad_w', 'grad_ffn_dw': 'grad_w', 'grad_ffn_w_out': 'grad_w', 'delta_mem_norm_g': 'delta_w', 'delta_norm_mix_g': 'delta_w', 'delta_norm_xattn_g': 'delta_w', 'delta_norm_ffn_g': 'delta_w', 'delta_final_norm_g': 'delta_w', 'delta_rwkv_mu': 'delta_w', 'delta_rwkv_w_r': 'delta_w', 'delta_rwkv_w_k': 'delta_w', 'delta_rwkv_w_v': 'delta_w', 'delta_rwkv_w_o': 'delta_w', 'delta_rwkv_w0': 'delta_w', 'delta_rwkv_w1': 'delta_w', 'delta_rwkv_w2': 'delta_w', 'delta_rwkv_a0': 'delta_w', 'delta_rwkv_a1': 'delta_w', 'delta_rwkv_a2': 'delta_w', 'delta_rwkv_g1': 'delta_w', 'delta_rwkv_g2': 'delta_w', 'delta_rwkv_k_k': 'delta_w', 'delta_rwkv_k_a': 'delta_w', 'delta_rwkv_r_k': 'delta_w', 'delta_rwkv_ln_g': 'delta_w', 'delta_rwkv_ln_b': 'delta_w', 'delta_rwkv_v0': 'delta_w', 'delta_rwkv_v1': 'delta_w', 'delta_rwkv_v2': 'delta_w', 'delta_conv_w_in': 'delta_w', 'delta_conv_b_in': 'delta_w', 'delta_conv_dw': 'delta_w', 'delta_conv_dw_b': 'delta_w', 'delta_conv_ln_g': 'delta_w', 'delta_conv_ln_b': 'delta_w', 'delta_conv_w_out': 'delta_w', 'delta_conv_b_out': 'delta_w', 'delta_xattn_w_q': 'delta_w', 'delta_xattn_w_kv': 'delta_w', 'delta_xattn_w_o': 'delta_w', 'delta_ffn_w_in': 'delta_w', 'delta_ffn_dw': 'delta_w', 'delta_ffn_w_out': 'delta_w', 'new_m_mem_norm_g': 'new_m', 'new_m_norm_mix_g': 'new_m', 'new_m_norm_xattn_g': 'new_m', 'new_m_norm_ffn_g': 'new_m', 'new_m_final_norm_g': 'new_m', 'new_m_rwkv_mu': 'new_m', 'new_m_rwkv_w_r': 'new_m', 'new_m_rwkv_w_k': 'new_m', 'new_m_rwkv_w_v': 'new_m', 'new_m_rwkv_w_o': 'new_m', 'new_m_rwkv_w0': 'new_m', 'new_m_rwkv_w1': 'new_m', 'new_m_rwkv_w2': 'new_m', 'new_m_rwkv_a0': 'new_m', 'new_m_rwkv_a1': 'new_m', 'new_m_rwkv_a2': 'new_m', 'new_m_rwkv_g1': 'new_m', 'new_m_rwkv_g2': 'new_m', 'new_m_rwkv_k_k': 'new_m', 'new_m_rwkv_k_a': 'new_m', 'new_m_rwkv_r_k': 'new_m', 'new_m_rwkv_ln_g': 'new_m', 'new_m_rwkv_ln_b': 'new_m', 'new_m_rwkv_v0': 'new_m', 'new_m_rwkv_v1': 'new_m', 'new_m_rwkv_v2': 'new_m', 'new_m_conv_w_in': 'new_m', 'new_m_conv_b_in': 'new_m', 'new_m_conv_dw': 'new_m', 'new_m_conv_dw_b': 'new_m', 'new_m_conv_ln_g': 'new_m', 'new_m_conv_ln_b': 'new_m', 'new_m_conv_w_out': 'new_m', 'new_m_conv_b_out': 'new_m', 'new_m_xattn_w_q': 'new_m', 'new_m_xattn_w_kv': 'new_m', 'new_m_xattn_w_o': 'new_m', 'new_m_ffn_w_in': 'new_m', 'new_m_ffn_dw': 'new_m', 'new_m_ffn_w_out': 'new_m', 'new_v_mem_norm_g': 'new_v', 'new_v_norm_mix_g': 'new_v', 'new_v_norm_xattn_g': 'new_v', 'new_v_norm_ffn_g': 'new_v', 'new_v_final_norm_g': 'new_v', 'new_v_rwkv_mu': 'new_v', 'new_v_rwkv_w_r': 'new_v', 'new_v_rwkv_w_k': 'new_v', 'new_v_rwkv_w_v': 'new_v', 'new_v_rwkv_w_o': 'new_v', 'new_v_rwkv_w0': 'new_v', 'new_v_rwkv_w1': 'new_v', 'new_v_rwkv_w2': 'new_v', 'new_v_rwkv_a0': 'new_v', 'new_v_rwkv_a1': 'new_v', 'new_v_rwkv_a2': 'new_v', 'new_v_rwkv_g1': 'new_v', 'new_v_rwkv_g2': 'new_v', 'new_v_rwkv_k_k': 'new_v', 'new_v_rwkv_k_a': 'new_v', 'new_v_rwkv_r_k': 'new_v', 'new_v_rwkv_ln_g': 'new_v', 'new_v_rwkv_ln_b': 'new_v', 'new_v_rwkv_v0': 'new_v', 'new_v_rwkv_v1': 'new_v', 'new_v_rwkv_v2': 'new_v', 'new_v_conv_w_in': 'new_v', 'new_v_conv_b_in': 'new_v', 'new_v_conv_dw': 'new_v', 'new_v_conv_dw_b': 'new_v', 'new_v_conv_ln_g': 'new_v', 'new_v_conv_ln_b': 'new_v', 'new_v_conv_w_out': 'new_v', 'new_v_conv_b_out': 'new_v', 'new_v_xattn_w_q': 'new_v', 'new_v_xattn_w_kv': 'new_v', 'new_v_xattn_w_o': 'new_v', 'new_v_ffn_w_in': 'new_v', 'new_v_ffn_dw': 'new_v', 'new_v_ffn_w_out': 'new_v'}


def _forward(args):
    return _fwd_reference(*[args[k] for k in FWD_PARAMS])


def _output_shape():
    out = _jax.eval_shape(lambda: _forward(_fwd_setup_inputs(0)))
    return out.shape, out.dtype

N_MICROBATCH = 1
ADAM_LR = 0.001
ADAM_B1 = 0.9
ADAM_B2 = 0.999
ADAM_EPS = 1e-08
ADAM_WD = 0.01
ADAM_STEP = 10
PER_EXAMPLE_BATCH_AXIS = {'x': 0, 'mem': 0, 'loss_target': 0}
SHARED_INPUTS = []
_WEIGHT_DTYPES = {'mem_norm_g': _jnp.float32, 'norm_mix_g': _jnp.float32, 'norm_xattn_g': _jnp.float32, 'norm_ffn_g': _jnp.float32, 'final_norm_g': _jnp.float32, 'rwkv_mu': _jnp.float32, 'rwkv_w_r': _jnp.float32, 'rwkv_w_k': _jnp.float32, 'rwkv_w_v': _jnp.float32, 'rwkv_w_o': _jnp.float32, 'rwkv_w0': _jnp.float32, 'rwkv_w1': _jnp.float32, 'rwkv_w2': _jnp.float32, 'rwkv_a0': _jnp.float32, 'rwkv_a1': _jnp.float32, 'rwkv_a2': _jnp.float32, 'rwkv_g1': _jnp.float32, 'rwkv_g2': _jnp.float32, 'rwkv_k_k': _jnp.float32, 'rwkv_k_a': _jnp.float32, 'rwkv_r_k': _jnp.float32, 'rwkv_ln_g': _jnp.float32, 'rwkv_ln_b': _jnp.float32, 'rwkv_v0': _jnp.float32, 'rwkv_v1': _jnp.float32, 'rwkv_v2': _jnp.float32, 'conv_w_in': _jnp.float32, 'conv_b_in': _jnp.float32, 'conv_dw': _jnp.float32, 'conv_dw_b': _jnp.float32, 'conv_ln_g': _jnp.float32, 'conv_ln_b': _jnp.float32, 'conv_w_out': _jnp.float32, 'conv_b_out': _jnp.float32, 'xattn_w_q': _jnp.float32, 'xattn_w_kv': _jnp.float32, 'xattn_w_o': _jnp.float32, 'ffn_w_in': _jnp.float32, 'ffn_dw': _jnp.float32, 'ffn_w_out': _jnp.float32}
MOMENT_SCALE = {'mem_norm_g': 3.611912e-02, 'norm_mix_g': 1.074054e-01, 'norm_xattn_g': 1.257082e-02, 'norm_ffn_g': 8.528957e-02, 'final_norm_g': 1.603326e+01, 'rwkv_mu': 9.165151e-02, 'rwkv_w_r': 7.807562e-02, 'rwkv_w_k': 8.148969e-02, 'rwkv_w_v': 7.476850e-02, 'rwkv_w_o': 7.418102e-02, 'rwkv_w0': 2.965184e-02, 'rwkv_w1': 8.236349e-03, 'rwkv_w2': 3.984713e-03, 'rwkv_a0': 3.196334e-02, 'rwkv_a1': 5.826876e-02, 'rwkv_a2': 2.908752e-02, 'rwkv_g1': 6.198264e-02, 'rwkv_g2': 7.455550e-02, 'rwkv_k_k': 1.027960e-01, 'rwkv_k_a': 8.199242e-02, 'rwkv_r_k': 1.569006e-01, 'rwkv_ln_g': 7.451806e-02, 'rwkv_ln_b': 8.521095e-02, 'rwkv_v0': 2.227958e-02, 'rwkv_v1': 4.784545e-02, 'rwkv_v2': 1.698292e-02, 'conv_w_in': 4.562556e-02, 'conv_b_in': 5.349473e-02, 'conv_dw': 5.996715e-02, 'conv_dw_b': 1.563507e-01, 'conv_ln_g': 7.414749e-02, 'conv_ln_b': 7.322783e-02, 'conv_w_out': 6.133158e-02, 'conv_b_out': 1.400712e-01, 'xattn_w_q': 1.214383e-02, 'xattn_w_kv': 1.235329e-02, 'xattn_w_o': 1.274822e-02, 'ffn_w_in': 3.642135e-02, 'ffn_dw': 3.719957e-02, 'ffn_w_out': 5.967274e-02}


def _to_microbatches(a, axis):
    t = _jnp.moveaxis(a, axis, 0)
    t = t.reshape((N_MICROBATCH, t.shape[0] // N_MICROBATCH) + t.shape[1:])
    return _jnp.moveaxis(t, 1, axis + 1)


def setup_inputs(seed: int = 0) -> dict:
    inp = _fwd_setup_inputs(seed)
    key = _jax.random.fold_in(_jax.random.key(seed), 7919)
    shape, _ = _output_shape()
    out = dict(inp)
    out["loss_target"] = _jax.random.normal(_jax.random.fold_in(key, 0), shape, _jnp.float32)
    for i, name in enumerate(TWIN_WEIGHTS):
        w = inp[name].astype(_jnp.float32)
        if MOMENT_SCALE is None:
            s = _jnp.sqrt(_jnp.mean(_jnp.square(w)) + 1e-30)
        else:
            s = MOMENT_SCALE[name]
        km, kv = _jax.random.split(_jax.random.fold_in(key, i + 1))
        out[name] = w
        out["m_" + name] = s * _jax.random.normal(km, w.shape, _jnp.float32)
        out["v_" + name] = (s * s) * _jax.random.uniform(kv, w.shape, _jnp.float32, 0.5, 1.5)
    if N_MICROBATCH > 1:
        for name, axis in PER_EXAMPLE_BATCH_AXIS.items():
            out[name] = _to_microbatches(out[name], axis)
    return {'x': out['x'], 'mem': out['mem'], 'mem_norm_g': out['mem_norm_g'], 'norm_mix_g': out['norm_mix_g'], 'norm_xattn_g': out['norm_xattn_g'], 'norm_ffn_g': out['norm_ffn_g'], 'final_norm_g': out['final_norm_g'], 'rwkv_mu': out['rwkv_mu'], 'rwkv_w_r': out['rwkv_w_r'], 'rwkv_w_k': out['rwkv_w_k'], 'rwkv_w_v': out['rwkv_w_v'], 'rwkv_w_o': out['rwkv_w_o'], 'rwkv_w0': out['rwkv_w0'], 'rwkv_w1': out['rwkv_w1'], 'rwkv_w2': out['rwkv_w2'], 'rwkv_a0': out['rwkv_a0'], 'rwkv_a1': out['rwkv_a1'], 'rwkv_a2': out['rwkv_a2'], 'rwkv_g1': out['rwkv_g1'], 'rwkv_g2': out['rwkv_g2'], 'rwkv_k_k': out['rwkv_k_k'], 'rwkv_k_a': out['rwkv_k_a'], 'rwkv_r_k': out['rwkv_r_k'], 'rwkv_ln_g': out['rwkv_ln_g'], 'rwkv_ln_b': out['rwkv_ln_b'], 'rwkv_v0': out['rwkv_v0'], 'rwkv_v1': out['rwkv_v1'], 'rwkv_v2': out['rwkv_v2'], 'conv_w_in': out['conv_w_in'], 'conv_b_in': out['conv_b_in'], 'conv_dw': out['conv_dw'], 'conv_dw_b': out['conv_dw_b'], 'conv_ln_g': out['conv_ln_g'], 'conv_ln_b': out['conv_ln_b'], 'conv_w_out': out['conv_w_out'], 'conv_b_out': out['conv_b_out'], 'xattn_w_q': out['xattn_w_q'], 'xattn_w_kv': out['xattn_w_kv'], 'xattn_w_o': out['xattn_w_o'], 'ffn_w_in': out['ffn_w_in'], 'ffn_dw': out['ffn_dw'], 'ffn_w_out': out['ffn_w_out'], 'loss_target': out['loss_target'], 'm_mem_norm_g': out['m_mem_norm_g'], 'm_norm_mix_g': out['m_norm_mix_g'], 'm_norm_xattn_g': out['m_norm_xattn_g'], 'm_norm_ffn_g': out['m_norm_ffn_g'], 'm_final_norm_g': out['m_final_norm_g'], 'm_rwkv_mu': out['m_rwkv_mu'], 'm_rwkv_w_r': out['m_rwkv_w_r'], 'm_rwkv_w_k': out['m_rwkv_w_k'], 'm_rwkv_w_v': out['m_rwkv_w_v'], 'm_rwkv_w_o': out['m_rwkv_w_o'], 'm_rwkv_w0': out['m_rwkv_w0'], 'm_rwkv_w1': out['m_rwkv_w1'], 'm_rwkv_w2': out['m_rwkv_w2'], 'm_rwkv_a0': out['m_rwkv_a0'], 'm_rwkv_a1': out['m_rwkv_a1'], 'm_rwkv_a2': out['m_rwkv_a2'], 'm_rwkv_g1': out['m_rwkv_g1'], 'm_rwkv_g2': out['m_rwkv_g2'], 'm_rwkv_k_k': out['m_rwkv_k_k'], 'm_rwkv_k_a': out['m_rwkv_k_a'], 'm_rwkv_r_k': out['m_rwkv_r_k'], 'm_rwkv_ln_g': out['m_rwkv_ln_g'], 'm_rwkv_ln_b': out['m_rwkv_ln_b'], 'm_rwkv_v0': out['m_rwkv_v0'], 'm_rwkv_v1': out['m_rwkv_v1'], 'm_rwkv_v2': out['m_rwkv_v2'], 'm_conv_w_in': out['m_conv_w_in'], 'm_conv_b_in': out['m_conv_b_in'], 'm_conv_dw': out['m_conv_dw'], 'm_conv_dw_b': out['m_conv_dw_b'], 'm_conv_ln_g': out['m_conv_ln_g'], 'm_conv_ln_b': out['m_conv_ln_b'], 'm_conv_w_out': out['m_conv_w_out'], 'm_conv_b_out': out['m_conv_b_out'], 'm_xattn_w_q': out['m_xattn_w_q'], 'm_xattn_w_kv': out['m_xattn_w_kv'], 'm_xattn_w_o': out['m_xattn_w_o'], 'm_ffn_w_in': out['m_ffn_w_in'], 'm_ffn_dw': out['m_ffn_dw'], 'm_ffn_w_out': out['m_ffn_w_out'], 'v_mem_norm_g': out['v_mem_norm_g'], 'v_norm_mix_g': out['v_norm_mix_g'], 'v_norm_xattn_g': out['v_norm_xattn_g'], 'v_norm_ffn_g': out['v_norm_ffn_g'], 'v_final_norm_g': out['v_final_norm_g'], 'v_rwkv_mu': out['v_rwkv_mu'], 'v_rwkv_w_r': out['v_rwkv_w_r'], 'v_rwkv_w_k': out['v_rwkv_w_k'], 'v_rwkv_w_v': out['v_rwkv_w_v'], 'v_rwkv_w_o': out['v_rwkv_w_o'], 'v_rwkv_w0': out['v_rwkv_w0'], 'v_rwkv_w1': out['v_rwkv_w1'], 'v_rwkv_w2': out['v_rwkv_w2'], 'v_rwkv_a0': out['v_rwkv_a0'], 'v_rwkv_a1': out['v_rwkv_a1'], 'v_rwkv_a2': out['v_rwkv_a2'], 'v_rwkv_g1': out['v_rwkv_g1'], 'v_rwkv_g2': out['v_rwkv_g2'], 'v_rwkv_k_k': out['v_rwkv_k_k'], 'v_rwkv_k_a': out['v_rwkv_k_a'], 'v_rwkv_r_k': out['v_rwkv_r_k'], 'v_rwkv_ln_g': out['v_rwkv_ln_g'], 'v_rwkv_ln_b': out['v_rwkv_ln_b'], 'v_rwkv_v0': out['v_rwkv_v0'], 'v_rwkv_v1': out['v_rwkv_v1'], 'v_rwkv_v2': out['v_rwkv_v2'], 'v_conv_w_in': out['v_conv_w_in'], 'v_conv_b_in': out['v_conv_b_in'], 'v_conv_dw': out['v_conv_dw'], 'v_conv_dw_b': out['v_conv_dw_b'], 'v_conv_ln_g': out['v_conv_ln_g'], 'v_conv_ln_b': out['v_conv_ln_b'], 'v_conv_w_out': out['v_conv_w_out'], 'v_conv_b_out': out['v_conv_b_out'], 'v_xattn_w_q': out['v_xattn_w_q'], 'v_xattn_w_kv': out['v_xattn_w_kv'], 'v_xattn_w_o': out['v_xattn_w_o'], 'v_ffn_w_in': out['v_ffn_w_in'], 'v_ffn_dw': out['v_ffn_dw'], 'v_ffn_w_out': out['v_ffn_w_out']}


def _loss(weights, diff, rest, loss_target):
    with _jax.named_scope("forward"):
        args = {**rest, TWIN_DIFF_INPUT: diff, **{k: w.astype(_WEIGHT_DTYPES[k]) for k, w in weights.items()}}
        y = _forward(args)
    with _jax.named_scope("loss_head"):
        err = _jnp.square(y.astype(_jnp.float32) - loss_target)
        return 0.5 * _jnp.sum(_jnp.mean(err, axis=-1)) if err.ndim else 0.5 * err


def _adamw(w, g, m, v):
    m = ADAM_B1 * m + (1.0 - ADAM_B1) * g
    v = ADAM_B2 * v + (1.0 - ADAM_B2) * _jnp.square(g)
    m_hat = m / (1.0 - ADAM_B1 ** ADAM_STEP)
    v_hat = v / (1.0 - ADAM_B2 ** ADAM_STEP)
    delta = -ADAM_LR * (m_hat / (_jnp.sqrt(v_hat) + ADAM_EPS) + ADAM_WD * w)
    return delta, m, v


def reference(x, mem, mem_norm_g, norm_mix_g, norm_xattn_g, norm_ffn_g, final_norm_g, rwkv_mu, rwkv_w_r, rwkv_w_k, rwkv_w_v, rwkv_w_o, rwkv_w0, rwkv_w1, rwkv_w2, rwkv_a0, rwkv_a1, rwkv_a2, rwkv_g1, rwkv_g2, rwkv_k_k, rwkv_k_a, rwkv_r_k, rwkv_ln_g, rwkv_ln_b, rwkv_v0, rwkv_v1, rwkv_v2, conv_w_in, conv_b_in, conv_dw, conv_dw_b, conv_ln_g, conv_ln_b, conv_w_out, conv_b_out, xattn_w_q, xattn_w_kv, xattn_w_o, ffn_w_in, ffn_dw, ffn_w_out, loss_target, m_mem_norm_g, m_norm_mix_g, m_norm_xattn_g, m_norm_ffn_g, m_final_norm_g, m_rwkv_mu, m_rwkv_w_r, m_rwkv_w_k, m_rwkv_w_v, m_rwkv_w_o, m_rwkv_w0, m_rwkv_w1, m_rwkv_w2, m_rwkv_a0, m_rwkv_a1, m_rwkv_a2, m_rwkv_g1, m_rwkv_g2, m_rwkv_k_k, m_rwkv_k_a, m_rwkv_r_k, m_rwkv_ln_g, m_rwkv_ln_b, m_rwkv_v0, m_rwkv_v1, m_rwkv_v2, m_conv_w_in, m_conv_b_in, m_conv_dw, m_conv_dw_b, m_conv_ln_g, m_conv_ln_b, m_conv_w_out, m_conv_b_out, m_xattn_w_q, m_xattn_w_kv, m_xattn_w_o, m_ffn_w_in, m_ffn_dw, m_ffn_w_out, v_mem_norm_g, v_norm_mix_g, v_norm_xattn_g, v_norm_ffn_g, v_final_norm_g, v_rwkv_mu, v_rwkv_w_r, v_rwkv_w_k, v_rwkv_w_v, v_rwkv_w_o, v_rwkv_w0, v_rwkv_w1, v_rwkv_w2, v_rwkv_a0, v_rwkv_a1, v_rwkv_a2, v_rwkv_g1, v_rwkv_g2, v_rwkv_k_k, v_rwkv_k_a, v_rwkv_r_k, v_rwkv_ln_g, v_rwkv_ln_b, v_rwkv_v0, v_rwkv_v1, v_rwkv_v2, v_conv_w_in, v_conv_b_in, v_conv_dw, v_conv_dw_b, v_conv_ln_g, v_conv_ln_b, v_conv_w_out, v_conv_b_out, v_xattn_w_q, v_xattn_w_kv, v_xattn_w_o, v_ffn_w_in, v_ffn_dw, v_ffn_w_out):
    given = dict(x=x, mem=mem, mem_norm_g=mem_norm_g, norm_mix_g=norm_mix_g, norm_xattn_g=norm_xattn_g, norm_ffn_g=norm_ffn_g, final_norm_g=final_norm_g, rwkv_mu=rwkv_mu, rwkv_w_r=rwkv_w_r, rwkv_w_k=rwkv_w_k, rwkv_w_v=rwkv_w_v, rwkv_w_o=rwkv_w_o, rwkv_w0=rwkv_w0, rwkv_w1=rwkv_w1, rwkv_w2=rwkv_w2, rwkv_a0=rwkv_a0, rwkv_a1=rwkv_a1, rwkv_a2=rwkv_a2, rwkv_g1=rwkv_g1, rwkv_g2=rwkv_g2, rwkv_k_k=rwkv_k_k, rwkv_k_a=rwkv_k_a, rwkv_r_k=rwkv_r_k, rwkv_ln_g=rwkv_ln_g, rwkv_ln_b=rwkv_ln_b, rwkv_v0=rwkv_v0, rwkv_v1=rwkv_v1, rwkv_v2=rwkv_v2, conv_w_in=conv_w_in, conv_b_in=conv_b_in, conv_dw=conv_dw, conv_dw_b=conv_dw_b, conv_ln_g=conv_ln_g, conv_ln_b=conv_ln_b, conv_w_out=conv_w_out, conv_b_out=conv_b_out, xattn_w_q=xattn_w_q, xattn_w_kv=xattn_w_kv, xattn_w_o=xattn_w_o, ffn_w_in=ffn_w_in, ffn_dw=ffn_dw, ffn_w_out=ffn_w_out, loss_target=loss_target, m_mem_norm_g=m_mem_norm_g, m_norm_mix_g=m_norm_mix_g, m_norm_xattn_g=m_norm_xattn_g, m_norm_ffn_g=m_norm_ffn_g, m_final_norm_g=m_final_norm_g, m_rwkv_mu=m_rwkv_mu, m_rwkv_w_r=m_rwkv_w_r, m_rwkv_w_k=m_rwkv_w_k, m_rwkv_w_v=m_rwkv_w_v, m_rwkv_w_o=m_rwkv_w_o, m_rwkv_w0=m_rwkv_w0, m_rwkv_w1=m_rwkv_w1, m_rwkv_w2=m_rwkv_w2, m_rwkv_a0=m_rwkv_a0, m_rwkv_a1=m_rwkv_a1, m_rwkv_a2=m_rwkv_a2, m_rwkv_g1=m_rwkv_g1, m_rwkv_g2=m_rwkv_g2, m_rwkv_k_k=m_rwkv_k_k, m_rwkv_k_a=m_rwkv_k_a, m_rwkv_r_k=m_rwkv_r_k, m_rwkv_ln_g=m_rwkv_ln_g, m_rwkv_ln_b=m_rwkv_ln_b, m_rwkv_v0=m_rwkv_v0, m_rwkv_v1=m_rwkv_v1, m_rwkv_v2=m_rwkv_v2, m_conv_w_in=m_conv_w_in, m_conv_b_in=m_conv_b_in, m_conv_dw=m_conv_dw, m_conv_dw_b=m_conv_dw_b, m_conv_ln_g=m_conv_ln_g, m_conv_ln_b=m_conv_ln_b, m_conv_w_out=m_conv_w_out, m_conv_b_out=m_conv_b_out, m_xattn_w_q=m_xattn_w_q, m_xattn_w_kv=m_xattn_w_kv, m_xattn_w_o=m_xattn_w_o, m_ffn_w_in=m_ffn_w_in, m_ffn_dw=m_ffn_dw, m_ffn_w_out=m_ffn_w_out, v_mem_norm_g=v_mem_norm_g, v_norm_mix_g=v_norm_mix_g, v_norm_xattn_g=v_norm_xattn_g, v_norm_ffn_g=v_norm_ffn_g, v_final_norm_g=v_final_norm_g, v_rwkv_mu=v_rwkv_mu, v_rwkv_w_r=v_rwkv_w_r, v_rwkv_w_k=v_rwkv_w_k, v_rwkv_w_v=v_rwkv_w_v, v_rwkv_w_o=v_rwkv_w_o, v_rwkv_w0=v_rwkv_w0, v_rwkv_w1=v_rwkv_w1, v_rwkv_w2=v_rwkv_w2, v_rwkv_a0=v_rwkv_a0, v_rwkv_a1=v_rwkv_a1, v_rwkv_a2=v_rwkv_a2, v_rwkv_g1=v_rwkv_g1, v_rwkv_g2=v_rwkv_g2, v_rwkv_k_k=v_rwkv_k_k, v_rwkv_k_a=v_rwkv_k_a, v_rwkv_r_k=v_rwkv_r_k, v_rwkv_ln_g=v_rwkv_ln_g, v_rwkv_ln_b=v_rwkv_ln_b, v_rwkv_v0=v_rwkv_v0, v_rwkv_v1=v_rwkv_v1, v_rwkv_v2=v_rwkv_v2, v_conv_w_in=v_conv_w_in, v_conv_b_in=v_conv_b_in, v_conv_dw=v_conv_dw, v_conv_dw_b=v_conv_dw_b, v_conv_ln_g=v_conv_ln_g, v_conv_ln_b=v_conv_ln_b, v_conv_w_out=v_conv_w_out, v_conv_b_out=v_conv_b_out, v_xattn_w_q=v_xattn_w_q, v_xattn_w_kv=v_xattn_w_kv, v_xattn_w_o=v_xattn_w_o, v_ffn_w_in=v_ffn_w_in, v_ffn_dw=v_ffn_dw, v_ffn_w_out=v_ffn_w_out)
    weights = {n: given[n] for n in TWIN_WEIGHTS}
    shared = {n: given[n] for n in SHARED_INPUTS}
    per_example = {n: given[n] for n in ['x', 'mem']}
    grad_fn = _jax.value_and_grad(_loss, argnums=(0, 1))

    def one_microbatch(ex, loss_target):
        ex = dict(ex)
        diff = ex.pop(TWIN_DIFF_INPUT)
        return grad_fn(weights, diff, {**shared, **ex}, loss_target)

    if N_MICROBATCH == 1:
        loss, (grad_w, grad_x) = one_microbatch(per_example, given["loss_target"])
    else:
        def body(carry, xs):
            loss_sum, grad_sum = carry
            l_k, (gw_k, gx_k) = one_microbatch(xs[0], xs[1])
            with _jax.named_scope("update"):
                return (loss_sum + l_k, _jax.tree.map(_jnp.add, grad_sum, gw_k)), gx_k

        init = (_jnp.zeros((), _jnp.float32), _jax.tree.map(_jnp.zeros_like, weights))
        (loss, grad_w), grad_x = _jax.lax.scan(body, init, (per_example, given["loss_target"]))
    with _jax.named_scope("update"):
        delta_w, new_m, new_v = {}, {}, {}
        for n in TWIN_WEIGHTS:
            delta_w[n], new_m[n], new_v[n] = _adamw(weights[n], grad_w[n], given["m_" + n], given["v_" + n])
    return (loss, grad_x, *[grad_w[n] for n in TWIN_WEIGHTS], *[delta_w[n] for n in TWIN_WEIGHTS],
            *[new_m[n] for n in TWIN_WEIGHTS], *[new_v[n] for n in TWIN_WEIGHTS])
```

```python
import functools
import math

import jax
import jax.numpy as jnp
from jax import lax
from jax.experimental import pallas as pl
from jax.experimental.pallas import tpu as pltpu

f32 = jnp.float32
bf16 = jnp.bfloat16

N_DEV = 8
HEAD = 64
XATTN_HEADS = 4
NORM_EPS = 1e-6
LN_EPS = 1e-5
GN_EPS = 64e-5
ADAM_LR, ADAM_B1, ADAM_B2, ADAM_EPS, ADAM_WD, ADAM_STEP = 0.001, 0.9, 0.999, 1e-08, 0.01, 10
LANE = 128
PACK_COLS = 1024
PACK_ROWS = 256
VMEM_LIMIT = 48 * 1024 * 1024
SCAN_CHUNK = 32
SCAN_PAIRS = 4

W_SPEC = {
    'mem_norm_g': (None, False), 'norm_mix_g': (None, False), 'norm_xattn_g': (None, False),
    'norm_ffn_g': (None, False), 'final_norm_g': (None, False),
    'rwkv_mu': (2, False), 'rwkv_w_r': (1, True), 'rwkv_w_k': (1, True), 'rwkv_w_v': (1, True),
    'rwkv_w_o': (1, True), 'rwkv_w0': (None, False), 'rwkv_w1': (1, True), 'rwkv_w2': (2, True),
    'rwkv_a0': (None, False), 'rwkv_a1': (1, True), 'rwkv_a2': (2, True), 'rwkv_g1': (1, True),
    'rwkv_g2': (2, True), 'rwkv_k_k': (None, False), 'rwkv_k_a': (None, False), 'rwkv_r_k': (None, False),
    'rwkv_ln_g': (None, False), 'rwkv_ln_b': (None, False), 'rwkv_v0': (None, False),
    'rwkv_v1': (1, True), 'rwkv_v2': (2, True),
    'conv_w_in': (2, True), 'conv_b_in': (1, False), 'conv_dw': (2, False), 'conv_dw_b': (1, False),
    'conv_ln_g': (1, False), 'conv_ln_b': (1, False), 'conv_w_out': (1, True), 'conv_b_out': (1, False),
    'xattn_w_q': (1, True), 'xattn_w_kv': (2, True), 'xattn_w_o': (1, True),
    'ffn_w_in': (2, True), 'ffn_dw': (2, False), 'ffn_w_out': (1, True),
}
W_NAMES = list(W_SPEC)


def _tile(n, prefs):
    for p in prefs:
        if n % p == 0:
            return p
    return n


def _cparams(sem):
    return pltpu.CompilerParams(dimension_semantics=sem, vmem_limit_bytes=VMEM_LIMIT)


def _sigmoid(x):
    return 1.0 / (1.0 + jnp.exp(-x))


def _softplus(x):
    return jnp.maximum(x, 0.0) + jnp.log(1.0 + jnp.exp(-jnp.abs(x)))


def mm(a, b, *, name, ta=False, tb=False, bias=None, res=None, act=None):
    M, K = (a.shape[1], a.shape[0]) if ta else a.shape
    N = b.shape[0] if tb else b.shape[1]
    assert (b.shape[1] if tb else b.shape[0]) == K, (name, a.shape, b.shape)
    tm = _tile(M, (1024, 512, 256, 128))
    tn = _tile(N, (1024, 512, 256, 128))
    tk = _tile(K, (512, 256, 128))
    nk = K // tk
    dims = (((0 if ta else 1,), (1 if tb else 0,)), ((), ()))
    has_bias, has_res = bias is not None, res is not None

    def body(*refs):
        a_ref, b_ref = refs[0], refs[1]
        pos = 2
        bias_ref = res_ref = None
        if has_bias:
            bias_ref = refs[pos]; pos += 1
        if has_res:
            res_ref = refs[pos]; pos += 1
        o_ref, acc_ref = refs[pos], refs[pos + 1]
        kstep = pl.program_id(2)

        @pl.when(kstep == 0)
        def _():
            acc_ref[...] = jnp.zeros_like(acc_ref)

        acc_ref[...] += lax.dot_general(a_ref[...].astype(bf16), b_ref[...].astype(bf16), dims,
                                        preferred_element_type=f32)

        @pl.when(kstep == nk - 1)
        def _():
            out = acc_ref[...]
            if has_bias:
                out = out + bias_ref[...]
            if act == 'tanh':
                out = jnp.tanh(out)
            elif act == 'sigmoid':
                out = _sigmoid(out)
            if has_res:
                out = out + res_ref[...]
            o_ref[...] = out

    a_spec = pl.BlockSpec((tk, tm), lambda i, j, k: (k, i)) if ta else pl.BlockSpec((tm, tk), lambda i, j, k: (i, k))
    b_spec = pl.BlockSpec((tn, tk), lambda i, j, k: (j, k)) if tb else pl.BlockSpec((tk, tn), lambda i, j, k: (k, j))
    in_specs, args = [a_spec, b_spec], [a, b]
    if has_bias:
        in_specs.append(pl.BlockSpec((1, tn), lambda i, j, k: (0, j))); args.append(bias)
    if has_res:
        in_specs.append(pl.BlockSpec((tm, tn), lambda i, j, k: (i, j))); args.append(res)
    return pl.pallas_call(
        body, name=name, grid=(M // tm, N // tn, nk), in_specs=in_specs,
        out_specs=pl.BlockSpec((tm, tn), lambda i, j, k: (i, j)),
        out_shape=jax.ShapeDtypeStruct((M, N), f32),
        scratch_shapes=[pltpu.VMEM((tm, tn), f32)],
        compiler_params=_cparams(("parallel", "parallel", "arbitrary")),
    )(*args)


def rowwise(fn, rows, pars, *, name, tt=256):
    T = rows[0].shape[0]
    tt = min(tt, T)
    nr, npar = len(rows), len(pars)
    outs = jax.eval_shape(fn, *[jax.ShapeDtypeStruct((tt, r.shape[1]), r.dtype) for r in rows],
                          *[jax.ShapeDtypeStruct(p.shape, p.dtype) for p in pars])

    def body(*refs):
        res = fn(*[r[...] for r in refs[:nr + npar]])
        for o_ref, o in zip(refs[nr + npar:], res):
            o_ref[...] = o

    return pl.pallas_call(
        body, name=name, grid=(T // tt,),
        in_specs=[pl.BlockSpec((tt, r.shape[1]), lambda i: (i, 0)) for r in rows]
        + [pl.BlockSpec(p.shape, lambda i: (0, 0)) for p in pars],
        out_specs=[pl.BlockSpec((tt, o.shape[1]), lambda i: (i, 0)) for o in outs],
        out_shape=[jax.ShapeDtypeStruct((T, o.shape[1]), o.dtype) for o in outs],
        compiler_params=_cparams(("parallel",)),
    )(*rows, *pars)


def rowwise_bwd(fn, rows, pars, cots, *, name, n_drow, n_dpar, add0=None, tt=128):
    T = rows[0].shape[0]
    tt = min(tt, T)
    nr, npar, nc = len(rows), len(pars), len(cots)
    has_add = add0 is not None

    def body(*refs):
        rv = [r[...] for r in refs[:nr]]
        pv = [r[...] for r in refs[nr:nr + npar]]
        cv = [r[...] for r in refs[nr + npar:nr + npar + nc]]
        pos = nr + npar + nc
        add_ref = None
        if has_add:
            add_ref = refs[pos]; pos += 1
        drow_refs = refs[pos:pos + n_drow]
        dpar_refs = refs[pos + n_drow:pos + n_drow + n_dpar]

        def f(*d):
            return fn(*d[:n_drow], *rv[n_drow:], *d[n_drow:], *pv[n_dpar:])

        _, vjp = jax.vjp(f, *rv[:n_drow], *pv[:n_dpar])
        g = vjp(tuple(cv))
        for k in range(n_drow):
            gk = g[k]
            if k == 0 and has_add:
                gk = gk + add_ref[...]
            drow_refs[k][...] = gk

        @pl.when(pl.program_id(0) == 0)
        def _():
            for k in range(n_dpar):
                dpar_refs[k][...] = jnp.zeros_like(dpar_refs[k])

        for k in range(n_dpar):
            dpar_refs[k][...] += g[n_drow + k]

    row_spec = lambda r: pl.BlockSpec((tt, r.shape[1]), lambda i: (i, 0))
    par_spec = lambda p: pl.BlockSpec(p.shape, lambda i: (0, 0))
    in_specs = [row_spec(r) for r in rows] + [par_spec(p) for p in pars] + [row_spec(c) for c in cots]
    args = [*rows, *pars, *cots]
    if has_add:
        in_specs.append(row_spec(add0)); args.append(add0)
    return pl.pallas_call(
        body, name=name, grid=(T // tt,), in_specs=in_specs,
        out_specs=[row_spec(r) for r in rows[:n_drow]] + [par_spec(p) for p in pars[:n_dpar]],
        out_shape=[jax.ShapeDtypeStruct(r.shape, f32) for r in rows[:n_drow]]
        + [jax.ShapeDtypeStruct(p.shape, f32) for p in pars[:n_dpar]],
        compiler_params=_cparams(("arbitrary",)),
    )(*args)


def colwise(fn, cols, out_rows, *, name, nblk):
    def body(*refs):
        res = fn(*[r[...] for r in refs[:len(cols)]])
        for o_ref, o in zip(refs[len(cols):], res):
            o_ref[...] = o

    def spec(rows, off):
        return pl.BlockSpec((rows, LANE), lambda j: (0, j + off))

    return pl.pallas_call(
        body, name=name, grid=(nblk,),
        in_specs=[spec(a.shape[0], off) for a, off in cols],
        out_specs=[spec(r, 0) for r in out_rows],
        out_shape=[jax.ShapeDtypeStruct((r, nblk * LANE), f32) for r in out_rows],
        compiler_params=_cparams(("parallel",)),
    )(*[a for a, _ in cols])


def _shift_dn(x, s):
    if s == 0:
        return x
    rid = lax.broadcasted_iota(jnp.int32, x.shape, 0)
    return jnp.where(rid >= s, pltpu.roll(x, s, 0), 0.0)


def _shift_up(x, s):
    if s == 0:
        return x
    n = x.shape[0]
    rid = lax.broadcasted_iota(jnp.int32, x.shape, 0)
    return jnp.where(rid < n - s, pltpu.roll(x, n - s, 0), 0.0)


def _colsum(x):
    return jnp.sum(x, axis=0, keepdims=True)


def _stack_rows(rows, n):
    c = rows[0].shape[1]
    rid = lax.broadcasted_iota(jnp.int32, (n, c), 0)
    out = jnp.zeros((n, c), f32)
    for i, r in enumerate(rows):
        out = jnp.where(rid == i, jnp.broadcast_to(r, (n, c)), out)
    return out


def _dwconv(x, w, kw):
    acc = None
    for k in range(kw):
        term = w[k:k + 1, :] * _shift_dn(x, kw - 1 - k)
        acc = term if acc is None else acc + term
    return acc


def _dwconv_bwd(x, w, dy, kw, pad_rows):
    dx = None
    rows = []
    for k in range(kw):
        s = kw - 1 - k
        rows.append(_colsum(dy * _shift_dn(x, s)))
        term = w[k:k + 1, :] * _shift_up(dy, s)
        dx = term if dx is None else dx + term
    return dx, _stack_rows(rows, pad_rows)


def _mix_fn(h, mu):
    xx = _shift_dn(h, 1) - h
    return tuple(h + xx * mu[i:i + 1, :] for i in range(6))


def _mix_bwd_fn(h, mu, *ds):
    xx = _shift_dn(h, 1) - h
    s1 = ds[0]
    s2 = ds[0] * mu[0:1, :]
    rows = [_colsum(ds[0] * xx)]
    for i in range(1, 6):
        s1 = s1 + ds[i]
        s2 = s2 + ds[i] * mu[i:i + 1, :]
        rows.append(_colsum(ds[i] * xx))
    return s1 - s2 + _shift_up(s2, 1), _stack_rows(rows, 8)


def _glu_conv_fn(kw, u1, u2, w, b):
    return (_dwconv(u1 * _sigmoid(u2), w, kw) + b,)


def _glu_conv_bwd_fn(kw, pad_rows, u1, u2, w, dc):
    sig = _sigmoid(u2)
    g = u1 * sig
    dg, dw = _dwconv_bwd(g, w, dc, kw, pad_rows)
    return dg * sig, dg * g * (1.0 - sig), dw, _colsum(dc)


def _ffn_act_fn(kw, ug, uv, wg, wv):
    gc = _dwconv(ug, wg, kw)
    vc = _dwconv(uv, wv, kw)
    return (gc * _sigmoid(gc) * vc,)


def _ffn_act_bwd_fn(kw, pad_rows, ug, uv, wg, wv, dact):
    gc = _dwconv(ug, wg, kw)
    vc = _dwconv(uv, wv, kw)
    sg = _sigmoid(gc)
    dvc = dact * gc * sg
    dgc = dact * vc * (sg * (1.0 + gc * (1.0 - sg)))
    dug, dwg = _dwconv_bwd(ug, wg, dgc, kw, pad_rows)
    duv, dwv = _dwconv_bwd(uv, wv, dvc, kw, pad_rows)
    return dug, duv, dwg, dwv


def _rms_fn(x, g):
    return x * lax.rsqrt(jnp.mean(x * x, axis=-1, keepdims=True) + NORM_EPS) * g


def _hsum(x, e, et):
    s = jnp.dot(x, e, precision=lax.Precision.HIGHEST, preferred_element_type=f32)
    return jnp.dot(s, et, precision=lax.Precision.HIGHEST, preferred_element_type=f32)


def _mid_fn(vres, k, v, lw, aa, *rest):
    if vres:
        vv, vf, w0, a0, k_k, k_a, v0, e, et = rest
    else:
        w0, a0, k_k, k_a, e, et = rest
    logw = -_softplus(-(w0 + lw)) - 0.5
    decay = jnp.exp(-jnp.exp(logw))
    a = _sigmoid(a0 + aa)
    kk = k * k_k
    kk = kk / jnp.maximum(jnp.sqrt(_hsum(kk * kk, e, et)), 1e-12)
    k2 = k * (1.0 + (a - 1.0) * k_a)
    v2 = v + (vf - v) * _sigmoid(v0 + vv) if vres else v
    return decay, a, kk, k2, v2


def _post_fn(y, r, k2, v2, gg, ln_g, ln_b, rk, e, et):
    inv = 1.0 / HEAD
    yc = y - _hsum(y, e, et) * inv
    var = _hsum(yc * yc, e, et) * inv
    yn = yc * lax.rsqrt(var + GN_EPS) * ln_g + ln_b
    bonus = _hsum(r * k2 * rk, e, et) * v2
    return ((yn + bonus) * gg,)


def _ln_silu_fn(c, g, b):
    mu = jnp.mean(c, axis=-1, keepdims=True)
    var = jnp.mean(jnp.square(c - mu), axis=-1, keepdims=True)
    ln = (c - mu) * lax.rsqrt(var + LN_EPS) * g + b
    return (ln * _sigmoid(ln),)


def _bias_fn(x, b):
    return (x + b,)


def _dtanh_fn(d, th):
    return (d * (1.0 - th * th),)


def _dsig_fn(d, sg):
    return (d * sg * (1.0 - sg),)


def _add_fn(a, b):
    return (a + b,)


def _seg(x, bd):
    n = x.shape[0]
    h0 = x.astype(bf16)
    r1 = x - h0.astype(f32)
    h1 = r1.astype(bf16)
    h2 = (r1 - h1.astype(f32)).astype(bf16)
    out = jnp.dot(jnp.concatenate([h0, h1, h2], axis=0), bd, preferred_element_type=f32)
    return (out[2 * n:3 * n] + out[n:2 * n]) + out[0:n]


def _scan_consts():
    li = lax.broadcasted_iota(jnp.int32, (LANE, LANE), 0) // HEAD
    lj = lax.broadcasted_iota(jnp.int32, (LANE, LANE), 1) // HEAD
    bd = (li == lj).astype(bf16)
    si = lax.broadcasted_iota(jnp.int32, (HEAD, LANE), 0)
    sj = lax.broadcasted_iota(jnp.int32, (HEAD, LANE), 1) % HEAD
    dg = (si == sj).astype(f32)
    return bd, dg


def _scan_dims(T, D):
    npair = D // LANE
    g = SCAN_PAIRS if npair % SCAN_PAIRS == 0 else npair
    tc = min(SCAN_CHUNK, T)
    return npair, g, tc


def scan_fwd(r, w, k, v, kk, a, *, name):
    T, D = r.shape
    npair, G, tc = _scan_dims(T, D)
    W = G * LANE
    bd, dg = _scan_consts()

    def body(r_ref, w_ref, k_ref, v_ref, kk_ref, a_ref, bd_ref, dg_ref, y_ref, st_ref, s_ref):
        @pl.when(pl.program_id(1) == 0)
        def _():
            s_ref[...] = jnp.zeros_like(s_ref)

        bdv, dgv = bd_ref[...], dg_ref[...]

        def step(t, carry):
            row = pl.ds(t, 1)
            rr, ww, kr, vr, kkr, ar = (x[row, :] for x in (r_ref, w_ref, k_ref, v_ref, kk_ref, a_ref))
            bb = kkr * ar
            sl = [slice(g * LANE, (g + 1) * LANE) for g in range(G)]
            ps = [s_ref[g] for g in range(G)]
            for g in range(G):
                st_ref[t, g] = ps[g]
            lhs = [ps[g] * (-kkr[:, sl[g]]) for g in range(G)]
            lhs += [jnp.broadcast_to(vr[:, sl[g]], (HEAD, LANE)) * dgv for g in range(G)]
            both = _seg(jnp.concatenate(lhs, axis=0), bdv)
            zs = []
            for g in range(G):
                sab = both[g * HEAD:(g + 1) * HEAD]
                vb = both[(G + g) * HEAD:(G + g + 1) * HEAD]
                sn = ps[g] * ww[:, sl[g]] + sab * bb[:, sl[g]] + vb * kr[:, sl[g]]
                s_ref[g] = sn
                zs.append(sn * rr[:, sl[g]])
            yb = _seg(jnp.concatenate(zs, axis=0), bdv)
            yrow = jnp.concatenate([_colsum(yb[g * HEAD:(g + 1) * HEAD] * dgv) for g in range(G)], axis=1)
            y_ref[row, :] = yrow
            return carry

        lax.fori_loop(0, tc, step, 0)

    vec = pl.BlockSpec((tc, W), lambda gi, c: (c, gi))
    return pl.pallas_call(
        body, name=name, grid=(npair // G, T // tc),
        in_specs=[vec] * 6 + [pl.BlockSpec((LANE, LANE), lambda gi, c: (0, 0)),
                              pl.BlockSpec((HEAD, LANE), lambda gi, c: (0, 0))],
        out_specs=[vec, pl.BlockSpec((tc, G, HEAD, LANE), lambda gi, c: (c, gi, 0, 0))],
        out_shape=[jax.ShapeDtypeStruct((T, D), f32), jax.ShapeDtypeStruct((T, npair, HEAD, LANE), f32)],
        scratch_shapes=[pltpu.VMEM((G, HEAD, LANE), f32)],
        compiler_params=_cparams(("parallel", "arbitrary")),
    )(r, w, k, v, kk, a, bd, dg)


def scan_bwd(r, w, k, v, kk, a, dy, states, dr0, dk0, dv0, *, name):
    T, D = r.shape
    npair, G, tc = _scan_dims(T, D)
    W = G * LANE
    nch = T // tc
    bd, dg = _scan_consts()

    def body(r_ref, w_ref, k_ref, v_ref, kk_ref, a_ref, dy_ref, st_ref, dr0_ref, dk0_ref, dv0_ref, bd_ref, dg_ref,
             dr_ref, dw_ref, dk_ref, dv_ref, dkk_ref, da_ref, ds_ref):
        @pl.when(pl.program_id(1) == 0)
        def _():
            ds_ref[...] = jnp.zeros_like(ds_ref)

        bdv, dgv = bd_ref[...], dg_ref[...]

        def step(i, carry):
            t = tc - 1 - i
            row = pl.ds(t, 1)
            rr, ww, kr, vr, kkr, ar, dyr = (x[row, :] for x in (r_ref, w_ref, k_ref, v_ref, kk_ref, a_ref, dy_ref))
            bb = kkr * ar
            sl = [slice(g * LANE, (g + 1) * LANE) for g in range(G)]
            ps = [st_ref[t, g] for g in range(G)]
            lhs = [ps[g] * (-kkr[:, sl[g]]) for g in range(G)]
            lhs += [jnp.broadcast_to(vr[:, sl[g]], (HEAD, LANE)) * dgv for g in range(G)]
            lhs += [jnp.broadcast_to(dyr[:, sl[g]], (HEAD, LANE)) * dgv for g in range(G)]
            tri = _seg(jnp.concatenate(lhs, axis=0), bdv)
            dsts, sabs, dr_rows, dw_rows, db_rows, dk_rows, lhs2a, lhs2b = [], [], [], [], [], [], [], []
            for g in range(G):
                sab = tri[g * HEAD:(g + 1) * HEAD]
                vb = tri[(G + g) * HEAD:(G + g + 1) * HEAD]
                dyb = tri[(2 * G + g) * HEAD:(2 * G + g + 1) * HEAD]
                st = ps[g] * ww[:, sl[g]] + sab * bb[:, sl[g]] + vb * kr[:, sl[g]]
                dst = ds_ref[g] + dyb * rr[:, sl[g]]
                dr_rows.append(_colsum(st * dyb))
                dw_rows.append(_colsum(dst * ps[g]))
                db_rows.append(_colsum(dst * sab))
                dk_rows.append(_colsum(dst * vb))
                lhs2a.append(dst * bb[:, sl[g]])
                lhs2b.append(dst * kr[:, sl[g]])
                dsts.append(dst)
                sabs.append(sab)
            two = _seg(jnp.concatenate(lhs2a + lhs2b, axis=0), bdv)
            dv_rows, dkk_rows, da_rows = [], [], []
            for g in range(G):
                dsab = two[g * HEAD:(g + 1) * HEAD]
                dvb = two[(G + g) * HEAD:(G + g + 1) * HEAD]
                dv_rows.append(_colsum(dvb * dgv))
                ds_ref[g] = dsts[g] * ww[:, sl[g]] - dsab * kkr[:, sl[g]]
                dkk_rows.append(db_rows[g] * ar[:, sl[g]] - _colsum(ps[g] * dsab))
                da_rows.append(db_rows[g] * kkr[:, sl[g]])
            cat = lambda rows: jnp.concatenate(rows, axis=1)
            dr_ref[row, :] = cat(dr_rows) + dr0_ref[row, :]
            dw_ref[row, :] = cat(dw_rows)
            dk_ref[row, :] = cat(dk_rows) + dk0_ref[row, :]
            dv_ref[row, :] = cat(dv_rows) + dv0_ref[row, :]
            dkk_ref[row, :] = cat(dkk_rows)
            da_ref[row, :] = cat(da_rows)
            return carry

        lax.fori_loop(0, tc, step, 0)

    vec = pl.BlockSpec((tc, W), lambda gi, c: (nch - 1 - c, gi))
    return pl.pallas_call(
        body, name=name, grid=(npair // G, nch),
        in_specs=[vec] * 7 + [pl.BlockSpec((tc, G, HEAD, LANE), lambda gi, c: (nch - 1 - c, gi, 0, 0))] + [vec] * 3
        + [pl.BlockSpec((LANE, LANE), lambda gi, c: (0, 0)), pl.BlockSpec((HEAD, LANE), lambda gi, c: (0, 0))],
        out_specs=[vec] * 6,
        out_shape=[jax.ShapeDtypeStruct((T, D), f32)] * 6,
        scratch_shapes=[pltpu.VMEM((G, HEAD, LANE), f32)],
        compiler_params=_cparams(("parallel", "arbitrary")),
    )(r, w, k, v, kk, a, dy, states, dr0, dk0, dv0, bd, dg)


def _attn_p(q, k, scale):
    s = lax.dot_general(q.astype(bf16), k.astype(bf16), (((1,), (1,)), ((), ())), preferred_element_type=f32) * scale
    s = s - jnp.max(s, axis=-1, keepdims=True)
    p = jnp.exp(s)
    return p / jnp.sum(p, axis=-1, keepdims=True)


def attn_fwd(q, kv, *, name):
    T, D = q.shape
    M = kv.shape[0]
    hd = D // XATTN_HEADS
    scale = hd ** -0.5
    tq = _tile(T, (512, 256, 128))

    def body(q_ref, k_ref, v_ref, o_ref):
        p = _attn_p(q_ref[...], k_ref[...], scale)
        o_ref[...] = jnp.dot(p.astype(bf16), v_ref[...].astype(bf16), preferred_element_type=f32)

    return pl.pallas_call(
        body, name=name, grid=(XATTN_HEADS, T // tq),
        in_specs=[pl.BlockSpec((tq, hd), lambda h, i: (i, h)), pl.BlockSpec((M, hd), lambda h, i: (0, h)),
                  pl.BlockSpec((M, hd), lambda h, i: (0, XATTN_HEADS + h))],
        out_specs=pl.BlockSpec((tq, hd), lambda h, i: (i, h)),
        out_shape=jax.ShapeDtypeStruct((T, D), f32),
        compiler_params=_cparams(("parallel", "parallel")),
    )(q, kv, kv)


def attn_bwd(q, kv, do, *, name):
    T, D = q.shape
    M = kv.shape[0]
    hd = D // XATTN_HEADS
    scale = hd ** -0.5
    tq = _tile(T, (512, 256, 128))

    def body(q_ref, k_ref, v_ref, do_ref, dq_ref, dk_ref, dv_ref):
        qv, kvv, vv, dov = q_ref[...], k_ref[...], v_ref[...], do_ref[...]
        p = _attn_p(qv, kvv, scale)
        dob = dov.astype(bf16)
        dp = lax.dot_general(dob, vv.astype(bf16), (((1,), (1,)), ((), ())), preferred_element_type=f32)
        ds = p * (dp - jnp.sum(dp * p, axis=-1, keepdims=True)) * scale
        dsb = ds.astype(bf16)
        dq_ref[...] = jnp.dot(dsb, kvv.astype(bf16), preferred_element_type=f32)

        @pl.when(pl.program_id(1) == 0)
        def _():
            dk_ref[...] = jnp.zeros_like(dk_ref)
            dv_ref[...] = jnp.zeros_like(dv_ref)

        dk_ref[...] += lax.dot_general(dsb, qv.astype(bf16), (((0,), (0,)), ((), ())), preferred_element_type=f32)
        dv_ref[...] += lax.dot_general(p.astype(bf16), dob, (((0,), (0,)), ((), ())), preferred_element_type=f32)

    qspec = pl.BlockSpec((tq, hd), lambda h, i: (i, h))
    mspec = pl.BlockSpec((M, hd), lambda h, i: (0, h))
    return pl.pallas_call(
        body, name=name, grid=(XATTN_HEADS, T // tq),
        in_specs=[qspec, mspec, pl.BlockSpec((M, hd), lambda h, i: (0, XATTN_HEADS + h)), qspec],
        out_specs=[qspec, mspec, mspec],
        out_shape=[jax.ShapeDtypeStruct((T, D), f32), jax.ShapeDtypeStruct((M, D), f32),
                   jax.ShapeDtypeStruct((M, D), f32)],
        compiler_params=_cparams(("parallel", "arbitrary")),
    )(q, kv, kv, do)


def final_loss(x, tgt, g, *, name):
    T, D = x.shape
    tt = min(256, T)

    def body(x_ref, t_ref, g_ref, dx_ref, dg_ref, loss_ref):
        tv = t_ref[...]

        def f(xv, gv):
            e = _rms_fn(xv, gv) - tv
            return 0.5 * jnp.sum(jnp.mean(e * e, axis=-1))

        val, vjp = jax.vjp(f, x_ref[...], g_ref[...])
        dx, dgv = vjp(jnp.ones((), f32))
        dx_ref[...] = dx

        @pl.when(pl.program_id(0) == 0)
        def _():
            dg_ref[...] = jnp.zeros_like(dg_ref)
            loss_ref[...] = jnp.zeros_like(loss_ref)

        dg_ref[...] += dgv
        loss_ref[...] += jnp.full(loss_ref.shape, val, f32)

    row = pl.BlockSpec((tt, D), lambda i: (i, 0))
    return pl.pallas_call(
        body, name=name, grid=(T // tt,),
        in_specs=[row, row, pl.BlockSpec((1, D), lambda i: (0, 0))],
        out_specs=[row, pl.BlockSpec((1, D), lambda i: (0, 0)), pl.BlockSpec((8, LANE), lambda i: (0, 0))],
        out_shape=[jax.ShapeDtypeStruct((T, D), f32), jax.ShapeDtypeStruct((1, D), f32),
                   jax.ShapeDtypeStruct((8, LANE), f32)],
        compiler_params=_cparams(("arbitrary",)),
    )(x, tgt, g)


def _me_and_peers():
    x, y, c = lax.axis_index("x"), lax.axis_index("y"), lax.axis_index("c")
    peers = []
    for k in range(1, N_DEV):
        px, py, pc = x ^ ((k >> 2) & 1), y ^ ((k >> 1) & 1), c ^ (k & 1)
        peers.append(((px, py, pc), 4 * px + 2 * py + pc))
    return 4 * x + 2 * y + c, peers


def all_gather(x, *, name):
    def body(x_ref, o_ref, send_sems, recv_sems, local_sem):
        me, peers = _me_and_peers()
        mine = pltpu.make_async_copy(x_ref, o_ref.at[me], local_sem)
        mine.start()
        sends = []
        for k, (dev, _) in enumerate(peers):
            cp = pltpu.make_async_remote_copy(src_ref=x_ref, dst_ref=o_ref.at[me], send_sem=send_sems.at[k],
                                              recv_sem=recv_sems.at[k], device_id=dev,
                                              device_id_type=pl.DeviceIdType.MESH)
            cp.start()
            sends.append(cp)
        for k, (dev, idx) in enumerate(peers):
            pltpu.make_async_remote_copy(src_ref=x_ref, dst_ref=o_ref.at[idx], send_sem=send_sems.at[k],
                                         recv_sem=recv_sems.at[k], device_id=dev,
                                         device_id_type=pl.DeviceIdType.MESH).wait_recv()
        for cp in sends:
            cp.wait_send()
        mine.wait()

    return pl.pallas_call(
        body, name=name, out_shape=jax.ShapeDtypeStruct((N_DEV,) + x.shape, x.dtype),
        in_specs=[pl.BlockSpec(memory_space=pl.ANY)], out_specs=pl.BlockSpec(memory_space=pl.ANY),
        scratch_shapes=[pltpu.SemaphoreType.DMA((N_DEV - 1,)), pltpu.SemaphoreType.DMA((N_DEV - 1,)),
                        pltpu.SemaphoreType.DMA],
    )(x)


def all_to_all(x, *, name):
    def body(x_ref, o_ref, send_sems, recv_sems, local_sem):
        me, peers = _me_and_peers()
        mine = pltpu.make_async_copy(x_ref.at[me], o_ref.at[me], local_sem)
        mine.start()
        sends = []
        for k, (dev, idx) in enumerate(peers):
            cp = pltpu.make_async_remote_copy(src_ref=x_ref.at[idx], dst_ref=o_ref.at[me], send_sem=send_sems.at[k],
                                              recv_sem=recv_sems.at[k], device_id=dev,
                                              device_id_type=pl.DeviceIdType.MESH)
            cp.start()
            sends.append(cp)
        for k, (dev, idx) in enumerate(peers):
            pltpu.make_async_remote_copy(src_ref=x_ref.at[idx], dst_ref=o_ref.at[idx], send_sem=send_sems.at[k],
                                         recv_sem=recv_sems.at[k], device_id=dev,
                                         device_id_type=pl.DeviceIdType.MESH).wait_recv()
        for cp in sends:
            cp.wait_send()
        mine.wait()

    return pl.pallas_call(
        body, name=name, out_shape=jax.ShapeDtypeStruct(x.shape, x.dtype),
        in_specs=[pl.BlockSpec(memory_space=pl.ANY)], out_specs=pl.BlockSpec(memory_space=pl.ANY),
        scratch_shapes=[pltpu.SemaphoreType.DMA((N_DEV - 1,)), pltpu.SemaphoreType.DMA((N_DEV - 1,)),
                        pltpu.SemaphoreType.DMA],
    )(x)


def adamw(gparts, w, m, v, *, name):
    R, C = w.shape
    tr = min(PACK_ROWS, R)
    c1 = 1.0 / (1.0 - ADAM_B1 ** ADAM_STEP)
    c2 = 1.0 / (1.0 - ADAM_B2 ** ADAM_STEP)

    def body(g_ref, w_ref, m_ref, v_ref, go_ref, d_ref, mo_ref, vo_ref):
        g = g_ref[0]
        for i in range(1, N_DEV):
            g = g + g_ref[i]
        mn = ADAM_B1 * m_ref[...] + (1.0 - ADAM_B1) * g
        vn = ADAM_B2 * v_ref[...] + (1.0 - ADAM_B2) * (g * g)
        go_ref[...] = g
        mo_ref[...] = mn
        vo_ref[...] = vn
        d_ref[...] = -ADAM_LR * ((mn * c1) / (jnp.sqrt(vn * c2) + ADAM_EPS) + ADAM_WD * w_ref[...])

    blk = pl.BlockSpec((tr, C), lambda i: (i, 0))
    return pl.pallas_call(
        body, name=name, grid=(R // tr,),
        in_specs=[pl.BlockSpec((N_DEV, tr, C), lambda i: (0, i, 0)), blk, blk, blk],
        out_specs=[blk] * 4, out_shape=[jax.ShapeDtypeStruct((R, C), f32)] * 4,
        compiler_params=_cparams(("parallel",)),
    )(gparts, w, m, v)


def _pack(arrs, dtype, lead=None):
    if lead is None:
        flat = jnp.concatenate([a.reshape(-1).astype(dtype) for a in arrs])
        n = flat.shape[0]
        gran = PACK_ROWS * PACK_COLS
        tot = -(-n // gran) * gran
        return jnp.pad(flat, (0, tot - n)).reshape(tot // PACK_COLS, PACK_COLS)
    flat = jnp.concatenate([a.reshape(lead, -1).astype(dtype) for a in arrs], axis=1)
    n = flat.shape[1]
    gran = PACK_ROWS * PACK_COLS
    tot = -(-n // gran) * gran
    return jnp.pad(flat, ((0, 0), (0, tot - n))).reshape(lead, tot // PACK_COLS, PACK_COLS)


def _split_shards(full, ax):
    shp = full.shape
    t = full.reshape(shp[:ax] + (N_DEV, shp[ax] // N_DEV) + shp[ax + 1:])
    return jnp.moveaxis(t, ax, 0)


def _join_shards(parts, ax):
    t = jnp.moveaxis(parts, 0, ax)
    shp = t.shape
    return t.reshape(shp[:ax] + (shp[ax] * shp[ax + 1],) + shp[ax + 2:])


def _unpack(buf, shapes, lead=None):
    out, off = [], 0
    flat = buf.reshape(-1) if lead is None else buf.reshape(lead, -1)
    for s in shapes:
        n = math.prod(s)
        if lead is None:
            out.append(flat[off:off + n].reshape(s))
        else:
            out.append(flat[:, off:off + n].reshape((lead,) + tuple(s)))
        off += n
    return out


def _row(v):
    return v.reshape(1, -1)


def _head_mats(D):
    e = (lax.broadcasted_iota(jnp.int32, (D, D // HEAD), 0) // HEAD
         == lax.broadcasted_iota(jnp.int32, (D, D // HEAD), 1)).astype(f32)
    return e, e.T


def rms_fwd(x, g, name):
    return rowwise(lambda xv, gv: (_rms_fn(xv, gv),), [x], [g], name=name)[0]


def rms_bwd(x, g, dh, add, name):
    return rowwise_bwd(lambda xv, gv: (_rms_fn(xv, gv),), [x], [g], [dh], name=name, n_drow=1, n_dpar=1, add0=add)


def rwkv_fwd(x, p, vf, tag):
    vres = vf is not None
    D = x.shape[1]
    e, et = _head_mats(D)
    h = rms_fwd(x, p['norm_g'], tag + "_norm")
    xr, xw, xk, xv, xa, xg = colwise(_mix_fn, [(h, 0), (p['mu'], 0)], [h.shape[0]] * 6, name=tag + "_mix",
                                     nblk=D // LANE)
    r = mm(xr, p['w_r'], name=tag + "_r")
    k = mm(xk, p['w_k'], name=tag + "_k")
    v = mm(xv, p['w_v'], name=tag + "_v")
    th = mm(xw, p['w1'], name=tag + "_w1", act='tanh')
    lw = mm(th, p['w2'], name=tag + "_w2")
    t2 = mm(xa, p['a1'], name=tag + "_a1")
    aa = mm(t2, p['a2'], name=tag + "_a2")
    sg = mm(xg, p['g1'], name=tag + "_g1", act='sigmoid')
    gg = mm(sg, p['g2'], name=tag + "_g2")
    rows = [k, v, lw, aa]
    pars = [p['w0'], p['a0'], p['k_k'], p['k_a']]
    t4 = None
    if vres:
        t4 = mm(xv, p['v1'], name=tag + "_v1")
        vv = mm(t4, p['v2'], name=tag + "_v2")
        rows += [vv, vf]
        pars += [p['v0']]
    pars += [e, et]
    mid = functools.partial(_mid_fn, vres)
    decay, a, kk, k2, v2 = rowwise(mid, rows, pars, name=tag + "_mid")
    y, states = scan_fwd(r, decay, k2, v2, kk, a, name=tag + "_scan")
    post_rows = [y, r, k2, v2, gg]
    post_pars = [p['ln_g'], p['ln_b'], p['r_k'], e, et]
    z = rowwise(_post_fn, post_rows, post_pars, name=tag + "_post")[0]
    xo = mm(z, p['w_o'], name=tag + "_o", res=x)
    saved = dict(x=x, h=h, xs=(xr, xw, xk, xv, xa, xg), r=r, th=th, t2=t2, sg=sg, t4=t4, mid_rows=rows, mid_pars=pars,
                 mid=mid, scan_in=(r, decay, k2, v2, kk, a), states=states, post_rows=post_rows, post_pars=post_pars,
                 z=z, vres=vres)
    return xo, v2, saved


def rwkv_bwd(dxo, dvf_in, p, s, tag):
    D = dxo.shape[1]
    g = {}
    xr, xw, xk, xv, xa, xg = s['xs']
    dz = mm(dxo, p['w_o'], name=tag + "_bo", tb=True)
    g['w_o'] = mm(s['z'], dxo, name=tag + "_bwo", ta=True)
    dy, dr1, dk1, dv1, dgg, g['ln_g'], g['ln_b'], g['r_k'] = rowwise_bwd(
        _post_fn, s['post_rows'], s['post_pars'], [dz], name=tag + "_bpost", n_drow=5, n_dpar=3)
    if dvf_in is not None:
        dv1 = rowwise(_add_fn, [dv1, dvf_in], [], name=tag + "_bvadd")[0]
    dsg = mm(dgg, p['g2'], name=tag + "_bg2", tb=True)
    g['g2'] = mm(s['sg'], dgg, name=tag + "_bwg2", ta=True)
    dt3 = rowwise(_dsig_fn, [dsg, s['sg']], [], name=tag + "_bdsig")[0]
    dxg = mm(dt3, p['g1'], name=tag + "_bg1", tb=True)
    g['g1'] = mm(xg, dt3, name=tag + "_bwg1", ta=True)
    dr, dw, dk2, dv2, dkk, da = scan_bwd(*s['scan_in'], dy, s['states'], dr1, dk1, dv1, name=tag + "_bscan")
    vres = s['vres']
    n_drow = 6 if vres else 4
    n_dpar = 5 if vres else 4
    outs = rowwise_bwd(s['mid'], s['mid_rows'], s['mid_pars'], [dw, da, dkk, dk2, dv2], name=tag + "_bmid",
                       n_drow=n_drow, n_dpar=n_dpar)
    dk, dv, dlw, daa = outs[:4]
    dvf = None
    if vres:
        dvv, dvf = outs[4:6]
        g['w0'], g['a0'], g['k_k'], g['k_a'], g['v0'] = outs[6:]
    else:
        g['w0'], g['a0'], g['k_k'], g['k_a'] = outs[4:]
    dth = mm(dlw, p['w2'], name=tag + "_bw2", tb=True)
    g['w2'] = mm(s['th'], dlw, name=tag + "_bww2", ta=True)
    dt1 = rowwise(_dtanh_fn, [dth, s['th']], [], name=tag + "_bdtanh")[0]
    dxw = mm(dt1, p['w1'], name=tag + "_bw1", tb=True)
    g['w1'] = mm(xw, dt1, name=tag + "_bww1", ta=True)
    dt2 = mm(daa, p['a2'], name=tag + "_ba2", tb=True)
    g['a2'] = mm(s['t2'], daa, name=tag + "_bwa2", ta=True)
    dxa = mm(dt2, p['a1'], name=tag + "_ba1", tb=True)
    g['a1'] = mm(xa, dt2, name=tag + "_bwa1", ta=True)
    dxv = mm(dv, p['w_v'], name=tag + "_bv", tb=True)
    g['w_v'] = mm(xv, dv, name=tag + "_bwv", ta=True)
    if vres:
        dt4 = mm(dvv, p['v2'], name=tag + "_bv2", tb=True)
        g['v2'] = mm(s['t4'], dvv, name=tag + "_bwv2", ta=True)
        dxv = mm(dt4, p['v1'], name=tag + "_bv1", tb=True, res=dxv)
        g['v1'] = mm(xv, dt4, name=tag + "_bwv1", ta=True)
    dxr = mm(dr, p['w_r'], name=tag + "_br", tb=True)
    g['w_r'] = mm(xr, dr, name=tag + "_bwr", ta=True)
    dxk = mm(dk, p['w_k'], name=tag + "_bk", tb=True)
    g['w_k'] = mm(xk, dk, name=tag + "_bwk", ta=True)
    T = dxo.shape[0]
    dh, dmu = colwise(_mix_bwd_fn, [(s['h'], 0), (p['mu'], 0), (dxr, 0), (dxw, 0), (dxk, 0), (dxv, 0), (dxa, 0),
                                    (dxg, 0)], [T, 8], name=tag + "_bmix", nblk=D // LANE)
    g['mu'] = dmu[:6]
    dx, g['norm_g'] = rms_bwd(s['x'], p['norm_g'], dh, dxo, tag + "_bnorm")
    return dx, dvf, g


def conv_fwd(x, p, tag):
    T, D = x.shape
    nb = D // LANE
    kw = p['dw'].shape[0]
    h = rms_fwd(x, p['norm_g'], tag + "_norm")
    u = mm(h, p['w_in'], name=tag + "_in", bias=p['b_in'])
    c = colwise(functools.partial(_glu_conv_fn, kw), [(u, 0), (u, nb), (p['dw'], 0), (p['dw_b'], 0)], [T],
                name=tag + "_dw", nblk=nb)[0]
    sl = rowwise(_ln_silu_fn, [c], [p['ln_g'], p['ln_b']], name=tag + "_ln")[0]
    xo = mm(sl, p['w_out'], name=tag + "_out", bias=p['b_out'], res=x)
    return xo, dict(x=x, h=h, u=u, c=c, sl=sl)


def conv_bwd(dxo, p, s, tag):
    T, D = dxo.shape
    nb = D // LANE
    kw = p['dw'].shape[0]
    kpad = -(-kw // 8) * 8
    g = {}
    dsl = mm(dxo, p['w_out'], name=tag + "_bout", tb=True)
    g['w_out'] = mm(s['sl'], dxo, name=tag + "_bwout", ta=True)
    g['b_out'] = rowwise_bwd(_bias_fn, [dxo], [p['b_out']], [dxo], name=tag + "_bbout", n_drow=0, n_dpar=1)[0]
    dc, g['ln_g'], g['ln_b'] = rowwise_bwd(_ln_silu_fn, [s['c']], [p['ln_g'], p['ln_b']], [dsl], name=tag + "_bln",
                                           n_drow=1, n_dpar=2)
    u = s['u']
    du1, du2, ddw, g['dw_b'] = colwise(functools.partial(_glu_conv_bwd_fn, kw, kpad),
                                       [(u, 0), (u, nb), (p['dw'], 0), (dc, 0)], [T, T, kpad, 1],
                                       name=tag + "_bdw", nblk=nb)
    g['dw'] = ddw[:kw]
    du = jnp.concatenate([du1, du2], axis=1)
    g['b_in'] = rowwise_bwd(_bias_fn, [du], [p['b_in']], [du], name=tag + "_bbin", n_drow=0, n_dpar=1)[0]
    dh = mm(du, p['w_in'], name=tag + "_bin", tb=True)
    g['w_in'] = mm(s['h'], du, name=tag + "_bwin", ta=True)
    dx, g['norm_g'] = rms_bwd(s['x'], p['norm_g'], dh, dxo, tag + "_bnorm")
    return dx, g


def xattn_fwd(x, memn, p, tag):
    hn = rms_fwd(x, p['norm_g'], tag + "_norm")
    q = mm(hn, p['w_q'], name=tag + "_q")
    kv = mm(memn, p['w_kv'], name=tag + "_kv")
    o = attn_fwd(q, kv, name=tag + "_attn")
    xo = mm(o, p['w_o'], name=tag + "_o", res=x)
    return xo, dict(x=x, hn=hn, q=q, kv=kv, o=o)


def xattn_bwd(dxo, dmemn, memn, p, s, tag):
    g = {}
    do = mm(dxo, p['w_o'], name=tag + "_bo", tb=True)
    g['w_o'] = mm(s['o'], dxo, name=tag + "_bwo", ta=True)
    dq, dk, dv = attn_bwd(s['q'], s['kv'], do, name=tag + "_battn")
    dkv = jnp.concatenate([dk, dv], axis=1)
    dmemn = mm(dkv, p['w_kv'], name=tag + "_bkv", tb=True, res=dmemn)
    g['w_kv'] = mm(memn, dkv, name=tag + "_bwkv", ta=True)
    dhn = mm(dq, p['w_q'], name=tag + "_bq", tb=True)
    g['w_q'] = mm(s['hn'], dq, name=tag + "_bwq", ta=True)
    dx, g['norm_g'] = rms_bwd(s['x'], p['norm_g'], dhn, dxo, tag + "_bnorm")
    return dx, dmemn, g


def ffn_fwd(x, p, tag):
    T, D = x.shape
    F = p['w_out'].shape[0]
    nb = F // LANE
    kw = p['dw'].shape[0]
    hn = rms_fwd(x, p['norm_g'], tag + "_norm")
    u = mm(hn, p['w_in'], name=tag + "_in")
    act = colwise(functools.partial(_ffn_act_fn, kw), [(u, 0), (u, nb), (p['dw'], 0), (p['dw'], nb)], [T],
                  name=tag + "_act", nblk=nb)[0]
    xo = mm(act, p['w_out'], name=tag + "_out", res=x)
    return xo, dict(x=x, hn=hn, u=u, act=act)


def ffn_bwd(dxo, p, s, tag):
    T, D = dxo.shape
    F = p['w_out'].shape[0]
    nb = F // LANE
    kw = p['dw'].shape[0]
    g = {}
    dact = mm(dxo, p['w_out'], name=tag + "_bout", tb=True)
    g['w_out'] = mm(s['act'], dxo, name=tag + "_bwout", ta=True)
    u = s['u']
    dug, duv, dwg, dwv = colwise(functools.partial(_ffn_act_bwd_fn, kw, 8),
                                 [(u, 0), (u, nb), (p['dw'], 0), (p['dw'], nb), (dact, 0)], [T, T, 8, 8],
                                 name=tag + "_bact", nblk=nb)
    g['dw'] = jnp.concatenate([dwg[:kw], dwv[:kw]], axis=1)
    du = jnp.concatenate([dug, duv], axis=1)
    dhn = mm(du, p['w_in'], name=tag + "_bin", tb=True)
    g['w_in'] = mm(s['hn'], du, name=tag + "_bwin", ta=True)
    dx, g['norm_g'] = rms_bwd(s['x'], p['norm_g'], dhn, dxo, tag + "_bnorm")
    return dx, g


def _layer_params(W, layer):
    ia = ib = layer // 2
    mixer = {}
    if layer % 2 == 0:
        mixer = dict(norm_g=_row(W['norm_mix_g'][layer]), mu=W['rwkv_mu'][ia], w_r=W['rwkv_w_r'][ia],
                     w_k=W['rwkv_w_k'][ia], w_v=W['rwkv_w_v'][ia], w_o=W['rwkv_w_o'][ia], w0=_row(W['rwkv_w0'][ia]),
                     w1=W['rwkv_w1'][ia], w2=W['rwkv_w2'][ia], a0=_row(W['rwkv_a0'][ia]), a1=W['rwkv_a1'][ia],
                     a2=W['rwkv_a2'][ia], g1=W['rwkv_g1'][ia], g2=W['rwkv_g2'][ia], k_k=_row(W['rwkv_k_k'][ia]),
                     k_a=_row(W['rwkv_k_a'][ia]), r_k=_row(W['rwkv_r_k'][ia]), ln_g=_row(W['rwkv_ln_g'][ia]),
                     ln_b=_row(W['rwkv_ln_b'][ia]))
        if ia > 0:
            mixer.update(v0=_row(W['rwkv_v0'][ia - 1]), v1=W['rwkv_v1'][ia - 1], v2=W['rwkv_v2'][ia - 1])
    else:
        mixer = dict(norm_g=_row(W['norm_mix_g'][layer]), w_in=W['conv_w_in'][ib], b_in=_row(W['conv_b_in'][ib]),
                     dw=W['conv_dw'][ib], dw_b=_row(W['conv_dw_b'][ib]), ln_g=_row(W['conv_ln_g'][ib]),
                     ln_b=_row(W['conv_ln_b'][ib]), w_out=W['conv_w_out'][ib], b_out=_row(W['conv_b_out'][ib]))
    xat = dict(norm_g=_row(W['norm_xattn_g'][layer]), w_q=W['xattn_w_q'][layer], w_kv=W['xattn_w_kv'][layer],
               w_o=W['xattn_w_o'][layer])
    ffn = dict(norm_g=_row(W['norm_ffn_g'][layer]), w_in=W['ffn_w_in'][layer], dw=W['ffn_dw'][layer],
               w_out=W['ffn_w_out'][layer])
    return mixer, xat, ffn


def _gather_weights(local):
    full = {n: local[n] for n in W_NAMES if W_SPEC[n][0] is None}
    for as_bf16, dtype, tag in ((True, bf16, "ag_mat"), (False, f32, "ag_vec")):
        names = [n for n in W_NAMES if W_SPEC[n][0] is not None and W_SPEC[n][1] == as_bf16]
        got = all_gather(_pack([local[n] for n in names], dtype), name=tag)
        parts = _unpack(got, [local[n].shape for n in names], lead=N_DEV)
        for n, part in zip(names, parts):
            full[n] = _join_shards(part, W_SPEC[n][0])
    return full


def _step(local, x, mem, tgt):
    W = _gather_weights(local)
    depth = W['norm_mix_g'].shape[0]
    g_mem = _row(W['mem_norm_g'])
    memn = rms_fwd(mem, g_mem, "mem_norm")
    layers = [_layer_params(W, l) for l in range(depth)]
    saved = []
    vf = None
    for l, (pm, px, pf) in enumerate(layers):
        if l % 2 == 0:
            x, v, sm = rwkv_fwd(x, pm, vf, f"rw{l}")
            if vf is None:
                vf = v
        else:
            x, sm = conv_fwd(x, pm, f"cv{l}")
        x, sx = xattn_fwd(x, memn, px, f"xa{l}")
        x, sf = ffn_fwd(x, pf, f"ff{l}")
        saved.append((sm, sx, sf))
    g_fin = _row(W['final_norm_g'])
    dx, dg_fin, loss_blk = final_loss(x, tgt, g_fin, name="final_loss")

    grads = {n: [None] * W[n].shape[0] for n in W_NAMES if W[n].ndim >= 2}
    grads['final_norm_g'] = dg_fin.reshape(-1)
    dmemn = jnp.zeros_like(memn)
    dvf = None
    for l in reversed(range(depth)):
        pm, px, pf = layers[l]
        sm, sx, sf = saved[l]
        dx, gf = ffn_bwd(dx, pf, sf, f"ff{l}")
        dx, dmemn, gx = xattn_bwd(dx, dmemn, memn, px, sx, f"xa{l}")
        grads['norm_ffn_g'][l] = gf['norm_g'].reshape(-1)
        grads['ffn_w_in'][l], grads['ffn_dw'][l], grads['ffn_w_out'][l] = gf['w_in'], gf['dw'], gf['w_out']
        grads['norm_xattn_g'][l] = gx['norm_g'].reshape(-1)
        grads['xattn_w_q'][l], grads['xattn_w_kv'][l], grads['xattn_w_o'][l] = gx['w_q'], gx['w_kv'], gx['w_o']
        i = l // 2
        if l % 2 == 0:
            dx, dvf_l, gm = rwkv_bwd(dx, dvf if i == 0 else None, pm, sm, f"rw{l}")
            if dvf_l is not None:
                dvf = dvf_l if dvf is None else rowwise(_add_fn, [dvf, dvf_l], [], name=f"rw{l}_dvfadd")[0]
            for short in ('mu', 'w_r', 'w_k', 'w_v', 'w_o', 'w1', 'w2', 'a1', 'a2', 'g1', 'g2'):
                grads['rwkv_' + short][i] = gm[short]
            for short in ('w0', 'a0', 'k_k', 'k_a', 'ln_g', 'ln_b'):
                grads['rwkv_' + short][i] = gm[short].reshape(-1)
            grads['rwkv_r_k'][i] = gm['r_k'].reshape(W['rwkv_r_k'].shape[1:])
            if i > 0:
                grads['rwkv_v0'][i - 1] = gm['v0'].reshape(-1)
                grads['rwkv_v1'][i - 1], grads['rwkv_v2'][i - 1] = gm['v1'], gm['v2']
        else:
            dx, gm = conv_bwd(dx, pm, sm, f"cv{l}")
            for short in ('w_in', 'dw', 'w_out'):
                grads['conv_' + short][i] = gm[short]
            for short in ('b_in', 'dw_b', 'ln_g', 'ln_b', 'b_out'):
                grads['conv_' + short][i] = gm[short].reshape(-1)
        grads['norm_mix_g'][l] = gm['norm_g'].reshape(-1)
    _, dg_mem = rowwise_bwd(lambda xv, gv: (_rms_fn(xv, gv),), [mem], [g_mem], [dmemn], name="mem_norm_b",
                            n_drow=1, n_dpar=1)
    grads['mem_norm_g'] = dg_mem.reshape(-1)
    full_grads = {n: (jnp.stack(gv) if isinstance(gv, list) else gv) for n, gv in grads.items()}
    return loss_blk[0, 0], dx, full_grads


def kernel(x, mem, mem_norm_g, norm_mix_g, norm_xattn_g, norm_ffn_g, final_norm_g, rwkv_mu, rwkv_w_r, rwkv_w_k, rwkv_w_v, rwkv_w_o, rwkv_w0, rwkv_w1, rwkv_w2, rwkv_a0, rwkv_a1, rwkv_a2, rwkv_g1, rwkv_g2, rwkv_k_k, rwkv_k_a, rwkv_r_k, rwkv_ln_g, rwkv_ln_b, rwkv_v0, rwkv_v1, rwkv_v2, conv_w_in, conv_b_in, conv_dw, conv_dw_b, conv_ln_g, conv_ln_b, conv_w_out, conv_b_out, xattn_w_q, xattn_w_kv, xattn_w_o, ffn_w_in, ffn_dw, ffn_w_out, loss_target, m_mem_norm_g, m_norm_mix_g, m_norm_xattn_g, m_norm_ffn_g, m_final_norm_g, m_rwkv_mu, m_rwkv_w_r, m_rwkv_w_k, m_rwkv_w_v, m_rwkv_w_o, m_rwkv_w0, m_rwkv_w1, m_rwkv_w2, m_rwkv_a0, m_rwkv_a1, m_rwkv_a2, m_rwkv_g1, m_rwkv_g2, m_rwkv_k_k, m_rwkv_k_a, m_rwkv_r_k, m_rwkv_ln_g, m_rwkv_ln_b, m_rwkv_v0, m_rwkv_v1, m_rwkv_v2, m_conv_w_in, m_conv_b_in, m_conv_dw, m_conv_dw_b, m_conv_ln_g, m_conv_ln_b, m_conv_w_out, m_conv_b_out, m_xattn_w_q, m_xattn_w_kv, m_xattn_w_o, m_ffn_w_in, m_ffn_dw, m_ffn_w_out, v_mem_norm_g, v_norm_mix_g, v_norm_xattn_g, v_norm_ffn_g, v_final_norm_g, v_rwkv_mu, v_rwkv_w_r, v_rwkv_w_k, v_rwkv_w_v, v_rwkv_w_o, v_rwkv_w0, v_rwkv_w1, v_rwkv_w2, v_rwkv_a0, v_rwkv_a1, v_rwkv_a2, v_rwkv_g1, v_rwkv_g2, v_rwkv_k_k, v_rwkv_k_a, v_rwkv_r_k, v_rwkv_ln_g, v_rwkv_ln_b, v_rwkv_v0, v_rwkv_v1, v_rwkv_v2, v_conv_w_in, v_conv_b_in, v_conv_dw, v_conv_dw_b, v_conv_ln_g, v_conv_ln_b, v_conv_w_out, v_conv_b_out, v_xattn_w_q, v_xattn_w_kv, v_xattn_w_o, v_ffn_w_in, v_ffn_dw, v_ffn_w_out):
    given = dict(locals())
    local = {n: given[n] for n in W_NAMES}
    loss_local, dx, grads = _step(local, x[0], mem[0], loss_target[0])
    loss = lax.psum(loss_local, ("x", "y", "c"))

    sharded = [n for n in W_NAMES if W_SPEC[n][0] is not None]
    repl = [n for n in W_NAMES if W_SPEC[n][0] is None]
    out = {}

    pieces = all_to_all(_pack([_split_shards(grads[n], W_SPEC[n][0]) for n in sharded], f32, lead=N_DEV),
                        name="grad_exchange")
    res = adamw(pieces, *[_pack([given[pre + n] for n in sharded], f32) for pre in ("", "m_", "v_")],
                name="adamw_sharded")
    for kind, buf in zip(("grad_", "delta_", "new_m_", "new_v_"), res):
        for n, arr in zip(sharded, _unpack(buf, [given[n].shape for n in sharded])):
            out[kind + n] = arr

    parts = all_gather(_pack([grads[n] for n in repl], f32), name="grad_gather_repl")
    res = adamw(parts, *[_pack([given[pre + n] for n in repl], f32) for pre in ("", "m_", "v_")],
                name="adamw_repl")
    for kind, buf in zip(("grad_", "delta_", "new_m_", "new_v_"), res):
        for n, arr in zip(repl, _unpack(buf, [given[n].shape for n in repl])):
            out[kind + n] = arr

    return (loss, dx[None], *[out[kind + n] for kind in ("grad_", "delta_", "new_m_", "new_v_") for n in W_NAMES])
```

```python
import functools
import math

import jax
import jax.numpy as jnp
from jax import lax
from jax.experimental import pallas as pl
from jax.experimental.pallas import tpu as pltpu

f32 = jnp.float32
bf16 = jnp.bfloat16

N_DEV = 8
HEAD = 64
XATTN_HEADS = 4
NORM_EPS = 1e-6
LN_EPS = 1e-5
GN_EPS = 64e-5
ADAM_LR, ADAM_B1, ADAM_B2, ADAM_EPS, ADAM_WD, ADAM_STEP = 0.001, 0.9, 0.999, 1e-08, 0.01, 10
LANE = 128
PACK_COLS = 1024
PACK_ROWS = 256
VMEM_LIMIT = 48 * 1024 * 1024
SCAN_CHUNK = 32

W_SPEC = {
    'mem_norm_g': (None, False), 'norm_mix_g': (None, False), 'norm_xattn_g': (None, False),
    'norm_ffn_g': (None, False), 'final_norm_g': (None, False),
    'rwkv_mu': (2, False), 'rwkv_w_r': (1, True), 'rwkv_w_k': (1, True), 'rwkv_w_v': (1, True),
    'rwkv_w_o': (1, True), 'rwkv_w0': (None, False), 'rwkv_w1': (1, True), 'rwkv_w2': (2, True),
    'rwkv_a0': (None, False), 'rwkv_a1': (1, True), 'rwkv_a2': (2, True), 'rwkv_g1': (1, True),
    'rwkv_g2': (2, True), 'rwkv_k_k': (None, False), 'rwkv_k_a': (None, False), 'rwkv_r_k': (None, False),
    'rwkv_ln_g': (None, False), 'rwkv_ln_b': (None, False), 'rwkv_v0': (None, False),
    'rwkv_v1': (1, True), 'rwkv_v2': (2, True),
    'conv_w_in': (2, True), 'conv_b_in': (1, False), 'conv_dw': (2, False), 'conv_dw_b': (1, False),
    'conv_ln_g': (1, False), 'conv_ln_b': (1, False), 'conv_w_out': (1, True), 'conv_b_out': (1, False),
    'xattn_w_q': (1, True), 'xattn_w_kv': (2, True), 'xattn_w_o': (1, True),
    'ffn_w_in': (2, True), 'ffn_dw': (2, False), 'ffn_w_out': (1, True),
}
W_NAMES = list(W_SPEC)


def _tile(n, prefs):
    for p in prefs:
        if n % p == 0:
            return p
    return n


def _cparams(sem):
    return pltpu.CompilerParams(dimension_semantics=sem, vmem_limit_bytes=VMEM_LIMIT)


def _sigmoid(x):
    return 1.0 / (1.0 + jnp.exp(-x))


def _softplus(x):
    return jnp.maximum(x, 0.0) + jnp.log(1.0 + jnp.exp(-jnp.abs(x)))


def mm(a, b, *, name, ta=False, tb=False, bias=None, res=None, act=None, b_dev=None, out_dev=False):
    M, K = (a.shape[1], a.shape[0]) if ta else a.shape
    tm = _tile(M, (1024, 512, 256, 128))
    if b_dev is None:
        N = b.shape[0] if tb else b.shape[1]
        assert (b.shape[1] if tb else b.shape[0]) == K, (name, a.shape, b.shape)
        tn = _tile(N, (1024, 512, 256, 128))
        tk = _tile(K, (512, 256, 128))
    else:
        b_off, b_rows = b_dev
        width = b.shape[2]
        if tb:
            N, tk = b_rows, width
            tn = _tile(N, (1024, 512, 256, 128))
            assert K == N_DEV * width and b_off % tn == 0, (name, a.shape, b.shape)
        else:
            N, tn = N_DEV * width, width
            tk = _tile(K, (512, 256, 128))
            assert K == b_rows and b_off % tk == 0, (name, a.shape, b.shape)
    if out_dev:
        tn = N // N_DEV
    nk = K // tk
    dims = (((0 if ta else 1,), (1 if tb else 0,)), ((), ()))
    has_bias, has_res = bias is not None, res is not None

    def body(*refs):
        a_ref, b_ref = refs[0], refs[1]
        pos = 2
        bias_ref = res_ref = None
        if has_bias:
            bias_ref = refs[pos]; pos += 1
        if has_res:
            res_ref = refs[pos]; pos += 1
        o_ref, acc_ref = refs[pos], refs[pos + 1]
        kstep = pl.program_id(2)

        @pl.when(kstep == 0)
        def _():
            acc_ref[...] = jnp.zeros_like(acc_ref)

        acc_ref[...] += lax.dot_general(a_ref[...].astype(bf16), b_ref[...].astype(bf16), dims,
                                        preferred_element_type=f32)

        @pl.when(kstep == nk - 1)
        def _():
            out = acc_ref[...]
            if has_bias:
                out = out + bias_ref[...]
            if act == 'tanh':
                out = jnp.tanh(out)
            elif act == 'sigmoid':
                out = _sigmoid(out)
            if has_res:
                out = out + res_ref[...]
            o_ref[...] = out

    a_spec = pl.BlockSpec((tk, tm), lambda i, j, k: (k, i)) if ta else pl.BlockSpec((tm, tk), lambda i, j, k: (i, k))
    if b_dev is None:
        b_spec = pl.BlockSpec((tn, tk), lambda i, j, k: (j, k)) if tb else pl.BlockSpec((tk, tn), lambda i, j, k: (k, j))
    elif tb:
        b_spec = pl.BlockSpec((None, tn, tk), lambda i, j, k: (k, b_off // tn + j, 0))
    else:
        b_spec = pl.BlockSpec((None, tk, tn), lambda i, j, k: (j, b_off // tk + k, 0))
    in_specs, args = [a_spec, b_spec], [a, b]
    if has_bias:
        in_specs.append(pl.BlockSpec((1, tn), lambda i, j, k: (0, j))); args.append(bias)
    if has_res:
        in_specs.append(pl.BlockSpec((tm, tn), lambda i, j, k: (i, j))); args.append(res)
    if out_dev:
        out_spec = pl.BlockSpec((None, tm, tn), lambda i, j, k: (j, i, 0))
        out_shape = jax.ShapeDtypeStruct((N_DEV, M, tn), f32)
    else:
        out_spec = pl.BlockSpec((tm, tn), lambda i, j, k: (i, j))
        out_shape = jax.ShapeDtypeStruct((M, N), f32)
    return pl.pallas_call(
        body, name=name, grid=(M // tm, N // tn, nk), in_specs=in_specs,
        out_specs=out_spec, out_shape=out_shape,
        scratch_shapes=[pltpu.VMEM((tm, tn), f32)],
        compiler_params=_cparams(("parallel", "parallel", "arbitrary")),
    )(*args)


def rowwise(fn, rows, pars, *, name, tt=256):
    T = rows[0].shape[0]
    tt = min(tt, T)
    nr, npar = len(rows), len(pars)
    outs = jax.eval_shape(fn, *[jax.ShapeDtypeStruct((tt, r.shape[1]), r.dtype) for r in rows],
                          *[jax.ShapeDtypeStruct(p.shape, p.dtype) for p in pars])

    def body(*refs):
        res = fn(*[r[...] for r in refs[:nr + npar]])
        for o_ref, o in zip(refs[nr + npar:], res):
            o_ref[...] = o

    return pl.pallas_call(
        body, name=name, grid=(T // tt,),
        in_specs=[pl.BlockSpec((tt, r.shape[1]), lambda i: (i, 0)) for r in rows]
        + [pl.BlockSpec(p.shape, lambda i: (0, 0)) for p in pars],
        out_specs=[pl.BlockSpec((tt, o.shape[1]), lambda i: (i, 0)) for o in outs],
        out_shape=[jax.ShapeDtypeStruct((T, o.shape[1]), o.dtype) for o in outs],
        compiler_params=_cparams(("parallel",)),
    )(*rows, *pars)


def rowwise_bwd(fn, rows, pars, cots, *, name, n_drow, n_dpar, add0=None, tt=128):
    T = rows[0].shape[0]
    tt = min(tt, T)
    nr, npar, nc = len(rows), len(pars), len(cots)
    has_add = add0 is not None

    def body(*refs):
        rv = [r[...] for r in refs[:nr]]
        pv = [r[...] for r in refs[nr:nr + npar]]
        cv = [r[...] for r in refs[nr + npar:nr + npar + nc]]
        pos = nr + npar + nc
        add_ref = None
        if has_add:
            add_ref = refs[pos]; pos += 1
        drow_refs = refs[pos:pos + n_drow]
        dpar_refs = refs[pos + n_drow:pos + n_drow + n_dpar]

        def f(*d):
            return fn(*d[:n_drow], *rv[n_drow:], *d[n_drow:], *pv[n_dpar:])

        _, vjp = jax.vjp(f, *rv[:n_drow], *pv[:n_dpar])
        g = vjp(tuple(cv))
        for k in range(n_drow):
            gk = g[k]
            if k == 0 and has_add:
                gk = gk + add_ref[...]
            drow_refs[k][...] = gk

        @pl.when(pl.program_id(0) == 0)
        def _():
            for k in range(n_dpar):
                dpar_refs[k][...] = jnp.zeros_like(dpar_refs[k])

        for k in range(n_dpar):
            dpar_refs[k][...] += g[n_drow + k]

    row_spec = lambda r: pl.BlockSpec((tt, r.shape[1]), lambda i: (i, 0))
    par_spec = lambda p: pl.BlockSpec(p.shape, lambda i: (0, 0))
    in_specs = [row_spec(r) for r in rows] + [par_spec(p) for p in pars] + [row_spec(c) for c in cots]
    args = [*rows, *pars, *cots]
    if has_add:
        in_specs.append(row_spec(add0)); args.append(add0)
    return pl.pallas_call(
        body, name=name, grid=(T // tt,), in_specs=in_specs,
        out_specs=[row_spec(r) for r in rows[:n_drow]] + [par_spec(p) for p in pars[:n_dpar]],
        out_shape=[jax.ShapeDtypeStruct(r.shape, f32) for r in rows[:n_drow]]
        + [jax.ShapeDtypeStruct(p.shape, f32) for p in pars[:n_dpar]],
        compiler_params=_cparams(("arbitrary",)),
    )(*args)


def colwise(fn, cols, out_rows, *, name, nblk):
    def body(*refs):
        res = fn(*[r[...] for r in refs[:len(cols)]])
        for o_ref, o in zip(refs[len(cols):], res):
            o_ref[...] = o

    def spec(rows, off):
        return pl.BlockSpec((rows, LANE), lambda j: (0, j + off))

    return pl.pallas_call(
        body, name=name, grid=(nblk,),
        in_specs=[spec(a.shape[0], off) for a, off in cols],
        out_specs=[spec(r, 0) for r in out_rows],
        out_shape=[jax.ShapeDtypeStruct((r, nblk * LANE), f32) for r in out_rows],
        compiler_params=_cparams(("parallel",)),
    )(*[a for a, _ in cols])


def _shift_dn(x, s):
    if s == 0:
        return x
    rid = lax.broadcasted_iota(jnp.int32, x.shape, 0)
    return jnp.where(rid >= s, pltpu.roll(x, s, 0), 0.0)


def _shift_up(x, s):
    if s == 0:
        return x
    n = x.shape[0]
    rid = lax.broadcasted_iota(jnp.int32, x.shape, 0)
    return jnp.where(rid < n - s, pltpu.roll(x, n - s, 0), 0.0)


def _colsum(x):
    return jnp.sum(x, axis=0, keepdims=True)


def _stack_rows(rows, n):
    c = rows[0].shape[1]
    rid = lax.broadcasted_iota(jnp.int32, (n, c), 0)
    out = jnp.zeros((n, c), f32)
    for i, r in enumerate(rows):
        out = jnp.where(rid == i, jnp.broadcast_to(r, (n, c)), out)
    return out


def _dwconv(x, w, kw):
    acc = None
    for k in range(kw):
        term = w[k:k + 1, :] * _shift_dn(x, kw - 1 - k)
        acc = term if acc is None else acc + term
    return acc


def _dwconv_bwd(x, w, dy, kw, pad_rows):
    dx = None
    rows = []
    for k in range(kw):
        s = kw - 1 - k
        rows.append(_colsum(dy * _shift_dn(x, s)))
        term = w[k:k + 1, :] * _shift_up(dy, s)
        dx = term if dx is None else dx + term
    return dx, _stack_rows(rows, pad_rows)


def _mix_fn(h, mu):
    xx = _shift_dn(h, 1) - h
    return tuple(h + xx * mu[i:i + 1, :] for i in range(6))


def _mix_bwd_fn(h, mu, *ds):
    xx = _shift_dn(h, 1) - h
    s1 = ds[0]
    s2 = ds[0] * mu[0:1, :]
    rows = [_colsum(ds[0] * xx)]
    for i in range(1, 6):
        s1 = s1 + ds[i]
        s2 = s2 + ds[i] * mu[i:i + 1, :]
        rows.append(_colsum(ds[i] * xx))
    return s1 - s2 + _shift_up(s2, 1), _stack_rows(rows, 8)


def _glu_conv_fn(kw, u1, u2, w, b):
    return (_dwconv(u1 * _sigmoid(u2), w, kw) + b,)


def _glu_conv_bwd_fn(kw, pad_rows, u1, u2, w, dc):
    sig = _sigmoid(u2)
    g = u1 * sig
    dg, dw = _dwconv_bwd(g, w, dc, kw, pad_rows)
    return dg * sig, dg * g * (1.0 - sig), dw, _colsum(dc)


def _ffn_act_fn(kw, ug, uv, wg, wv):
    gc = _dwconv(ug, wg, kw)
    vc = _dwconv(uv, wv, kw)
    return (gc * _sigmoid(gc) * vc,)


def _ffn_act_bwd_fn(kw, pad_rows, ug, uv, wg, wv, dact):
    gc = _dwconv(ug, wg, kw)
    vc = _dwconv(uv, wv, kw)
    sg = _sigmoid(gc)
    dvc = dact * gc * sg
    dgc = dact * vc * (sg * (1.0 + gc * (1.0 - sg)))
    dug, dwg = _dwconv_bwd(ug, wg, dgc, kw, pad_rows)
    duv, dwv = _dwconv_bwd(uv, wv, dvc, kw, pad_rows)
    return dug, duv, dwg, dwv


def _rms_fn(x, g):
    return x * lax.rsqrt(jnp.mean(x * x, axis=-1, keepdims=True) + NORM_EPS) * g


def _hsum(x, e, et):
    s = jnp.dot(x, e, precision=lax.Precision.HIGHEST, preferred_element_type=f32)
    return jnp.dot(s, et, precision=lax.Precision.HIGHEST, preferred_element_type=f32)


def _mid_fn(vres, k, v, lw, aa, *rest):
    if vres:
        vv, vf, w0, a0, k_k, k_a, v0, e, et = rest
    else:
        w0, a0, k_k, k_a, e, et = rest
    logw = -_softplus(-(w0 + lw)) - 0.5
    decay = jnp.exp(-jnp.exp(logw))
    a = _sigmoid(a0 + aa)
    kk = k * k_k
    kk = kk / jnp.maximum(jnp.sqrt(_hsum(kk * kk, e, et)), 1e-12)
    k2 = k * (1.0 + (a - 1.0) * k_a)
    v2 = v + (vf - v) * _sigmoid(v0 + vv) if vres else v
    return decay, a, kk, k2, v2


def _post_fn(y, r, k2, v2, gg, ln_g, ln_b, rk, e, et):
    inv = 1.0 / HEAD
    yc = y - _hsum(y, e, et) * inv
    var = _hsum(yc * yc, e, et) * inv
    yn = yc * lax.rsqrt(var + GN_EPS) * ln_g + ln_b
    bonus = _hsum(r * k2 * rk, e, et) * v2
    return ((yn + bonus) * gg,)


def _ln_silu_fn(c, g, b):
    mu = jnp.mean(c, axis=-1, keepdims=True)
    var = jnp.mean(jnp.square(c - mu), axis=-1, keepdims=True)
    ln = (c - mu) * lax.rsqrt(var + LN_EPS) * g + b
    return (ln * _sigmoid(ln),)


def _bias_fn(x, b):
    return (x + b,)


def _dtanh_fn(d, th):
    return (d * (1.0 - th * th),)


def _dsig_fn(d, sg):
    return (d * sg * (1.0 - sg),)


def _add_fn(a, b):
    return (a + b,)


def _seg(blocks, bd):
    n = len(blocks) * HEAD
    x = jnp.concatenate(blocks, axis=0)
    h0 = x.astype(bf16)
    h1 = (x - h0.astype(f32)).astype(bf16)
    out = jnp.dot(jnp.concatenate([h0, h1], axis=0), bd, preferred_element_type=f32)
    res = out[n:2 * n] + out[0:n]
    return [res[i * HEAD:(i + 1) * HEAD] for i in range(len(blocks))]


def _scan_consts():
    li = lax.broadcasted_iota(jnp.int32, (LANE, LANE), 0) // HEAD
    lj = lax.broadcasted_iota(jnp.int32, (LANE, LANE), 1) // HEAD
    bd = (li == lj).astype(bf16)
    si = lax.broadcasted_iota(jnp.int32, (HEAD, LANE), 0)
    sj = lax.broadcasted_iota(jnp.int32, (HEAD, LANE), 1) % HEAD
    dg = (si == sj).astype(f32)
    return bd, dg


def _scan_dims(T, D):
    return D // LANE, min(SCAN_CHUNK, T)


def _head_dots_fn(r, k, kk, a, e, et):
    return _hsum(kk * a * r, e, et), _hsum(k * r, e, et)


def scan_fwd(r, w, k, v, kk, a, br, kr, *, name):
    T, D = r.shape
    G, tc = _scan_dims(T, D)
    bd, dg = _scan_consts()

    def body(r_ref, w_ref, k_ref, v_ref, kk_ref, a_ref, br_ref, kr_ref, bd_ref, dg_ref, y_ref, st_ref, s_ref):
        @pl.when(pl.program_id(0) == 0)
        def _():
            s_ref[...] = jnp.zeros_like(s_ref)

        bdv, dgv = bd_ref[...], dg_ref[...]

        def step(t, carry):
            row = pl.ds(t, 1)
            rr, ww, kr_, vr, kkr, ar, brr, krr = (x[row, :] for x in (r_ref, w_ref, k_ref, v_ref, kk_ref, a_ref,
                                                                    br_ref, kr_ref))
            bb = kkr * ar
            wr = ww * rr
            sl = [slice(g * LANE, (g + 1) * LANE) for g in range(G)]
            ps = [s_ref[g] for g in range(G)]
            for g in range(G):
                st_ref[t, g] = ps[g]
            blocks = [ps[g] * (-kkr[:, sl[g]]) for g in range(G)]
            blocks += [ps[g] * wr[:, sl[g]] for g in range(G)]
            blocks += [jnp.broadcast_to(vr[:, sl[g]], (HEAD, LANE)) * dgv for g in range(G)]
            res = _seg(blocks, bdv)
            yrows = []
            for g in range(G):
                sab, ub, vb = res[g], res[G + g], res[2 * G + g]
                s_ref[g] = ps[g] * ww[:, sl[g]] + sab * bb[:, sl[g]] + vb * kr_[:, sl[g]]
                yb = ub + sab * brr[:, sl[g]] + vb * krr[:, sl[g]]
                yrows.append(_colsum(yb * dgv))
            y_ref[row, :] = jnp.concatenate(yrows, axis=1)
            return carry

        lax.fori_loop(0, tc, step, 0)

    vec = pl.BlockSpec((tc, D), lambda c: (c, 0))
    return pl.pallas_call(
        body, name=name, grid=(T // tc,),
        in_specs=[vec] * 8 + [pl.BlockSpec((LANE, LANE), lambda c: (0, 0)), pl.BlockSpec((HEAD, LANE), lambda c: (0, 0))],
        out_specs=[vec, pl.BlockSpec((tc, G, HEAD, LANE), lambda c: (c, 0, 0, 0))],
        out_shape=[jax.ShapeDtypeStruct((T, D), f32), jax.ShapeDtypeStruct((T, G, HEAD, LANE), f32)],
        scratch_shapes=[pltpu.VMEM((G, HEAD, LANE), f32)],
        compiler_params=_cparams(("arbitrary",)),
    )(r, w, k, v, kk, a, br, kr, bd, dg)


def scan_bwd(r, w, k, v, kk, a, br, kr, dy, states, dr0, dk0, dv0, *, name):
    T, D = r.shape
    G, tc = _scan_dims(T, D)
    nch = T // tc
    bd, dg = _scan_consts()

    def body(r_ref, w_ref, k_ref, v_ref, kk_ref, a_ref, br_ref, kr_ref, dy_ref, st_ref, dr0_ref, dk0_ref, dv0_ref,
             bd_ref, dg_ref, dr_ref, dw_ref, dk_ref, dv_ref, dkk_ref, da_ref, ds_ref):
        @pl.when(pl.program_id(0) == 0)
        def _():
            ds_ref[...] = jnp.zeros_like(ds_ref)

        bdv, dgv = bd_ref[...], dg_ref[...]

        def step(i, carry):
            t = tc - 1 - i
            row = pl.ds(t, 1)
            rr, ww, kr_, vr, kkr, ar, brr, krr, dyr = (x[row, :] for x in (r_ref, w_ref, k_ref, v_ref, kk_ref, a_ref,
                                                                         br_ref, kr_ref, dy_ref))
            bb = kkr * ar
            sl = [slice(g * LANE, (g + 1) * LANE) for g in range(G)]
            ps = [st_ref[t, g] for g in range(G)]
            dss = [ds_ref[g] for g in range(G)]
            blocks = [ps[g] * (-kkr[:, sl[g]]) for g in range(G)]
            blocks += [jnp.broadcast_to(vr[:, sl[g]], (HEAD, LANE)) * dgv for g in range(G)]
            blocks += [jnp.broadcast_to(dyr[:, sl[g]], (HEAD, LANE)) * dgv for g in range(G)]
            blocks += [dss[g] * bb[:, sl[g]] for g in range(G)]
            blocks += [dss[g] * kr_[:, sl[g]] for g in range(G)]
            res = _seg(blocks, bdv)
            dr_rows, dw_rows, dk_rows, dv_rows, dkk_rows, da_rows = [], [], [], [], [], []
            for g in range(G):
                sab, vb, dyb = res[g], res[G + g], res[2 * G + g]
                dsab = res[3 * G + g] + dyb * brr[:, sl[g]]
                dvb = res[4 * G + g] + dyb * krr[:, sl[g]]
                st = ps[g] * ww[:, sl[g]] + sab * bb[:, sl[g]] + vb * kr_[:, sl[g]]
                dst = dss[g] + dyb * rr[:, sl[g]]
                dr_rows.append(_colsum(st * dyb))
                dw_rows.append(_colsum(dst * ps[g]))
                db_row = _colsum(dst * sab)
                dk_rows.append(_colsum(dst * vb))
                dv_rows.append(_colsum(dvb * dgv))
                ds_ref[g] = dst * ww[:, sl[g]] - dsab * kkr[:, sl[g]]
                dkk_rows.append(db_row * ar[:, sl[g]] - _colsum(ps[g] * dsab))
                da_rows.append(db_row * kkr[:, sl[g]])
            cat = lambda rows: jnp.concatenate(rows, axis=1)
            dr_ref[row, :] = cat(dr_rows) + dr0_ref[row, :]
            dw_ref[row, :] = cat(dw_rows)
            dk_ref[row, :] = cat(dk_rows) + dk0_ref[row, :]
            dv_ref[row, :] = cat(dv_rows) + dv0_ref[row, :]
            dkk_ref[row, :] = cat(dkk_rows)
            da_ref[row, :] = cat(da_rows)
            return carry

        lax.fori_loop(0, tc, step, 0)

    vec = pl.BlockSpec((tc, D), lambda c: (nch - 1 - c, 0))
    return pl.pallas_call(
        body, name=name, grid=(nch,),
        in_specs=[vec] * 9 + [pl.BlockSpec((tc, G, HEAD, LANE), lambda c: (nch - 1 - c, 0, 0, 0))] + [vec] * 3
        + [pl.BlockSpec((LANE, LANE), lambda c: (0, 0)), pl.BlockSpec((HEAD, LANE), lambda c: (0, 0))],
        out_specs=[vec] * 6,
        out_shape=[jax.ShapeDtypeStruct((T, D), f32)] * 6,
        scratch_shapes=[pltpu.VMEM((G, HEAD, LANE), f32)],
        compiler_params=_cparams(("arbitrary",)),
    )(r, w, k, v, kk, a, br, kr, dy, states, dr0, dk0, dv0, bd, dg)


def _attn_p(q, k, scale):
    s = lax.dot_general(q.astype(bf16), k.astype(bf16), (((1,), (1,)), ((), ())), preferred_element_type=f32) * scale
    s = s - jnp.max(s, axis=-1, keepdims=True)
    p = jnp.exp(s)
    return p / jnp.sum(p, axis=-1, keepdims=True)


def attn_fwd(q, kv, *, name):
    T, D = q.shape
    M = kv.shape[0]
    hd = D // XATTN_HEADS
    scale = hd ** -0.5
    tq = _tile(T, (512, 256, 128))

    def body(q_ref, k_ref, v_ref, o_ref):
        p = _attn_p(q_ref[...], k_ref[...], scale)
        o_ref[...] = jnp.dot(p.astype(bf16), v_ref[...].astype(bf16), preferred_element_type=f32)

    return pl.pallas_call(
        body, name=name, grid=(XATTN_HEADS, T // tq),
        in_specs=[pl.BlockSpec((tq, hd), lambda h, i: (i, h)), pl.BlockSpec((M, hd), lambda h, i: (0, h)),
                  pl.BlockSpec((M, hd), lambda h, i: (0, XATTN_HEADS + h))],
        out_specs=pl.BlockSpec((tq, hd), lambda h, i: (i, h)),
        out_shape=jax.ShapeDtypeStruct((T, D), f32),
        compiler_params=_cparams(("parallel", "parallel")),
    )(q, kv, kv)


def attn_bwd(q, kv, do, *, name):
    T, D = q.shape
    M = kv.shape[0]
    hd = D // XATTN_HEADS
    scale = hd ** -0.5
    tq = _tile(T, (512, 256, 128))

    def body(q_ref, k_ref, v_ref, do_ref, dq_ref, dk_ref, dv_ref):
        qv, kvv, vv, dov = q_ref[...], k_ref[...], v_ref[...], do_ref[...]
        p = _attn_p(qv, kvv, scale)
        dob = dov.astype(bf16)
        dp = lax.dot_general(dob, vv.astype(bf16), (((1,), (1,)), ((), ())), preferred_element_type=f32)
        ds = p * (dp - jnp.sum(dp * p, axis=-1, keepdims=True)) * scale
        dsb = ds.astype(bf16)
        dq_ref[...] = jnp.dot(dsb, kvv.astype(bf16), preferred_element_type=f32)

        @pl.when(pl.program_id(1) == 0)
        def _():
            dk_ref[...] = jnp.zeros_like(dk_ref)
            dv_ref[...] = jnp.zeros_like(dv_ref)

        dk_ref[...] += lax.dot_general(dsb, qv.astype(bf16), (((0,), (0,)), ((), ())), preferred_element_type=f32)
        dv_ref[...] += lax.dot_general(p.astype(bf16), dob, (((0,), (0,)), ((), ())), preferred_element_type=f32)

    qspec = pl.BlockSpec((tq, hd), lambda h, i: (i, h))
    mspec = pl.BlockSpec((M, hd), lambda h, i: (0, h))
    return pl.pallas_call(
        body, name=name, grid=(XATTN_HEADS, T // tq),
        in_specs=[qspec, mspec, pl.BlockSpec((M, hd), lambda h, i: (0, XATTN_HEADS + h)), qspec],
        out_specs=[qspec, mspec, mspec],
        out_shape=[jax.ShapeDtypeStruct((T, D), f32), jax.ShapeDtypeStruct((M, D), f32),
                   jax.ShapeDtypeStruct((M, D), f32)],
        compiler_params=_cparams(("parallel", "arbitrary")),
    )(q, kv, kv, do)


def final_loss(x, tgt, g, *, name):
    T, D = x.shape
    tt = min(256, T)

    def body(x_ref, t_ref, g_ref, dx_ref, dg_ref, loss_ref):
        tv = t_ref[...]

        def f(xv, gv):
            e = _rms_fn(xv, gv) - tv
            return 0.5 * jnp.sum(jnp.mean(e * e, axis=-1))

        val, vjp = jax.vjp(f, x_ref[...], g_ref[...])
        dx, dgv = vjp(jnp.ones((), f32))
        dx_ref[...] = dx

        @pl.when(pl.program_id(0) == 0)
        def _():
            dg_ref[...] = jnp.zeros_like(dg_ref)
            loss_ref[...] = jnp.zeros_like(loss_ref)

        dg_ref[...] += dgv
        loss_ref[...] += jnp.full(loss_ref.shape, val, f32)

    row = pl.BlockSpec((tt, D), lambda i: (i, 0))
    return pl.pallas_call(
        body, name=name, grid=(T // tt,),
        in_specs=[row, row, pl.BlockSpec((1, D), lambda i: (0, 0))],
        out_specs=[row, pl.BlockSpec((1, D), lambda i: (0, 0)), pl.BlockSpec((8, LANE), lambda i: (0, 0))],
        out_shape=[jax.ShapeDtypeStruct((T, D), f32), jax.ShapeDtypeStruct((1, D), f32),
                   jax.ShapeDtypeStruct((8, LANE), f32)],
        compiler_params=_cparams(("arbitrary",)),
    )(x, tgt, g)


def _place():
    x, y, c = lax.axis_index("x"), lax.axis_index("y"), lax.axis_index("c")
    chips = [(1 - x, y), (x, 1 - y), (1 - x, 1 - y)]
    return x, y, c, chips


def _rcopy(src, dst, send_sems, recv_sems, k, dev):
    return pltpu.make_async_remote_copy(src_ref=src, dst_ref=dst, send_sem=send_sems.at[k], recv_sem=recv_sems.at[k],
                                        device_id=dev, device_id_type=pl.DeviceIdType.MESH)


def _comm_call(body, name, x, out_shape, n_sems):
    return pl.pallas_call(
        body, name=name, out_shape=out_shape,
        in_specs=[pl.BlockSpec(memory_space=pl.ANY)], out_specs=pl.BlockSpec(memory_space=pl.ANY),
        scratch_shapes=[pltpu.SemaphoreType.DMA((n_sems,)), pltpu.SemaphoreType.DMA((n_sems,)),
                        pltpu.SemaphoreType.DMA],
    )(x)


def all_gather(x, *, name):
    def body(x_ref, o_ref, send_sems, recv_sems, local_sem):
        x_, y_, c_, chips = _place()
        sibling = (x_, y_, 1 - c_)
        slot = lambda px, py, pc: o_ref.at[4 * px + 2 * py + pc]
        mine = pltpu.make_async_copy(x_ref, slot(x_, y_, c_), local_sem)
        mine.start()
        first = [_rcopy(x_ref, slot(x_, y_, c_), send_sems, recv_sems, 0, sibling)]
        first += [_rcopy(x_ref, slot(x_, y_, c_), send_sems, recv_sems, 1 + j, (*chip, c_))
                  for j, chip in enumerate(chips)]
        for cp in first:
            cp.start()
        passed = [_rcopy(slot(*chip, c_), slot(*chip, c_), send_sems, recv_sems, 4 + j, sibling)
                  for j, chip in enumerate(chips)]
        for j, chip in enumerate(chips):
            _rcopy(x_ref, slot(*chip, c_), send_sems, recv_sems, 1 + j, (*chip, c_)).wait_recv()
            passed[j].start()
        _rcopy(x_ref, slot(x_, y_, 1 - c_), send_sems, recv_sems, 0, sibling).wait_recv()
        for j, chip in enumerate(chips):
            _rcopy(x_ref, slot(*chip, 1 - c_), send_sems, recv_sems, 4 + j, sibling).wait_recv()
        for cp in first + passed:
            cp.wait_send()
        mine.wait()

    return _comm_call(body, name, x, jax.ShapeDtypeStruct((N_DEV,) + x.shape, x.dtype), 7)


def pair_exchange(x, *, name):
    def body(x_ref, o_ref, send_sems, recv_sems, local_sem):
        x_, y_, c_, _ = _place()
        cp = _rcopy(x_ref, o_ref, send_sems, recv_sems, 0, (x_, y_, 1 - c_))
        cp.start()
        cp.wait()

    return _comm_call(body, name, x, jax.ShapeDtypeStruct(x.shape, x.dtype), 1)


def chip_exchange(x, *, name):
    def body(x_ref, o_ref, send_sems, recv_sems, local_sem):
        x_, y_, c_, chips = _place()
        myq = 2 * x_ + y_
        mine = pltpu.make_async_copy(x_ref.at[myq], o_ref.at[myq], local_sem)
        mine.start()
        sends = [_rcopy(x_ref.at[2 * px + py], o_ref.at[myq], send_sems, recv_sems, j, (px, py, c_))
                 for j, (px, py) in enumerate(chips)]
        for cp in sends:
            cp.start()
        for j, (px, py) in enumerate(chips):
            _rcopy(x_ref.at[myq], o_ref.at[2 * px + py], send_sems, recv_sems, j, (px, py, c_)).wait_recv()
        for cp in sends:
            cp.wait_send()
        mine.wait()

    return _comm_call(body, name, x, jax.ShapeDtypeStruct(x.shape, x.dtype), 3)


def _add_cast(a, b, *, name):
    n, R, C = a.shape
    tr = min(PACK_ROWS, R)
    blk = pl.BlockSpec((n, tr, C), lambda i: (0, i, 0))

    def body(a_ref, b_ref, o_ref):
        o_ref[...] = (a_ref[...] + b_ref[...]).astype(bf16)

    return pl.pallas_call(body, name=name, grid=(R // tr,), in_specs=[blk, blk], out_specs=blk,
                          out_shape=jax.ShapeDtypeStruct(a.shape, bf16), compiler_params=_cparams(("parallel",)))(a, b)


def reduce_scatter(pieces, *, name):
    n, R, C = pieces.shape
    c = lax.axis_index("c")
    by_core = pieces.reshape(n // 2, 2, R, C)
    keep = lax.dynamic_index_in_dim(by_core, c, axis=1, keepdims=False)
    give = lax.dynamic_index_in_dim(by_core, 1 - c, axis=1, keepdims=False)
    got = pair_exchange(give, name=name + "_pair")
    return chip_exchange(_add_cast(keep, got, name=name + "_add"), name=name + "_chip")


def adamw(gparts, w, m, v, *, name):
    R, C = w.shape
    n_parts = gparts.shape[0]
    tr = min(PACK_ROWS, R)
    c1 = 1.0 / (1.0 - ADAM_B1 ** ADAM_STEP)
    c2 = 1.0 / (1.0 - ADAM_B2 ** ADAM_STEP)

    def body(g_ref, w_ref, m_ref, v_ref, go_ref, d_ref, mo_ref, vo_ref):
        g = g_ref[0].astype(f32)
        for i in range(1, n_parts):
            g = g + g_ref[i].astype(f32)
        mn = ADAM_B1 * m_ref[...] + (1.0 - ADAM_B1) * g
        vn = ADAM_B2 * v_ref[...] + (1.0 - ADAM_B2) * (g * g)
        go_ref[...] = g
        mo_ref[...] = mn
        vo_ref[...] = vn
        d_ref[...] = -ADAM_LR * ((mn * c1) / (jnp.sqrt(vn * c2) + ADAM_EPS) + ADAM_WD * w_ref[...])

    blk = pl.BlockSpec((tr, C), lambda i: (i, 0))
    return pl.pallas_call(
        body, name=name, grid=(R // tr,),
        in_specs=[pl.BlockSpec((n_parts, tr, C), lambda i: (0, i, 0)), blk, blk, blk],
        out_specs=[blk] * 4, out_shape=[jax.ShapeDtypeStruct((R, C), f32)] * 4,
        compiler_params=_cparams(("parallel",)),
    )(gparts, w, m, v)


def _pack(arrs, dtype, lead=None):
    if lead is None:
        flat = jnp.concatenate([a.reshape(-1).astype(dtype) for a in arrs])
        n = flat.shape[0]
        gran = PACK_ROWS * PACK_COLS
        tot = -(-n // gran) * gran
        return jnp.pad(flat, (0, tot - n)).reshape(tot // PACK_COLS, PACK_COLS)
    flat = jnp.concatenate([a.reshape(lead, -1).astype(dtype) for a in arrs], axis=1)
    n = flat.shape[1]
    gran = PACK_ROWS * PACK_COLS
    tot = -(-n // gran) * gran
    return jnp.pad(flat, ((0, 0), (0, tot - n))).reshape(lead, tot // PACK_COLS, PACK_COLS)


def _split_shards(full, ax):
    shp = full.shape
    t = full.reshape(shp[:ax] + (N_DEV, shp[ax] // N_DEV) + shp[ax + 1:])
    return jnp.moveaxis(t, ax, 0)


def _join_shards(parts, ax):
    t = jnp.moveaxis(parts, 0, ax)
    shp = t.shape
    return t.reshape(shp[:ax] + (shp[ax] * shp[ax + 1],) + shp[ax + 2:])


def _unpack(buf, shapes, lead=None):
    out, off = [], 0
    flat = buf.reshape(-1) if lead is None else buf.reshape(lead, -1)
    for s in shapes:
        n = math.prod(s)
        if lead is None:
            out.append(flat[off:off + n].reshape(s))
        else:
            out.append(flat[:, off:off + n].reshape((lead,) + tuple(s)))
        off += n
    return out


def _row(v):
    return v.reshape(1, -1)


def _head_mats(D):
    e = (lax.broadcasted_iota(jnp.int32, (D, D // HEAD), 0) // HEAD
         == lax.broadcasted_iota(jnp.int32, (D, D // HEAD), 1)).astype(f32)
    return e, e.T


def rms_fwd(x, g, name):
    return rowwise(lambda xv, gv: (_rms_fn(xv, gv),), [x], [g], name=name)[0]


def rms_bwd(x, g, dh, add, name):
    return rowwise_bwd(lambda xv, gv: (_rms_fn(xv, gv),), [x], [g], [dh], name=name, n_drow=1, n_dpar=1, add0=add)


def rwkv_fwd(x, p, vf, tag):
    vres = vf is not None
    D = x.shape[1]
    e, et = _head_mats(D)
    h = rms_fwd(x, p['norm_g'], tag + "_norm")
    xr, xw, xk, xv, xa, xg = colwise(_mix_fn, [(h, 0), (p['mu'], 0)], [h.shape[0]] * 6, name=tag + "_mix",
                                     nblk=D // LANE)
    r = mm(xr, p['w_r'], name=tag + "_r")
    k = mm(xk, p['w_k'], name=tag + "_k")
    v = mm(xv, p['w_v'], name=tag + "_v")
    th = mm(xw, p['w1'], name=tag + "_w1", act='tanh')
    lw = mm(th, p['w2'], name=tag + "_w2")
    t2 = mm(xa, p['a1'], name=tag + "_a1")
    aa = mm(t2, p['a2'], name=tag + "_a2")
    sg = mm(xg, p['g1'], name=tag + "_g1", act='sigmoid')
    gg = mm(sg, p['g2'], name=tag + "_g2")
    rows = [k, v, lw, aa]
    pars = [p['w0'], p['a0'], p['k_k'], p['k_a']]
    t4 = None
    if vres:
        t4 = mm(xv, p['v1'], name=tag + "_v1")
        vv = mm(t4, p['v2'], name=tag + "_v2")
        rows += [vv, vf]
        pars += [p['v0']]
    pars += [e, et]
    mid = functools.partial(_mid_fn, vres)
    decay, a, kk, k2, v2 = rowwise(mid, rows, pars, name=tag + "_mid")
    br, kr = rowwise(_head_dots_fn, [r, k2, kk, a], [e, et], name=tag + "_hdots")
    y, states = scan_fwd(r, decay, k2, v2, kk, a, br, kr, name=tag + "_scan")
    post_rows = [y, r, k2, v2, gg]
    post_pars = [p['ln_g'], p['ln_b'], p['r_k'], e, et]
    z = rowwise(_post_fn, post_rows, post_pars, name=tag + "_post")[0]
    xo = mm(z, p['w_o'], name=tag + "_o", res=x)
    saved = dict(x=x, h=h, xs=(xr, xw, xk, xv, xa, xg), r=r, th=th, t2=t2, sg=sg, t4=t4, mid_rows=rows, mid_pars=pars,
                 mid=mid, scan_in=(r, decay, k2, v2, kk, a, br, kr), states=states, post_rows=post_rows, post_pars=post_pars,
                 z=z, vres=vres)
    return xo, v2, saved


def rwkv_bwd(dxo, dvf_in, p, s, tag):
    D = dxo.shape[1]
    g = {}
    xr, xw, xk, xv, xa, xg = s['xs']
    dz = mm(dxo, p['w_o'], name=tag + "_bo", tb=True)
    g['w_o'] = mm(s['z'], dxo, name=tag + "_bwo", ta=True)
    dy, dr1, dk1, dv1, dgg, g['ln_g'], g['ln_b'], g['r_k'] = rowwise_bwd(
        _post_fn, s['post_rows'], s['post_pars'], [dz], name=tag + "_bpost", n_drow=5, n_dpar=3)
    if dvf_in is not None:
        dv1 = rowwise(_add_fn, [dv1, dvf_in], [], name=tag + "_bvadd")[0]
    dsg = mm(dgg, p['g2'], name=tag + "_bg2", tb=True)
    g['g2'] = mm(s['sg'], dgg, name=tag + "_bwg2", ta=True)
    dt3 = rowwise(_dsig_fn, [dsg, s['sg']], [], name=tag + "_bdsig")[0]
    dxg = mm(dt3, p['g1'], name=tag + "_bg1", tb=True)
    g['g1'] = mm(xg, dt3, name=tag + "_bwg1", ta=True)
    dr, dw, dk2, dv2, dkk, da = scan_bwd(*s['scan_in'], dy, s['states'], dr1, dk1, dv1, name=tag + "_bscan")
    vres = s['vres']
    n_drow = 6 if vres else 4
    n_dpar = 5 if vres else 4
    outs = rowwise_bwd(s['mid'], s['mid_rows'], s['mid_pars'], [dw, da, dkk, dk2, dv2], name=tag + "_bmid",
                       n_drow=n_drow, n_dpar=n_dpar)
    dk, dv, dlw, daa = outs[:4]
    dvf = None
    if vres:
        dvv, dvf = outs[4:6]
        g['w0'], g['a0'], g['k_k'], g['k_a'], g['v0'] = outs[6:]
    else:
        g['w0'], g['a0'], g['k_k'], g['k_a'] = outs[4:]
    dth = mm(dlw, p['w2'], name=tag + "_bw2", tb=True)
    g['w2'] = mm(s['th'], dlw, name=tag + "_bww2", ta=True)
    dt1 = rowwise(_dtanh_fn, [dth, s['th']], [], name=tag + "_bdtanh")[0]
    dxw = mm(dt1, p['w1'], name=tag + "_bw1", tb=True)
    g['w1'] = mm(xw, dt1, name=tag + "_bww1", ta=True)
    dt2 = mm(daa, p['a2'], name=tag + "_ba2", tb=True)
    g['a2'] = mm(s['t2'], daa, name=tag + "_bwa2", ta=True)
    dxa = mm(dt2, p['a1'], name=tag + "_ba1", tb=True)
    g['a1'] = mm(xa, dt2, name=tag + "_bwa1", ta=True)
    dxv = mm(dv, p['w_v'], name=tag + "_bv", tb=True)
    g['w_v'] = mm(xv, dv, name=tag + "_bwv", ta=True)
    if vres:
        dt4 = mm(dvv, p['v2'], name=tag + "_bv2", tb=True)
        g['v2'] = mm(s['t4'], dvv, name=tag + "_bwv2", ta=True)
        dxv = mm(dt4, p['v1'], name=tag + "_bv1", tb=True, res=dxv)
        g['v1'] = mm(xv, dt4, name=tag + "_bwv1", ta=True)
    dxr = mm(dr, p['w_r'], name=tag + "_br", tb=True)
    g['w_r'] = mm(xr, dr, name=tag + "_bwr", ta=True)
    dxk = mm(dk, p['w_k'], name=tag + "_bk", tb=True)
    g['w_k'] = mm(xk, dk, name=tag + "_bwk", ta=True)
    T = dxo.shape[0]
    dh, dmu = colwise(_mix_bwd_fn, [(s['h'], 0), (p['mu'], 0), (dxr, 0), (dxw, 0), (dxk, 0), (dxv, 0), (dxa, 0),
                                    (dxg, 0)], [T, 8], name=tag + "_bmix", nblk=D // LANE)
    g['mu'] = dmu[:6]
    dx, g['norm_g'] = rms_bwd(s['x'], p['norm_g'], dh, dxo, tag + "_bnorm")
    return dx, dvf, g


def conv_fwd(x, p, tag):
    T, D = x.shape
    nb = D // LANE
    kw = p['dw'].shape[0]
    h = rms_fwd(x, p['norm_g'], tag + "_norm")
    u = mm(h, p['w_in'], name=tag + "_in", bias=p['b_in'])
    c = colwise(functools.partial(_glu_conv_fn, kw), [(u, 0), (u, nb), (p['dw'], 0), (p['dw_b'], 0)], [T],
                name=tag + "_dw", nblk=nb)[0]
    sl = rowwise(_ln_silu_fn, [c], [p['ln_g'], p['ln_b']], name=tag + "_ln")[0]
    xo = mm(sl, p['w_out'], name=tag + "_out", bias=p['b_out'], res=x)
    return xo, dict(x=x, h=h, u=u, c=c, sl=sl)


def conv_bwd(dxo, p, s, tag):
    T, D = dxo.shape
    nb = D // LANE
    kw = p['dw'].shape[0]
    kpad = -(-kw // 8) * 8
    g = {}
    dsl = mm(dxo, p['w_out'], name=tag + "_bout", tb=True)
    g['w_out'] = mm(s['sl'], dxo, name=tag + "_bwout", ta=True)
    g['b_out'] = rowwise_bwd(_bias_fn, [dxo], [p['b_out']], [dxo], name=tag + "_bbout", n_drow=0, n_dpar=1)[0]
    dc, g['ln_g'], g['ln_b'] = rowwise_bwd(_ln_silu_fn, [s['c']], [p['ln_g'], p['ln_b']], [dsl], name=tag + "_bln",
                                           n_drow=1, n_dpar=2)
    u = s['u']
    du1, du2, ddw, g['dw_b'] = colwise(functools.partial(_glu_conv_bwd_fn, kw, kpad),
                                       [(u, 0), (u, nb), (p['dw'], 0), (dc, 0)], [T, T, kpad, 1],
                                       name=tag + "_bdw", nblk=nb)
    g['dw'] = ddw[:kw]
    du = jnp.concatenate([du1, du2], axis=1)
    g['b_in'] = rowwise_bwd(_bias_fn, [du], [p['b_in']], [du], name=tag + "_bbin", n_drow=0, n_dpar=1)[0]
    dh = mm(du, p['w_in'], name=tag + "_bin", tb=True)
    g['w_in'] = mm(s['h'], du, name=tag + "_bwin", ta=True)
    dx, g['norm_g'] = rms_bwd(s['x'], p['norm_g'], dh, dxo, tag + "_bnorm")
    return dx, g


def xattn_fwd(x, memn, p, tag):
    hn = rms_fwd(x, p['norm_g'], tag + "_norm")
    q = mm(hn, p['w_q'], name=tag + "_q")
    kv = mm(memn, p['w_kv'], name=tag + "_kv")
    o = attn_fwd(q, kv, name=tag + "_attn")
    xo = mm(o, p['w_o'], name=tag + "_o", res=x)
    return xo, dict(x=x, hn=hn, q=q, kv=kv, o=o)


def xattn_bwd(dxo, dmemn, memn, p, s, tag):
    g = {}
    do = mm(dxo, p['w_o'], name=tag + "_bo", tb=True)
    g['w_o'] = mm(s['o'], dxo, name=tag + "_bwo", ta=True)
    dq, dk, dv = attn_bwd(s['q'], s['kv'], do, name=tag + "_battn")
    dkv = jnp.concatenate([dk, dv], axis=1)
    dmemn = mm(dkv, p['w_kv'], name=tag + "_bkv", tb=True, res=dmemn)
    g['w_kv'] = mm(memn, dkv, name=tag + "_bwkv", ta=True)
    dhn = mm(dq, p['w_q'], name=tag + "_bq", tb=True)
    g['w_q'] = mm(s['hn'], dq, name=tag + "_bwq", ta=True)
    dx, g['norm_g'] = rms_bwd(s['x'], p['norm_g'], dhn, dxo, tag + "_bnorm")
    return dx, dmemn, g


def ffn_fwd(x, p, tag):
    T, D = x.shape
    w_dev, layer = p['w_in']
    nb = (N_DEV // 2) * w_dev.shape[2] // LANE
    kw = p['dw'].shape[0]
    hn = rms_fwd(x, p['norm_g'], tag + "_norm")
    u = mm(hn, w_dev, name=tag + "_in", b_dev=(layer * D, D))
    act = colwise(functools.partial(_ffn_act_fn, kw), [(u, 0), (u, nb), (p['dw'], 0), (p['dw'], nb)], [T],
                  name=tag + "_act", nblk=nb)[0]
    xo = mm(act, p['w_out'], name=tag + "_out", res=x)
    return xo, dict(x=x, hn=hn, u=u, act=act)


def ffn_bwd(dxo, p, s, tag):
    T, D = dxo.shape
    w_dev, layer = p['w_in']
    nb = (N_DEV // 2) * w_dev.shape[2] // LANE
    kw = p['dw'].shape[0]
    g = {}
    dact = mm(dxo, p['w_out'], name=tag + "_bout", tb=True)
    g['w_out'] = mm(s['act'], dxo, name=tag + "_bwout", ta=True)
    u = s['u']
    dug, duv, dwg, dwv = colwise(functools.partial(_ffn_act_bwd_fn, kw, 8),
                                 [(u, 0), (u, nb), (p['dw'], 0), (p['dw'], nb), (dact, 0)], [T, T, 8, 8],
                                 name=tag + "_bact", nblk=nb)
    g['dw'] = jnp.concatenate([dwg[:kw], dwv[:kw]], axis=1)
    du = jnp.concatenate([dug, duv], axis=1)
    dhn = mm(du, w_dev, name=tag + "_bin", tb=True, b_dev=(layer * D, D))
    g['w_in'] = mm(s['hn'], du, name=tag + "_bwin", ta=True, out_dev=True)
    dx, g['norm_g'] = rms_bwd(s['x'], p['norm_g'], dhn, dxo, tag + "_bnorm")
    return dx, g


def _lane_pad(n):
    return -(-n // LANE) * LANE


def _pad_blocks(a, axis, nblk):
    shp = a.shape
    n = shp[axis] // nblk
    t = a.reshape(shp[:axis] + (nblk, n) + shp[axis + 1:])
    pad = [(0, 0)] * t.ndim
    pad[axis + 1] = (0, _lane_pad(n) - n)
    t = jnp.pad(t, pad)
    return t.reshape(shp[:axis] + (nblk * _lane_pad(n),) + shp[axis + 1:])


def _unpad_blocks(a, axis, nblk, n):
    shp = a.shape
    t = a.reshape(shp[:axis] + (nblk, shp[axis] // nblk) + shp[axis + 1:])
    t = lax.slice_in_dim(t, 0, n, axis=axis + 1)
    return t.reshape(shp[:axis] + (nblk * n,) + shp[axis + 1:])


def _layer_params(W, layer):
    ia = ib = layer // 2
    mixer = {}
    if layer % 2 == 0:
        mixer = dict(norm_g=_row(W['norm_mix_g'][layer]), mu=W['rwkv_mu'][ia], w_r=W['rwkv_w_r'][ia],
                     w_k=W['rwkv_w_k'][ia], w_v=W['rwkv_w_v'][ia], w_o=W['rwkv_w_o'][ia], w0=_row(W['rwkv_w0'][ia]),
                     w1=W['rwkv_w1'][ia], w2=W['rwkv_w2'][ia], a0=_row(W['rwkv_a0'][ia]), a1=W['rwkv_a1'][ia],
                     a2=W['rwkv_a2'][ia], g1=W['rwkv_g1'][ia], g2=W['rwkv_g2'][ia], k_k=_row(W['rwkv_k_k'][ia]),
                     k_a=_row(W['rwkv_k_a'][ia]), r_k=_row(W['rwkv_r_k'][ia]), ln_g=_row(W['rwkv_ln_g'][ia]),
                     ln_b=_row(W['rwkv_ln_b'][ia]))
        if ia > 0:
            mixer.update(v0=_row(W['rwkv_v0'][ia - 1]), v1=W['rwkv_v1'][ia - 1], v2=W['rwkv_v2'][ia - 1])
    else:
        mixer = dict(norm_g=_row(W['norm_mix_g'][layer]), w_in=W['conv_w_in'][ib], b_in=_row(W['conv_b_in'][ib]),
                     dw=W['conv_dw'][ib], dw_b=_row(W['conv_dw_b'][ib]), ln_g=_row(W['conv_ln_g'][ib]),
                     ln_b=_row(W['conv_ln_b'][ib]), w_out=W['conv_w_out'][ib], b_out=_row(W['conv_b_out'][ib]))
    xat = dict(norm_g=_row(W['norm_xattn_g'][layer]), w_q=W['xattn_w_q'][layer], w_kv=W['xattn_w_kv'][layer],
               w_o=W['xattn_w_o'][layer])
    ffn = dict(norm_g=_row(W['norm_ffn_g'][layer]), w_in=(W['ffn_w_in'], layer),
               dw=_pad_blocks(W['ffn_dw'][layer], 1, N_DEV), w_out=_pad_blocks(W['ffn_w_out'][layer], 0, N_DEV // 2))
    return mixer, xat, ffn


NATIVE = 'ffn_w_in'


def _native_rows(a, dtype):
    L, D, n = a.shape
    return jnp.pad(a.astype(dtype), ((0, 0), (0, 0), (0, _lane_pad(n) - n))).reshape(L * D, _lane_pad(n))


def _gather_weights(local):
    full = {n: local[n] for n in W_NAMES if W_SPEC[n][0] is None}
    full[NATIVE] = all_gather(_native_rows(local[NATIVE], bf16), name="ag_ffn_in")
    for as_bf16, dtype, tag in ((True, bf16, "ag_mat"), (False, f32, "ag_vec")):
        names = [n for n in W_NAMES if W_SPEC[n][0] is not None and W_SPEC[n][1] == as_bf16 and n != NATIVE]
        got = all_gather(_pack([local[n] for n in names], dtype), name=tag)
        parts = _unpack(got, [local[n].shape for n in names], lead=N_DEV)
        for n, part in zip(names, parts):
            full[n] = _join_shards(part, W_SPEC[n][0])
    return full


def _step(local, x, mem, tgt):
    W = _gather_weights(local)
    depth = W['norm_mix_g'].shape[0]
    g_mem = _row(W['mem_norm_g'])
    memn = rms_fwd(mem, g_mem, "mem_norm")
    layers = [_layer_params(W, l) for l in range(depth)]
    saved = []
    vf = None
    for l, (pm, px, pf) in enumerate(layers):
        if l % 2 == 0:
            x, v, sm = rwkv_fwd(x, pm, vf, f"rw{l}")
            if vf is None:
                vf = v
        else:
            x, sm = conv_fwd(x, pm, f"cv{l}")
        x, sx = xattn_fwd(x, memn, px, f"xa{l}")
        x, sf = ffn_fwd(x, pf, f"ff{l}")
        saved.append((sm, sx, sf))
    g_fin = _row(W['final_norm_g'])
    dx, dg_fin, loss_blk = final_loss(x, tgt, g_fin, name="final_loss")

    grads = {n: [None] * local[n].shape[0] for n in W_NAMES if local[n].ndim >= 2}
    grads['final_norm_g'] = dg_fin.reshape(-1)
    n_in = local[NATIVE].shape[2]
    dmemn = jnp.zeros_like(memn)
    dvf = None
    for l in reversed(range(depth)):
        pm, px, pf = layers[l]
        sm, sx, sf = saved[l]
        dx, gf = ffn_bwd(dx, pf, sf, f"ff{l}")
        dx, dmemn, gx = xattn_bwd(dx, dmemn, memn, px, sx, f"xa{l}")
        grads['norm_ffn_g'][l] = gf['norm_g'].reshape(-1)
        grads['ffn_w_in'][l] = gf['w_in']
        grads['ffn_dw'][l] = _unpad_blocks(gf['dw'], 1, N_DEV, n_in)
        grads['ffn_w_out'][l] = _unpad_blocks(gf['w_out'], 0, N_DEV // 2, n_in)
        grads['norm_xattn_g'][l] = gx['norm_g'].reshape(-1)
        grads['xattn_w_q'][l], grads['xattn_w_kv'][l], grads['xattn_w_o'][l] = gx['w_q'], gx['w_kv'], gx['w_o']
        i = l // 2
        if l % 2 == 0:
            dx, dvf_l, gm = rwkv_bwd(dx, dvf if i == 0 else None, pm, sm, f"rw{l}")
            if dvf_l is not None:
                dvf = dvf_l if dvf is None else rowwise(_add_fn, [dvf, dvf_l], [], name=f"rw{l}_dvfadd")[0]
            for short in ('mu', 'w_r', 'w_k', 'w_v', 'w_o', 'w1', 'w2', 'a1', 'a2', 'g1', 'g2'):
                grads['rwkv_' + short][i] = gm[short]
            for short in ('w0', 'a0', 'k_k', 'k_a', 'ln_g', 'ln_b'):
                grads['rwkv_' + short][i] = gm[short].reshape(-1)
            grads['rwkv_r_k'][i] = gm['r_k'].reshape(W['rwkv_r_k'].shape[1:])
            if i > 0:
                grads['rwkv_v0'][i - 1] = gm['v0'].reshape(-1)
                grads['rwkv_v1'][i - 1], grads['rwkv_v2'][i - 1] = gm['v1'], gm['v2']
        else:
            dx, gm = conv_bwd(dx, pm, sm, f"cv{l}")
            for short in ('w_in', 'dw', 'w_out'):
                grads['conv_' + short][i] = gm[short]
            for short in ('b_in', 'dw_b', 'ln_g', 'ln_b', 'b_out'):
                grads['conv_' + short][i] = gm[short].reshape(-1)
        grads['norm_mix_g'][l] = gm['norm_g'].reshape(-1)
    _, dg_mem = rowwise_bwd(lambda xv, gv: (_rms_fn(xv, gv),), [mem], [g_mem], [dmemn], name="mem_norm_b",
                            n_drow=1, n_dpar=1)
    grads['mem_norm_g'] = dg_mem.reshape(-1)
    native = jnp.concatenate(grads.pop(NATIVE), axis=1)
    full_grads = {n: (jnp.stack(gv) if isinstance(gv, list) else gv) for n, gv in grads.items()}
    return loss_blk[0, 0], dx, full_grads, native


def kernel(x, mem, mem_norm_g, norm_mix_g, norm_xattn_g, norm_ffn_g, final_norm_g, rwkv_mu, rwkv_w_r, rwkv_w_k, rwkv_w_v, rwkv_w_o, rwkv_w0, rwkv_w1, rwkv_w2, rwkv_a0, rwkv_a1, rwkv_a2, rwkv_g1, rwkv_g2, rwkv_k_k, rwkv_k_a, rwkv_r_k, rwkv_ln_g, rwkv_ln_b, rwkv_v0, rwkv_v1, rwkv_v2, conv_w_in, conv_b_in, conv_dw, conv_dw_b, conv_ln_g, conv_ln_b, conv_w_out, conv_b_out, xattn_w_q, xattn_w_kv, xattn_w_o, ffn_w_in, ffn_dw, ffn_w_out, loss_target, m_mem_norm_g, m_norm_mix_g, m_norm_xattn_g, m_norm_ffn_g, m_final_norm_g, m_rwkv_mu, m_rwkv_w_r, m_rwkv_w_k, m_rwkv_w_v, m_rwkv_w_o, m_rwkv_w0, m_rwkv_w1, m_rwkv_w2, m_rwkv_a0, m_rwkv_a1, m_rwkv_a2, m_rwkv_g1, m_rwkv_g2, m_rwkv_k_k, m_rwkv_k_a, m_rwkv_r_k, m_rwkv_ln_g, m_rwkv_ln_b, m_rwkv_v0, m_rwkv_v1, m_rwkv_v2, m_conv_w_in, m_conv_b_in, m_conv_dw, m_conv_dw_b, m_conv_ln_g, m_conv_ln_b, m_conv_w_out, m_conv_b_out, m_xattn_w_q, m_xattn_w_kv, m_xattn_w_o, m_ffn_w_in, m_ffn_dw, m_ffn_w_out, v_mem_norm_g, v_norm_mix_g, v_norm_xattn_g, v_norm_ffn_g, v_final_norm_g, v_rwkv_mu, v_rwkv_w_r, v_rwkv_w_k, v_rwkv_w_v, v_rwkv_w_o, v_rwkv_w0, v_rwkv_w1, v_rwkv_w2, v_rwkv_a0, v_rwkv_a1, v_rwkv_a2, v_rwkv_g1, v_rwkv_g2, v_rwkv_k_k, v_rwkv_k_a, v_rwkv_r_k, v_rwkv_ln_g, v_rwkv_ln_b, v_rwkv_v0, v_rwkv_v1, v_rwkv_v2, v_conv_w_in, v_conv_b_in, v_conv_dw, v_conv_dw_b, v_conv_ln_g, v_conv_ln_b, v_conv_w_out, v_conv_b_out, v_xattn_w_q, v_xattn_w_kv, v_xattn_w_o, v_ffn_w_in, v_ffn_dw, v_ffn_w_out):
    given = dict(locals())
    local = {n: given[n] for n in W_NAMES}
    loss_local, dx, grads, native = _step(local, x[0], mem[0], loss_target[0])
    loss = lax.psum(loss_local, ("x", "y", "c"))

    sharded = [n for n in W_NAMES if W_SPEC[n][0] is not None and n != NATIVE]
    repl = [n for n in W_NAMES if W_SPEC[n][0] is None]
    out = {}
    kinds = ("grad_", "delta_", "new_m_", "new_v_")

    parts = reduce_scatter(native, name="rs_ffn_in")
    res = adamw(parts, *[_native_rows(given[pre + NATIVE], f32) for pre in ("", "m_", "v_")], name="adamw_ffn_in")
    shp = given[NATIVE].shape
    for kind, buf in zip(kinds, res):
        out[kind + NATIVE] = buf.reshape(shp[0], shp[1], -1)[:, :, :shp[2]]

    parts = reduce_scatter(_pack([_split_shards(grads[n], W_SPEC[n][0]) for n in sharded], f32, lead=N_DEV),
                           name="rs_packed")
    res = adamw(parts, *[_pack([given[pre + n] for n in sharded], f32) for pre in ("", "m_", "v_")],
                name="adamw_sharded")
    for kind, buf in zip(kinds, res):
        for n, arr in zip(sharded, _unpack(buf, [given[n].shape for n in sharded])):
            out[kind + n] = arr

    parts = all_gather(_pack([grads[n] for n in repl], f32), name="grad_gather_repl")
    res = adamw(parts, *[_pack([given[pre + n] for n in repl], f32) for pre in ("", "m_", "v_")],
                name="adamw_repl")
    for kind, buf in zip(("grad_", "delta_", "new_m_", "new_v_"), res):
        for n, arr in zip(repl, _unpack(buf, [given[n].shape for n in repl])):
            out[kind + n] = arr

    return (loss, dx[None], *[out[kind + n] for kind in ("grad_", "delta_", "new_m_", "new_v_") for n in W_NAMES])
```

```python
import functools
import math

import jax
import jax.numpy as jnp
from jax import lax
from jax.experimental import pallas as pl
from jax.experimental.pallas import tpu as pltpu

f32 = jnp.float32
bf16 = jnp.bfloat16

N_DEV = 8
HEAD = 64
XATTN_HEADS = 4
NORM_EPS = 1e-6
LN_EPS = 1e-5
GN_EPS = 64e-5
ADAM_LR, ADAM_B1, ADAM_B2, ADAM_EPS, ADAM_WD, ADAM_STEP = 0.001, 0.9, 0.999, 1e-08, 0.01, 10
LANE = 128
PACK_COLS = 1024
PACK_ROWS = 256
VMEM_LIMIT = 48 * 1024 * 1024
SCAN_CHUNK = 32

W_SPEC = {
    'mem_norm_g': (None, False), 'norm_mix_g': (None, False), 'norm_xattn_g': (None, False),
    'norm_ffn_g': (None, False), 'final_norm_g': (None, False),
    'rwkv_mu': (2, False), 'rwkv_w_r': (1, True), 'rwkv_w_k': (1, True), 'rwkv_w_v': (1, True),
    'rwkv_w_o': (1, True), 'rwkv_w0': (None, False), 'rwkv_w1': (1, True), 'rwkv_w2': (2, True),
    'rwkv_a0': (None, False), 'rwkv_a1': (1, True), 'rwkv_a2': (2, True), 'rwkv_g1': (1, True),
    'rwkv_g2': (2, True), 'rwkv_k_k': (None, False), 'rwkv_k_a': (None, False), 'rwkv_r_k': (None, False),
    'rwkv_ln_g': (None, False), 'rwkv_ln_b': (None, False), 'rwkv_v0': (None, False),
    'rwkv_v1': (1, True), 'rwkv_v2': (2, True),
    'conv_w_in': (2, True), 'conv_b_in': (1, False), 'conv_dw': (2, False), 'conv_dw_b': (1, False),
    'conv_ln_g': (1, False), 'conv_ln_b': (1, False), 'conv_w_out': (1, True), 'conv_b_out': (1, False),
    'xattn_w_q': (1, True), 'xattn_w_kv': (2, True), 'xattn_w_o': (1, True),
    'ffn_w_in': (2, True), 'ffn_dw': (2, False), 'ffn_w_out': (1, True),
}
W_NAMES = list(W_SPEC)


def _tile(n, prefs):
    for p in prefs:
        if n % p == 0:
            return p
    return n


def _cparams(sem):
    return pltpu.CompilerParams(dimension_semantics=sem, vmem_limit_bytes=VMEM_LIMIT)


def _sigmoid(x):
    return 1.0 / (1.0 + jnp.exp(-x))


def _softplus(x):
    return jnp.maximum(x, 0.0) + jnp.log(1.0 + jnp.exp(-jnp.abs(x)))


def mm(a, b, *, name, ta=False, tb=False, bias=None, res=None, act=None, b_dev=None, out_dev=False, out_dtype=f32):
    M, K = (a.shape[1], a.shape[0]) if ta else a.shape
    tm = _tile(M, (1024, 512, 256, 128))
    if b_dev is None:
        N = b.shape[0] if tb else b.shape[1]
        assert (b.shape[1] if tb else b.shape[0]) == K, (name, a.shape, b.shape)
        tn = _tile(N, (1024, 512, 256, 128))
        tk = _tile(K, (1024, 512, 256, 128))
    else:
        b_off, b_rows = b_dev
        width = b.shape[2]
        if tb:
            N, tk = b_rows, width
            tn = _tile(N, (1024, 512, 256, 128))
            assert K == N_DEV * width and b_off % tn == 0, (name, a.shape, b.shape)
        else:
            N, tn = N_DEV * width, width
            tk = _tile(K, (1024, 512, 256, 128))
            assert K == b_rows and b_off % tk == 0, (name, a.shape, b.shape)
    if out_dev:
        tn = N // N_DEV
    nk = K // tk
    dims = (((0 if ta else 1,), (1 if tb else 0,)), ((), ()))
    has_bias, has_res = bias is not None, res is not None

    def body(*refs):
        a_ref, b_ref = refs[0], refs[1]
        pos = 2
        bias_ref = res_ref = None
        if has_bias:
            bias_ref = refs[pos]; pos += 1
        if has_res:
            res_ref = refs[pos]; pos += 1
        o_ref, acc_ref = refs[pos], refs[pos + 1]
        kstep = pl.program_id(2)

        @pl.when(kstep == 0)
        def _():
            acc_ref[...] = jnp.zeros_like(acc_ref)

        acc_ref[...] += lax.dot_general(a_ref[...].astype(bf16), b_ref[...].astype(bf16), dims,
                                        preferred_element_type=f32)

        @pl.when(kstep == nk - 1)
        def _():
            out = acc_ref[...]
            if has_bias:
                out = out + bias_ref[...]
            if act == 'tanh':
                out = jnp.tanh(out)
            elif act == 'sigmoid':
                out = _sigmoid(out)
            if has_res:
                out = out + res_ref[...]
            o_ref[...] = out.astype(o_ref.dtype)

    a_spec = pl.BlockSpec((tk, tm), lambda i, j, k: (k, i)) if ta else pl.BlockSpec((tm, tk), lambda i, j, k: (i, k))
    if b_dev is None:
        b_spec = pl.BlockSpec((tn, tk), lambda i, j, k: (j, k)) if tb else pl.BlockSpec((tk, tn), lambda i, j, k: (k, j))
    elif tb:
        b_spec = pl.BlockSpec((None, tn, tk), lambda i, j, k: (k, b_off // tn + j, 0))
    else:
        b_spec = pl.BlockSpec((None, tk, tn), lambda i, j, k: (j, b_off // tk + k, 0))
    in_specs, args = [a_spec, b_spec], [a, b]
    if has_bias:
        in_specs.append(pl.BlockSpec((1, tn), lambda i, j, k: (0, j))); args.append(bias)
    if has_res:
        in_specs.append(pl.BlockSpec((tm, tn), lambda i, j, k: (i, j))); args.append(res)
    if out_dev:
        out_spec = pl.BlockSpec((None, tm, tn), lambda i, j, k: (j, i, 0))
        out_shape = jax.ShapeDtypeStruct((N_DEV, M, tn), out_dtype)
    else:
        out_spec = pl.BlockSpec((tm, tn), lambda i, j, k: (i, j))
        out_shape = jax.ShapeDtypeStruct((M, N), out_dtype)
    return pl.pallas_call(
        body, name=name, grid=(M // tm, N // tn, nk), in_specs=in_specs,
        out_specs=out_spec, out_shape=out_shape,
        scratch_shapes=[pltpu.VMEM((tm, tn), f32)],
        compiler_params=_cparams(("parallel", "parallel", "arbitrary")),
    )(*args)


def rowwise(fn, rows, pars, *, name, tt=256, out_dtype=f32):
    T = rows[0].shape[0]
    tt = min(tt, T)
    nr, npar = len(rows), len(pars)
    outs = jax.eval_shape(fn, *[jax.ShapeDtypeStruct((tt, r.shape[1]), r.dtype) for r in rows],
                          *[jax.ShapeDtypeStruct(p.shape, p.dtype) for p in pars])

    def body(*refs):
        res = fn(*[r[...] for r in refs[:nr + npar]])
        for o_ref, o in zip(refs[nr + npar:], res):
            o_ref[...] = o.astype(o_ref.dtype)

    return pl.pallas_call(
        body, name=name, grid=(T // tt,),
        in_specs=[pl.BlockSpec((tt, r.shape[1]), lambda i: (i, 0)) for r in rows]
        + [pl.BlockSpec(p.shape, lambda i: (0, 0)) for p in pars],
        out_specs=[pl.BlockSpec((tt, o.shape[1]), lambda i: (i, 0)) for o in outs],
        out_shape=[jax.ShapeDtypeStruct((T, o.shape[1]), out_dtype) for o in outs],
        compiler_params=_cparams(("parallel",)),
    )(*rows, *pars)


def rowwise_bwd(fn, rows, pars, cots, *, name, n_drow, n_dpar, add0=None, tt=128):
    T = rows[0].shape[0]
    tt = min(tt, T)
    nr, npar, nc = len(rows), len(pars), len(cots)
    has_add = add0 is not None

    def body(*refs):
        rv = [r[...] for r in refs[:nr]]
        pv = [r[...] for r in refs[nr:nr + npar]]
        cv = [r[...] for r in refs[nr + npar:nr + npar + nc]]
        pos = nr + npar + nc
        add_ref = None
        if has_add:
            add_ref = refs[pos]; pos += 1
        drow_refs = refs[pos:pos + n_drow]
        dpar_refs = refs[pos + n_drow:pos + n_drow + n_dpar]

        def f(*d):
            return fn(*d[:n_drow], *rv[n_drow:], *d[n_drow:], *pv[n_dpar:])

        _, vjp = jax.vjp(f, *rv[:n_drow], *pv[:n_dpar])
        g = vjp(tuple(cv))
        for k in range(n_drow):
            gk = g[k]
            if k == 0 and has_add:
                gk = gk + add_ref[...]
            drow_refs[k][...] = gk

        @pl.when(pl.program_id(0) == 0)
        def _():
            for k in range(n_dpar):
                dpar_refs[k][...] = jnp.zeros_like(dpar_refs[k])

        for k in range(n_dpar):
            dpar_refs[k][...] += g[n_drow + k]

    row_spec = lambda r: pl.BlockSpec((tt, r.shape[1]), lambda i: (i, 0))
    par_spec = lambda p: pl.BlockSpec(p.shape, lambda i: (0, 0))
    in_specs = [row_spec(r) for r in rows] + [par_spec(p) for p in pars] + [row_spec(c) for c in cots]
    args = [*rows, *pars, *cots]
    if has_add:
        in_specs.append(row_spec(add0)); args.append(add0)
    return pl.pallas_call(
        body, name=name, grid=(T // tt,), in_specs=in_specs,
        out_specs=[row_spec(r) for r in rows[:n_drow]] + [par_spec(p) for p in pars[:n_dpar]],
        out_shape=[jax.ShapeDtypeStruct(r.shape, f32) for r in rows[:n_drow]]
        + [jax.ShapeDtypeStruct(p.shape, f32) for p in pars[:n_dpar]],
        compiler_params=_cparams(("arbitrary",)),
    )(*args)


def colwise(fn, cols, out_rows, *, name, nblk, out_dtype=f32):
    def body(*refs):
        res = fn(*[r[...] for r in refs[:len(cols)]])
        for o_ref, o in zip(refs[len(cols):], res):
            o_ref[...] = o.astype(o_ref.dtype)

    def spec(rows, off):
        return pl.BlockSpec((rows, LANE), lambda j: (0, j + off))

    return pl.pallas_call(
        body, name=name, grid=(nblk,),
        in_specs=[spec(a.shape[0], off) for a, off in cols],
        out_specs=[spec(r, 0) for r in out_rows],
        out_shape=[jax.ShapeDtypeStruct((r, nblk * LANE), out_dtype) for r in out_rows],
        compiler_params=_cparams(("parallel",)),
    )(*[a for a, _ in cols])


def _shift_dn(x, s):
    if s == 0:
        return x
    rid = lax.broadcasted_iota(jnp.int32, x.shape, 0)
    return jnp.where(rid >= s, pltpu.roll(x, s, 0), 0.0)


def _shift_up(x, s):
    if s == 0:
        return x
    n = x.shape[0]
    rid = lax.broadcasted_iota(jnp.int32, x.shape, 0)
    return jnp.where(rid < n - s, pltpu.roll(x, n - s, 0), 0.0)


def _colsum(x):
    return jnp.sum(x, axis=0, keepdims=True)


def _stack_rows(rows, n):
    c = rows[0].shape[1]
    rid = lax.broadcasted_iota(jnp.int32, (n, c), 0)
    out = jnp.zeros((n, c), f32)
    for i, r in enumerate(rows):
        out = jnp.where(rid == i, jnp.broadcast_to(r, (n, c)), out)
    return out


def _dwconv(x, w, kw):
    acc = None
    for k in range(kw):
        term = w[k:k + 1, :] * _shift_dn(x, kw - 1 - k)
        acc = term if acc is None else acc + term
    return acc


def _dwconv_bwd(x, w, dy, kw, pad_rows):
    dx = None
    rows = []
    for k in range(kw):
        s = kw - 1 - k
        rows.append(_colsum(dy * _shift_dn(x, s)))
        term = w[k:k + 1, :] * _shift_up(dy, s)
        dx = term if dx is None else dx + term
    return dx, _stack_rows(rows, pad_rows)


def _mix_fn(h, mu):
    xx = _shift_dn(h, 1) - h
    return tuple(h + xx * mu[i:i + 1, :] for i in range(6))


def _mix_bwd_fn(h, mu, *ds):
    xx = _shift_dn(h, 1) - h
    s1 = ds[0]
    s2 = ds[0] * mu[0:1, :]
    rows = [_colsum(ds[0] * xx)]
    for i in range(1, 6):
        s1 = s1 + ds[i]
        s2 = s2 + ds[i] * mu[i:i + 1, :]
        rows.append(_colsum(ds[i] * xx))
    return s1 - s2 + _shift_up(s2, 1), _stack_rows(rows, 8)


def _glu_conv_fn(kw, u1, u2, w, b):
    return (_dwconv(u1 * _sigmoid(u2), w, kw) + b,)


def _glu_conv_bwd_fn(kw, pad_rows, u1, u2, w, dc):
    sig = _sigmoid(u2)
    g = u1 * sig
    dg, dw = _dwconv_bwd(g, w, dc, kw, pad_rows)
    return dg * sig, dg * g * (1.0 - sig), dw, _colsum(dc)


def _ffn_act_fn(kw, ug, uv, wg, wv):
    gc = _dwconv(ug, wg, kw)
    vc = _dwconv(uv, wv, kw)
    return (gc * _sigmoid(gc) * vc,)


def _ffn_act_bwd_fn(kw, pad_rows, ug, uv, wg, wv, dact):
    gc = _dwconv(ug, wg, kw)
    vc = _dwconv(uv, wv, kw)
    sg = _sigmoid(gc)
    dvc = dact * gc * sg
    dgc = dact * vc * (sg * (1.0 + gc * (1.0 - sg)))
    dug, dwg = _dwconv_bwd(ug, wg, dgc, kw, pad_rows)
    duv, dwv = _dwconv_bwd(uv, wv, dvc, kw, pad_rows)
    return dug, duv, dwg, dwv


def _rms_fn(x, g):
    return x * lax.rsqrt(jnp.mean(x * x, axis=-1, keepdims=True) + NORM_EPS) * g


def _hsum(x, e, et):
    s = jnp.dot(x, e, precision=lax.Precision.HIGHEST, preferred_element_type=f32)
    return jnp.dot(s, et, precision=lax.Precision.HIGHEST, preferred_element_type=f32)


def _mid_fn(vres, k, v, lw, aa, *rest):
    if vres:
        vv, vf, w0, a0, k_k, k_a, v0, e, et = rest
    else:
        w0, a0, k_k, k_a, e, et = rest
    logw = -_softplus(-(w0 + lw)) - 0.5
    decay = jnp.exp(-jnp.exp(logw))
    a = _sigmoid(a0 + aa)
    kk = k * k_k
    kk = kk / jnp.maximum(jnp.sqrt(_hsum(kk * kk, e, et)), 1e-12)
    k2 = k * (1.0 + (a - 1.0) * k_a)
    v2 = v + (vf - v) * _sigmoid(v0 + vv) if vres else v
    return decay, a, kk, k2, v2


def _post_fn(y, r, k2, v2, gg, ln_g, ln_b, rk, e, et):
    inv = 1.0 / HEAD
    yc = y - _hsum(y, e, et) * inv
    var = _hsum(yc * yc, e, et) * inv
    yn = yc * lax.rsqrt(var + GN_EPS) * ln_g + ln_b
    bonus = _hsum(r * k2 * rk, e, et) * v2
    return ((yn + bonus) * gg,)


def _ln_silu_fn(c, g, b):
    mu = jnp.mean(c, axis=-1, keepdims=True)
    var = jnp.mean(jnp.square(c - mu), axis=-1, keepdims=True)
    ln = (c - mu) * lax.rsqrt(var + LN_EPS) * g + b
    return (ln * _sigmoid(ln),)


def _bias_fn(x, b):
    return (x + b,)


def _dtanh_fn(d, th):
    return (d * (1.0 - th * th),)


def _dsig_fn(d, sg):
    return (d * sg * (1.0 - sg),)


def _add_fn(a, b):
    return (a + b,)


def _seg(blocks, bd, coarse=()):
    n = len(blocks) * HEAD
    x = jnp.concatenate(blocks, axis=0)
    h0 = x.astype(bf16)
    h1 = (x - h0.astype(f32)).astype(bf16)
    lhs = [h0, h1] + [c.astype(bf16) for c in coarse]
    out = jnp.dot(jnp.concatenate(lhs, axis=0), bd, preferred_element_type=f32)
    res = out[n:2 * n] + out[0:n]
    fine = [res[i * HEAD:(i + 1) * HEAD] for i in range(len(blocks))]
    return fine + [out[2 * n + i * HEAD:2 * n + (i + 1) * HEAD] for i in range(len(coarse))]


def _scan_consts():
    li = lax.broadcasted_iota(jnp.int32, (LANE, LANE), 0) // HEAD
    lj = lax.broadcasted_iota(jnp.int32, (LANE, LANE), 1) // HEAD
    bd = (li == lj).astype(bf16)
    si = lax.broadcasted_iota(jnp.int32, (HEAD, LANE), 0)
    sj = lax.broadcasted_iota(jnp.int32, (HEAD, LANE), 1) % HEAD
    dg = (si == sj).astype(f32)
    return bd, dg


def _scan_dims(T, D):
    return D // LANE, min(SCAN_CHUNK, T)


def _head_dots_fn(r, k, kk, a, e, et):
    return _hsum(kk * a * r, e, et), _hsum(k * r, e, et)


def scan_fwd(r, w, k, v, kk, a, br, kr, *, name):
    T, D = r.shape
    G, tc = _scan_dims(T, D)
    bd, dg = _scan_consts()

    def body(r_ref, w_ref, k_ref, v_ref, kk_ref, a_ref, br_ref, kr_ref, bd_ref, dg_ref, y_ref, st_ref, s_ref):
        @pl.when(pl.program_id(0) == 0)
        def _():
            s_ref[...] = jnp.zeros_like(s_ref)

        bdv, dgv = bd_ref[...], dg_ref[...]

        def step(t, carry):
            row = pl.ds(t, 1)
            rr, ww, kr_, vr, kkr, ar, brr, krr = (x[row, :] for x in (r_ref, w_ref, k_ref, v_ref, kk_ref, a_ref,
                                                                    br_ref, kr_ref))
            bb = kkr * ar
            wr = ww * rr
            sl = [slice(g * LANE, (g + 1) * LANE) for g in range(G)]
            ps = [s_ref[g] for g in range(G)]
            for g in range(G):
                st_ref[t, g] = ps[g]
            blocks = [ps[g] * (-kkr[:, sl[g]]) for g in range(G)]
            blocks += [ps[g] * wr[:, sl[g]] for g in range(G)]
            vds = [jnp.broadcast_to(vr[:, sl[g]], (HEAD, LANE)) * dgv for g in range(G)]
            res = _seg(blocks, bdv, vds)
            yrows = []
            for g in range(G):
                sab, ub, vb = res[g], res[G + g], res[2 * G + g]
                s_ref[g] = ps[g] * ww[:, sl[g]] + sab * bb[:, sl[g]] + vb * kr_[:, sl[g]]
                yb = ub + sab * brr[:, sl[g]] + vb * krr[:, sl[g]]
                yrows.append(_colsum(yb * dgv))
            y_ref[row, :] = jnp.concatenate(yrows, axis=1)
            return carry

        lax.fori_loop(0, tc, step, 0)

    vec = pl.BlockSpec((tc, D), lambda c: (c, 0))
    return pl.pallas_call(
        body, name=name, grid=(T // tc,),
        in_specs=[vec] * 8 + [pl.BlockSpec((LANE, LANE), lambda c: (0, 0)), pl.BlockSpec((HEAD, LANE), lambda c: (0, 0))],
        out_specs=[vec, pl.BlockSpec((tc, G, HEAD, LANE), lambda c: (c, 0, 0, 0))],
        out_shape=[jax.ShapeDtypeStruct((T, D), f32), jax.ShapeDtypeStruct((T, G, HEAD, LANE), f32)],
        scratch_shapes=[pltpu.VMEM((G, HEAD, LANE), f32)],
        compiler_params=_cparams(("arbitrary",)),
    )(r, w, k, v, kk, a, br, kr, bd, dg)


def scan_bwd(r, w, k, v, kk, a, br, kr, dy, states, dr0, dk0, dv0, *, name):
    T, D = r.shape
    G, tc = _scan_dims(T, D)
    nch = T // tc
    bd, dg = _scan_consts()

    def body(r_ref, w_ref, k_ref, v_ref, kk_ref, a_ref, br_ref, kr_ref, dy_ref, st_ref, dr0_ref, dk0_ref, dv0_ref,
             bd_ref, dg_ref, dr_ref, dw_ref, dk_ref, dv_ref, dkk_ref, da_ref, ds_ref):
        @pl.when(pl.program_id(0) == 0)
        def _():
            ds_ref[...] = jnp.zeros_like(ds_ref)

        bdv, dgv = bd_ref[...], dg_ref[...]

        def step(i, carry):
            t = tc - 1 - i
            row = pl.ds(t, 1)
            rr, ww, kr_, vr, kkr, ar, brr, krr, dyr = (x[row, :] for x in (r_ref, w_ref, k_ref, v_ref, kk_ref, a_ref,
                                                                         br_ref, kr_ref, dy_ref))
            bb = kkr * ar
            sl = [slice(g * LANE, (g + 1) * LANE) for g in range(G)]
            ps = [st_ref[t, g] for g in range(G)]
            dss = [ds_ref[g] for g in range(G)]
            blocks = [ps[g] * (-kkr[:, sl[g]]) for g in range(G)]
            blocks += [dss[g] * bb[:, sl[g]] for g in range(G)]
            blocks += [dss[g] * kr_[:, sl[g]] for g in range(G)]
            diag = [jnp.broadcast_to(vr[:, sl[g]], (HEAD, LANE)) * dgv for g in range(G)]
            diag += [jnp.broadcast_to(dyr[:, sl[g]], (HEAD, LANE)) * dgv for g in range(G)]
            res = _seg(blocks, bdv, diag)
            dr_rows, dw_rows, dk_rows, dv_rows, dkk_rows, da_rows = [], [], [], [], [], []
            for g in range(G):
                sab, vb, dyb = res[g], res[3 * G + g], res[4 * G + g]
                dsab = res[G + g] + dyb * brr[:, sl[g]]
                dvb = res[2 * G + g] + dyb * krr[:, sl[g]]
                st = ps[g] * ww[:, sl[g]] + sab * bb[:, sl[g]] + vb * kr_[:, sl[g]]
                dst = dss[g] + dyb * rr[:, sl[g]]
                dr_rows.append(_colsum(st * dyb))
                dw_rows.append(_colsum(dst * ps[g]))
                db_row = _colsum(dst * sab)
                dk_rows.append(_colsum(dst * vb))
                dv_rows.append(_colsum(dvb * dgv))
                ds_ref[g] = dst * ww[:, sl[g]] - dsab * kkr[:, sl[g]]
                dkk_rows.append(db_row * ar[:, sl[g]] - _colsum(ps[g] * dsab))
                da_rows.append(db_row * kkr[:, sl[g]])
            cat = lambda rows: jnp.concatenate(rows, axis=1)
            dr_ref[row, :] = cat(dr_rows) + dr0_ref[row, :]
            dw_ref[row, :] = cat(dw_rows)
            dk_ref[row, :] = cat(dk_rows) + dk0_ref[row, :]
            dv_ref[row, :] = cat(dv_rows) + dv0_ref[row, :]
            dkk_ref[row, :] = cat(dkk_rows)
            da_ref[row, :] = cat(da_rows)
            return carry

        lax.fori_loop(0, tc, step, 0)

    vec = pl.BlockSpec((tc, D), lambda c: (nch - 1 - c, 0))
    return pl.pallas_call(
        body, name=name, grid=(nch,),
        in_specs=[vec] * 9 + [pl.BlockSpec((tc, G, HEAD, LANE), lambda c: (nch - 1 - c, 0, 0, 0))] + [vec] * 3
        + [pl.BlockSpec((LANE, LANE), lambda c: (0, 0)), pl.BlockSpec((HEAD, LANE), lambda c: (0, 0))],
        out_specs=[vec] * 6,
        out_shape=[jax.ShapeDtypeStruct((T, D), f32)] * 6,
        scratch_shapes=[pltpu.VMEM((G, HEAD, LANE), f32)],
        compiler_params=_cparams(("arbitrary",)),
    )(r, w, k, v, kk, a, br, kr, dy, states, dr0, dk0, dv0, bd, dg)


def _attn_p(q, k, scale):
    s = lax.dot_general(q.astype(bf16), k.astype(bf16), (((1,), (1,)), ((), ())), preferred_element_type=f32) * scale
    s = s - jnp.max(s, axis=-1, keepdims=True)
    p = jnp.exp(s)
    return p / jnp.sum(p, axis=-1, keepdims=True)


def attn_fwd(q, kv, *, name):
    T, D = q.shape
    M = kv.shape[0]
    hd = D // XATTN_HEADS
    scale = hd ** -0.5
    tq = _tile(T, (512, 256, 128))

    def body(q_ref, k_ref, v_ref, o_ref):
        p = _attn_p(q_ref[...], k_ref[...], scale)
        o_ref[...] = jnp.dot(p.astype(bf16), v_ref[...].astype(bf16), preferred_element_type=f32).astype(o_ref.dtype)

    return pl.pallas_call(
        body, name=name, grid=(XATTN_HEADS, T // tq),
        in_specs=[pl.BlockSpec((tq, hd), lambda h, i: (i, h)), pl.BlockSpec((M, hd), lambda h, i: (0, h)),
                  pl.BlockSpec((M, hd), lambda h, i: (0, XATTN_HEADS + h))],
        out_specs=pl.BlockSpec((tq, hd), lambda h, i: (i, h)),
        out_shape=jax.ShapeDtypeStruct((T, D), bf16),
        compiler_params=_cparams(("parallel", "parallel")),
    )(q, kv, kv)


def attn_bwd(q, kv, do, *, name):
    T, D = q.shape
    M = kv.shape[0]
    hd = D // XATTN_HEADS
    scale = hd ** -0.5
    tq = _tile(T, (512, 256, 128))

    def body(q_ref, k_ref, v_ref, do_ref, dq_ref, dk_ref, dv_ref):
        qv, kvv, vv, dov = q_ref[...], k_ref[...], v_ref[...], do_ref[...]
        p = _attn_p(qv, kvv, scale)
        dob = dov.astype(bf16)
        dp = lax.dot_general(dob, vv.astype(bf16), (((1,), (1,)), ((), ())), preferred_element_type=f32)
        ds = p * (dp - jnp.sum(dp * p, axis=-1, keepdims=True)) * scale
        dsb = ds.astype(bf16)
        dq_ref[...] = jnp.dot(dsb, kvv.astype(bf16), preferred_element_type=f32)

        @pl.when(pl.program_id(1) == 0)
        def _():
            dk_ref[...] = jnp.zeros_like(dk_ref)
            dv_ref[...] = jnp.zeros_like(dv_ref)

        dk_ref[...] += lax.dot_general(dsb, qv.astype(bf16), (((0,), (0,)), ((), ())), preferred_element_type=f32)
        dv_ref[...] += lax.dot_general(p.astype(bf16), dob, (((0,), (0,)), ((), ())), preferred_element_type=f32)

    qspec = pl.BlockSpec((tq, hd), lambda h, i: (i, h))
    mspec = pl.BlockSpec((M, hd), lambda h, i: (0, h))
    return pl.pallas_call(
        body, name=name, grid=(XATTN_HEADS, T // tq),
        in_specs=[qspec, mspec, pl.BlockSpec((M, hd), lambda h, i: (0, XATTN_HEADS + h)), qspec],
        out_specs=[qspec, mspec, mspec],
        out_shape=[jax.ShapeDtypeStruct((T, D), f32), jax.ShapeDtypeStruct((M, D), f32),
                   jax.ShapeDtypeStruct((M, D), f32)],
        compiler_params=_cparams(("parallel", "arbitrary")),
    )(q, kv, kv, do)


def final_loss(x, tgt, g, *, name):
    T, D = x.shape
    tt = min(256, T)

    def body(x_ref, t_ref, g_ref, dx_ref, dg_ref, loss_ref):
        tv = t_ref[...]

        def f(xv, gv):
            e = _rms_fn(xv, gv) - tv
            return 0.5 * jnp.sum(jnp.mean(e * e, axis=-1))

        val, vjp = jax.vjp(f, x_ref[...], g_ref[...])
        dx, dgv = vjp(jnp.ones((), f32))
        dx_ref[...] = dx

        @pl.when(pl.program_id(0) == 0)
        def _():
            dg_ref[...] = jnp.zeros_like(dg_ref)
            loss_ref[...] = jnp.zeros_like(loss_ref)

        dg_ref[...] += dgv
        loss_ref[...] += jnp.full(loss_ref.shape, val, f32)

    row = pl.BlockSpec((tt, D), lambda i: (i, 0))
    return pl.pallas_call(
        body, name=name, grid=(T // tt,),
        in_specs=[row, row, pl.BlockSpec((1, D), lambda i: (0, 0))],
        out_specs=[row, pl.BlockSpec((1, D), lambda i: (0, 0)), pl.BlockSpec((8, LANE), lambda i: (0, 0))],
        out_shape=[jax.ShapeDtypeStruct((T, D), f32), jax.ShapeDtypeStruct((1, D), f32),
                   jax.ShapeDtypeStruct((8, LANE), f32)],
        compiler_params=_cparams(("arbitrary",)),
    )(x, tgt, g)


def _place():
    x, y, c = lax.axis_index("x"), lax.axis_index("y"), lax.axis_index("c")
    chips = [(1 - x, y), (x, 1 - y), (1 - x, 1 - y)]
    return x, y, c, chips


def _rcopy(src, dst, send_sems, recv_sems, k, dev):
    return pltpu.make_async_remote_copy(src_ref=src, dst_ref=dst, send_sem=send_sems.at[k], recv_sem=recv_sems.at[k],
                                        device_id=dev, device_id_type=pl.DeviceIdType.MESH)


def _comm_call(body, name, x, out_shape, n_sems):
    return pl.pallas_call(
        body, name=name, out_shape=out_shape,
        in_specs=[pl.BlockSpec(memory_space=pl.ANY)], out_specs=pl.BlockSpec(memory_space=pl.ANY),
        scratch_shapes=[pltpu.SemaphoreType.DMA((n_sems,)), pltpu.SemaphoreType.DMA((n_sems,)),
                        pltpu.SemaphoreType.DMA],
    )(x)


def all_gather(x, *, name):
    def body(x_ref, o_ref, send_sems, recv_sems, local_sem):
        x_, y_, c_, chips = _place()
        sibling = (x_, y_, 1 - c_)
        slot = lambda px, py, pc: o_ref.at[4 * px + 2 * py + pc]
        mine = pltpu.make_async_copy(x_ref, slot(x_, y_, c_), local_sem)
        mine.start()
        first = [_rcopy(x_ref, slot(x_, y_, c_), send_sems, recv_sems, 0, sibling)]
        first += [_rcopy(x_ref, slot(x_, y_, c_), send_sems, recv_sems, 1 + j, (*chip, c_))
                  for j, chip in enumerate(chips)]
        for cp in first:
            cp.start()
        passed = [_rcopy(slot(*chip, c_), slot(*chip, c_), send_sems, recv_sems, 4 + j, sibling)
                  for j, chip in enumerate(chips)]
        for j, chip in enumerate(chips):
            _rcopy(x_ref, slot(*chip, c_), send_sems, recv_sems, 1 + j, (*chip, c_)).wait_recv()
            passed[j].start()
        _rcopy(x_ref, slot(x_, y_, 1 - c_), send_sems, recv_sems, 0, sibling).wait_recv()
        for j, chip in enumerate(chips):
            _rcopy(x_ref, slot(*chip, 1 - c_), send_sems, recv_sems, 4 + j, sibling).wait_recv()
        for cp in first + passed:
            cp.wait_send()
        mine.wait()

    return _comm_call(body, name, x, jax.ShapeDtypeStruct((N_DEV,) + x.shape, x.dtype), 7)


def pair_exchange(x, *, name):
    def body(x_ref, o_ref, send_sems, recv_sems, local_sem):
        x_, y_, c_, _ = _place()
        cp = _rcopy(x_ref, o_ref, send_sems, recv_sems, 0, (x_, y_, 1 - c_))
        cp.start()
        cp.wait()

    return _comm_call(body, name, x, jax.ShapeDtypeStruct(x.shape, x.dtype), 1)


def chip_exchange(x, *, name):
    def body(x_ref, o_ref, send_sems, recv_sems, local_sem):
        x_, y_, c_, chips = _place()
        myq = 2 * x_ + y_
        mine = pltpu.make_async_copy(x_ref.at[myq], o_ref.at[myq], local_sem)
        mine.start()
        sends = [_rcopy(x_ref.at[2 * px + py], o_ref.at[myq], send_sems, recv_sems, j, (px, py, c_))
                 for j, (px, py) in enumerate(chips)]
        for cp in sends:
            cp.start()
        for j, (px, py) in enumerate(chips):
            _rcopy(x_ref.at[myq], o_ref.at[2 * px + py], send_sems, recv_sems, j, (px, py, c_)).wait_recv()
        for cp in sends:
            cp.wait_send()
        mine.wait()

    return _comm_call(body, name, x, jax.ShapeDtypeStruct(x.shape, x.dtype), 3)


def _add_cast(a, b, *, name):
    n, R, C = a.shape
    tr = min(PACK_ROWS, R)
    blk = pl.BlockSpec((n, tr, C), lambda i: (0, i, 0))

    def body(a_ref, b_ref, o_ref):
        o_ref[...] = (a_ref[...] + b_ref[...]).astype(bf16)

    return pl.pallas_call(body, name=name, grid=(R // tr,), in_specs=[blk, blk], out_specs=blk,
                          out_shape=jax.ShapeDtypeStruct(a.shape, bf16), compiler_params=_cparams(("parallel",)))(a, b)


def reduce_scatter(pieces, *, name):
    n, R, C = pieces.shape
    c = lax.axis_index("c")
    by_core = pieces.reshape(n // 2, 2, R, C)
    keep = lax.dynamic_index_in_dim(by_core, c, axis=1, keepdims=False)
    give = lax.dynamic_index_in_dim(by_core, 1 - c, axis=1, keepdims=False)
    got = pair_exchange(give, name=name + "_pair")
    return chip_exchange(_add_cast(keep, got, name=name + "_add"), name=name + "_chip")


def adamw(gparts, w, m, v, *, name):
    R, C = w.shape
    n_parts = gparts.shape[0]
    tr = min(PACK_ROWS, R)
    c1 = 1.0 / (1.0 - ADAM_B1 ** ADAM_STEP)
    c2 = 1.0 / (1.0 - ADAM_B2 ** ADAM_STEP)

    def body(g_ref, w_ref, m_ref, v_ref, go_ref, d_ref, mo_ref, vo_ref):
        g = g_ref[0].astype(f32)
        for i in range(1, n_parts):
            g = g + g_ref[i].astype(f32)
        mn = ADAM_B1 * m_ref[...] + (1.0 - ADAM_B1) * g
        vn = ADAM_B2 * v_ref[...] + (1.0 - ADAM_B2) * (g * g)
        go_ref[...] = g
        mo_ref[...] = mn
        vo_ref[...] = vn
        d_ref[...] = -ADAM_LR * ((mn * c1) / (jnp.sqrt(vn * c2) + ADAM_EPS) + ADAM_WD * w_ref[...])

    blk = pl.BlockSpec((tr, C), lambda i: (i, 0))
    return pl.pallas_call(
        body, name=name, grid=(R // tr,),
        in_specs=[pl.BlockSpec((n_parts, tr, C), lambda i: (0, i, 0)), blk, blk, blk],
        out_specs=[blk] * 4, out_shape=[jax.ShapeDtypeStruct((R, C), f32)] * 4,
        compiler_params=_cparams(("parallel",)),
    )(gparts, w, m, v)


def _pack(arrs, dtype, lead=None):
    nl = 1 if lead is None else lead
    blocks = []
    for a in arrs:
        n = a.size // nl
        r = -(-n // PACK_COLS)
        a = a.astype(dtype)
        if n != r * PACK_COLS:
            a = jnp.pad(a.reshape(nl, n), ((0, 0), (0, r * PACK_COLS - n)))
        blocks.append(a.reshape(nl, r, PACK_COLS))
    rows = sum(b.shape[1] for b in blocks)
    tot = -(-rows // PACK_ROWS) * PACK_ROWS
    if tot != rows:
        blocks.append(jnp.zeros((nl, tot - rows, PACK_COLS), dtype))
    buf = jnp.concatenate(blocks, axis=1)
    return buf[0] if lead is None else buf


def _split_shards(full, ax):
    shp = full.shape
    t = full.reshape(shp[:ax] + (N_DEV, shp[ax] // N_DEV) + shp[ax + 1:])
    return jnp.moveaxis(t, ax, 0)


def _join_shards(parts, ax):
    t = jnp.moveaxis(parts, 0, ax)
    shp = t.shape
    return t.reshape(shp[:ax] + (shp[ax] * shp[ax + 1],) + shp[ax + 2:])


def _unpack(buf, shapes, lead=None):
    out, off = [], 0
    nl = 1 if lead is None else lead
    buf = buf.reshape(nl, -1, PACK_COLS)
    for s in shapes:
        n = math.prod(s)
        r = -(-n // PACK_COLS)
        blk = buf[:, off:off + r]
        if n != r * PACK_COLS:
            blk = blk.reshape(nl, r * PACK_COLS)[:, :n]
        out.append(blk.reshape(tuple(s) if lead is None else (lead,) + tuple(s)))
        off += r
    return out


def _row(v):
    return v.reshape(1, -1)


def _head_mats(D):
    e = (lax.broadcasted_iota(jnp.int32, (D, D // HEAD), 0) // HEAD
         == lax.broadcasted_iota(jnp.int32, (D, D // HEAD), 1)).astype(f32)
    return e, e.T


def rms_fwd(x, g, name, out_dtype=f32):
    return rowwise(lambda xv, gv: (_rms_fn(xv, gv),), [x], [g], name=name, out_dtype=out_dtype)[0]


def rms_bwd(x, g, dh, add, name):
    return rowwise_bwd(lambda xv, gv: (_rms_fn(xv, gv),), [x], [g], [dh], name=name, n_drow=1, n_dpar=1, add0=add)


def rwkv_fwd(x, p, vf, tag):
    vres = vf is not None
    D = x.shape[1]
    e, et = _head_mats(D)
    h = rms_fwd(x, p['norm_g'], tag + "_norm")
    xr, xw, xk, xv, xa, xg = colwise(_mix_fn, [(h, 0), (p['mu'], 0)], [h.shape[0]] * 6, name=tag + "_mix",
                                     nblk=D // LANE, out_dtype=bf16)
    r = mm(xr, p['w_r'], name=tag + "_r")
    k = mm(xk, p['w_k'], name=tag + "_k")
    v = mm(xv, p['w_v'], name=tag + "_v")
    th = mm(xw, p['w1'], name=tag + "_w1", act='tanh')
    lw = mm(th, p['w2'], name=tag + "_w2")
    t2 = mm(xa, p['a1'], name=tag + "_a1", out_dtype=bf16)
    aa = mm(t2, p['a2'], name=tag + "_a2")
    sg = mm(xg, p['g1'], name=tag + "_g1", act='sigmoid')
    gg = mm(sg, p['g2'], name=tag + "_g2")
    rows = [k, v, lw, aa]
    pars = [p['w0'], p['a0'], p['k_k'], p['k_a']]
    t4 = None
    if vres:
        t4 = mm(xv, p['v1'], name=tag + "_v1", out_dtype=bf16)
        vv = mm(t4, p['v2'], name=tag + "_v2")
        rows += [vv, vf]
        pars += [p['v0']]
    pars += [e, et]
    mid = functools.partial(_mid_fn, vres)
    decay, a, kk, k2, v2 = rowwise(mid, rows, pars, name=tag + "_mid")
    br, kr = rowwise(_head_dots_fn, [r, k2, kk, a], [e, et], name=tag + "_hdots")
    y, states = scan_fwd(r, decay, k2, v2, kk, a, br, kr, name=tag + "_scan")
    post_rows = [y, r, k2, v2, gg]
    post_pars = [p['ln_g'], p['ln_b'], p['r_k'], e, et]
    z = rowwise(_post_fn, post_rows, post_pars, name=tag + "_post", out_dtype=bf16)[0]
    xo = mm(z, p['w_o'], name=tag + "_o", res=x)
    saved = dict(x=x, h=h, xs=(xr, xw, xk, xv, xa, xg), r=r, th=th, t2=t2, sg=sg, t4=t4, mid_rows=rows, mid_pars=pars,
                 mid=mid, scan_in=(r, decay, k2, v2, kk, a, br, kr), states=states, post_rows=post_rows, post_pars=post_pars,
                 z=z, vres=vres)
    return xo, v2, saved


def rwkv_bwd(dxo, dvf_in, p, s, tag):
    D = dxo.shape[1]
    g = {}
    xr, xw, xk, xv, xa, xg = s['xs']
    dz = mm(dxo, p['w_o'], name=tag + "_bo", tb=True)
    g['w_o'] = mm(s['z'], dxo, name=tag + "_bwo", ta=True)
    dy, dr1, dk1, dv1, dgg, g['ln_g'], g['ln_b'], g['r_k'] = rowwise_bwd(
        _post_fn, s['post_rows'], s['post_pars'], [dz], name=tag + "_bpost", n_drow=5, n_dpar=3)
    if dvf_in is not None:
        dv1 = rowwise(_add_fn, [dv1, dvf_in], [], name=tag + "_bvadd")[0]
    dsg = mm(dgg, p['g2'], name=tag + "_bg2", tb=True)
    g['g2'] = mm(s['sg'], dgg, name=tag + "_bwg2", ta=True)
    dt3 = rowwise(_dsig_fn, [dsg, s['sg']], [], name=tag + "_bdsig")[0]
    dxg = mm(dt3, p['g1'], name=tag + "_bg1", tb=True)
    g['g1'] = mm(xg, dt3, name=tag + "_bwg1", ta=True)
    dr, dw, dk2, dv2, dkk, da = scan_bwd(*s['scan_in'], dy, s['states'], dr1, dk1, dv1, name=tag + "_bscan")
    vres = s['vres']
    n_drow = 6 if vres else 4
    n_dpar = 5 if vres else 4
    outs = rowwise_bwd(s['mid'], s['mid_rows'], s['mid_pars'], [dw, da, dkk, dk2, dv2], name=tag + "_bmid",
                       n_drow=n_drow, n_dpar=n_dpar)
    dk, dv, dlw, daa = outs[:4]
    dvf = None
    if vres:
        dvv, dvf = outs[4:6]
        g['w0'], g['a0'], g['k_k'], g['k_a'], g['v0'] = outs[6:]
    else:
        g['w0'], g['a0'], g['k_k'], g['k_a'] = outs[4:]
    dth = mm(dlw, p['w2'], name=tag + "_bw2", tb=True)
    g['w2'] = mm(s['th'], dlw, name=tag + "_bww2", ta=True)
    dt1 = rowwise(_dtanh_fn, [dth, s['th']], [], name=tag + "_bdtanh")[0]
    dxw = mm(dt1, p['w1'], name=tag + "_bw1", tb=True)
    g['w1'] = mm(xw, dt1, name=tag + "_bww1", ta=True)
    dt2 = mm(daa, p['a2'], name=tag + "_ba2", tb=True)
    g['a2'] = mm(s['t2'], daa, name=tag + "_bwa2", ta=True)
    dxa = mm(dt2, p['a1'], name=tag + "_ba1", tb=True)
    g['a1'] = mm(xa, dt2, name=tag + "_bwa1", ta=True)
    dxv = mm(dv, p['w_v'], name=tag + "_bv", tb=True)
    g['w_v'] = mm(xv, dv, name=tag + "_bwv", ta=True)
    if vres:
        dt4 = mm(dvv, p['v2'], name=tag + "_bv2", tb=True)
        g['v2'] = mm(s['t4'], dvv, name=tag + "_bwv2", ta=True)
        dxv = mm(dt4, p['v1'], name=tag + "_bv1", tb=True, res=dxv)
        g['v1'] = mm(xv, dt4, name=tag + "_bwv1", ta=True)
    dxr = mm(dr, p['w_r'], name=tag + "_br", tb=True)
    g['w_r'] = mm(xr, dr, name=tag + "_bwr", ta=True)
    dxk = mm(dk, p['w_k'], name=tag + "_bk", tb=True)
    g['w_k'] = mm(xk, dk, name=tag + "_bwk", ta=True)
    T = dxo.shape[0]
    dh, dmu = colwise(_mix_bwd_fn, [(s['h'], 0), (p['mu'], 0), (dxr, 0), (dxw, 0), (dxk, 0), (dxv, 0), (dxa, 0),
                                    (dxg, 0)], [T, 8], name=tag + "_bmix", nblk=D // LANE)
    g['mu'] = dmu[:6]
    dx, g['norm_g'] = rms_bwd(s['x'], p['norm_g'], dh, dxo, tag + "_bnorm")
    return dx, dvf, g


def conv_fwd(x, p, tag):
    T, D = x.shape
    nb = D // LANE
    kw = p['dw'].shape[0]
    h = rms_fwd(x, p['norm_g'], tag + "_norm", bf16)
    u = mm(h, p['w_in'], name=tag + "_in", bias=p['b_in'])
    c = colwise(functools.partial(_glu_conv_fn, kw), [(u, 0), (u, nb), (p['dw'], 0), (p['dw_b'], 0)], [T],
                name=tag + "_dw", nblk=nb)[0]
    sl = rowwise(_ln_silu_fn, [c], [p['ln_g'], p['ln_b']], name=tag + "_ln", out_dtype=bf16)[0]
    xo = mm(sl, p['w_out'], name=tag + "_out", bias=p['b_out'], res=x)
    return xo, dict(x=x, h=h, u=u, c=c, sl=sl)


def conv_bwd(dxo, p, s, tag):
    T, D = dxo.shape
    nb = D // LANE
    kw = p['dw'].shape[0]
    kpad = -(-kw // 8) * 8
    g = {}
    dsl = mm(dxo, p['w_out'], name=tag + "_bout", tb=True)
    g['w_out'] = mm(s['sl'], dxo, name=tag + "_bwout", ta=True)
    g['b_out'] = rowwise_bwd(_bias_fn, [dxo], [p['b_out']], [dxo], name=tag + "_bbout", n_drow=0, n_dpar=1)[0]
    dc, g['ln_g'], g['ln_b'] = rowwise_bwd(_ln_silu_fn, [s['c']], [p['ln_g'], p['ln_b']], [dsl], name=tag + "_bln",
                                           n_drow=1, n_dpar=2)
    u = s['u']
    du1, du2, ddw, g['dw_b'] = colwise(functools.partial(_glu_conv_bwd_fn, kw, kpad),
                                       [(u, 0), (u, nb), (p['dw'], 0), (dc, 0)], [T, T, kpad, 1],
                                       name=tag + "_bdw", nblk=nb)
    g['dw'] = ddw[:kw]
    du = jnp.concatenate([du1, du2], axis=1)
    g['b_in'] = rowwise_bwd(_bias_fn, [du], [p['b_in']], [du], name=tag + "_bbin", n_drow=0, n_dpar=1)[0]
    dh = mm(du, p['w_in'], name=tag + "_bin", tb=True)
    g['w_in'] = mm(s['h'], du, name=tag + "_bwin", ta=True)
    dx, g['norm_g'] = rms_bwd(s['x'], p['norm_g'], dh, dxo, tag + "_bnorm")
    return dx, g


def xattn_fwd(x, memn, p, tag):
    hn = rms_fwd(x, p['norm_g'], tag + "_norm", bf16)
    q = mm(hn, p['w_q'], name=tag + "_q", out_dtype=bf16)
    kv = mm(memn, p['w_kv'], name=tag + "_kv", out_dtype=bf16)
    o = attn_fwd(q, kv, name=tag + "_attn")
    xo = mm(o, p['w_o'], name=tag + "_o", res=x)
    return xo, dict(x=x, hn=hn, q=q, kv=kv, o=o)


def xattn_bwd(dxo, dmemn, memn, p, s, tag):
    g = {}
    do = mm(dxo, p['w_o'], name=tag + "_bo", tb=True)
    g['w_o'] = mm(s['o'], dxo, name=tag + "_bwo", ta=True)
    dq, dk, dv = attn_bwd(s['q'], s['kv'], do, name=tag + "_battn")
    dkv = jnp.concatenate([dk, dv], axis=1)
    dmemn = mm(dkv, p['w_kv'], name=tag + "_bkv", tb=True, res=dmemn)
    g['w_kv'] = mm(memn, dkv, name=tag + "_bwkv", ta=True)
    dhn = mm(dq, p['w_q'], name=tag + "_bq", tb=True)
    g['w_q'] = mm(s['hn'], dq, name=tag + "_bwq", ta=True)
    dx, g['norm_g'] = rms_bwd(s['x'], p['norm_g'], dhn, dxo, tag + "_bnorm")
    return dx, dmemn, g


def ffn_fwd(x, p, tag):
    T, D = x.shape
    w_dev, layer = p['w_in']
    nb = (N_DEV // 2) * w_dev.shape[2] // LANE
    kw = p['dw'].shape[0]
    hn = rms_fwd(x, p['norm_g'], tag + "_norm", bf16)
    u = mm(hn, w_dev, name=tag + "_in", b_dev=(layer * D, D))
    act = colwise(functools.partial(_ffn_act_fn, kw), [(u, 0), (u, nb), (p['dw'], 0), (p['dw'], nb)], [T],
                  name=tag + "_act", nblk=nb, out_dtype=bf16)[0]
    xo = mm(act, p['w_out'], name=tag + "_out", res=x)
    return xo, dict(x=x, hn=hn, u=u, act=act)


def ffn_bwd(dxo, p, s, tag):
    T, D = dxo.shape
    w_dev, layer = p['w_in']
    nb = (N_DEV // 2) * w_dev.shape[2] // LANE
    kw = p['dw'].shape[0]
    g = {}
    dact = mm(dxo, p['w_out'], name=tag + "_bout", tb=True)
    g['w_out'] = mm(s['act'], dxo, name=tag + "_bwout", ta=True)
    u = s['u']
    dug, duv, dwg, dwv = colwise(functools.partial(_ffn_act_bwd_fn, kw, 8),
                                 [(u, 0), (u, nb), (p['dw'], 0), (p['dw'], nb), (dact, 0)], [T, T, 8, 8],
                                 name=tag + "_bact", nblk=nb)
    g['dw'] = jnp.concatenate([dwg[:kw], dwv[:kw]], axis=1)
    du = jnp.concatenate([dug, duv], axis=1)
    dhn = mm(du, w_dev, name=tag + "_bin", tb=True, b_dev=(layer * D, D))
    g['w_in'] = mm(s['hn'], du, name=tag + "_bwin", ta=True, out_dev=True)
    dx, g['norm_g'] = rms_bwd(s['x'], p['norm_g'], dhn, dxo, tag + "_bnorm")
    return dx, g


def _lane_pad(n):
    return -(-n // LANE) * LANE


def _pad_blocks(a, axis, nblk):
    shp = a.shape
    n = shp[axis] // nblk
    t = a.reshape(shp[:axis] + (nblk, n) + shp[axis + 1:])
    pad = [(0, 0)] * t.ndim
    pad[axis + 1] = (0, _lane_pad(n) - n)
    t = jnp.pad(t, pad)
    return t.reshape(shp[:axis] + (nblk * _lane_pad(n),) + shp[axis + 1:])


def _unpad_blocks(a, axis, nblk, n):
    shp = a.shape
    t = a.reshape(shp[:axis] + (nblk, shp[axis] // nblk) + shp[axis + 1:])
    t = lax.slice_in_dim(t, 0, n, axis=axis + 1)
    return t.reshape(shp[:axis] + (nblk * n,) + shp[axis + 1:])


def _layer_params(W, layer):
    ia = ib = layer // 2
    mixer = {}
    if layer % 2 == 0:
        mixer = dict(norm_g=_row(W['norm_mix_g'][layer]), mu=W['rwkv_mu'][ia], w_r=W['rwkv_w_r'][ia],
                     w_k=W['rwkv_w_k'][ia], w_v=W['rwkv_w_v'][ia], w_o=W['rwkv_w_o'][ia], w0=_row(W['rwkv_w0'][ia]),
                     w1=W['rwkv_w1'][ia], w2=W['rwkv_w2'][ia], a0=_row(W['rwkv_a0'][ia]), a1=W['rwkv_a1'][ia],
                     a2=W['rwkv_a2'][ia], g1=W['rwkv_g1'][ia], g2=W['rwkv_g2'][ia], k_k=_row(W['rwkv_k_k'][ia]),
                     k_a=_row(W['rwkv_k_a'][ia]), r_k=_row(W['rwkv_r_k'][ia]), ln_g=_row(W['rwkv_ln_g'][ia]),
                     ln_b=_row(W['rwkv_ln_b'][ia]))
        if ia > 0:
            mixer.update(v0=_row(W['rwkv_v0'][ia - 1]), v1=W['rwkv_v1'][ia - 1], v2=W['rwkv_v2'][ia - 1])
    else:
        mixer = dict(norm_g=_row(W['norm_mix_g'][layer]), w_in=W['conv_w_in'][ib], b_in=_row(W['conv_b_in'][ib]),
                     dw=W['conv_dw'][ib], dw_b=_row(W['conv_dw_b'][ib]), ln_g=_row(W['conv_ln_g'][ib]),
                     ln_b=_row(W['conv_ln_b'][ib]), w_out=W['conv_w_out'][ib], b_out=_row(W['conv_b_out'][ib]))
    xat = dict(norm_g=_row(W['norm_xattn_g'][layer]), w_q=W['xattn_w_q'][layer], w_kv=W['xattn_w_kv'][layer],
               w_o=W['xattn_w_o'][layer])
    ffn = dict(norm_g=_row(W['norm_ffn_g'][layer]), w_in=(W['ffn_w_in'], layer),
               dw=_pad_blocks(W['ffn_dw'][layer], 1, N_DEV), w_out=_pad_blocks(W['ffn_w_out'][layer], 0, N_DEV // 2))
    return mixer, xat, ffn


NATIVE = 'ffn_w_in'


def _native_rows(a, dtype):
    L, D, n = a.shape
    return jnp.pad(a.astype(dtype), ((0, 0), (0, 0), (0, _lane_pad(n) - n))).reshape(L * D, _lane_pad(n))


def _gather_weights(local):
    full = {n: local[n] for n in W_NAMES if W_SPEC[n][0] is None}
    full[NATIVE] = all_gather(_native_rows(local[NATIVE], bf16), name="ag_ffn_in")
    for as_bf16, dtype, tag in ((True, bf16, "ag_mat"), (False, f32, "ag_vec")):
        names = [n for n in W_NAMES if W_SPEC[n][0] is not None and W_SPEC[n][1] == as_bf16 and n != NATIVE]
        got = all_gather(_pack([local[n] for n in names], dtype), name=tag)
        parts = _unpack(got, [local[n].shape for n in names], lead=N_DEV)
        for n, part in zip(names, parts):
            full[n] = _join_shards(part, W_SPEC[n][0])
    return full


def _step(local, x, mem, tgt):
    W = _gather_weights(local)
    depth = W['norm_mix_g'].shape[0]
    g_mem = _row(W['mem_norm_g'])
    memn = rms_fwd(mem, g_mem, "mem_norm", bf16)
    layers = [_layer_params(W, l) for l in range(depth)]
    saved = []
    vf = None
    for l, (pm, px, pf) in enumerate(layers):
        if l % 2 == 0:
            x, v, sm = rwkv_fwd(x, pm, vf, f"rw{l}")
            if vf is None:
                vf = v
        else:
            x, sm = conv_fwd(x, pm, f"cv{l}")
        x, sx = xattn_fwd(x, memn, px, f"xa{l}")
        x, sf = ffn_fwd(x, pf, f"ff{l}")
        saved.append((sm, sx, sf))
    g_fin = _row(W['final_norm_g'])
    dx, dg_fin, loss_blk = final_loss(x, tgt, g_fin, name="final_loss")

    grads = {n: [None] * local[n].shape[0] for n in W_NAMES if local[n].ndim >= 2}
    grads['final_norm_g'] = dg_fin.reshape(-1)
    n_in = local[NATIVE].shape[2]
    dmemn = jnp.zeros(memn.shape, f32)
    dvf = None
    for l in reversed(range(depth)):
        pm, px, pf = layers[l]
        sm, sx, sf = saved[l]
        dx, gf = ffn_bwd(dx, pf, sf, f"ff{l}")
        dx, dmemn, gx = xattn_bwd(dx, dmemn, memn, px, sx, f"xa{l}")
        grads['norm_ffn_g'][l] = gf['norm_g'].reshape(-1)
        grads['ffn_w_in'][l] = gf['w_in']
        grads['ffn_dw'][l] = _unpad_blocks(gf['dw'], 1, N_DEV, n_in)
        grads['ffn_w_out'][l] = _unpad_blocks(gf['w_out'], 0, N_DEV // 2, n_in)
        grads['norm_xattn_g'][l] = gx['norm_g'].reshape(-1)
        grads['xattn_w_q'][l], grads['xattn_w_kv'][l], grads['xattn_w_o'][l] = gx['w_q'], gx['w_kv'], gx['w_o']
        i = l // 2
        if l % 2 == 0:
            dx, dvf_l, gm = rwkv_bwd(dx, dvf if i == 0 else None, pm, sm, f"rw{l}")
            if dvf_l is not None:
                dvf = dvf_l if dvf is None else rowwise(_add_fn, [dvf, dvf_l], [], name=f"rw{l}_dvfadd")[0]
            for short in ('mu', 'w_r', 'w_k', 'w_v', 'w_o', 'w1', 'w2', 'a1', 'a2', 'g1', 'g2'):
                grads['rwkv_' + short][i] = gm[short]
            for short in ('w0', 'a0', 'k_k', 'k_a', 'ln_g', 'ln_b'):
                grads['rwkv_' + short][i] = gm[short].reshape(-1)
            grads['rwkv_r_k'][i] = gm['r_k'].reshape(W['rwkv_r_k'].shape[1:])
            if i > 0:
                grads['rwkv_v0'][i - 1] = gm['v0'].reshape(-1)
                grads['rwkv_v1'][i - 1], grads['rwkv_v2'][i - 1] = gm['v1'], gm['v2']
        else:
            dx, gm = conv_bwd(dx, pm, sm, f"cv{l}")
            for short in ('w_in', 'dw', 'w_out'):
                grads['conv_' + short][i] = gm[short]
            for short in ('b_in', 'dw_b', 'ln_g', 'ln_b', 'b_out'):
                grads['conv_' + short][i] = gm[short].reshape(-1)
        grads['norm_mix_g'][l] = gm['norm_g'].reshape(-1)
    _, dg_mem = rowwise_bwd(lambda xv, gv: (_rms_fn(xv, gv),), [mem], [g_mem], [dmemn], name="mem_norm_b",
                            n_drow=1, n_dpar=1)
    grads['mem_norm_g'] = dg_mem.reshape(-1)
    native = jnp.concatenate(grads.pop(NATIVE), axis=1)
    full_grads = {n: (jnp.stack(gv) if isinstance(gv, list) else gv) for n, gv in grads.items()}
    return loss_blk[0, 0], dx, full_grads, native


def kernel(x, mem, mem_norm_g, norm_mix_g, norm_xattn_g, norm_ffn_g, final_norm_g, rwkv_mu, rwkv_w_r, rwkv_w_k, rwkv_w_v, rwkv_w_o, rwkv_w0, rwkv_w1, rwkv_w2, rwkv_a0, rwkv_a1, rwkv_a2, rwkv_g1, rwkv_g2, rwkv_k_k, rwkv_k_a, rwkv_r_k, rwkv_ln_g, rwkv_ln_b, rwkv_v0, rwkv_v1, rwkv_v2, conv_w_in, conv_b_in, conv_dw, conv_dw_b, conv_ln_g, conv_ln_b, conv_w_out, conv_b_out, xattn_w_q, xattn_w_kv, xattn_w_o, ffn_w_in, ffn_dw, ffn_w_out, loss_target, m_mem_norm_g, m_norm_mix_g, m_norm_xattn_g, m_norm_ffn_g, m_final_norm_g, m_rwkv_mu, m_rwkv_w_r, m_rwkv_w_k, m_rwkv_w_v, m_rwkv_w_o, m_rwkv_w0, m_rwkv_w1, m_rwkv_w2, m_rwkv_a0, m_rwkv_a1, m_rwkv_a2, m_rwkv_g1, m_rwkv_g2, m_rwkv_k_k, m_rwkv_k_a, m_rwkv_r_k, m_rwkv_ln_g, m_rwkv_ln_b, m_rwkv_v0, m_rwkv_v1, m_rwkv_v2, m_conv_w_in, m_conv_b_in, m_conv_dw, m_conv_dw_b, m_conv_ln_g, m_conv_ln_b, m_conv_w_out, m_conv_b_out, m_xattn_w_q, m_xattn_w_kv, m_xattn_w_o, m_ffn_w_in, m_ffn_dw, m_ffn_w_out, v_mem_norm_g, v_norm_mix_g, v_norm_xattn_g, v_norm_ffn_g, v_final_norm_g, v_rwkv_mu, v_rwkv_w_r, v_rwkv_w_k, v_rwkv_w_v, v_rwkv_w_o, v_rwkv_w0, v_rwkv_w1, v_rwkv_w2, v_rwkv_a0, v_rwkv_a1, v_rwkv_a2, v_rwkv_g1, v_rwkv_g2, v_rwkv_k_k, v_rwkv_k_a, v_rwkv_r_k, v_rwkv_ln_g, v_rwkv_ln_b, v_rwkv_v0, v_rwkv_v1, v_rwkv_v2, v_conv_w_in, v_conv_b_in, v_conv_dw, v_conv_dw_b, v_conv_ln_g, v_conv_ln_b, v_conv_w_out, v_conv_b_out, v_xattn_w_q, v_xattn_w_kv, v_xattn_w_o, v_ffn_w_in, v_ffn_dw, v_ffn_w_out):
    given = dict(locals())
    local = {n: given[n] for n in W_NAMES}
    loss_local, dx, grads, native = _step(local, x[0], mem[0], loss_target[0])
    loss = lax.psum(loss_local, ("x", "y", "c"))

    sharded = [n for n in W_NAMES if W_SPEC[n][0] is not None and n != NATIVE]
    repl = [n for n in W_NAMES if W_SPEC[n][0] is None]
    out = {}
    kinds = ("grad_", "delta_", "new_m_", "new_v_")

    parts = reduce_scatter(native, name="rs_ffn_in")
    res = adamw(parts, *[_native_rows(given[pre + NATIVE], f32) for pre in ("", "m_", "v_")], name="adamw_ffn_in")
    shp = given[NATIVE].shape
    for kind, buf in zip(kinds, res):
        out[kind + NATIVE] = buf.reshape(shp[0], shp[1], -1)[:, :, :shp[2]]

    parts = reduce_scatter(_pack([_split_shards(grads[n], W_SPEC[n][0]) for n in sharded], f32, lead=N_DEV),
                           name="rs_packed")
    res = adamw(parts, *[_pack([given[pre + n] for n in sharded], f32) for pre in ("", "m_", "v_")],
                name="adamw_sharded")
    for kind, buf in zip(kinds, res):
        for n, arr in zip(sharded, _unpack(buf, [given[n].shape for n in sharded])):
            out[kind + n] = arr

    parts = all_gather(_pack([grads[n] for n in repl], f32), name="grad_gather_repl")
    res = adamw(parts, *[_pack([given[pre + n] for n in repl], f32) for pre in ("", "m_", "v_")],
                name="adamw_repl")
    for kind, buf in zip(("grad_", "delta_", "new_m_", "new_v_"), res):
        for n, arr in zip(repl, _unpack(buf, [given[n].shape for n in repl])):
            out[kind + n] = arr

    return (loss, dx[None], *[out[kind + n] for kind in ("grad_", "delta_", "new_m_", "new_v_") for n in W_NAMES])
```

```python
import functools
import math

import jax
import jax.numpy as jnp
from jax import lax
from jax.experimental import pallas as pl
from jax.experimental.pallas import tpu as pltpu

f32 = jnp.float32
bf16 = jnp.bfloat16

N_DEV = 8
HEAD = 64
XATTN_HEADS = 4
NORM_EPS = 1e-6
LN_EPS = 1e-5
GN_EPS = 64e-5
ADAM_LR, ADAM_B1, ADAM_B2, ADAM_EPS, ADAM_WD, ADAM_STEP = 0.001, 0.9, 0.999, 1e-08, 0.01, 10
LANE = 128
PACK_COLS = 1024
PACK_ROWS = 256
VMEM_LIMIT = 48 * 1024 * 1024
SCAN_CHUNK = 16

W_SPEC = {
    'mem_norm_g': (None, False), 'norm_mix_g': (None, False), 'norm_xattn_g': (None, False),
    'norm_ffn_g': (None, False), 'final_norm_g': (None, False),
    'rwkv_mu': (2, False), 'rwkv_w_r': (1, True), 'rwkv_w_k': (1, True), 'rwkv_w_v': (1, True),
    'rwkv_w_o': (1, True), 'rwkv_w0': (None, False), 'rwkv_w1': (1, True), 'rwkv_w2': (2, True),
    'rwkv_a0': (None, False), 'rwkv_a1': (1, True), 'rwkv_a2': (2, True), 'rwkv_g1': (1, True),
    'rwkv_g2': (2, True), 'rwkv_k_k': (None, False), 'rwkv_k_a': (None, False), 'rwkv_r_k': (None, False),
    'rwkv_ln_g': (None, False), 'rwkv_ln_b': (None, False), 'rwkv_v0': (None, False),
    'rwkv_v1': (1, True), 'rwkv_v2': (2, True),
    'conv_w_in': (2, True), 'conv_b_in': (1, False), 'conv_dw': (2, False), 'conv_dw_b': (1, False),
    'conv_ln_g': (1, False), 'conv_ln_b': (1, False), 'conv_w_out': (1, True), 'conv_b_out': (1, False),
    'xattn_w_q': (1, True), 'xattn_w_kv': (2, True), 'xattn_w_o': (1, True),
    'ffn_w_in': (2, True), 'ffn_dw': (2, False), 'ffn_w_out': (1, True),
}
W_NAMES = list(W_SPEC)


def _tile(n, prefs):
    for p in prefs:
        if n % p == 0:
            return p
    return n


def _cparams(sem):
    return pltpu.CompilerParams(dimension_semantics=sem, vmem_limit_bytes=VMEM_LIMIT)


def _sigmoid(x):
    return 1.0 / (1.0 + jnp.exp(-x))


def _softplus(x):
    return jnp.maximum(x, 0.0) + jnp.log(1.0 + jnp.exp(-jnp.abs(x)))


def mm(a, b, *, name, ta=False, tb=False, bias=None, res=None, act=None, b_dev=None, out_dev=False, out_dtype=f32):
    M, K = (a.shape[1], a.shape[0]) if ta else a.shape
    tm = _tile(M, (1024, 512, 256, 128))
    if b_dev is None:
        N = b.shape[0] if tb else b.shape[1]
        assert (b.shape[1] if tb else b.shape[0]) == K, (name, a.shape, b.shape)
        tn = _tile(N, (1024, 512, 256, 128))
        tk = _tile(K, (1024, 512, 256, 128))
    else:
        b_off, b_rows = b_dev
        width = b.shape[2]
        if tb:
            N, tk = b_rows, width
            tn = _tile(N, (1024, 512, 256, 128))
            assert K == N_DEV * width and b_off % tn == 0, (name, a.shape, b.shape)
        else:
            N, tn = N_DEV * width, width
            tk = _tile(K, (1024, 512, 256, 128))
            assert K == b_rows and b_off % tk == 0, (name, a.shape, b.shape)
    if out_dev:
        tn = N // N_DEV
    nk = K // tk
    dims = (((0 if ta else 1,), (1 if tb else 0,)), ((), ()))
    has_bias, has_res = bias is not None, res is not None

    def body(*refs):
        a_ref, b_ref = refs[0], refs[1]
        pos = 2
        bias_ref = res_ref = None
        if has_bias:
            bias_ref = refs[pos]; pos += 1
        if has_res:
            res_ref = refs[pos]; pos += 1
        o_ref, acc_ref = refs[pos], refs[pos + 1]
        kstep = pl.program_id(2)

        @pl.when(kstep == 0)
        def _():
            acc_ref[...] = jnp.zeros_like(acc_ref)

        acc_ref[...] += lax.dot_general(a_ref[...].astype(bf16), b_ref[...].astype(bf16), dims,
                                        preferred_element_type=f32)

        @pl.when(kstep == nk - 1)
        def _():
            out = acc_ref[...]
            if has_bias:
                out = out + bias_ref[...]
            if act == 'tanh':
                out = jnp.tanh(out)
            elif act == 'sigmoid':
                out = _sigmoid(out)
            if has_res:
                out = out + res_ref[...]
            o_ref[...] = out.astype(o_ref.dtype)

    a_spec = pl.BlockSpec((tk, tm), lambda i, j, k: (k, i)) if ta else pl.BlockSpec((tm, tk), lambda i, j, k: (i, k))
    if b_dev is None:
        b_spec = pl.BlockSpec((tn, tk), lambda i, j, k: (j, k)) if tb else pl.BlockSpec((tk, tn), lambda i, j, k: (k, j))
    elif tb:
        b_spec = pl.BlockSpec((None, tn, tk), lambda i, j, k: (k, b_off // tn + j, 0))
    else:
        b_spec = pl.BlockSpec((None, tk, tn), lambda i, j, k: (j, b_off // tk + k, 0))
    in_specs, args = [a_spec, b_spec], [a, b]
    if has_bias:
        in_specs.append(pl.BlockSpec((1, tn), lambda i, j, k: (0, j))); args.append(bias)
    if has_res:
        in_specs.append(pl.BlockSpec((tm, tn), lambda i, j, k: (i, j))); args.append(res)
    if out_dev:
        out_spec = pl.BlockSpec((None, tm, tn), lambda i, j, k: (j, i, 0))
        out_shape = jax.ShapeDtypeStruct((N_DEV, M, tn), out_dtype)
    else:
        out_spec = pl.BlockSpec((tm, tn), lambda i, j, k: (i, j))
        out_shape = jax.ShapeDtypeStruct((M, N), out_dtype)
    return pl.pallas_call(
        body, name=name, grid=(M // tm, N // tn, nk), in_specs=in_specs,
        out_specs=out_spec, out_shape=out_shape,
        scratch_shapes=[pltpu.VMEM((tm, tn), f32)],
        compiler_params=_cparams(("parallel", "parallel", "arbitrary")),
    )(*args)


def rowwise(fn, rows, pars, *, name, tt=256, out_dtype=f32):
    T = rows[0].shape[0]
    tt = min(tt, T)
    nr, npar = len(rows), len(pars)
    outs = jax.eval_shape(fn, *[jax.ShapeDtypeStruct((tt, r.shape[1]), r.dtype) for r in rows],
                          *[jax.ShapeDtypeStruct(p.shape, p.dtype) for p in pars])

    def body(*refs):
        res = fn(*[r[...] for r in refs[:nr + npar]])
        for o_ref, o in zip(refs[nr + npar:], res):
            o_ref[...] = o.astype(o_ref.dtype)

    return pl.pallas_call(
        body, name=name, grid=(T // tt,),
        in_specs=[pl.BlockSpec((tt, r.shape[1]), lambda i: (i, 0)) for r in rows]
        + [pl.BlockSpec(p.shape, lambda i: (0, 0)) for p in pars],
        out_specs=[pl.BlockSpec((tt, o.shape[1]), lambda i: (i, 0)) for o in outs],
        out_shape=[jax.ShapeDtypeStruct((T, o.shape[1]), out_dtype) for o in outs],
        compiler_params=_cparams(("parallel",)),
    )(*rows, *pars)


def rowwise_bwd(fn, rows, pars, cots, *, name, n_drow, n_dpar, add0=None, tt=128):
    T = rows[0].shape[0]
    tt = min(tt, T)
    nr, npar, nc = len(rows), len(pars), len(cots)
    has_add = add0 is not None

    def body(*refs):
        rv = [r[...] for r in refs[:nr]]
        pv = [r[...] for r in refs[nr:nr + npar]]
        cv = [r[...] for r in refs[nr + npar:nr + npar + nc]]
        pos = nr + npar + nc
        add_ref = None
        if has_add:
            add_ref = refs[pos]; pos += 1
        drow_refs = refs[pos:pos + n_drow]
        dpar_refs = refs[pos + n_drow:pos + n_drow + n_dpar]

        def f(*d):
            return fn(*d[:n_drow], *rv[n_drow:], *d[n_drow:], *pv[n_dpar:])

        _, vjp = jax.vjp(f, *rv[:n_drow], *pv[:n_dpar])
        g = vjp(tuple(cv))
        for k in range(n_drow):
            gk = g[k]
            if k == 0 and has_add:
                gk = gk + add_ref[...]
            drow_refs[k][...] = gk

        @pl.when(pl.program_id(0) == 0)
        def _():
            for k in range(n_dpar):
                dpar_refs[k][...] = jnp.zeros_like(dpar_refs[k])

        for k in range(n_dpar):
            dpar_refs[k][...] += g[n_drow + k]

    row_spec = lambda r: pl.BlockSpec((tt, r.shape[1]), lambda i: (i, 0))
    par_spec = lambda p: pl.BlockSpec(p.shape, lambda i: (0, 0))
    in_specs = [row_spec(r) for r in rows] + [par_spec(p) for p in pars] + [row_spec(c) for c in cots]
    args = [*rows, *pars, *cots]
    if has_add:
        in_specs.append(row_spec(add0)); args.append(add0)
    return pl.pallas_call(
        body, name=name, grid=(T // tt,), in_specs=in_specs,
        out_specs=[row_spec(r) for r in rows[:n_drow]] + [par_spec(p) for p in pars[:n_dpar]],
        out_shape=[jax.ShapeDtypeStruct(r.shape, f32) for r in rows[:n_drow]]
        + [jax.ShapeDtypeStruct(p.shape, f32) for p in pars[:n_dpar]],
        compiler_params=_cparams(("arbitrary",)),
    )(*args)


def colwise(fn, cols, out_rows, *, name, nblk, out_dtype=f32):
    def body(*refs):
        res = fn(*[r[...] for r in refs[:len(cols)]])
        for o_ref, o in zip(refs[len(cols):], res):
            o_ref[...] = o.astype(o_ref.dtype)

    def spec(rows, off):
        return pl.BlockSpec((rows, LANE), lambda j: (0, j + off))

    return pl.pallas_call(
        body, name=name, grid=(nblk,),
        in_specs=[spec(a.shape[0], off) for a, off in cols],
        out_specs=[spec(r, 0) for r in out_rows],
        out_shape=[jax.ShapeDtypeStruct((r, nblk * LANE), out_dtype) for r in out_rows],
        compiler_params=_cparams(("parallel",)),
    )(*[a for a, _ in cols])


def _shift_dn(x, s):
    if s == 0:
        return x
    rid = lax.broadcasted_iota(jnp.int32, x.shape, 0)
    return jnp.where(rid >= s, pltpu.roll(x, s, 0), 0.0)


def _shift_up(x, s):
    if s == 0:
        return x
    n = x.shape[0]
    rid = lax.broadcasted_iota(jnp.int32, x.shape, 0)
    return jnp.where(rid < n - s, pltpu.roll(x, n - s, 0), 0.0)


def _colsum(x):
    return jnp.sum(x, axis=0, keepdims=True)


def _stack_rows(rows, n):
    c = rows[0].shape[1]
    rid = lax.broadcasted_iota(jnp.int32, (n, c), 0)
    out = jnp.zeros((n, c), f32)
    for i, r in enumerate(rows):
        out = jnp.where(rid == i, jnp.broadcast_to(r, (n, c)), out)
    return out


def _dwconv(x, w, kw):
    acc = None
    for k in range(kw):
        term = w[k:k + 1, :] * _shift_dn(x, kw - 1 - k)
        acc = term if acc is None else acc + term
    return acc


def _dwconv_bwd(x, w, dy, kw, pad_rows):
    dx = None
    rows = []
    for k in range(kw):
        s = kw - 1 - k
        rows.append(_colsum(dy * _shift_dn(x, s)))
        term = w[k:k + 1, :] * _shift_up(dy, s)
        dx = term if dx is None else dx + term
    return dx, _stack_rows(rows, pad_rows)


def _mix_fn(h, mu):
    xx = _shift_dn(h, 1) - h
    return tuple(h + xx * mu[i:i + 1, :] for i in range(6))


def _mix_bwd_fn(h, mu, *ds):
    xx = _shift_dn(h, 1) - h
    s1 = ds[0]
    s2 = ds[0] * mu[0:1, :]
    rows = [_colsum(ds[0] * xx)]
    for i in range(1, 6):
        s1 = s1 + ds[i]
        s2 = s2 + ds[i] * mu[i:i + 1, :]
        rows.append(_colsum(ds[i] * xx))
    return s1 - s2 + _shift_up(s2, 1), _stack_rows(rows, 8)


def _glu_conv_fn(kw, u1, u2, w, b):
    return (_dwconv(u1 * _sigmoid(u2), w, kw) + b,)


def _glu_conv_bwd_fn(kw, pad_rows, u1, u2, w, dc):
    sig = _sigmoid(u2)
    g = u1 * sig
    dg, dw = _dwconv_bwd(g, w, dc, kw, pad_rows)
    return dg * sig, dg * g * (1.0 - sig), dw, _colsum(dc)


def _ffn_act_fn(kw, ug, uv, wg, wv):
    gc = _dwconv(ug, wg, kw)
    vc = _dwconv(uv, wv, kw)
    return (gc * _sigmoid(gc) * vc,)


def _ffn_act_bwd_fn(kw, pad_rows, ug, uv, wg, wv, dact):
    gc = _dwconv(ug, wg, kw)
    vc = _dwconv(uv, wv, kw)
    sg = _sigmoid(gc)
    dvc = dact * gc * sg
    dgc = dact * vc * (sg * (1.0 + gc * (1.0 - sg)))
    dug, dwg = _dwconv_bwd(ug, wg, dgc, kw, pad_rows)
    duv, dwv = _dwconv_bwd(uv, wv, dvc, kw, pad_rows)
    return dug, duv, dwg, dwv


def _rms_fn(x, g):
    return x * lax.rsqrt(jnp.mean(x * x, axis=-1, keepdims=True) + NORM_EPS) * g


def _hsum(x, e, et):
    s = jnp.dot(x, e, precision=lax.Precision.HIGHEST, preferred_element_type=f32)
    return jnp.dot(s, et, precision=lax.Precision.HIGHEST, preferred_element_type=f32)


def _mid_fn(vres, k, v, lw, aa, *rest):
    if vres:
        vv, vf, w0, a0, k_k, k_a, v0, e, et = rest
    else:
        w0, a0, k_k, k_a, e, et = rest
    logw = -_softplus(-(w0 + lw)) - 0.5
    decay = jnp.exp(-jnp.exp(logw))
    a = _sigmoid(a0 + aa)
    kk = k * k_k
    kk = kk / jnp.maximum(jnp.sqrt(_hsum(kk * kk, e, et)), 1e-12)
    k2 = k * (1.0 + (a - 1.0) * k_a)
    v2 = v + (vf - v) * _sigmoid(v0 + vv) if vres else v
    return decay, a, kk, k2, v2


def _post_fn(y, r, k2, v2, gg, ln_g, ln_b, rk, e, et):
    inv = 1.0 / HEAD
    yc = y - _hsum(y, e, et) * inv
    var = _hsum(yc * yc, e, et) * inv
    yn = yc * lax.rsqrt(var + GN_EPS) * ln_g + ln_b
    bonus = _hsum(r * k2 * rk, e, et) * v2
    return ((yn + bonus) * gg,)


def _ln_silu_fn(c, g, b):
    mu = jnp.mean(c, axis=-1, keepdims=True)
    var = jnp.mean(jnp.square(c - mu), axis=-1, keepdims=True)
    ln = (c - mu) * lax.rsqrt(var + LN_EPS) * g + b
    return (ln * _sigmoid(ln),)


def _bias_fn(x, b):
    return (x + b,)


def _dtanh_fn(d, th):
    return (d * (1.0 - th * th),)


def _dsig_fn(d, sg):
    return (d * sg * (1.0 - sg),)


def _add_fn(a, b):
    return (a + b,)


def _seg(blocks, bd, coarse=()):
    n = len(blocks) * HEAD
    x = jnp.concatenate(blocks, axis=0)
    h0 = x.astype(bf16)
    h1 = (x - h0.astype(f32)).astype(bf16)
    lhs = [h0, h1] + [c.astype(bf16) for c in coarse]
    out = jnp.dot(jnp.concatenate(lhs, axis=0), bd, preferred_element_type=f32)
    res = out[n:2 * n] + out[0:n]
    fine = [res[i * HEAD:(i + 1) * HEAD] for i in range(len(blocks))]
    return fine + [out[2 * n + i * HEAD:2 * n + (i + 1) * HEAD] for i in range(len(coarse))]


def _scan_consts():
    li = lax.broadcasted_iota(jnp.int32, (LANE, LANE), 0) // HEAD
    lj = lax.broadcasted_iota(jnp.int32, (LANE, LANE), 1) // HEAD
    bd = (li == lj).astype(bf16)
    si = lax.broadcasted_iota(jnp.int32, (HEAD, LANE), 0)
    sj = lax.broadcasted_iota(jnp.int32, (HEAD, LANE), 1) % HEAD
    dg = (si == sj).astype(f32)
    return bd, dg


def _scan_dims(T, D):
    return D // LANE, min(SCAN_CHUNK, T)


def _head_dots_fn(r, k, kk, a, e, et):
    return _hsum(kk * a * r, e, et), _hsum(k * r, e, et)


def scan_fwd(r, w, k, v, kk, a, br, kr, *, name, gathers=()):
    T, D = r.shape
    G, tc = _scan_dims(T, D)
    nch = T // tc
    ng = len(gathers)
    bd, dg = _scan_consts()

    def body(*refs):
        r_ref, w_ref, k_ref, v_ref, kk_ref, a_ref, br_ref, kr_ref, bd_ref, dg_ref = refs[:10]
        y_ref, st_ref, sa_ref = refs[10 + ng:13 + ng]
        s_ref = refs[13 + 2 * ng]
        jobs = [(refs[10 + i], refs[13 + ng + i], *refs[14 + 2 * ng + 3 * i:17 + 2 * ng + 3 * i]) for i in range(ng)]

        @pl.when(pl.program_id(0) == 0)
        def _():
            s_ref[...] = jnp.zeros_like(s_ref)
            for job in jobs:
                _ag_start(*job)

        bdv, dgv = bd_ref[...], dg_ref[...]

        def step(t, carry):
            row = pl.ds(t, 1)
            rr, ww, kr_, vr, kkr, ar, brr, krr = (x[row, :] for x in (r_ref, w_ref, k_ref, v_ref, kk_ref, a_ref,
                                                                    br_ref, kr_ref))
            bb = kkr * ar
            wr = ww * rr
            sl = [slice(g * LANE, (g + 1) * LANE) for g in range(G)]
            ps = [s_ref[g] for g in range(G)]
            blocks = [ps[g] * (-kkr[:, sl[g]]) for g in range(G)]
            blocks += [ps[g] * wr[:, sl[g]] for g in range(G)]
            vds = [jnp.broadcast_to(vr[:, sl[g]], (HEAD, LANE)) * dgv for g in range(G)]
            res = _seg(blocks, bdv, vds)
            yrows = []
            for g in range(G):
                sab, ub, vb = res[g], res[G + g], res[2 * G + g]
                sn = ps[g] * ww[:, sl[g]] + sab * bb[:, sl[g]] + vb * kr_[:, sl[g]]
                s_ref[g] = sn
                st_ref[t, g] = sn
                sa_ref[t, g] = sab
                yb = ub + sab * brr[:, sl[g]] + vb * krr[:, sl[g]]
                yrows.append(_colsum(yb * dgv))
            y_ref[row, :] = jnp.concatenate(yrows, axis=1)
            return carry

        lax.fori_loop(0, tc, step, 0)

        @pl.when(pl.program_id(0) == nch - 1)
        def _():
            for job in jobs:
                _ag_finish(*job)

    vec = pl.BlockSpec((tc, D), lambda c: (c, 0))
    big = pl.BlockSpec((tc, G, HEAD, LANE), lambda c: (c, 0, 0, 0))
    hbm = pl.BlockSpec(memory_space=pl.ANY)
    return pl.pallas_call(
        body, name=name, grid=(nch,),
        in_specs=[vec] * 8 + [pl.BlockSpec((LANE, LANE), lambda c: (0, 0)), pl.BlockSpec((HEAD, LANE), lambda c: (0, 0))]
        + [hbm] * ng,
        out_specs=[vec, big, big] + [hbm] * ng,
        out_shape=[jax.ShapeDtypeStruct((T, D), f32)] + [jax.ShapeDtypeStruct((T, G, HEAD, LANE), f32)] * 2
        + [jax.ShapeDtypeStruct((N_DEV,) + x.shape, x.dtype) for x in gathers],
        scratch_shapes=[pltpu.VMEM((G, HEAD, LANE), f32)] + _comm_scratch(AG_SEMS) * ng,
        compiler_params=_cparams(("arbitrary",)),
    )(r, w, k, v, kk, a, br, kr, bd, dg, *gathers)


def scan_bwd(r, w, k, v, kk, a, br, kr, dy, states, sas, dr0, dk0, dv0, *, name, exchanges=()):
    T, D = r.shape
    G, tc = _scan_dims(T, D)
    nch = T // tc
    ne = len(exchanges)
    bd, dg = _scan_consts()

    def body(*refs):
        (r_ref, w_ref, k_ref, v_ref, kk_ref, a_ref, br_ref, kr_ref, dy_ref, st_ref, prev_ref, sa_ref,
         dr0_ref, dk0_ref, dv0_ref, bd_ref, dg_ref) = refs[:17]
        dr_ref, dw_ref, dk_ref, dv_ref, dkk_ref, da_ref = refs[17 + ne:23 + ne]
        ds_ref = refs[23 + 2 * ne]
        jobs = [(refs[17 + i], refs[23 + ne + i], *refs[24 + 2 * ne + 3 * i:27 + 2 * ne + 3 * i]) for i in range(ne)]

        @pl.when(pl.program_id(0) == 0)
        def _():
            ds_ref[...] = jnp.zeros_like(ds_ref)
            for job in jobs:
                _cx_start(*job)

        bdv, dgv = bd_ref[...], dg_ref[...]

        def step_at(t, ps):
            row = pl.ds(t, 1)
            rr, ww, kr_, vr, kkr, ar, brr, krr, dyr = (x[row, :] for x in (r_ref, w_ref, k_ref, v_ref, kk_ref, a_ref,
                                                                         br_ref, kr_ref, dy_ref))
            bb = kkr * ar
            sl = [slice(g * LANE, (g + 1) * LANE) for g in range(G)]
            dss = [ds_ref[g] for g in range(G)]
            blocks = [dss[g] * bb[:, sl[g]] for g in range(G)]
            blocks += [dss[g] * kr_[:, sl[g]] for g in range(G)]
            diag = [jnp.broadcast_to(vr[:, sl[g]], (HEAD, LANE)) * dgv for g in range(G)]
            diag += [jnp.broadcast_to(dyr[:, sl[g]], (HEAD, LANE)) * dgv for g in range(G)]
            res = _seg(blocks, bdv, diag)
            dr_rows, dw_rows, dk_rows, dv_rows, dkk_rows, da_rows = [], [], [], [], [], []
            for g in range(G):
                sab, vb, dyb = sa_ref[t, g], res[2 * G + g], res[3 * G + g]
                dsab = res[g] + dyb * brr[:, sl[g]]
                dvb = res[G + g] + dyb * krr[:, sl[g]]
                dst = dss[g] + dyb * rr[:, sl[g]]
                dr_rows.append(_colsum(st_ref[t, g] * dyb))
                dw_rows.append(_colsum(dst * ps[g]))
                db_row = _colsum(dst * sab)
                dk_rows.append(_colsum(dst * vb))
                dv_rows.append(_colsum(dvb * dgv))
                ds_ref[g] = dst * ww[:, sl[g]] - dsab * kkr[:, sl[g]]
                dkk_rows.append(db_row * ar[:, sl[g]] - _colsum(ps[g] * dsab))
                da_rows.append(db_row * kkr[:, sl[g]])
            cat = lambda rows: jnp.concatenate(rows, axis=1)
            dr_ref[row, :] = cat(dr_rows) + dr0_ref[row, :]
            dw_ref[row, :] = cat(dw_rows)
            dk_ref[row, :] = cat(dk_rows) + dk0_ref[row, :]
            dv_ref[row, :] = cat(dv_rows) + dv0_ref[row, :]
            dkk_ref[row, :] = cat(dkk_rows)
            da_ref[row, :] = cat(da_rows)

        def step(i, carry):
            t = tc - 1 - i
            step_at(t, [st_ref[t - 1, g] for g in range(G)])
            return carry

        lax.fori_loop(0, tc - 1, step, 0)
        first = (pl.program_id(0) < nch - 1).astype(f32)
        step_at(0, [prev_ref[0, g] * first for g in range(G)])

        @pl.when(pl.program_id(0) == nch - 1)
        def _():
            for job in jobs:
                _cx_finish(*job)

    vec = pl.BlockSpec((tc, D), lambda c: (nch - 1 - c, 0))
    big = pl.BlockSpec((tc, G, HEAD, LANE), lambda c: (nch - 1 - c, 0, 0, 0))
    prev = pl.BlockSpec((1, G, HEAD, LANE), lambda c: (jnp.maximum((nch - 1 - c) * tc - 1, 0), 0, 0, 0))
    hbm = pl.BlockSpec(memory_space=pl.ANY)
    return pl.pallas_call(
        body, name=name, grid=(nch,),
        in_specs=[vec] * 9 + [big, prev, big] + [vec] * 3
        + [pl.BlockSpec((LANE, LANE), lambda c: (0, 0)), pl.BlockSpec((HEAD, LANE), lambda c: (0, 0))] + [hbm] * ne,
        out_specs=[vec] * 6 + [hbm] * ne,
        out_shape=[jax.ShapeDtypeStruct((T, D), f32)] * 6 + [jax.ShapeDtypeStruct(x.shape, x.dtype) for x in exchanges],
        scratch_shapes=[pltpu.VMEM((G, HEAD, LANE), f32)] + _comm_scratch(CX_SEMS) * ne,
        compiler_params=_cparams(("arbitrary",)),
    )(r, w, k, v, kk, a, br, kr, dy, states, states, sas, dr0, dk0, dv0, bd, dg, *exchanges)


def _attn_p(q, k, scale):
    s = lax.dot_general(q.astype(bf16), k.astype(bf16), (((1,), (1,)), ((), ())), preferred_element_type=f32) * scale
    s = s - jnp.max(s, axis=-1, keepdims=True)
    p = jnp.exp(s)
    return p / jnp.sum(p, axis=-1, keepdims=True)


def attn_fwd(q, kv, *, name):
    T, D = q.shape
    M = kv.shape[0]
    hd = D // XATTN_HEADS
    scale = hd ** -0.5
    tq = _tile(T, (512, 256, 128))

    def body(q_ref, k_ref, v_ref, o_ref):
        p = _attn_p(q_ref[...], k_ref[...], scale)
        o_ref[...] = jnp.dot(p.astype(bf16), v_ref[...].astype(bf16), preferred_element_type=f32).astype(o_ref.dtype)

    return pl.pallas_call(
        body, name=name, grid=(XATTN_HEADS, T // tq),
        in_specs=[pl.BlockSpec((tq, hd), lambda h, i: (i, h)), pl.BlockSpec((M, hd), lambda h, i: (0, h)),
                  pl.BlockSpec((M, hd), lambda h, i: (0, XATTN_HEADS + h))],
        out_specs=pl.BlockSpec((tq, hd), lambda h, i: (i, h)),
        out_shape=jax.ShapeDtypeStruct((T, D), bf16),
        compiler_params=_cparams(("parallel", "parallel")),
    )(q, kv, kv)


def attn_bwd(q, kv, do, *, name):
    T, D = q.shape
    M = kv.shape[0]
    hd = D // XATTN_HEADS
    scale = hd ** -0.5
    tq = _tile(T, (512, 256, 128))

    def body(q_ref, k_ref, v_ref, do_ref, dq_ref, dk_ref, dv_ref):
        qv, kvv, vv, dov = q_ref[...], k_ref[...], v_ref[...], do_ref[...]
        p = _attn_p(qv, kvv, scale)
        dob = dov.astype(bf16)
        dp = lax.dot_general(dob, vv.astype(bf16), (((1,), (1,)), ((), ())), preferred_element_type=f32)
        ds = p * (dp - jnp.sum(dp * p, axis=-1, keepdims=True)) * scale
        dsb = ds.astype(bf16)
        dq_ref[...] = jnp.dot(dsb, kvv.astype(bf16), preferred_element_type=f32)

        @pl.when(pl.program_id(1) == 0)
        def _():
            dk_ref[...] = jnp.zeros_like(dk_ref)
            dv_ref[...] = jnp.zeros_like(dv_ref)

        dk_ref[...] += lax.dot_general(dsb, qv.astype(bf16), (((0,), (0,)), ((), ())), preferred_element_type=f32)
        dv_ref[...] += lax.dot_general(p.astype(bf16), dob, (((0,), (0,)), ((), ())), preferred_element_type=f32)

    qspec = pl.BlockSpec((tq, hd), lambda h, i: (i, h))
    mspec = pl.BlockSpec((M, hd), lambda h, i: (0, h))
    return pl.pallas_call(
        body, name=name, grid=(XATTN_HEADS, T // tq),
        in_specs=[qspec, mspec, pl.BlockSpec((M, hd), lambda h, i: (0, XATTN_HEADS + h)), qspec],
        out_specs=[qspec, mspec, mspec],
        out_shape=[jax.ShapeDtypeStruct((T, D), f32), jax.ShapeDtypeStruct((M, D), f32),
                   jax.ShapeDtypeStruct((M, D), f32)],
        compiler_params=_cparams(("parallel", "arbitrary")),
    )(q, kv, kv, do)


def final_loss(x, tgt, g, *, name):
    T, D = x.shape
    tt = min(256, T)

    def body(x_ref, t_ref, g_ref, dx_ref, dg_ref, loss_ref):
        tv = t_ref[...]

        def f(xv, gv):
            e = _rms_fn(xv, gv) - tv
            return 0.5 * jnp.sum(jnp.mean(e * e, axis=-1))

        val, vjp = jax.vjp(f, x_ref[...], g_ref[...])
        dx, dgv = vjp(jnp.ones((), f32))
        dx_ref[...] = dx

        @pl.when(pl.program_id(0) == 0)
        def _():
            dg_ref[...] = jnp.zeros_like(dg_ref)
            loss_ref[...] = jnp.zeros_like(loss_ref)

        dg_ref[...] += dgv
        loss_ref[...] += jnp.full(loss_ref.shape, val, f32)

    row = pl.BlockSpec((tt, D), lambda i: (i, 0))
    return pl.pallas_call(
        body, name=name, grid=(T // tt,),
        in_specs=[row, row, pl.BlockSpec((1, D), lambda i: (0, 0))],
        out_specs=[row, pl.BlockSpec((1, D), lambda i: (0, 0)), pl.BlockSpec((8, LANE), lambda i: (0, 0))],
        out_shape=[jax.ShapeDtypeStruct((T, D), f32), jax.ShapeDtypeStruct((1, D), f32),
                   jax.ShapeDtypeStruct((8, LANE), f32)],
        compiler_params=_cparams(("arbitrary",)),
    )(x, tgt, g)


def _place():
    x, y, c = lax.axis_index("x"), lax.axis_index("y"), lax.axis_index("c")
    chips = [(1 - x, y), (x, 1 - y), (1 - x, 1 - y)]
    return x, y, c, chips


def _rcopy(src, dst, send_sems, recv_sems, k, dev):
    return pltpu.make_async_remote_copy(src_ref=src, dst_ref=dst, send_sem=send_sems.at[k], recv_sem=recv_sems.at[k],
                                        device_id=dev, device_id_type=pl.DeviceIdType.MESH)


def _comm_call(body, name, x, out_shape, n_sems):
    return pl.pallas_call(
        body, name=name, out_shape=out_shape,
        in_specs=[pl.BlockSpec(memory_space=pl.ANY)], out_specs=pl.BlockSpec(memory_space=pl.ANY),
        scratch_shapes=_comm_scratch(n_sems),
    )(x)


AG_SEMS = 7
CX_SEMS = 3


def _comm_scratch(n_sems):
    return [pltpu.SemaphoreType.DMA((n_sems,)), pltpu.SemaphoreType.DMA((n_sems,)), pltpu.SemaphoreType.DMA]


def _ag_first(x_ref, o_ref, send_sems, recv_sems, local_sem):
    x_, y_, c_, chips = _place()
    me = o_ref.at[4 * x_ + 2 * y_ + c_]
    copies = [pltpu.make_async_copy(x_ref, me, local_sem), _rcopy(x_ref, me, send_sems, recv_sems, 0, (x_, y_, 1 - c_))]
    copies += [_rcopy(x_ref, me, send_sems, recv_sems, 1 + j, (*chip, c_)) for j, chip in enumerate(chips)]
    return copies


def _ag_start(x_ref, o_ref, send_sems, recv_sems, local_sem):
    for cp in _ag_first(x_ref, o_ref, send_sems, recv_sems, local_sem):
        cp.start()


def _ag_finish(x_ref, o_ref, send_sems, recv_sems, local_sem):
    x_, y_, c_, chips = _place()
    sibling = (x_, y_, 1 - c_)
    slot = lambda px, py, pc: o_ref.at[4 * px + 2 * py + pc]
    passed = [_rcopy(slot(*chip, c_), slot(*chip, c_), send_sems, recv_sems, 4 + j, sibling)
              for j, chip in enumerate(chips)]
    for j, chip in enumerate(chips):
        _rcopy(x_ref, slot(*chip, c_), send_sems, recv_sems, 1 + j, (*chip, c_)).wait_recv()
        passed[j].start()
    _rcopy(x_ref, slot(x_, y_, 1 - c_), send_sems, recv_sems, 0, sibling).wait_recv()
    for j, chip in enumerate(chips):
        _rcopy(x_ref, slot(*chip, 1 - c_), send_sems, recv_sems, 4 + j, sibling).wait_recv()
    first = _ag_first(x_ref, o_ref, send_sems, recv_sems, local_sem)
    for cp in first[1:] + passed:
        cp.wait_send()
    first[0].wait()


def all_gather(x, *, name):
    def body(*refs):
        _ag_start(*refs)
        _ag_finish(*refs)

    return _comm_call(body, name, x, jax.ShapeDtypeStruct((N_DEV,) + x.shape, x.dtype), AG_SEMS)


def pair_exchange(x, *, name):
    def body(x_ref, o_ref, send_sems, recv_sems, local_sem):
        x_, y_, c_, _ = _place()
        cp = _rcopy(x_ref, o_ref, send_sems, recv_sems, 0, (x_, y_, 1 - c_))
        cp.start()
        cp.wait()

    return _comm_call(body, name, x, jax.ShapeDtypeStruct(x.shape, x.dtype), 1)


def _cx_copies(x_ref, o_ref, send_sems, recv_sems, local_sem):
    x_, y_, c_, chips = _place()
    myq = 2 * x_ + y_
    copies = [pltpu.make_async_copy(x_ref.at[myq], o_ref.at[myq], local_sem)]
    copies += [_rcopy(x_ref.at[2 * px + py], o_ref.at[myq], send_sems, recv_sems, j, (px, py, c_))
               for j, (px, py) in enumerate(chips)]
    return copies


def _cx_start(x_ref, o_ref, send_sems, recv_sems, local_sem):
    for cp in _cx_copies(x_ref, o_ref, send_sems, recv_sems, local_sem):
        cp.start()


def _cx_finish(x_ref, o_ref, send_sems, recv_sems, local_sem):
    x_, y_, c_, chips = _place()
    myq = 2 * x_ + y_
    for j, (px, py) in enumerate(chips):
        _rcopy(x_ref.at[myq], o_ref.at[2 * px + py], send_sems, recv_sems, j, (px, py, c_)).wait_recv()
    copies = _cx_copies(x_ref, o_ref, send_sems, recv_sems, local_sem)
    for cp in copies[1:]:
        cp.wait_send()
    copies[0].wait()


def chip_exchange(x, *, name):
    def body(*refs):
        _cx_start(*refs)
        _cx_finish(*refs)

    return _comm_call(body, name, x, jax.ShapeDtypeStruct(x.shape, x.dtype), CX_SEMS)


def _add_cast(a, b, *, name):
    n, R, C = a.shape
    tr = min(PACK_ROWS, R)
    blk = pl.BlockSpec((n, tr, C), lambda i: (0, i, 0))

    def body(a_ref, b_ref, o_ref):
        o_ref[...] = (a_ref[...] + b_ref[...]).astype(bf16)

    return pl.pallas_call(body, name=name, grid=(R // tr,), in_specs=[blk, blk], out_specs=blk,
                          out_shape=jax.ShapeDtypeStruct(a.shape, bf16), compiler_params=_cparams(("parallel",)))(a, b)


def chip_partials(pieces, *, name):
    n, R, C = pieces.shape
    c = lax.axis_index("c")
    by_core = pieces.reshape(n // 2, 2, R, C)
    keep = lax.dynamic_index_in_dim(by_core, c, axis=1, keepdims=False)
    give = lax.dynamic_index_in_dim(by_core, 1 - c, axis=1, keepdims=False)
    got = pair_exchange(give, name=name + "_pair")
    return _add_cast(keep, got, name=name + "_add")


def reduce_scatter(pieces, *, name):
    return chip_exchange(chip_partials(pieces, name=name), name=name + "_chip")


def adamw(gparts, w, m, v, *, name):
    R, C = w.shape
    n_parts = gparts.shape[0]
    tr = min(PACK_ROWS, R)
    c1 = 1.0 / (1.0 - ADAM_B1 ** ADAM_STEP)
    c2 = 1.0 / (1.0 - ADAM_B2 ** ADAM_STEP)

    def body(g_ref, w_ref, m_ref, v_ref, go_ref, d_ref, mo_ref, vo_ref):
        g = g_ref[0].astype(f32)
        for i in range(1, n_parts):
            g = g + g_ref[i].astype(f32)
        mn = ADAM_B1 * m_ref[...] + (1.0 - ADAM_B1) * g
        vn = ADAM_B2 * v_ref[...] + (1.0 - ADAM_B2) * (g * g)
        go_ref[...] = g
        mo_ref[...] = mn
        vo_ref[...] = vn
        d_ref[...] = -ADAM_LR * ((mn * c1) / (jnp.sqrt(vn * c2) + ADAM_EPS) + ADAM_WD * w_ref[...])

    blk = pl.BlockSpec((tr, C), lambda i: (i, 0))
    return pl.pallas_call(
        body, name=name, grid=(R // tr,),
        in_specs=[pl.BlockSpec((n_parts, tr, C), lambda i: (0, i, 0)), blk, blk, blk],
        out_specs=[blk] * 4, out_shape=[jax.ShapeDtypeStruct((R, C), f32)] * 4,
        compiler_params=_cparams(("parallel",)),
    )(gparts, w, m, v)


def _pack(arrs, dtype, lead=None):
    nl = 1 if lead is None else lead
    blocks = []
    for a in arrs:
        n = a.size // nl
        r = -(-n // PACK_COLS)
        a = a.astype(dtype)
        if n != r * PACK_COLS:
            a = jnp.pad(a.reshape(nl, n), ((0, 0), (0, r * PACK_COLS - n)))
        blocks.append(a.reshape(nl, r, PACK_COLS))
    rows = sum(b.shape[1] for b in blocks)
    tot = -(-rows // PACK_ROWS) * PACK_ROWS
    if tot != rows:
        blocks.append(jnp.zeros((nl, tot - rows, PACK_COLS), dtype))
    buf = jnp.concatenate(blocks, axis=1)
    return buf[0] if lead is None else buf


def _split_shards(full, ax):
    shp = full.shape
    t = full.reshape(shp[:ax] + (N_DEV, shp[ax] // N_DEV) + shp[ax + 1:])
    return jnp.moveaxis(t, ax, 0)


def _join_shards(parts, ax):
    t = jnp.moveaxis(parts, 0, ax)
    shp = t.shape
    return t.reshape(shp[:ax] + (shp[ax] * shp[ax + 1],) + shp[ax + 2:])


def _unpack(buf, shapes, lead=None):
    out, off = [], 0
    nl = 1 if lead is None else lead
    buf = buf.reshape(nl, -1, PACK_COLS)
    for s in shapes:
        n = math.prod(s)
        r = -(-n // PACK_COLS)
        blk = buf[:, off:off + r]
        if n != r * PACK_COLS:
            blk = blk.reshape(nl, r * PACK_COLS)[:, :n]
        out.append(blk.reshape(tuple(s) if lead is None else (lead,) + tuple(s)))
        off += r
    return out


def _row(v):
    return v.reshape(1, -1)


def _head_mats(D):
    e = (lax.broadcasted_iota(jnp.int32, (D, D // HEAD), 0) // HEAD
         == lax.broadcasted_iota(jnp.int32, (D, D // HEAD), 1)).astype(f32)
    return e, e.T


def rms_fwd(x, g, name, out_dtype=f32):
    return rowwise(lambda xv, gv: (_rms_fn(xv, gv),), [x], [g], name=name, out_dtype=out_dtype)[0]


def rms_bwd(x, g, dh, add, name):
    return rowwise_bwd(lambda xv, gv: (_rms_fn(xv, gv),), [x], [g], [dh], name=name, n_drow=1, n_dpar=1, add0=add)


def rwkv_fwd(x, p, vf, tag, gathers=()):
    vres = vf is not None
    D = x.shape[1]
    e, et = _head_mats(D)
    h = rms_fwd(x, p['norm_g'], tag + "_norm")
    xr, xw, xk, xv, xa, xg = colwise(_mix_fn, [(h, 0), (p['mu'], 0)], [h.shape[0]] * 6, name=tag + "_mix",
                                     nblk=D // LANE, out_dtype=bf16)
    r = mm(xr, p['w_r'], name=tag + "_r")
    k = mm(xk, p['w_k'], name=tag + "_k")
    v = mm(xv, p['w_v'], name=tag + "_v")
    th = mm(xw, p['w1'], name=tag + "_w1", act='tanh')
    lw = mm(th, p['w2'], name=tag + "_w2")
    t2 = mm(xa, p['a1'], name=tag + "_a1", out_dtype=bf16)
    aa = mm(t2, p['a2'], name=tag + "_a2")
    sg = mm(xg, p['g1'], name=tag + "_g1", act='sigmoid')
    gg = mm(sg, p['g2'], name=tag + "_g2")
    rows = [k, v, lw, aa]
    pars = [p['w0'], p['a0'], p['k_k'], p['k_a']]
    t4 = None
    if vres:
        t4 = mm(xv, p['v1'], name=tag + "_v1", out_dtype=bf16)
        vv = mm(t4, p['v2'], name=tag + "_v2")
        rows += [vv, vf]
        pars += [p['v0']]
    pars += [e, et]
    mid = functools.partial(_mid_fn, vres)
    decay, a, kk, k2, v2 = rowwise(mid, rows, pars, name=tag + "_mid")
    br, kr = rowwise(_head_dots_fn, [r, k2, kk, a], [e, et], name=tag + "_hdots")
    y, states, sas, *gathered = scan_fwd(r, decay, k2, v2, kk, a, br, kr, name=tag + "_scan", gathers=gathers)
    post_rows = [y, r, k2, v2, gg]
    post_pars = [p['ln_g'], p['ln_b'], p['r_k'], e, et]
    z = rowwise(_post_fn, post_rows, post_pars, name=tag + "_post", out_dtype=bf16)[0]
    xo = mm(z, p['w_o'], name=tag + "_o", res=x)
    saved = dict(x=x, h=h, xs=(xr, xw, xk, xv, xa, xg), r=r, th=th, t2=t2, sg=sg, t4=t4, mid_rows=rows, mid_pars=pars,
                 mid=mid, scan_in=(r, decay, k2, v2, kk, a, br, kr), states=(states, sas), post_rows=post_rows, post_pars=post_pars,
                 z=z, vres=vres)
    return xo, v2, saved, gathered


def rwkv_bwd(dxo, dvf_in, p, s, tag, exchanges=()):
    D = dxo.shape[1]
    g = {}
    xr, xw, xk, xv, xa, xg = s['xs']
    dz = mm(dxo, p['w_o'], name=tag + "_bo", tb=True)
    g['w_o'] = mm(s['z'], dxo, name=tag + "_bwo", ta=True)
    dy, dr1, dk1, dv1, dgg, g['ln_g'], g['ln_b'], g['r_k'] = rowwise_bwd(
        _post_fn, s['post_rows'], s['post_pars'], [dz], name=tag + "_bpost", n_drow=5, n_dpar=3)
    if dvf_in is not None:
        dv1 = rowwise(_add_fn, [dv1, dvf_in], [], name=tag + "_bvadd")[0]
    dsg = mm(dgg, p['g2'], name=tag + "_bg2", tb=True)
    g['g2'] = mm(s['sg'], dgg, name=tag + "_bwg2", ta=True)
    dt3 = rowwise(_dsig_fn, [dsg, s['sg']], [], name=tag + "_bdsig")[0]
    dxg = mm(dt3, p['g1'], name=tag + "_bg1", tb=True)
    g['g1'] = mm(xg, dt3, name=tag + "_bwg1", ta=True)
    dr, dw, dk2, dv2, dkk, da, *exchanged = scan_bwd(*s['scan_in'], dy, *s['states'], dr1, dk1, dv1,
                                                     name=tag + "_bscan", exchanges=exchanges)
    vres = s['vres']
    n_drow = 6 if vres else 4
    n_dpar = 5 if vres else 4
    outs = rowwise_bwd(s['mid'], s['mid_rows'], s['mid_pars'], [dw, da, dkk, dk2, dv2], name=tag + "_bmid",
                       n_drow=n_drow, n_dpar=n_dpar)
    dk, dv, dlw, daa = outs[:4]
    dvf = None
    if vres:
        dvv, dvf = outs[4:6]
        g['w0'], g['a0'], g['k_k'], g['k_a'], g['v0'] = outs[6:]
    else:
        g['w0'], g['a0'], g['k_k'], g['k_a'] = outs[4:]
    dth = mm(dlw, p['w2'], name=tag + "_bw2", tb=True)
    g['w2'] = mm(s['th'], dlw, name=tag + "_bww2", ta=True)
    dt1 = rowwise(_dtanh_fn, [dth, s['th']], [], name=tag + "_bdtanh")[0]
    dxw = mm(dt1, p['w1'], name=tag + "_bw1", tb=True)
    g['w1'] = mm(xw, dt1, name=tag + "_bww1", ta=True)
    dt2 = mm(daa, p['a2'], name=tag + "_ba2", tb=True)
    g['a2'] = mm(s['t2'], daa, name=tag + "_bwa2", ta=True)
    dxa = mm(dt2, p['a1'], name=tag + "_ba1", tb=True)
    g['a1'] = mm(xa, dt2, name=tag + "_bwa1", ta=True)
    dxv = mm(dv, p['w_v'], name=tag + "_bv", tb=True)
    g['w_v'] = mm(xv, dv, name=tag + "_bwv", ta=True)
    if vres:
        dt4 = mm(dvv, p['v2'], name=tag + "_bv2", tb=True)
        g['v2'] = mm(s['t4'], dvv, name=tag + "_bwv2", ta=True)
        dxv = mm(dt4, p['v1'], name=tag + "_bv1", tb=True, res=dxv)
        g['v1'] = mm(xv, dt4, name=tag + "_bwv1", ta=True)
    dxr = mm(dr, p['w_r'], name=tag + "_br", tb=True)
    g['w_r'] = mm(xr, dr, name=tag + "_bwr", ta=True)
    dxk = mm(dk, p['w_k'], name=tag + "_bk", tb=True)
    g['w_k'] = mm(xk, dk, name=tag + "_bwk", ta=True)
    T = dxo.shape[0]
    dh, dmu = colwise(_mix_bwd_fn, [(s['h'], 0), (p['mu'], 0), (dxr, 0), (dxw, 0), (dxk, 0), (dxv, 0), (dxa, 0),
                                    (dxg, 0)], [T, 8], name=tag + "_bmix", nblk=D // LANE)
    g['mu'] = dmu[:6]
    dx, g['norm_g'] = rms_bwd(s['x'], p['norm_g'], dh, dxo, tag + "_bnorm")
    return dx, dvf, g, exchanged


def conv_fwd(x, p, tag):
    T, D = x.shape
    nb = D // LANE
    kw = p['dw'].shape[0]
    h = rms_fwd(x, p['norm_g'], tag + "_norm", bf16)
    u = mm(h, p['w_in'], name=tag + "_in", bias=p['b_in'])
    c = colwise(functools.partial(_glu_conv_fn, kw), [(u, 0), (u, nb), (p['dw'], 0), (p['dw_b'], 0)], [T],
                name=tag + "_dw", nblk=nb)[0]
    sl = rowwise(_ln_silu_fn, [c], [p['ln_g'], p['ln_b']], name=tag + "_ln", out_dtype=bf16)[0]
    xo = mm(sl, p['w_out'], name=tag + "_out", bias=p['b_out'], res=x)
    return xo, dict(x=x, h=h, u=u, c=c, sl=sl)


def conv_bwd(dxo, p, s, tag):
    T, D = dxo.shape
    nb = D // LANE
    kw = p['dw'].shape[0]
    kpad = -(-kw // 8) * 8
    g = {}
    dsl = mm(dxo, p['w_out'], name=tag + "_bout", tb=True)
    g['w_out'] = mm(s['sl'], dxo, name=tag + "_bwout", ta=True)
    g['b_out'] = rowwise_bwd(_bias_fn, [dxo], [p['b_out']], [dxo], name=tag + "_bbout", n_drow=0, n_dpar=1)[0]
    dc, g['ln_g'], g['ln_b'] = rowwise_bwd(_ln_silu_fn, [s['c']], [p['ln_g'], p['ln_b']], [dsl], name=tag + "_bln",
                                           n_drow=1, n_dpar=2)
    u = s['u']
    du1, du2, ddw, g['dw_b'] = colwise(functools.partial(_glu_conv_bwd_fn, kw, kpad),
                                       [(u, 0), (u, nb), (p['dw'], 0), (dc, 0)], [T, T, kpad, 1],
                                       name=tag + "_bdw", nblk=nb)
    g['dw'] = ddw[:kw]
    du = jnp.concatenate([du1, du2], axis=1)
    g['b_in'] = rowwise_bwd(_bias_fn, [du], [p['b_in']], [du], name=tag + "_bbin", n_drow=0, n_dpar=1)[0]
    dh = mm(du, p['w_in'], name=tag + "_bin", tb=True)
    g['w_in'] = mm(s['h'], du, name=tag + "_bwin", ta=True)
    dx, g['norm_g'] = rms_bwd(s['x'], p['norm_g'], dh, dxo, tag + "_bnorm")
    return dx, g


def xattn_fwd(x, memn, p, tag):
    hn = rms_fwd(x, p['norm_g'], tag + "_norm", bf16)
    q = mm(hn, p['w_q'], name=tag + "_q", out_dtype=bf16)
    kv = mm(memn, p['w_kv'], name=tag + "_kv", out_dtype=bf16)
    o = attn_fwd(q, kv, name=tag + "_attn")
    xo = mm(o, p['w_o'], name=tag + "_o", res=x)
    return xo, dict(x=x, hn=hn, q=q, kv=kv, o=o)


def xattn_bwd(dxo, dmemn, memn, p, s, tag):
    g = {}
    do = mm(dxo, p['w_o'], name=tag + "_bo", tb=True)
    g['w_o'] = mm(s['o'], dxo, name=tag + "_bwo", ta=True)
    dq, dk, dv = attn_bwd(s['q'], s['kv'], do, name=tag + "_battn")
    dkv = jnp.concatenate([dk, dv], axis=1)
    dmemn = mm(dkv, p['w_kv'], name=tag + "_bkv", tb=True, res=dmemn)
    g['w_kv'] = mm(memn, dkv, name=tag + "_bwkv", ta=True)
    dhn = mm(dq, p['w_q'], name=tag + "_bq", tb=True)
    g['w_q'] = mm(s['hn'], dq, name=tag + "_bwq", ta=True)
    dx, g['norm_g'] = rms_bwd(s['x'], p['norm_g'], dhn, dxo, tag + "_bnorm")
    return dx, dmemn, g


def ffn_fwd(x, p, tag):
    T, D = x.shape
    w_dev, layer = p['w_in']
    nb = (N_DEV // 2) * w_dev.shape[2] // LANE
    kw = p['dw'].shape[0]
    hn = rms_fwd(x, p['norm_g'], tag + "_norm", bf16)
    u = mm(hn, w_dev, name=tag + "_in", b_dev=(layer * D, D))
    act = colwise(functools.partial(_ffn_act_fn, kw), [(u, 0), (u, nb), (p['dw'], 0), (p['dw'], nb)], [T],
                  name=tag + "_act", nblk=nb, out_dtype=bf16)[0]
    xo = mm(act, p['w_out'], name=tag + "_out", res=x)
    return xo, dict(x=x, hn=hn, u=u, act=act)


def ffn_bwd(dxo, p, s, tag):
    T, D = dxo.shape
    w_dev, layer = p['w_in']
    nb = (N_DEV // 2) * w_dev.shape[2] // LANE
    kw = p['dw'].shape[0]
    g = {}
    dact = mm(dxo, p['w_out'], name=tag + "_bout", tb=True)
    g['w_out'] = mm(s['act'], dxo, name=tag + "_bwout", ta=True)
    u = s['u']
    dug, duv, dwg, dwv = colwise(functools.partial(_ffn_act_bwd_fn, kw, 8),
                                 [(u, 0), (u, nb), (p['dw'], 0), (p['dw'], nb), (dact, 0)], [T, T, 8, 8],
                                 name=tag + "_bact", nblk=nb)
    g['dw'] = jnp.concatenate([dwg[:kw], dwv[:kw]], axis=1)
    du = jnp.concatenate([dug, duv], axis=1)
    dhn = mm(du, w_dev, name=tag + "_bin", tb=True, b_dev=(layer * D, D))
    g['w_in'] = mm(s['hn'], du, name=tag + "_bwin", ta=True, out_dev=True)
    dx, g['norm_g'] = rms_bwd(s['x'], p['norm_g'], dhn, dxo, tag + "_bnorm")
    return dx, g


def _lane_pad(n):
    return -(-n // LANE) * LANE


def _pad_blocks(a, axis, nblk):
    shp = a.shape
    n = shp[axis] // nblk
    t = a.reshape(shp[:axis] + (nblk, n) + shp[axis + 1:])
    pad = [(0, 0)] * t.ndim
    pad[axis + 1] = (0, _lane_pad(n) - n)
    t = jnp.pad(t, pad)
    return t.reshape(shp[:axis] + (nblk * _lane_pad(n),) + shp[axis + 1:])


def _unpad_blocks(a, axis, nblk, n):
    shp = a.shape
    t = a.reshape(shp[:axis] + (nblk, shp[axis] // nblk) + shp[axis + 1:])
    t = lax.slice_in_dim(t, 0, n, axis=axis + 1)
    return t.reshape(shp[:axis] + (nblk * n,) + shp[axis + 1:])


def _layer_params(W, layer):
    ia = ib = layer // 2
    mixer = {}
    if layer % 2 == 0:
        mixer = dict(norm_g=_row(W['norm_mix_g'][layer]), mu=W['rwkv_mu'][ia], w_r=W['rwkv_w_r'][ia],
                     w_k=W['rwkv_w_k'][ia], w_v=W['rwkv_w_v'][ia], w_o=W['rwkv_w_o'][ia], w0=_row(W['rwkv_w0'][ia]),
                     w1=W['rwkv_w1'][ia], w2=W['rwkv_w2'][ia], a0=_row(W['rwkv_a0'][ia]), a1=W['rwkv_a1'][ia],
                     a2=W['rwkv_a2'][ia], g1=W['rwkv_g1'][ia], g2=W['rwkv_g2'][ia], k_k=_row(W['rwkv_k_k'][ia]),
                     k_a=_row(W['rwkv_k_a'][ia]), r_k=_row(W['rwkv_r_k'][ia]), ln_g=_row(W['rwkv_ln_g'][ia]),
                     ln_b=_row(W['rwkv_ln_b'][ia]))
        if ia > 0:
            mixer.update(v0=_row(W['rwkv_v0'][ia - 1]), v1=W['rwkv_v1'][ia - 1], v2=W['rwkv_v2'][ia - 1])
    else:
        mixer = dict(norm_g=_row(W['norm_mix_g'][layer]), w_in=W['conv_w_in'][ib], b_in=_row(W['conv_b_in'][ib]),
                     dw=W['conv_dw'][ib], dw_b=_row(W['conv_dw_b'][ib]), ln_g=_row(W['conv_ln_g'][ib]),
                     ln_b=_row(W['conv_ln_b'][ib]), w_out=W['conv_w_out'][ib], b_out=_row(W['conv_b_out'][ib]))
    return mixer


def _rest_params(W, layer):
    xat = dict(norm_g=_row(W['norm_xattn_g'][layer]), w_q=W['xattn_w_q'][layer], w_kv=W['xattn_w_kv'][layer],
               w_o=W['xattn_w_o'][layer])
    ffn = dict(norm_g=_row(W['norm_ffn_g'][layer]), w_in=(W['ffn_w_in'], layer),
               dw=_pad_blocks(W['ffn_dw'][layer], 1, N_DEV), w_out=_pad_blocks(W['ffn_w_out'][layer], 0, N_DEV // 2))
    return xat, ffn


NATIVE = 'ffn_w_in'
EARLY = [n for n in W_NAMES if W_SPEC[n][0] is not None and (n.startswith('rwkv_') or not W_SPEC[n][1])]
LATE = [n for n in W_NAMES if W_SPEC[n][0] is not None and n not in EARLY and n != NATIVE]


def _native_rows(a, dtype):
    L, D, n = a.shape
    return jnp.pad(a.astype(dtype), ((0, 0), (0, 0), (0, _lane_pad(n) - n))).reshape(L * D, _lane_pad(n))


def _unpack_full(got, local, names):
    parts = _unpack(got, [local[n].shape for n in names], lead=N_DEV)
    return {n: _join_shards(part, W_SPEC[n][0]) for n, part in zip(names, parts)}


def _gather_early(local):
    full = {n: local[n] for n in W_NAMES if W_SPEC[n][0] is None}
    for as_bf16, dtype, tag in ((True, bf16, "ag_mat"), (False, f32, "ag_vec")):
        names = [n for n in EARLY if W_SPEC[n][1] == as_bf16]
        full.update(_unpack_full(all_gather(_pack([local[n] for n in names], dtype), name=tag), local, names))
    return full


def _step(local, x, mem, tgt):
    W = _gather_early(local)
    late_bufs = (_pack([local[n] for n in LATE], bf16), _native_rows(local[NATIVE], bf16))
    depth = W['norm_mix_g'].shape[0]
    g_mem = _row(W['mem_norm_g'])
    memn = rms_fwd(mem, g_mem, "mem_norm", bf16)
    layers, saved = [], []
    vf = None
    for l in range(depth):
        if l % 2 == 0:
            pm = _layer_params(W, l)
            x, v, sm, gathered = rwkv_fwd(x, pm, vf, f"rw{l}", gathers=late_bufs if l == 0 else ())
            if l == 0:
                W.update(_unpack_full(gathered[0], local, LATE))
                W[NATIVE] = gathered[1]
            if vf is None:
                vf = v
        else:
            pm = _layer_params(W, l)
            x, sm = conv_fwd(x, pm, f"cv{l}")
        px, pf = _rest_params(W, l)
        x, sx = xattn_fwd(x, memn, px, f"xa{l}")
        x, sf = ffn_fwd(x, pf, f"ff{l}")
        layers.append((pm, px, pf))
        saved.append((sm, sx, sf))
    g_fin = _row(W['final_norm_g'])
    dx, dg_fin, loss_blk = final_loss(x, tgt, g_fin, name="final_loss")

    grads = {n: [None] * local[n].shape[0] for n in W_NAMES if local[n].ndim >= 2}
    grads['final_norm_g'] = dg_fin.reshape(-1)
    n_in = local[NATIVE].shape[2]
    dmemn = jnp.zeros(memn.shape, f32)
    dvf = None
    for l in reversed(range(depth)):
        pm, px, pf = layers[l]
        sm, sx, sf = saved[l]
        dx, gf = ffn_bwd(dx, pf, sf, f"ff{l}")
        dx, dmemn, gx = xattn_bwd(dx, dmemn, memn, px, sx, f"xa{l}")
        grads['norm_ffn_g'][l] = gf['norm_g'].reshape(-1)
        grads['ffn_w_in'][l] = gf['w_in']
        grads['ffn_dw'][l] = _unpad_blocks(gf['dw'], 1, N_DEV, n_in)
        grads['ffn_w_out'][l] = _unpad_blocks(gf['w_out'], 0, N_DEV // 2, n_in)
        grads['norm_xattn_g'][l] = gx['norm_g'].reshape(-1)
        grads['xattn_w_q'][l], grads['xattn_w_kv'][l], grads['xattn_w_o'][l] = gx['w_q'], gx['w_kv'], gx['w_o']
        i = l // 2
        if l % 2 == 0:
            pre = ()
            if l == 0:
                late = _pack([_split_shards(jnp.stack(grads[n]), W_SPEC[n][0]) for n in LATE], f32, lead=N_DEV)
                native = jnp.concatenate(grads[NATIVE], axis=1)
                pre = (chip_partials(late, name="rs_late"), chip_partials(native, name="rs_ffn_in"))
            dx, dvf_l, gm, exchanged = rwkv_bwd(dx, dvf if i == 0 else None, pm, sm, f"rw{l}", exchanges=pre)
            if l == 0:
                late_parts, native_parts = exchanged
            if dvf_l is not None:
                dvf = dvf_l if dvf is None else rowwise(_add_fn, [dvf, dvf_l], [], name=f"rw{l}_dvfadd")[0]
            for short in ('mu', 'w_r', 'w_k', 'w_v', 'w_o', 'w1', 'w2', 'a1', 'a2', 'g1', 'g2'):
                grads['rwkv_' + short][i] = gm[short]
            for short in ('w0', 'a0', 'k_k', 'k_a', 'ln_g', 'ln_b'):
                grads['rwkv_' + short][i] = gm[short].reshape(-1)
            grads['rwkv_r_k'][i] = gm['r_k'].reshape(W['rwkv_r_k'].shape[1:])
            if i > 0:
                grads['rwkv_v0'][i - 1] = gm['v0'].reshape(-1)
                grads['rwkv_v1'][i - 1], grads['rwkv_v2'][i - 1] = gm['v1'], gm['v2']
        else:
            dx, gm = conv_bwd(dx, pm, sm, f"cv{l}")
            for short in ('w_in', 'dw', 'w_out'):
                grads['conv_' + short][i] = gm[short]
            for short in ('b_in', 'dw_b', 'ln_g', 'ln_b', 'b_out'):
                grads['conv_' + short][i] = gm[short].reshape(-1)
        grads['norm_mix_g'][l] = gm['norm_g'].reshape(-1)
    _, dg_mem = rowwise_bwd(lambda xv, gv: (_rms_fn(xv, gv),), [mem], [g_mem], [dmemn], name="mem_norm_b",
                            n_drow=1, n_dpar=1)
    grads['mem_norm_g'] = dg_mem.reshape(-1)
    full_grads = {n: (jnp.stack(gv) if isinstance(gv, list) else gv) for n, gv in grads.items()
                  if n in EARLY or W_SPEC[n][0] is None}
    return loss_blk[0, 0], dx, full_grads, late_parts, native_parts


def kernel(x, mem, mem_norm_g, norm_mix_g, norm_xattn_g, norm_ffn_g, final_norm_g, rwkv_mu, rwkv_w_r, rwkv_w_k, rwkv_w_v, rwkv_w_o, rwkv_w0, rwkv_w1, rwkv_w2, rwkv_a0, rwkv_a1, rwkv_a2, rwkv_g1, rwkv_g2, rwkv_k_k, rwkv_k_a, rwkv_r_k, rwkv_ln_g, rwkv_ln_b, rwkv_v0, rwkv_v1, rwkv_v2, conv_w_in, conv_b_in, conv_dw, conv_dw_b, conv_ln_g, conv_ln_b, conv_w_out, conv_b_out, xattn_w_q, xattn_w_kv, xattn_w_o, ffn_w_in, ffn_dw, ffn_w_out, loss_target, m_mem_norm_g, m_norm_mix_g, m_norm_xattn_g, m_norm_ffn_g, m_final_norm_g, m_rwkv_mu, m_rwkv_w_r, m_rwkv_w_k, m_rwkv_w_v, m_rwkv_w_o, m_rwkv_w0, m_rwkv_w1, m_rwkv_w2, m_rwkv_a0, m_rwkv_a1, m_rwkv_a2, m_rwkv_g1, m_rwkv_g2, m_rwkv_k_k, m_rwkv_k_a, m_rwkv_r_k, m_rwkv_ln_g, m_rwkv_ln_b, m_rwkv_v0, m_rwkv_v1, m_rwkv_v2, m_conv_w_in, m_conv_b_in, m_conv_dw, m_conv_dw_b, m_conv_ln_g, m_conv_ln_b, m_conv_w_out, m_conv_b_out, m_xattn_w_q, m_xattn_w_kv, m_xattn_w_o, m_ffn_w_in, m_ffn_dw, m_ffn_w_out, v_mem_norm_g, v_norm_mix_g, v_norm_xattn_g, v_norm_ffn_g, v_final_norm_g, v_rwkv_mu, v_rwkv_w_r, v_rwkv_w_k, v_rwkv_w_v, v_rwkv_w_o, v_rwkv_w0, v_rwkv_w1, v_rwkv_w2, v_rwkv_a0, v_rwkv_a1, v_rwkv_a2, v_rwkv_g1, v_rwkv_g2, v_rwkv_k_k, v_rwkv_k_a, v_rwkv_r_k, v_rwkv_ln_g, v_rwkv_ln_b, v_rwkv_v0, v_rwkv_v1, v_rwkv_v2, v_conv_w_in, v_conv_b_in, v_conv_dw, v_conv_dw_b, v_conv_ln_g, v_conv_ln_b, v_conv_w_out, v_conv_b_out, v_xattn_w_q, v_xattn_w_kv, v_xattn_w_o, v_ffn_w_in, v_ffn_dw, v_ffn_w_out):
    given = dict(locals())
    local = {n: given[n] for n in W_NAMES}
    loss_local, dx, grads, late_parts, native_parts = _step(local, x[0], mem[0], loss_target[0])
    loss = lax.psum(loss_local, ("x", "y", "c"))

    repl = [n for n in W_NAMES if W_SPEC[n][0] is None]
    out = {}
    kinds = ("grad_", "delta_", "new_m_", "new_v_")

    res = adamw(native_parts, *[_native_rows(given[pre + NATIVE], f32) for pre in ("", "m_", "v_")],
                name="adamw_ffn_in")
    shp = given[NATIVE].shape
    for kind, buf in zip(kinds, res):
        out[kind + NATIVE] = buf.reshape(shp[0], shp[1], -1)[:, :, :shp[2]]

    early_parts = reduce_scatter(_pack([_split_shards(grads[n], W_SPEC[n][0]) for n in EARLY], f32, lead=N_DEV),
                                 name="rs_early")
    for names, parts, tag in ((LATE, late_parts, "adamw_late"), (EARLY, early_parts, "adamw_early")):
        res = adamw(parts, *[_pack([given[pre + n] for n in names], f32) for pre in ("", "m_", "v_")], name=tag)
        for kind, buf in zip(kinds, res):
            for n, arr in zip(names, _unpack(buf, [given[n].shape for n in names])):
                out[kind + n] = arr

    parts = all_gather(_pack([grads[n] for n in repl], f32), name="grad_gather_repl")
    res = adamw(parts, *[_pack([given[pre + n] for n in repl], f32) for pre in ("", "m_", "v_")],
                name="adamw_repl")
    for kind, buf in zip(("grad_", "delta_", "new_m_", "new_v_"), res):
        for n, arr in zip(repl, _unpack(buf, [given[n].shape for n in repl])):
            out[kind + n] = arr

    return (loss, dx[None], *[out[kind + n] for kind in ("grad_", "delta_", "new_m_", "new_v_") for n in W_NAMES])
```

```python
import functools
import math

import jax
import jax.numpy as jnp
from jax import lax
from jax.experimental import pallas as pl
from jax.experimental.pallas import tpu as pltpu

f32 = jnp.float32
bf16 = jnp.bfloat16

N_DEV = 8
HEAD = 64
XATTN_HEADS = 4
NORM_EPS = 1e-6
LN_EPS = 1e-5
GN_EPS = 64e-5
ADAM_LR, ADAM_B1, ADAM_B2, ADAM_EPS, ADAM_WD, ADAM_STEP = 0.001, 0.9, 0.999, 1e-08, 0.01, 10
LANE = 128
PACK_COLS = 1024
PACK_ROWS = 256
VMEM_LIMIT = 48 * 1024 * 1024
SCAN_CHUNK = 16
SCAN_SPLIT = 1
MXU_DIM = 256

W_SPEC = {
    'mem_norm_g': (None, False), 'norm_mix_g': (None, False), 'norm_xattn_g': (None, False),
    'norm_ffn_g': (None, False), 'final_norm_g': (None, False),
    'rwkv_mu': (2, False), 'rwkv_w_r': (1, True), 'rwkv_w_k': (1, True), 'rwkv_w_v': (1, True),
    'rwkv_w_o': (1, True), 'rwkv_w0': (None, False), 'rwkv_w1': (1, True), 'rwkv_w2': (2, True),
    'rwkv_a0': (None, False), 'rwkv_a1': (1, True), 'rwkv_a2': (2, True), 'rwkv_g1': (1, True),
    'rwkv_g2': (2, True), 'rwkv_k_k': (None, False), 'rwkv_k_a': (None, False), 'rwkv_r_k': (None, False),
    'rwkv_ln_g': (None, False), 'rwkv_ln_b': (None, False), 'rwkv_v0': (None, False),
    'rwkv_v1': (1, True), 'rwkv_v2': (2, True),
    'conv_w_in': (2, True), 'conv_b_in': (1, False), 'conv_dw': (2, False), 'conv_dw_b': (1, False),
    'conv_ln_g': (1, False), 'conv_ln_b': (1, False), 'conv_w_out': (1, True), 'conv_b_out': (1, False),
    'xattn_w_q': (1, True), 'xattn_w_kv': (2, True), 'xattn_w_o': (1, True),
    'ffn_w_in': (2, True), 'ffn_dw': (2, False), 'ffn_w_out': (1, True),
}
W_NAMES = list(W_SPEC)


def _tile(n, prefs):
    for p in prefs:
        if n % p == 0:
            return p
    return n


def _cparams(sem):
    return pltpu.CompilerParams(dimension_semantics=sem, vmem_limit_bytes=VMEM_LIMIT)


def _sigmoid(x):
    return 1.0 / (1.0 + jnp.exp(-x))


def _softplus(x):
    return jnp.maximum(x, 0.0) + jnp.log(1.0 + jnp.exp(-jnp.abs(x)))


def mm(a, b, *, name, ta=False, tb=False, bias=None, res=None, act=None, b_dev=None, out_dev=False, out_dtype=f32):
    M, K = (a.shape[1], a.shape[0]) if ta else a.shape
    tm = _tile(M, (1024, 512, 256, 128))
    if b_dev is None:
        N = b.shape[0] if tb else b.shape[1]
        assert (b.shape[1] if tb else b.shape[0]) == K, (name, a.shape, b.shape)
        tn = _tile(N, (1024, 512, 256, 128))
        tk = _tile(K, (1024, 512, 256, 128))
    else:
        b_off, b_rows = b_dev
        width = b.shape[2]
        if tb:
            N, tk = b_rows, width
            tn = _tile(N, (1024, 512, 256, 128))
            assert K == N_DEV * width and b_off % tn == 0, (name, a.shape, b.shape)
        else:
            N, tn = N_DEV * width, width
            tk = _tile(K, (1024, 512, 256, 128))
            assert K == b_rows and b_off % tk == 0, (name, a.shape, b.shape)
    if out_dev:
        tn = N // N_DEV
    nk = K // tk
    dims = (((0 if ta else 1,), (1 if tb else 0,)), ((), ()))
    has_bias, has_res = bias is not None, res is not None

    def body(*refs):
        a_ref, b_ref = refs[0], refs[1]
        pos = 2
        bias_ref = res_ref = None
        if has_bias:
            bias_ref = refs[pos]; pos += 1
        if has_res:
            res_ref = refs[pos]; pos += 1
        o_ref, acc_ref = refs[pos], refs[pos + 1]
        kstep = pl.program_id(2)

        @pl.when(kstep == 0)
        def _():
            acc_ref[...] = jnp.zeros_like(acc_ref)

        acc_ref[...] += lax.dot_general(a_ref[...].astype(bf16), b_ref[...].astype(bf16), dims,
                                        preferred_element_type=f32)

        @pl.when(kstep == nk - 1)
        def _():
            out = acc_ref[...]
            if has_bias:
                out = out + bias_ref[...]
            if act == 'tanh':
                out = jnp.tanh(out)
            elif act == 'sigmoid':
                out = _sigmoid(out)
            if has_res:
                out = out + res_ref[...]
            o_ref[...] = out.astype(o_ref.dtype)

    a_spec = pl.BlockSpec((tk, tm), lambda i, j, k: (k, i)) if ta else pl.BlockSpec((tm, tk), lambda i, j, k: (i, k))
    if b_dev is None:
        b_spec = pl.BlockSpec((tn, tk), lambda i, j, k: (j, k)) if tb else pl.BlockSpec((tk, tn), lambda i, j, k: (k, j))
    elif tb:
        b_spec = pl.BlockSpec((None, tn, tk), lambda i, j, k: (k, b_off // tn + j, 0))
    else:
        b_spec = pl.BlockSpec((None, tk, tn), lambda i, j, k: (j, b_off // tk + k, 0))
    in_specs, args = [a_spec, b_spec], [a, b]
    if has_bias:
        in_specs.append(pl.BlockSpec((1, tn), lambda i, j, k: (0, j))); args.append(bias)
    if has_res:
        in_specs.append(pl.BlockSpec((tm, tn), lambda i, j, k: (i, j))); args.append(res)
    if out_dev:
        out_spec = pl.BlockSpec((None, tm, tn), lambda i, j, k: (j, i, 0))
        out_shape = jax.ShapeDtypeStruct((N_DEV, M, tn), out_dtype)
    else:
        out_spec = pl.BlockSpec((tm, tn), lambda i, j, k: (i, j))
        out_shape = jax.ShapeDtypeStruct((M, N), out_dtype)
    return pl.pallas_call(
        body, name=name, grid=(M // tm, N // tn, nk), in_specs=in_specs,
        out_specs=out_spec, out_shape=out_shape,
        scratch_shapes=[pltpu.VMEM((tm, tn), f32)],
        compiler_params=_cparams(("parallel", "parallel", "arbitrary")),
    )(*args)


def rowwise(fn, rows, pars, *, name, tt=256, out_dtype=f32):
    T = rows[0].shape[0]
    tt = min(tt, T)
    nr, npar = len(rows), len(pars)
    outs = jax.eval_shape(fn, *[jax.ShapeDtypeStruct((tt, r.shape[1]), r.dtype) for r in rows],
                          *[jax.ShapeDtypeStruct(p.shape, p.dtype) for p in pars])

    def body(*refs):
        res = fn(*[r[...] for r in refs[:nr + npar]])
        for o_ref, o in zip(refs[nr + npar:], res):
            o_ref[...] = o.astype(o_ref.dtype)

    return pl.pallas_call(
        body, name=name, grid=(T // tt,),
        in_specs=[pl.BlockSpec((tt, r.shape[1]), lambda i: (i, 0)) for r in rows]
        + [pl.BlockSpec(p.shape, lambda i: (0, 0)) for p in pars],
        out_specs=[pl.BlockSpec((tt, o.shape[1]), lambda i: (i, 0)) for o in outs],
        out_shape=[jax.ShapeDtypeStruct((T, o.shape[1]), out_dtype) for o in outs],
        compiler_params=_cparams(("parallel",)),
    )(*rows, *pars)


def rowwise_bwd(fn, rows, pars, cots, *, name, n_drow, n_dpar, add0=None, tt=128):
    T = rows[0].shape[0]
    tt = min(tt, T)
    nr, npar, nc = len(rows), len(pars), len(cots)
    has_add = add0 is not None

    def body(*refs):
        rv = [r[...] for r in refs[:nr]]
        pv = [r[...] for r in refs[nr:nr + npar]]
        cv = [r[...] for r in refs[nr + npar:nr + npar + nc]]
        pos = nr + npar + nc
        add_ref = None
        if has_add:
            add_ref = refs[pos]; pos += 1
        drow_refs = refs[pos:pos + n_drow]
        dpar_refs = refs[pos + n_drow:pos + n_drow + n_dpar]

        def f(*d):
            return fn(*d[:n_drow], *rv[n_drow:], *d[n_drow:], *pv[n_dpar:])

        _, vjp = jax.vjp(f, *rv[:n_drow], *pv[:n_dpar])
        g = vjp(tuple(cv))
        for k in range(n_drow):
            gk = g[k]
            if k == 0 and has_add:
                gk = gk + add_ref[...]
            drow_refs[k][...] = gk

        @pl.when(pl.program_id(0) == 0)
        def _():
            for k in range(n_dpar):
                dpar_refs[k][...] = jnp.zeros_like(dpar_refs[k])

        for k in range(n_dpar):
            dpar_refs[k][...] += g[n_drow + k]

    row_spec = lambda r: pl.BlockSpec((tt, r.shape[1]), lambda i: (i, 0))
    par_spec = lambda p: pl.BlockSpec(p.shape, lambda i: (0, 0))
    in_specs = [row_spec(r) for r in rows] + [par_spec(p) for p in pars] + [row_spec(c) for c in cots]
    args = [*rows, *pars, *cots]
    if has_add:
        in_specs.append(row_spec(add0)); args.append(add0)
    return pl.pallas_call(
        body, name=name, grid=(T // tt,), in_specs=in_specs,
        out_specs=[row_spec(r) for r in rows[:n_drow]] + [par_spec(p) for p in pars[:n_dpar]],
        out_shape=[jax.ShapeDtypeStruct(r.shape, f32) for r in rows[:n_drow]]
        + [jax.ShapeDtypeStruct(p.shape, f32) for p in pars[:n_dpar]],
        compiler_params=_cparams(("arbitrary",)),
    )(*args)


def colwise(fn, cols, out_rows, *, name, nblk, out_dtype=f32):
    def body(*refs):
        res = fn(*[r[...] for r in refs[:len(cols)]])
        for o_ref, o in zip(refs[len(cols):], res):
            o_ref[...] = o.astype(o_ref.dtype)

    def spec(rows, off):
        return pl.BlockSpec((rows, LANE), lambda j: (0, j + off))

    return pl.pallas_call(
        body, name=name, grid=(nblk,),
        in_specs=[spec(a.shape[0], off) for a, off in cols],
        out_specs=[spec(r, 0) for r in out_rows],
        out_shape=[jax.ShapeDtypeStruct((r, nblk * LANE), out_dtype) for r in out_rows],
        compiler_params=_cparams(("parallel",)),
    )(*[a for a, _ in cols])


def _shift_dn(x, s):
    if s == 0:
        return x
    rid = lax.broadcasted_iota(jnp.int32, x.shape, 0)
    return jnp.where(rid >= s, pltpu.roll(x, s, 0), 0.0)


def _shift_up(x, s):
    if s == 0:
        return x
    n = x.shape[0]
    rid = lax.broadcasted_iota(jnp.int32, x.shape, 0)
    return jnp.where(rid < n - s, pltpu.roll(x, n - s, 0), 0.0)


def _colsum(x):
    return jnp.sum(x, axis=0, keepdims=True)


def _stack_rows(rows, n):
    c = rows[0].shape[1]
    rid = lax.broadcasted_iota(jnp.int32, (n, c), 0)
    out = jnp.zeros((n, c), f32)
    for i, r in enumerate(rows):
        out = jnp.where(rid == i, jnp.broadcast_to(r, (n, c)), out)
    return out


def _dwconv(x, w, kw):
    acc = None
    for k in range(kw):
        term = w[k:k + 1, :] * _shift_dn(x, kw - 1 - k)
        acc = term if acc is None else acc + term
    return acc


def _dwconv_bwd(x, w, dy, kw, pad_rows):
    dx = None
    rows = []
    for k in range(kw):
        s = kw - 1 - k
        rows.append(_colsum(dy * _shift_dn(x, s)))
        term = w[k:k + 1, :] * _shift_up(dy, s)
        dx = term if dx is None else dx + term
    return dx, _stack_rows(rows, pad_rows)


def _mix_fn(h, mu):
    xx = _shift_dn(h, 1) - h
    return tuple(h + xx * mu[i:i + 1, :] for i in range(6))


def _mix_bwd_fn(h, mu, *ds):
    xx = _shift_dn(h, 1) - h
    s1 = ds[0]
    s2 = ds[0] * mu[0:1, :]
    rows = [_colsum(ds[0] * xx)]
    for i in range(1, 6):
        s1 = s1 + ds[i]
        s2 = s2 + ds[i] * mu[i:i + 1, :]
        rows.append(_colsum(ds[i] * xx))
    return s1 - s2 + _shift_up(s2, 1), _stack_rows(rows, 8)


def _glu_conv_fn(kw, u1, u2, w, b):
    return (_dwconv(u1 * _sigmoid(u2), w, kw) + b,)


def _glu_conv_bwd_fn(kw, pad_rows, u1, u2, w, dc):
    sig = _sigmoid(u2)
    g = u1 * sig
    dg, dw = _dwconv_bwd(g, w, dc, kw, pad_rows)
    return dg * sig, dg * g * (1.0 - sig), dw, _colsum(dc)


def _ffn_act_fn(kw, ug, uv, wg, wv):
    gc = _dwconv(ug, wg, kw)
    vc = _dwconv(uv, wv, kw)
    return (gc * _sigmoid(gc) * vc,)


def _ffn_act_bwd_fn(kw, pad_rows, ug, uv, wg, wv, dact):
    gc = _dwconv(ug, wg, kw)
    vc = _dwconv(uv, wv, kw)
    sg = _sigmoid(gc)
    dvc = dact * gc * sg
    dgc = dact * vc * (sg * (1.0 + gc * (1.0 - sg)))
    dug, dwg = _dwconv_bwd(ug, wg, dgc, kw, pad_rows)
    duv, dwv = _dwconv_bwd(uv, wv, dvc, kw, pad_rows)
    return dug, duv, dwg, dwv


def _rms_fn(x, g):
    return x * lax.rsqrt(jnp.mean(x * x, axis=-1, keepdims=True) + NORM_EPS) * g


def _hsum(x, e, et):
    s = jnp.dot(x, e, precision=lax.Precision.HIGHEST, preferred_element_type=f32)
    return jnp.dot(s, et, precision=lax.Precision.HIGHEST, preferred_element_type=f32)


def _mid_fn(vres, k, v, lw, aa, *rest):
    if vres:
        vv, vf, w0, a0, k_k, k_a, v0, e, et = rest
    else:
        w0, a0, k_k, k_a, e, et = rest
    logw = -_softplus(-(w0 + lw)) - 0.5
    decay = jnp.exp(-jnp.exp(logw))
    a = _sigmoid(a0 + aa)
    kk = k * k_k
    kk = kk / jnp.maximum(jnp.sqrt(_hsum(kk * kk, e, et)), 1e-12)
    k2 = k * (1.0 + (a - 1.0) * k_a)
    v2 = v + (vf - v) * _sigmoid(v0 + vv) if vres else v
    return decay, a, kk, k2, v2


def _post_fn(y, r, k2, v2, gg, ln_g, ln_b, rk, e, et):
    inv = 1.0 / HEAD
    yc = y - _hsum(y, e, et) * inv
    var = _hsum(yc * yc, e, et) * inv
    yn = yc * lax.rsqrt(var + GN_EPS) * ln_g + ln_b
    bonus = _hsum(r * k2 * rk, e, et) * v2
    return ((yn + bonus) * gg,)


def _ln_silu_fn(c, g, b):
    mu = jnp.mean(c, axis=-1, keepdims=True)
    var = jnp.mean(jnp.square(c - mu), axis=-1, keepdims=True)
    ln = (c - mu) * lax.rsqrt(var + LN_EPS) * g + b
    return (ln * _sigmoid(ln),)


def _bias_fn(x, b):
    return (x + b,)


def _dtanh_fn(d, th):
    return (d * (1.0 - th * th),)


def _dsig_fn(d, sg):
    return (d * sg * (1.0 - sg),)


def _add_fn(a, b):
    return (a + b,)


def _seg(blocks, bd, coarse=()):
    def side_by_side(parts):
        h = len(parts) // 2
        return jnp.concatenate([jnp.concatenate(parts[:h], axis=0), jnp.concatenate(parts[h:], axis=0)], axis=1)

    def apart(res, count):
        h = count // 2
        return ([res[i * HEAD:(i + 1) * HEAD, :LANE] for i in range(h)]
                + [res[i * HEAD:(i + 1) * HEAD, LANE:] for i in range(h)])

    x = side_by_side(blocks)
    n = x.shape[0]
    h0 = x.astype(bf16)
    h1 = (x - h0.astype(f32)).astype(bf16)
    lhs = [h0, h1] + ([side_by_side(coarse).astype(bf16)] if coarse else [])
    out = jnp.dot(jnp.concatenate(lhs, axis=0), bd, preferred_element_type=f32)
    fine = apart(out[n:2 * n] + out[0:n], len(blocks))
    return fine + (apart(out[2 * n:], len(coarse)) if coarse else [])


def _scan_consts():
    li = lax.broadcasted_iota(jnp.int32, (MXU_DIM, MXU_DIM), 0) // HEAD
    lj = lax.broadcasted_iota(jnp.int32, (MXU_DIM, MXU_DIM), 1) // HEAD
    bd = (li == lj).astype(bf16)
    si = lax.broadcasted_iota(jnp.int32, (HEAD, LANE), 0)
    sj = lax.broadcasted_iota(jnp.int32, (HEAD, LANE), 1) % HEAD
    dg = (si == sj).astype(f32)
    return bd, dg


def _scan_dims(T, D):
    return D // LANE, min(SCAN_CHUNK, T)


def _scan_groups(G):
    n = -(-G // SCAN_SPLIT)
    return [list(range(i, min(i + n, G))) for i in range(0, G, n)]


def _head_dots_fn(r, k, kk, a, e, et):
    return _hsum(kk * a * r, e, et), _hsum(k * r, e, et)


def scan_fwd(r, w, k, v, kk, a, br, kr, *, name, gathers=()):
    T, D = r.shape
    G, tc = _scan_dims(T, D)
    nch = T // tc
    ng = len(gathers)
    bd, dg = _scan_consts()

    def body(*refs):
        r_ref, w_ref, k_ref, v_ref, kk_ref, a_ref, br_ref, kr_ref, bd_ref, dg_ref = refs[:10]
        y_ref, st_ref, sa_ref = refs[10 + ng:13 + ng]
        s_ref = refs[13 + 2 * ng]
        jobs = [(refs[10 + i], refs[13 + ng + i], *refs[14 + 2 * ng + 3 * i:17 + 2 * ng + 3 * i]) for i in range(ng)]

        @pl.when(pl.program_id(0) == 0)
        def _():
            s_ref[...] = jnp.zeros_like(s_ref)
            for job in jobs:
                _ag_start(*job)

        bdv, dgv = bd_ref[...], dg_ref[...]

        def step(t, carry):
            row = pl.ds(t, 1)
            rr, ww, kr_, vr, kkr, ar, brr, krr = (x[row, :] for x in (r_ref, w_ref, k_ref, v_ref, kk_ref, a_ref,
                                                                    br_ref, kr_ref))
            bb = kkr * ar
            wr = ww * rr
            sl = [slice(g * LANE, (g + 1) * LANE) for g in range(G)]
            ps = [s_ref[g] for g in range(G)]
            results = []
            for grp in _scan_groups(G):
                blocks = [ps[g] * (-kkr[:, sl[g]]) for g in grp]
                blocks += [ps[g] * wr[:, sl[g]] for g in grp]
                vds = [jnp.broadcast_to(vr[:, sl[g]], (HEAD, LANE)) * dgv for g in grp]
                results.append(_seg(blocks, bdv, vds))
            yrows = []
            for grp, res in zip(_scan_groups(G), results):
                n = len(grp)
                for i, g in enumerate(grp):
                    sab, ub, vb = res[i], res[n + i], res[2 * n + i]
                    sn = ps[g] * ww[:, sl[g]] + sab * bb[:, sl[g]] + vb * kr_[:, sl[g]]
                    s_ref[g] = sn
                    st_ref[t, g] = sn
                    sa_ref[t, g] = sab
                    yb = ub + sab * brr[:, sl[g]] + vb * krr[:, sl[g]]
                    yrows.append(_colsum(yb * dgv))
            y_ref[row, :] = jnp.concatenate(yrows, axis=1)
            return carry

        lax.fori_loop(0, tc, step, 0)

        @pl.when(pl.program_id(0) == nch - 1)
        def _():
            for job in jobs:
                _ag_finish(*job)

    vec = pl.BlockSpec((tc, D), lambda c: (c, 0))
    big = pl.BlockSpec((tc, G, HEAD, LANE), lambda c: (c, 0, 0, 0))
    hbm = pl.BlockSpec(memory_space=pl.ANY)
    return pl.pallas_call(
        body, name=name, grid=(nch,),
        in_specs=[vec] * 8 + [pl.BlockSpec((MXU_DIM, MXU_DIM), lambda c: (0, 0)), pl.BlockSpec((HEAD, LANE), lambda c: (0, 0))]
        + [hbm] * ng,
        out_specs=[vec, big, big] + [hbm] * ng,
        out_shape=[jax.ShapeDtypeStruct((T, D), f32)] + [jax.ShapeDtypeStruct((T, G, HEAD, LANE), f32)] * 2
        + [jax.ShapeDtypeStruct((N_DEV,) + x.shape, x.dtype) for x in gathers],
        scratch_shapes=[pltpu.VMEM((G, HEAD, LANE), f32)] + _comm_scratch(AG_SEMS) * ng,
        compiler_params=_cparams(("arbitrary",)),
    )(r, w, k, v, kk, a, br, kr, bd, dg, *gathers)


def scan_bwd(r, w, k, v, kk, a, br, kr, dy, states, sas, dr0, dk0, dv0, *, name, exchanges=()):
    T, D = r.shape
    G, tc = _scan_dims(T, D)
    nch = T // tc
    ne = len(exchanges)
    bd, dg = _scan_consts()

    def body(*refs):
        (r_ref, w_ref, k_ref, v_ref, kk_ref, a_ref, br_ref, kr_ref, dy_ref, st_ref, prev_ref, sa_ref,
         dr0_ref, dk0_ref, dv0_ref, bd_ref, dg_ref) = refs[:17]
        dr_ref, dw_ref, dk_ref, dv_ref, dkk_ref, da_ref = refs[17 + ne:23 + ne]
        ds_ref = refs[23 + 2 * ne]
        jobs = [(refs[17 + i], refs[23 + ne + i], *refs[24 + 2 * ne + 3 * i:27 + 2 * ne + 3 * i]) for i in range(ne)]

        @pl.when(pl.program_id(0) == 0)
        def _():
            ds_ref[...] = jnp.zeros_like(ds_ref)
            for job in jobs:
                _cx_start(*job)

        bdv, dgv = bd_ref[...], dg_ref[...]

        def step_at(t, ps):
            row = pl.ds(t, 1)
            rr, ww, kr_, vr, kkr, ar, brr, krr, dyr = (x[row, :] for x in (r_ref, w_ref, k_ref, v_ref, kk_ref, a_ref,
                                                                         br_ref, kr_ref, dy_ref))
            bb = kkr * ar
            sl = [slice(g * LANE, (g + 1) * LANE) for g in range(G)]
            dr_rows, dw_rows, dk_rows, dv_rows, dkk_rows, da_rows = [], [], [], [], [], []
            dss = [ds_ref[g] for g in range(G)]
            sabs = [sa_ref[t, g] for g in range(G)]
            sts = [st_ref[t, g] for g in range(G)]
            results = []
            for grp in _scan_groups(G):
                blocks = [dss[g] * bb[:, sl[g]] for g in grp]
                blocks += [dss[g] * kr_[:, sl[g]] for g in grp]
                diag = [jnp.broadcast_to(vr[:, sl[g]], (HEAD, LANE)) * dgv for g in grp]
                diag += [jnp.broadcast_to(dyr[:, sl[g]], (HEAD, LANE)) * dgv for g in grp]
                results.append(_seg(blocks, bdv, diag))
            for grp, res in zip(_scan_groups(G), results):
                n = len(grp)
                for i, g in enumerate(grp):
                    sab, vb, dyb = sabs[g], res[2 * n + i], res[3 * n + i]
                    dsab = res[i] + dyb * brr[:, sl[g]]
                    dvb = res[n + i] + dyb * krr[:, sl[g]]
                    dst = dss[g] + dyb * rr[:, sl[g]]
                    dr_rows.append(_colsum(sts[g] * dyb))
                    dw_rows.append(_colsum(dst * ps[g]))
                    db_row = _colsum(dst * sab)
                    dk_rows.append(_colsum(dst * vb))
                    dv_rows.append(_colsum(dvb * dgv))
                    ds_ref[g] = dst * ww[:, sl[g]] - dsab * kkr[:, sl[g]]
                    dkk_rows.append(db_row * ar[:, sl[g]] - _colsum(ps[g] * dsab))
                    da_rows.append(db_row * kkr[:, sl[g]])
            cat = lambda rows: jnp.concatenate(rows, axis=1)
            dr_ref[row, :] = cat(dr_rows) + dr0_ref[row, :]
            dw_ref[row, :] = cat(dw_rows)
            dk_ref[row, :] = cat(dk_rows) + dk0_ref[row, :]
            dv_ref[row, :] = cat(dv_rows) + dv0_ref[row, :]
            dkk_ref[row, :] = cat(dkk_rows)
            da_ref[row, :] = cat(da_rows)

        def step(i, carry):
            t = tc - 1 - i
            step_at(t, [st_ref[t - 1, g] for g in range(G)])
            return carry

        lax.fori_loop(0, tc - 1, step, 0)
        first = (pl.program_id(0) < nch - 1).astype(f32)
        step_at(0, [prev_ref[0, g] * first for g in range(G)])

        @pl.when(pl.program_id(0) == nch - 1)
        def _():
            for job in jobs:
                _cx_finish(*job)

    vec = pl.BlockSpec((tc, D), lambda c: (nch - 1 - c, 0))
    big = pl.BlockSpec((tc, G, HEAD, LANE), lambda c: (nch - 1 - c, 0, 0, 0))
    prev = pl.BlockSpec((1, G, HEAD, LANE), lambda c: (jnp.maximum((nch - 1 - c) * tc - 1, 0), 0, 0, 0))
    hbm = pl.BlockSpec(memory_space=pl.ANY)
    return pl.pallas_call(
        body, name=name, grid=(nch,),
        in_specs=[vec] * 9 + [big, prev, big] + [vec] * 3
        + [pl.BlockSpec((MXU_DIM, MXU_DIM), lambda c: (0, 0)), pl.BlockSpec((HEAD, LANE), lambda c: (0, 0))] + [hbm] * ne,
        out_specs=[vec] * 6 + [hbm] * ne,
        out_shape=[jax.ShapeDtypeStruct((T, D), f32)] * 6 + [jax.ShapeDtypeStruct(x.shape, x.dtype) for x in exchanges],
        scratch_shapes=[pltpu.VMEM((G, HEAD, LANE), f32)] + _comm_scratch(CX_SEMS) * ne,
        compiler_params=_cparams(("arbitrary",)),
    )(r, w, k, v, kk, a, br, kr, dy, states, states, sas, dr0, dk0, dv0, bd, dg, *exchanges)


def _attn_p(q, k, scale):
    s = lax.dot_general(q.astype(bf16), k.astype(bf16), (((1,), (1,)), ((), ())), preferred_element_type=f32) * scale
    s = s - jnp.max(s, axis=-1, keepdims=True)
    p = jnp.exp(s)
    return p / jnp.sum(p, axis=-1, keepdims=True)


def attn_fwd(q, kv, *, name):
    T, D = q.shape
    M = kv.shape[0]
    hd = D // XATTN_HEADS
    scale = hd ** -0.5
    tq = _tile(T, (512, 256, 128))

    def body(q_ref, k_ref, v_ref, o_ref):
        p = _attn_p(q_ref[...], k_ref[...], scale)
        o_ref[...] = jnp.dot(p.astype(bf16), v_ref[...].astype(bf16), preferred_element_type=f32).astype(o_ref.dtype)

    return pl.pallas_call(
        body, name=name, grid=(XATTN_HEADS, T // tq),
        in_specs=[pl.BlockSpec((tq, hd), lambda h, i: (i, h)), pl.BlockSpec((M, hd), lambda h, i: (0, h)),
                  pl.BlockSpec((M, hd), lambda h, i: (0, XATTN_HEADS + h))],
        out_specs=pl.BlockSpec((tq, hd), lambda h, i: (i, h)),
        out_shape=jax.ShapeDtypeStruct((T, D), bf16),
        compiler_params=_cparams(("parallel", "parallel")),
    )(q, kv, kv)


def attn_bwd(q, kv, do, *, name):
    T, D = q.shape
    M = kv.shape[0]
    hd = D // XATTN_HEADS
    scale = hd ** -0.5
    tq = _tile(T, (512, 256, 128))

    def body(q_ref, k_ref, v_ref, do_ref, dq_ref, dk_ref, dv_ref):
        qv, kvv, vv, dov = q_ref[...], k_ref[...], v_ref[...], do_ref[...]
        p = _attn_p(qv, kvv, scale)
        dob = dov.astype(bf16)
        dp = lax.dot_general(dob, vv.astype(bf16), (((1,), (1,)), ((), ())), preferred_element_type=f32)
        ds = p * (dp - jnp.sum(dp * p, axis=-1, keepdims=True)) * scale
        dsb = ds.astype(bf16)
        dq_ref[...] = jnp.dot(dsb, kvv.astype(bf16), preferred_element_type=f32)

        @pl.when(pl.program_id(1) == 0)
        def _():
            dk_ref[...] = jnp.zeros_like(dk_ref)
            dv_ref[...] = jnp.zeros_like(dv_ref)

        dk_ref[...] += lax.dot_general(dsb, qv.astype(bf16), (((0,), (0,)), ((), ())), preferred_element_type=f32)
        dv_ref[...] += lax.dot_general(p.astype(bf16), dob, (((0,), (0,)), ((), ())), preferred_element_type=f32)

    qspec = pl.BlockSpec((tq, hd), lambda h, i: (i, h))
    mspec = pl.BlockSpec((M, hd), lambda h, i: (0, h))
    return pl.pallas_call(
        body, name=name, grid=(XATTN_HEADS, T // tq),
        in_specs=[qspec, mspec, pl.BlockSpec((M, hd), lambda h, i: (0, XATTN_HEADS + h)), qspec],
        out_specs=[qspec, mspec, mspec],
        out_shape=[jax.ShapeDtypeStruct((T, D), f32), jax.ShapeDtypeStruct((M, D), f32),
                   jax.ShapeDtypeStruct((M, D), f32)],
        compiler_params=_cparams(("parallel", "arbitrary")),
    )(q, kv, kv, do)


def final_loss(x, tgt, g, *, name):
    T, D = x.shape
    tt = min(256, T)

    def body(x_ref, t_ref, g_ref, dx_ref, dg_ref, loss_ref):
        tv = t_ref[...]

        def f(xv, gv):
            e = _rms_fn(xv, gv) - tv
            return 0.5 * jnp.sum(jnp.mean(e * e, axis=-1))

        val, vjp = jax.vjp(f, x_ref[...], g_ref[...])
        dx, dgv = vjp(jnp.ones((), f32))
        dx_ref[...] = dx

        @pl.when(pl.program_id(0) == 0)
        def _():
            dg_ref[...] = jnp.zeros_like(dg_ref)
            loss_ref[...] = jnp.zeros_like(loss_ref)

        dg_ref[...] += dgv
        loss_ref[...] += jnp.full(loss_ref.shape, val, f32)

    row = pl.BlockSpec((tt, D), lambda i: (i, 0))
    return pl.pallas_call(
        body, name=name, grid=(T // tt,),
        in_specs=[row, row, pl.BlockSpec((1, D), lambda i: (0, 0))],
        out_specs=[row, pl.BlockSpec((1, D), lambda i: (0, 0)), pl.BlockSpec((8, LANE), lambda i: (0, 0))],
        out_shape=[jax.ShapeDtypeStruct((T, D), f32), jax.ShapeDtypeStruct((1, D), f32),
                   jax.ShapeDtypeStruct((8, LANE), f32)],
        compiler_params=_cparams(("arbitrary",)),
    )(x, tgt, g)


def _place():
    x, y, c = lax.axis_index("x"), lax.axis_index("y"), lax.axis_index("c")
    chips = [(1 - x, y), (x, 1 - y), (1 - x, 1 - y)]
    return x, y, c, chips


def _rcopy(src, dst, send_sems, recv_sems, k, dev):
    return pltpu.make_async_remote_copy(src_ref=src, dst_ref=dst, send_sem=send_sems.at[k], recv_sem=recv_sems.at[k],
                                        device_id=dev, device_id_type=pl.DeviceIdType.MESH)


def _comm_call(body, name, x, out_shape, n_sems):
    return pl.pallas_call(
        body, name=name, out_shape=out_shape,
        in_specs=[pl.BlockSpec(memory_space=pl.ANY)], out_specs=pl.BlockSpec(memory_space=pl.ANY),
        scratch_shapes=_comm_scratch(n_sems),
    )(x)


AG_SEMS = 7
CX_SEMS = 3


def _comm_scratch(n_sems):
    return [pltpu.SemaphoreType.DMA((n_sems,)), pltpu.SemaphoreType.DMA((n_sems,)), pltpu.SemaphoreType.DMA]


def _ag_first(x_ref, o_ref, send_sems, recv_sems, local_sem):
    x_, y_, c_, chips = _place()
    me = o_ref.at[4 * x_ + 2 * y_ + c_]
    copies = [pltpu.make_async_copy(x_ref, me, local_sem), _rcopy(x_ref, me, send_sems, recv_sems, 0, (x_, y_, 1 - c_))]
    copies += [_rcopy(x_ref, me, send_sems, recv_sems, 1 + j, (*chip, c_)) for j, chip in enumerate(chips)]
    return copies


def _ag_start(x_ref, o_ref, send_sems, recv_sems, local_sem):
    for cp in _ag_first(x_ref, o_ref, send_sems, recv_sems, local_sem):
        cp.start()


def _ag_finish(x_ref, o_ref, send_sems, recv_sems, local_sem):
    x_, y_, c_, chips = _place()
    sibling = (x_, y_, 1 - c_)
    slot = lambda px, py, pc: o_ref.at[4 * px + 2 * py + pc]
    passed = [_rcopy(slot(*chip, c_), slot(*chip, c_), send_sems, recv_sems, 4 + j, sibling)
              for j, chip in enumerate(chips)]
    for j, chip in enumerate(chips):
        _rcopy(x_ref, slot(*chip, c_), send_sems, recv_sems, 1 + j, (*chip, c_)).wait_recv()
        passed[j].start()
    _rcopy(x_ref, slot(x_, y_, 1 - c_), send_sems, recv_sems, 0, sibling).wait_recv()
    for j, chip in enumerate(chips):
        _rcopy(x_ref, slot(*chip, 1 - c_), send_sems, recv_sems, 4 + j, sibling).wait_recv()
    first = _ag_first(x_ref, o_ref, send_sems, recv_sems, local_sem)
    for cp in first[1:] + passed:
        cp.wait_send()
    first[0].wait()


def all_gather(x, *, name):
    def body(*refs):
        _ag_start(*refs)
        _ag_finish(*refs)

    return _comm_call(body, name, x, jax.ShapeDtypeStruct((N_DEV,) + x.shape, x.dtype), AG_SEMS)


def pair_exchange(x, *, name):
    n = x.shape[0]

    def body(x_ref, o_ref, send_sems, recv_sems, local_sem):
        x_, y_, c_, _ = _place()
        copies = [_rcopy(x_ref.at[q, 1 - c_], o_ref.at[q], send_sems, recv_sems, q, (x_, y_, 1 - c_)) for q in range(n)]
        for cp in copies:
            cp.start()
        for cp in copies:
            cp.wait()

    return _comm_call(body, name, x, jax.ShapeDtypeStruct((n,) + x.shape[2:], x.dtype), n)


def _cx_copies(x_ref, o_ref, send_sems, recv_sems, local_sem):
    x_, y_, c_, chips = _place()
    myq = 2 * x_ + y_
    copies = [pltpu.make_async_copy(x_ref.at[myq], o_ref.at[myq], local_sem)]
    copies += [_rcopy(x_ref.at[2 * px + py], o_ref.at[myq], send_sems, recv_sems, j, (px, py, c_))
               for j, (px, py) in enumerate(chips)]
    return copies


def _cx_start(x_ref, o_ref, send_sems, recv_sems, local_sem):
    for cp in _cx_copies(x_ref, o_ref, send_sems, recv_sems, local_sem):
        cp.start()


def _cx_finish(x_ref, o_ref, send_sems, recv_sems, local_sem):
    x_, y_, c_, chips = _place()
    myq = 2 * x_ + y_
    for j, (px, py) in enumerate(chips):
        _rcopy(x_ref.at[myq], o_ref.at[2 * px + py], send_sems, recv_sems, j, (px, py, c_)).wait_recv()
    copies = _cx_copies(x_ref, o_ref, send_sems, recv_sems, local_sem)
    for cp in copies[1:]:
        cp.wait_send()
    copies[0].wait()


def chip_exchange(x, *, name):
    def body(*refs):
        _cx_start(*refs)
        _cx_finish(*refs)

    return _comm_call(body, name, x, jax.ShapeDtypeStruct(x.shape, x.dtype), CX_SEMS)


def _add_cast(a, b, *, name):
    n, _, R, C = a.shape
    tr = min(PACK_ROWS // 2, R)

    def body(a_ref, b_ref, o_ref):
        c = lax.axis_index("c")
        for q in range(n):
            o_ref[q] = (a_ref[q, c] + b_ref[q]).astype(bf16)

    return pl.pallas_call(
        body, name=name, grid=(R // tr,),
        in_specs=[pl.BlockSpec((n, 2, tr, C), lambda i: (0, 0, i, 0)), pl.BlockSpec((n, tr, C), lambda i: (0, i, 0))],
        out_specs=pl.BlockSpec((n, tr, C), lambda i: (0, i, 0)),
        out_shape=jax.ShapeDtypeStruct(b.shape, bf16), compiler_params=_cparams(("parallel",)))(a, b)


def chip_partials(pieces, *, name):
    n, R, C = pieces.shape
    by_core = pieces.reshape(n // 2, 2, R, C)
    return _add_cast(by_core, pair_exchange(by_core, name=name + "_pair"), name=name + "_add")


def reduce_scatter(pieces, *, name):
    return chip_exchange(chip_partials(pieces, name=name), name=name + "_chip")


def adamw(gparts, w, m, v, *, name):
    R, C = w.shape
    n_parts = gparts.shape[0]
    tr = min(PACK_ROWS, R)
    c1 = 1.0 / (1.0 - ADAM_B1 ** ADAM_STEP)
    c2 = 1.0 / (1.0 - ADAM_B2 ** ADAM_STEP)

    def body(g_ref, w_ref, m_ref, v_ref, go_ref, d_ref, mo_ref, vo_ref):
        g = g_ref[0].astype(f32)
        for i in range(1, n_parts):
            g = g + g_ref[i].astype(f32)
        mn = ADAM_B1 * m_ref[...] + (1.0 - ADAM_B1) * g
        vn = ADAM_B2 * v_ref[...] + (1.0 - ADAM_B2) * (g * g)
        go_ref[...] = g
        mo_ref[...] = mn
        vo_ref[...] = vn
        d_ref[...] = -ADAM_LR * ((mn * c1) / (jnp.sqrt(vn * c2) + ADAM_EPS) + ADAM_WD * w_ref[...])

    blk = pl.BlockSpec((tr, C), lambda i: (i, 0))
    return pl.pallas_call(
        body, name=name, grid=(R // tr,),
        in_specs=[pl.BlockSpec((n_parts, tr, C), lambda i: (0, i, 0)), blk, blk, blk],
        out_specs=[blk] * 4, out_shape=[jax.ShapeDtypeStruct((R, C), f32)] * 4,
        compiler_params=_cparams(("parallel",)),
    )(gparts, w, m, v)


def _pack(arrs, dtype, lead=None):
    nl = 1 if lead is None else lead
    blocks = []
    for a in arrs:
        n = a.size // nl
        r = -(-n // PACK_COLS)
        a = a.astype(dtype)
        if n != r * PACK_COLS:
            a = jnp.pad(a.reshape(nl, n), ((0, 0), (0, r * PACK_COLS - n)))
        blocks.append(a.reshape(nl, r, PACK_COLS))
    rows = sum(b.shape[1] for b in blocks)
    tot = -(-rows // PACK_ROWS) * PACK_ROWS
    if tot != rows:
        blocks.append(jnp.zeros((nl, tot - rows, PACK_COLS), dtype))
    buf = jnp.concatenate(blocks, axis=1)
    return buf[0] if lead is None else buf


def _split_shards(full, ax):
    shp = full.shape
    t = full.reshape(shp[:ax] + (N_DEV, shp[ax] // N_DEV) + shp[ax + 1:])
    return jnp.moveaxis(t, ax, 0)


def _owner_pieces(layers, ax):
    per_dev = layers[0].size // N_DEV
    if per_dev % PACK_COLS == 0:
        return [_split_shards(g[None], ax) for g in layers]
    return [_split_shards(jnp.stack(layers), ax)]


def _join_shards(parts, ax):
    t = jnp.moveaxis(parts, 0, ax)
    shp = t.shape
    return t.reshape(shp[:ax] + (shp[ax] * shp[ax + 1],) + shp[ax + 2:])


def _unpack(buf, shapes, lead=None):
    out, off = [], 0
    nl = 1 if lead is None else lead
    buf = buf.reshape(nl, -1, PACK_COLS)
    for s in shapes:
        n = math.prod(s)
        r = -(-n // PACK_COLS)
        blk = buf[:, off:off + r]
        if n != r * PACK_COLS:
            blk = blk.reshape(nl, r * PACK_COLS)[:, :n]
        out.append(blk.reshape(tuple(s) if lead is None else (lead,) + tuple(s)))
        off += r
    return out


def _row(v):
    return v.reshape(1, -1)


def _head_mats(D):
    e = (lax.broadcasted_iota(jnp.int32, (D, D // HEAD), 0) // HEAD
         == lax.broadcasted_iota(jnp.int32, (D, D // HEAD), 1)).astype(f32)
    return e, e.T


def rms_fwd(x, g, name, out_dtype=f32):
    return rowwise(lambda xv, gv: (_rms_fn(xv, gv),), [x], [g], name=name, out_dtype=out_dtype)[0]


def rms_bwd(x, g, dh, add, name):
    return rowwise_bwd(lambda xv, gv: (_rms_fn(xv, gv),), [x], [g], [dh], name=name, n_drow=1, n_dpar=1, add0=add)


def rwkv_fwd(x, p, vf, tag, gathers=()):
    vres = vf is not None
    D = x.shape[1]
    e, et = _head_mats(D)
    h = rms_fwd(x, p['norm_g'], tag + "_norm")
    xr, xw, xk, xv, xa, xg = colwise(_mix_fn, [(h, 0), (p['mu'], 0)], [h.shape[0]] * 6, name=tag + "_mix",
                                     nblk=D // LANE, out_dtype=bf16)
    r = mm(xr, p['w_r'], name=tag + "_r")
    k = mm(xk, p['w_k'], name=tag + "_k")
    v = mm(xv, p['w_v'], name=tag + "_v")
    th = mm(xw, p['w1'], name=tag + "_w1", act='tanh')
    lw = mm(th, p['w2'], name=tag + "_w2")
    t2 = mm(xa, p['a1'], name=tag + "_a1", out_dtype=bf16)
    aa = mm(t2, p['a2'], name=tag + "_a2")
    sg = mm(xg, p['g1'], name=tag + "_g1", act='sigmoid')
    gg = mm(sg, p['g2'], name=tag + "_g2")
    rows = [k, v, lw, aa]
    pars = [p['w0'], p['a0'], p['k_k'], p['k_a']]
    t4 = None
    if vres:
        t4 = mm(xv, p['v1'], name=tag + "_v1", out_dtype=bf16)
        vv = mm(t4, p['v2'], name=tag + "_v2")
        rows += [vv, vf]
        pars += [p['v0']]
    pars += [e, et]
    mid = functools.partial(_mid_fn, vres)
    decay, a, kk, k2, v2 = rowwise(mid, rows, pars, name=tag + "_mid")
    br, kr = rowwise(_head_dots_fn, [r, k2, kk, a], [e, et], name=tag + "_hdots")
    y, states, sas, *gathered = scan_fwd(r, decay, k2, v2, kk, a, br, kr, name=tag + "_scan", gathers=gathers)
    post_rows = [y, r, k2, v2, gg]
    post_pars = [p['ln_g'], p['ln_b'], p['r_k'], e, et]
    z = rowwise(_post_fn, post_rows, post_pars, name=tag + "_post", out_dtype=bf16)[0]
    xo = mm(z, p['w_o'], name=tag + "_o", res=x)
    saved = dict(x=x, h=h, xs=(xr, xw, xk, xv, xa, xg), r=r, th=th, t2=t2, sg=sg, t4=t4, mid_rows=rows, mid_pars=pars,
                 mid=mid, scan_in=(r, decay, k2, v2, kk, a, br, kr), states=(states, sas), post_rows=post_rows, post_pars=post_pars,
                 z=z, vres=vres)
    return xo, v2, saved, gathered


def rwkv_bwd(dxo, dvf_in, p, s, tag, exchanges=()):
    D = dxo.shape[1]
    g = {}
    xr, xw, xk, xv, xa, xg = s['xs']
    dz = mm(dxo, p['w_o'], name=tag + "_bo", tb=True)
    g['w_o'] = mm(s['z'], dxo, name=tag + "_bwo", ta=True)
    dy, dr1, dk1, dv1, dgg, g['ln_g'], g['ln_b'], g['r_k'] = rowwise_bwd(
        _post_fn, s['post_rows'], s['post_pars'], [dz], name=tag + "_bpost", n_drow=5, n_dpar=3)
    if dvf_in is not None:
        dv1 = rowwise(_add_fn, [dv1, dvf_in], [], name=tag + "_bvadd")[0]
    dsg = mm(dgg, p['g2'], name=tag + "_bg2", tb=True)
    g['g2'] = mm(s['sg'], dgg, name=tag + "_bwg2", ta=True)
    dt3 = rowwise(_dsig_fn, [dsg, s['sg']], [], name=tag + "_bdsig")[0]
    dxg = mm(dt3, p['g1'], name=tag + "_bg1", tb=True)
    g['g1'] = mm(xg, dt3, name=tag + "_bwg1", ta=True)
    dr, dw, dk2, dv2, dkk, da, *exchanged = scan_bwd(*s['scan_in'], dy, *s['states'], dr1, dk1, dv1,
                                                     name=tag + "_bscan", exchanges=exchanges)
    vres = s['vres']
    n_drow = 6 if vres else 4
    n_dpar = 5 if vres else 4
    outs = rowwise_bwd(s['mid'], s['mid_rows'], s['mid_pars'], [dw, da, dkk, dk2, dv2], name=tag + "_bmid",
                       n_drow=n_drow, n_dpar=n_dpar)
    dk, dv, dlw, daa = outs[:4]
    dvf = None
    if vres:
        dvv, dvf = outs[4:6]
        g['w0'], g['a0'], g['k_k'], g['k_a'], g['v0'] = outs[6:]
    else:
        g['w0'], g['a0'], g['k_k'], g['k_a'] = outs[4:]
    dth = mm(dlw, p['w2'], name=tag + "_bw2", tb=True)
    g['w2'] = mm(s['th'], dlw, name=tag + "_bww2", ta=True)
    dt1 = rowwise(_dtanh_fn, [dth, s['th']], [], name=tag + "_bdtanh")[0]
    dxw = mm(dt1, p['w1'], name=tag + "_bw1", tb=True)
    g['w1'] = mm(xw, dt1, name=tag + "_bww1", ta=True)
    dt2 = mm(daa, p['a2'], name=tag + "_ba2", tb=True)
    g['a2'] = mm(s['t2'], daa, name=tag + "_bwa2", ta=True)
    dxa = mm(dt2, p['a1'], name=tag + "_ba1", tb=True)
    g['a1'] = mm(xa, dt2, name=tag + "_bwa1", ta=True)
    dxv = mm(dv, p['w_v'], name=tag + "_bv", tb=True)
    g['w_v'] = mm(xv, dv, name=tag + "_bwv", ta=True)
    if vres:
        dt4 = mm(dvv, p['v2'], name=tag + "_bv2", tb=True)
        g['v2'] = mm(s['t4'], dvv, name=tag + "_bwv2", ta=True)
        dxv = mm(dt4, p['v1'], name=tag + "_bv1", tb=True, res=dxv)
        g['v1'] = mm(xv, dt4, name=tag + "_bwv1", ta=True)
    dxr = mm(dr, p['w_r'], name=tag + "_br", tb=True)
    g['w_r'] = mm(xr, dr, name=tag + "_bwr", ta=True)
    dxk = mm(dk, p['w_k'], name=tag + "_bk", tb=True)
    g['w_k'] = mm(xk, dk, name=tag + "_bwk", ta=True)
    T = dxo.shape[0]
    dh, dmu = colwise(_mix_bwd_fn, [(s['h'], 0), (p['mu'], 0), (dxr, 0), (dxw, 0), (dxk, 0), (dxv, 0), (dxa, 0),
                                    (dxg, 0)], [T, 8], name=tag + "_bmix", nblk=D // LANE)
    g['mu'] = dmu[:6]
    dx, g['norm_g'] = rms_bwd(s['x'], p['norm_g'], dh, dxo, tag + "_bnorm")
    return dx, dvf, g, exchanged


def conv_fwd(x, p, tag):
    T, D = x.shape
    nb = D // LANE
    kw = p['dw'].shape[0]
    h = rms_fwd(x, p['norm_g'], tag + "_norm", bf16)
    u = mm(h, p['w_in'], name=tag + "_in", bias=p['b_in'])
    c = colwise(functools.partial(_glu_conv_fn, kw), [(u, 0), (u, nb), (p['dw'], 0), (p['dw_b'], 0)], [T],
                name=tag + "_dw", nblk=nb)[0]
    sl = rowwise(_ln_silu_fn, [c], [p['ln_g'], p['ln_b']], name=tag + "_ln", out_dtype=bf16)[0]
    xo = mm(sl, p['w_out'], name=tag + "_out", bias=p['b_out'], res=x)
    return xo, dict(x=x, h=h, u=u, c=c, sl=sl)


def conv_bwd(dxo, p, s, tag):
    T, D = dxo.shape
    nb = D // LANE
    kw = p['dw'].shape[0]
    kpad = -(-kw // 8) * 8
    g = {}
    dsl = mm(dxo, p['w_out'], name=tag + "_bout", tb=True)
    g['w_out'] = mm(s['sl'], dxo, name=tag + "_bwout", ta=True)
    g['b_out'] = rowwise_bwd(_bias_fn, [dxo], [p['b_out']], [dxo], name=tag + "_bbout", n_drow=0, n_dpar=1)[0]
    dc, g['ln_g'], g['ln_b'] = rowwise_bwd(_ln_silu_fn, [s['c']], [p['ln_g'], p['ln_b']], [dsl], name=tag + "_bln",
                                           n_drow=1, n_dpar=2)
    u = s['u']
    du1, du2, ddw, g['dw_b'] = colwise(functools.partial(_glu_conv_bwd_fn, kw, kpad),
                                       [(u, 0), (u, nb), (p['dw'], 0), (dc, 0)], [T, T, kpad, 1],
                                       name=tag + "_bdw", nblk=nb)
    g['dw'] = ddw[:kw]
    du = jnp.concatenate([du1, du2], axis=1)
    g['b_in'] = rowwise_bwd(_bias_fn, [du], [p['b_in']], [du], name=tag + "_bbin", n_drow=0, n_dpar=1)[0]
    dh = mm(du, p['w_in'], name=tag + "_bin", tb=True)
    g['w_in'] = mm(s['h'], du, name=tag + "_bwin", ta=True)
    dx, g['norm_g'] = rms_bwd(s['x'], p['norm_g'], dh, dxo, tag + "_bnorm")
    return dx, g


def xattn_fwd(x, memn, p, tag):
    hn = rms_fwd(x, p['norm_g'], tag + "_norm", bf16)
    q = mm(hn, p['w_q'], name=tag + "_q", out_dtype=bf16)
    kv = mm(memn, p['w_kv'], name=tag + "_kv", out_dtype=bf16)
    o = attn_fwd(q, kv, name=tag + "_attn")
    xo = mm(o, p['w_o'], name=tag + "_o", res=x)
    return xo, dict(x=x, hn=hn, q=q, kv=kv, o=o)


def xattn_bwd(dxo, dmemn, memn, p, s, tag):
    g = {}
    do = mm(dxo, p['w_o'], name=tag + "_bo", tb=True)
    g['w_o'] = mm(s['o'], dxo, name=tag + "_bwo", ta=True)
    dq, dk, dv = attn_bwd(s['q'], s['kv'], do, name=tag + "_battn")
    dkv = jnp.concatenate([dk, dv], axis=1)
    dmemn = mm(dkv, p['w_kv'], name=tag + "_bkv", tb=True, res=dmemn)
    g['w_kv'] = mm(memn, dkv, name=tag + "_bwkv", ta=True)
    dhn = mm(dq, p['w_q'], name=tag + "_bq", tb=True)
    g['w_q'] = mm(s['hn'], dq, name=tag + "_bwq", ta=True)
    dx, g['norm_g'] = rms_bwd(s['x'], p['norm_g'], dhn, dxo, tag + "_bnorm")
    return dx, dmemn, g


def ffn_fwd(x, p, tag):
    T, D = x.shape
    w_dev, layer = p['w_in']
    nb = (N_DEV // 2) * w_dev.shape[2] // LANE
    kw = p['dw'].shape[0]
    hn = rms_fwd(x, p['norm_g'], tag + "_norm", bf16)
    u = mm(hn, w_dev, name=tag + "_in", b_dev=(layer * D, D))
    act = colwise(functools.partial(_ffn_act_fn, kw), [(u, 0), (u, nb), (p['dw'], 0), (p['dw'], nb)], [T],
                  name=tag + "_act", nblk=nb, out_dtype=bf16)[0]
    xo = mm(act, p['w_out'], name=tag + "_out", res=x)
    return xo, dict(x=x, hn=hn, u=u, act=act)


def ffn_bwd(dxo, p, s, tag):
    T, D = dxo.shape
    w_dev, layer = p['w_in']
    nb = (N_DEV // 2) * w_dev.shape[2] // LANE
    kw = p['dw'].shape[0]
    g = {}
    dact = mm(dxo, p['w_out'], name=tag + "_bout", tb=True)
    g['w_out'] = mm(s['act'], dxo, name=tag + "_bwout", ta=True)
    u = s['u']
    dug, duv, dwg, dwv = colwise(functools.partial(_ffn_act_bwd_fn, kw, 8),
                                 [(u, 0), (u, nb), (p['dw'], 0), (p['dw'], nb), (dact, 0)], [T, T, 8, 8],
                                 name=tag + "_bact", nblk=nb)
    g['dw'] = jnp.concatenate([dwg[:kw], dwv[:kw]], axis=1)
    du = jnp.concatenate([dug, duv], axis=1)
    dhn = mm(du, w_dev, name=tag + "_bin", tb=True, b_dev=(layer * D, D))
    g['w_in'] = mm(s['hn'], du, name=tag + "_bwin", ta=True, out_dev=True)
    dx, g['norm_g'] = rms_bwd(s['x'], p['norm_g'], dhn, dxo, tag + "_bnorm")
    return dx, g


def _lane_pad(n):
    return -(-n // LANE) * LANE


def _pad_blocks(a, axis, nblk):
    shp = a.shape
    n = shp[axis] // nblk
    t = a.reshape(shp[:axis] + (nblk, n) + shp[axis + 1:])
    pad = [(0, 0)] * t.ndim
    pad[axis + 1] = (0, _lane_pad(n) - n)
    t = jnp.pad(t, pad)
    return t.reshape(shp[:axis] + (nblk * _lane_pad(n),) + shp[axis + 1:])


def _unpad_blocks(a, axis, nblk, n):
    shp = a.shape
    t = a.reshape(shp[:axis] + (nblk, shp[axis] // nblk) + shp[axis + 1:])
    t = lax.slice_in_dim(t, 0, n, axis=axis + 1)
    return t.reshape(shp[:axis] + (nblk * n,) + shp[axis + 1:])


def _layer_params(W, layer):
    ia = ib = layer // 2
    mixer = {}
    if layer % 2 == 0:
        mixer = dict(norm_g=_row(W['norm_mix_g'][layer]), mu=W['rwkv_mu'][ia], w_r=W['rwkv_w_r'][ia],
                     w_k=W['rwkv_w_k'][ia], w_v=W['rwkv_w_v'][ia], w_o=W['rwkv_w_o'][ia], w0=_row(W['rwkv_w0'][ia]),
                     w1=W['rwkv_w1'][ia], w2=W['rwkv_w2'][ia], a0=_row(W['rwkv_a0'][ia]), a1=W['rwkv_a1'][ia],
                     a2=W['rwkv_a2'][ia], g1=W['rwkv_g1'][ia], g2=W['rwkv_g2'][ia], k_k=_row(W['rwkv_k_k'][ia]),
                     k_a=_row(W['rwkv_k_a'][ia]), r_k=_row(W['rwkv_r_k'][ia]), ln_g=_row(W['rwkv_ln_g'][ia]),
                     ln_b=_row(W['rwkv_ln_b'][ia]))
        if ia > 0:
            mixer.update(v0=_row(W['rwkv_v0'][ia - 1]), v1=W['rwkv_v1'][ia - 1], v2=W['rwkv_v2'][ia - 1])
    else:
        mixer = dict(norm_g=_row(W['norm_mix_g'][layer]), w_in=W['conv_w_in'][ib], b_in=_row(W['conv_b_in'][ib]),
                     dw=W['conv_dw'][ib], dw_b=_row(W['conv_dw_b'][ib]), ln_g=_row(W['conv_ln_g'][ib]),
                     ln_b=_row(W['conv_ln_b'][ib]), w_out=W['conv_w_out'][ib], b_out=_row(W['conv_b_out'][ib]))
    return mixer


def _rest_params(W, layer):
    xat = dict(norm_g=_row(W['norm_xattn_g'][layer]), w_q=W['xattn_w_q'][layer], w_kv=W['xattn_w_kv'][layer],
               w_o=W['xattn_w_o'][layer])
    ffn = dict(norm_g=_row(W['norm_ffn_g'][layer]), w_in=(W['ffn_w_in'], layer),
               dw=_pad_blocks(W['ffn_dw'][layer], 1, N_DEV), w_out=_pad_blocks(W['ffn_w_out'][layer], 0, N_DEV // 2))
    return xat, ffn


NATIVE = 'ffn_w_in'
EARLY = [n for n in W_NAMES if W_SPEC[n][0] is not None and (n.startswith('rwkv_') or not W_SPEC[n][1])]
LATE = [n for n in W_NAMES if W_SPEC[n][0] is not None and n not in EARLY and n != NATIVE]


def _native_rows(a, dtype):
    L, D, n = a.shape
    return jnp.pad(a.astype(dtype), ((0, 0), (0, 0), (0, _lane_pad(n) - n))).reshape(L * D, _lane_pad(n))


def _unpack_full(got, local, names):
    parts = _unpack(got, [local[n].shape for n in names], lead=N_DEV)
    return {n: _join_shards(part, W_SPEC[n][0]) for n, part in zip(names, parts)}


def _gather_early(local):
    full = {n: local[n] for n in W_NAMES if W_SPEC[n][0] is None}
    for as_bf16, dtype, tag in ((True, bf16, "ag_mat"), (False, f32, "ag_vec")):
        names = [n for n in EARLY if W_SPEC[n][1] == as_bf16]
        full.update(_unpack_full(all_gather(_pack([local[n] for n in names], dtype), name=tag), local, names))
    return full


def _step(local, x, mem, tgt):
    W = _gather_early(local)
    late_bufs = (_pack([local[n] for n in LATE], bf16), _native_rows(local[NATIVE], bf16))
    depth = W['norm_mix_g'].shape[0]
    g_mem = _row(W['mem_norm_g'])
    memn = rms_fwd(mem, g_mem, "mem_norm", bf16)
    layers, saved = [], []
    vf = None
    for l in range(depth):
        if l % 2 == 0:
            pm = _layer_params(W, l)
            x, v, sm, gathered = rwkv_fwd(x, pm, vf, f"rw{l}", gathers=late_bufs if l == 0 else ())
            if l == 0:
                W.update(_unpack_full(gathered[0], local, LATE))
                W[NATIVE] = gathered[1]
            if vf is None:
                vf = v
        else:
            pm = _layer_params(W, l)
            x, sm = conv_fwd(x, pm, f"cv{l}")
        px, pf = _rest_params(W, l)
        x, sx = xattn_fwd(x, memn, px, f"xa{l}")
        x, sf = ffn_fwd(x, pf, f"ff{l}")
        layers.append((pm, px, pf))
        saved.append((sm, sx, sf))
    g_fin = _row(W['final_norm_g'])
    dx, dg_fin, loss_blk = final_loss(x, tgt, g_fin, name="final_loss")

    grads = {n: [None] * local[n].shape[0] for n in W_NAMES if local[n].ndim >= 2}
    grads['final_norm_g'] = dg_fin.reshape(-1)
    n_in = local[NATIVE].shape[2]
    dmemn = jnp.zeros(memn.shape, f32)
    dvf = None
    for l in reversed(range(depth)):
        pm, px, pf = layers[l]
        sm, sx, sf = saved[l]
        dx, gf = ffn_bwd(dx, pf, sf, f"ff{l}")
        dx, dmemn, gx = xattn_bwd(dx, dmemn, memn, px, sx, f"xa{l}")
        grads['norm_ffn_g'][l] = gf['norm_g'].reshape(-1)
        grads['ffn_w_in'][l] = gf['w_in']
        grads['ffn_dw'][l] = _unpad_blocks(gf['dw'], 1, N_DEV, n_in)
        grads['ffn_w_out'][l] = _unpad_blocks(gf['w_out'], 0, N_DEV // 2, n_in)
        grads['norm_xattn_g'][l] = gx['norm_g'].reshape(-1)
        grads['xattn_w_q'][l], grads['xattn_w_kv'][l], grads['xattn_w_o'][l] = gx['w_q'], gx['w_kv'], gx['w_o']
        i = l // 2
        if l % 2 == 0:
            pre = ()
            if l == 0:
                late = _pack([p for n in LATE for p in _owner_pieces(grads[n], W_SPEC[n][0])], f32, lead=N_DEV)
                native = jnp.concatenate(grads[NATIVE], axis=1)
                pre = (chip_partials(late, name="rs_late"), chip_partials(native, name="rs_ffn_in"))
            dx, dvf_l, gm, exchanged = rwkv_bwd(dx, dvf if i == 0 else None, pm, sm, f"rw{l}", exchanges=pre)
            if l == 0:
                late_parts, native_parts = exchanged
            if dvf_l is not None:
                dvf = dvf_l if dvf is None else rowwise(_add_fn, [dvf, dvf_l], [], name=f"rw{l}_dvfadd")[0]
            for short in ('mu', 'w_r', 'w_k', 'w_v', 'w_o', 'w1', 'w2', 'a1', 'a2', 'g1', 'g2'):
                grads['rwkv_' + short][i] = gm[short]
            for short in ('w0', 'a0', 'k_k', 'k_a', 'ln_g', 'ln_b'):
                grads['rwkv_' + short][i] = gm[short].reshape(-1)
            grads['rwkv_r_k'][i] = gm['r_k'].reshape(W['rwkv_r_k'].shape[1:])
            if i > 0:
                grads['rwkv_v0'][i - 1] = gm['v0'].reshape(-1)
                grads['rwkv_v1'][i - 1], grads['rwkv_v2'][i - 1] = gm['v1'], gm['v2']
        else:
            dx, gm = conv_bwd(dx, pm, sm, f"cv{l}")
            for short in ('w_in', 'dw', 'w_out'):
                grads['conv_' + short][i] = gm[short]
            for short in ('b_in', 'dw_b', 'ln_g', 'ln_b', 'b_out'):
                grads['conv_' + short][i] = gm[short].reshape(-1)
        grads['norm_mix_g'][l] = gm['norm_g'].reshape(-1)
    _, dg_mem = rowwise_bwd(lambda xv, gv: (_rms_fn(xv, gv),), [mem], [g_mem], [dmemn], name="mem_norm_b",
                            n_drow=1, n_dpar=1)
    grads['mem_norm_g'] = dg_mem.reshape(-1)
    early_grads = {n: grads[n] for n in EARLY}
    repl_grads = {n: (jnp.stack(gv) if isinstance(gv, list) else gv) for n, gv in grads.items() if W_SPEC[n][0] is None}
    return loss_blk[0, 0], dx, early_grads, repl_grads, late_parts, native_parts


def kernel(x, mem, mem_norm_g, norm_mix_g, norm_xattn_g, norm_ffn_g, final_norm_g, rwkv_mu, rwkv_w_r, rwkv_w_k, rwkv_w_v, rwkv_w_o, rwkv_w0, rwkv_w1, rwkv_w2, rwkv_a0, rwkv_a1, rwkv_a2, rwkv_g1, rwkv_g2, rwkv_k_k, rwkv_k_a, rwkv_r_k, rwkv_ln_g, rwkv_ln_b, rwkv_v0, rwkv_v1, rwkv_v2, conv_w_in, conv_b_in, conv_dw, conv_dw_b, conv_ln_g, conv_ln_b, conv_w_out, conv_b_out, xattn_w_q, xattn_w_kv, xattn_w_o, ffn_w_in, ffn_dw, ffn_w_out, loss_target, m_mem_norm_g, m_norm_mix_g, m_norm_xattn_g, m_norm_ffn_g, m_final_norm_g, m_rwkv_mu, m_rwkv_w_r, m_rwkv_w_k, m_rwkv_w_v, m_rwkv_w_o, m_rwkv_w0, m_rwkv_w1, m_rwkv_w2, m_rwkv_a0, m_rwkv_a1, m_rwkv_a2, m_rwkv_g1, m_rwkv_g2, m_rwkv_k_k, m_rwkv_k_a, m_rwkv_r_k, m_rwkv_ln_g, m_rwkv_ln_b, m_rwkv_v0, m_rwkv_v1, m_rwkv_v2, m_conv_w_in, m_conv_b_in, m_conv_dw, m_conv_dw_b, m_conv_ln_g, m_conv_ln_b, m_conv_w_out, m_conv_b_out, m_xattn_w_q, m_xattn_w_kv, m_xattn_w_o, m_ffn_w_in, m_ffn_dw, m_ffn_w_out, v_mem_norm_g, v_norm_mix_g, v_norm_xattn_g, v_norm_ffn_g, v_final_norm_g, v_rwkv_mu, v_rwkv_w_r, v_rwkv_w_k, v_rwkv_w_v, v_rwkv_w_o, v_rwkv_w0, v_rwkv_w1, v_rwkv_w2, v_rwkv_a0, v_rwkv_a1, v_rwkv_a2, v_rwkv_g1, v_rwkv_g2, v_rwkv_k_k, v_rwkv_k_a, v_rwkv_r_k, v_rwkv_ln_g, v_rwkv_ln_b, v_rwkv_v0, v_rwkv_v1, v_rwkv_v2, v_conv_w_in, v_conv_b_in, v_conv_dw, v_conv_dw_b, v_conv_ln_g, v_conv_ln_b, v_conv_w_out, v_conv_b_out, v_xattn_w_q, v_xattn_w_kv, v_xattn_w_o, v_ffn_w_in, v_ffn_dw, v_ffn_w_out):
    given = dict(locals())
    local = {n: given[n] for n in W_NAMES}
    loss_local, dx, early_grads, grads, late_parts, native_parts = _step(local, x[0], mem[0], loss_target[0])
    loss = lax.psum(loss_local, ("x", "y", "c"))

    repl = [n for n in W_NAMES if W_SPEC[n][0] is None]
    out = {}
    kinds = ("grad_", "delta_", "new_m_", "new_v_")

    res = adamw(native_parts, *[_native_rows(given[pre + NATIVE], f32) for pre in ("", "m_", "v_")],
                name="adamw_ffn_in")
    shp = given[NATIVE].shape
    for kind, buf in zip(kinds, res):
        out[kind + NATIVE] = buf.reshape(shp[0], shp[1], -1)[:, :, :shp[2]]

    early_parts = reduce_scatter(
        _pack([p for n in EARLY for p in _owner_pieces(early_grads[n], W_SPEC[n][0])], f32, lead=N_DEV), name="rs_early")
    for names, parts, tag in ((LATE, late_parts, "adamw_late"), (EARLY, early_parts, "adamw_early")):
        res = adamw(parts, *[_pack([given[pre + n] for n in names], f32) for pre in ("", "m_", "v_")], name=tag)
        for kind, buf in zip(kinds, res):
            for n, arr in zip(names, _unpack(buf, [given[n].shape for n in names])):
                out[kind + n] = arr

    parts = all_gather(_pack([grads[n] for n in repl], f32), name="grad_gather_repl")
    res = adamw(parts, *[_pack([given[pre + n] for n in repl], f32) for pre in ("", "m_", "v_")],
                name="adamw_repl")
    for kind, buf in zip(("grad_", "delta_", "new_m_", "new_v_"), res):
        for n, arr in zip(repl, _unpack(buf, [given[n].shape for n in repl])):
            out[kind + n] = arr

    return (loss, dx[None], *[out[kind + n] for kind in ("grad_", "delta_", "new_m_", "new_v_") for n in W_NAMES])
```

```python
import functools
import math

import jax
import jax.numpy as jnp
from jax import lax
from jax.experimental import pallas as pl
from jax.experimental.pallas import tpu as pltpu

f32 = jnp.float32
bf16 = jnp.bfloat16

N_DEV = 8
HEAD = 64
XATTN_HEADS = 4
NORM_EPS = 1e-6
LN_EPS = 1e-5
GN_EPS = 64e-5
ADAM_LR, ADAM_B1, ADAM_B2, ADAM_EPS, ADAM_WD, ADAM_STEP = 0.001, 0.9, 0.999, 1e-08, 0.01, 10
LANE = 128
PACK_COLS = 1024
PACK_ROWS = 256
VMEM_LIMIT = 48 * 1024 * 1024
SCAN_CHUNK = 16
SCAN_SPLIT = 1
MXU_DIM = 256

W_SPEC = {
    'mem_norm_g': (None, False), 'norm_mix_g': (None, False), 'norm_xattn_g': (None, False),
    'norm_ffn_g': (None, False), 'final_norm_g': (None, False),
    'rwkv_mu': (2, False), 'rwkv_w_r': (1, True), 'rwkv_w_k': (1, True), 'rwkv_w_v': (1, True),
    'rwkv_w_o': (1, True), 'rwkv_w0': (None, False), 'rwkv_w1': (1, True), 'rwkv_w2': (2, True),
    'rwkv_a0': (None, False), 'rwkv_a1': (1, True), 'rwkv_a2': (2, True), 'rwkv_g1': (1, True),
    'rwkv_g2': (2, True), 'rwkv_k_k': (None, False), 'rwkv_k_a': (None, False), 'rwkv_r_k': (None, False),
    'rwkv_ln_g': (None, False), 'rwkv_ln_b': (None, False), 'rwkv_v0': (None, False),
    'rwkv_v1': (1, True), 'rwkv_v2': (2, True),
    'conv_w_in': (2, True), 'conv_b_in': (1, False), 'conv_dw': (2, False), 'conv_dw_b': (1, False),
    'conv_ln_g': (1, False), 'conv_ln_b': (1, False), 'conv_w_out': (1, True), 'conv_b_out': (1, False),
    'xattn_w_q': (1, True), 'xattn_w_kv': (2, True), 'xattn_w_o': (1, True),
    'ffn_w_in': (2, True), 'ffn_dw': (2, False), 'ffn_w_out': (1, True),
}
W_NAMES = list(W_SPEC)


def _tile(n, prefs):
    for p in prefs:
        if n % p == 0:
            return p
    return n


def _cparams(sem):
    return pltpu.CompilerParams(dimension_semantics=sem, vmem_limit_bytes=VMEM_LIMIT)


def _sigmoid(x):
    return 1.0 / (1.0 + jnp.exp(-x))


def _softplus(x):
    return jnp.maximum(x, 0.0) + jnp.log(1.0 + jnp.exp(-jnp.abs(x)))


def mm(a, b, *, name, ta=False, tb=False, bias=None, res=None, act=None, b_dev=None, out_dev=False, out_dtype=f32):
    M, K = (a.shape[1], a.shape[0]) if ta else a.shape
    tm = _tile(M, (1024, 512, 256, 128))
    if b_dev is None:
        N = b.shape[0] if tb else b.shape[1]
        assert (b.shape[1] if tb else b.shape[0]) == K, (name, a.shape, b.shape)
        tn = _tile(N, (1024, 512, 256, 128))
        tk = _tile(K, (1024, 512, 256, 128))
    else:
        b_off, b_rows = b_dev
        width = b.shape[2]
        if tb:
            N, tk = b_rows, width
            tn = _tile(N, (1024, 512, 256, 128))
            assert K == N_DEV * width and b_off % tn == 0, (name, a.shape, b.shape)
        else:
            N, tn = N_DEV * width, width
            tk = _tile(K, (1024, 512, 256, 128))
            assert K == b_rows and b_off % tk == 0, (name, a.shape, b.shape)
    if out_dev:
        tn = N // N_DEV
    nk = K // tk
    dims = (((0 if ta else 1,), (1 if tb else 0,)), ((), ()))
    has_bias, has_res = bias is not None, res is not None

    def body(*refs):
        a_ref, b_ref = refs[0], refs[1]
        pos = 2
        bias_ref = res_ref = None
        if has_bias:
            bias_ref = refs[pos]; pos += 1
        if has_res:
            res_ref = refs[pos]; pos += 1
        o_ref, acc_ref = refs[pos], refs[pos + 1]
        kstep = pl.program_id(2)

        @pl.when(kstep == 0)
        def _():
            acc_ref[...] = jnp.zeros_like(acc_ref)

        acc_ref[...] += lax.dot_general(a_ref[...].astype(bf16), b_ref[...].astype(bf16), dims,
                                        preferred_element_type=f32)

        @pl.when(kstep == nk - 1)
        def _():
            out = acc_ref[...]
            if has_bias:
                out = out + bias_ref[...]
            if act == 'tanh':
                out = jnp.tanh(out)
            elif act == 'sigmoid':
                out = _sigmoid(out)
            if has_res:
                out = out + res_ref[...]
            o_ref[...] = out.astype(o_ref.dtype)

    a_spec = pl.BlockSpec((tk, tm), lambda i, j, k: (k, i)) if ta else pl.BlockSpec((tm, tk), lambda i, j, k: (i, k))
    if b_dev is None:
        b_spec = pl.BlockSpec((tn, tk), lambda i, j, k: (j, k)) if tb else pl.BlockSpec((tk, tn), lambda i, j, k: (k, j))
    elif tb:
        b_spec = pl.BlockSpec((None, tn, tk), lambda i, j, k: (k, b_off // tn + j, 0))
    else:
        b_spec = pl.BlockSpec((None, tk, tn), lambda i, j, k: (j, b_off // tk + k, 0))
    in_specs, args = [a_spec, b_spec], [a, b]
    if has_bias:
        in_specs.append(pl.BlockSpec((1, tn), lambda i, j, k: (0, j))); args.append(bias)
    if has_res:
        in_specs.append(pl.BlockSpec((tm, tn), lambda i, j, k: (i, j))); args.append(res)
    if out_dev:
        out_spec = pl.BlockSpec((None, tm, tn), lambda i, j, k: (j, i, 0))
        out_shape = jax.ShapeDtypeStruct((N_DEV, M, tn), out_dtype)
    else:
        out_spec = pl.BlockSpec((tm, tn), lambda i, j, k: (i, j))
        out_shape = jax.ShapeDtypeStruct((M, N), out_dtype)
    return pl.pallas_call(
        body, name=name, grid=(M // tm, N // tn, nk), in_specs=in_specs,
        out_specs=out_spec, out_shape=out_shape,
        scratch_shapes=[pltpu.VMEM((tm, tn), f32)],
        compiler_params=_cparams(("parallel", "parallel", "arbitrary")),
    )(*args)


def rowwise(fn, rows, pars, *, name, tt=256, out_dtype=f32):
    T = rows[0].shape[0]
    tt = min(tt, T)
    nr, npar = len(rows), len(pars)
    outs = jax.eval_shape(fn, *[jax.ShapeDtypeStruct((tt, r.shape[1]), r.dtype) for r in rows],
                          *[jax.ShapeDtypeStruct(p.shape, p.dtype) for p in pars])

    def body(*refs):
        res = fn(*[r[...] for r in refs[:nr + npar]])
        for o_ref, o in zip(refs[nr + npar:], res):
            o_ref[...] = o.astype(o_ref.dtype)

    return pl.pallas_call(
        body, name=name, grid=(T // tt,),
        in_specs=[pl.BlockSpec((tt, r.shape[1]), lambda i: (i, 0)) for r in rows]
        + [pl.BlockSpec(p.shape, lambda i: (0, 0)) for p in pars],
        out_specs=[pl.BlockSpec((tt, o.shape[1]), lambda i: (i, 0)) for o in outs],
        out_shape=[jax.ShapeDtypeStruct((T, o.shape[1]), out_dtype) for o in outs],
        compiler_params=_cparams(("parallel",)),
    )(*rows, *pars)


def rowwise_bwd(fn, rows, pars, cots, *, name, n_drow, n_dpar, add0=None, tt=128):
    T = rows[0].shape[0]
    tt = min(tt, T)
    nr, npar, nc = len(rows), len(pars), len(cots)
    has_add = add0 is not None

    def body(*refs):
        rv = [r[...] for r in refs[:nr]]
        pv = [r[...] for r in refs[nr:nr + npar]]
        cv = [r[...] for r in refs[nr + npar:nr + npar + nc]]
        pos = nr + npar + nc
        add_ref = None
        if has_add:
            add_ref = refs[pos]; pos += 1
        drow_refs = refs[pos:pos + n_drow]
        dpar_refs = refs[pos + n_drow:pos + n_drow + n_dpar]

        def f(*d):
            return fn(*d[:n_drow], *rv[n_drow:], *d[n_drow:], *pv[n_dpar:])

        _, vjp = jax.vjp(f, *rv[:n_drow], *pv[:n_dpar])
        g = vjp(tuple(cv))
        for k in range(n_drow):
            gk = g[k]
            if k == 0 and has_add:
                gk = gk + add_ref[...]
            drow_refs[k][...] = gk

        @pl.when(pl.program_id(0) == 0)
        def _():
            for k in range(n_dpar):
                dpar_refs[k][...] = jnp.zeros_like(dpar_refs[k])

        for k in range(n_dpar):
            dpar_refs[k][...] += g[n_drow + k]

    row_spec = lambda r: pl.BlockSpec((tt, r.shape[1]), lambda i: (i, 0))
    par_spec = lambda p: pl.BlockSpec(p.shape, lambda i: (0, 0))
    in_specs = [row_spec(r) for r in rows] + [par_spec(p) for p in pars] + [row_spec(c) for c in cots]
    args = [*rows, *pars, *cots]
    if has_add:
        in_specs.append(row_spec(add0)); args.append(add0)
    return pl.pallas_call(
        body, name=name, grid=(T // tt,), in_specs=in_specs,
        out_specs=[row_spec(r) for r in rows[:n_drow]] + [par_spec(p) for p in pars[:n_dpar]],
        out_shape=[jax.ShapeDtypeStruct(r.shape, f32) for r in rows[:n_drow]]
        + [jax.ShapeDtypeStruct(p.shape, f32) for p in pars[:n_dpar]],
        compiler_params=_cparams(("arbitrary",)),
    )(*args)


def colwise(fn, cols, out_rows, *, name, nblk, out_dtype=f32):
    def body(*refs):
        res = fn(*[r[...] for r in refs[:len(cols)]])
        for o_ref, o in zip(refs[len(cols):], res):
            o_ref[...] = o.astype(o_ref.dtype)

    def spec(rows, off):
        return pl.BlockSpec((rows, LANE), lambda j: (0, j + off))

    return pl.pallas_call(
        body, name=name, grid=(nblk,),
        in_specs=[spec(a.shape[0], off) for a, off in cols],
        out_specs=[spec(r, 0) for r in out_rows],
        out_shape=[jax.ShapeDtypeStruct((r, nblk * LANE), out_dtype) for r in out_rows],
        compiler_params=_cparams(("parallel",)),
    )(*[a for a, _ in cols])


def _shift_dn(x, s):
    if s == 0:
        return x
    rid = lax.broadcasted_iota(jnp.int32, x.shape, 0)
    return jnp.where(rid >= s, pltpu.roll(x, s, 0), 0.0)


def _shift_up(x, s):
    if s == 0:
        return x
    n = x.shape[0]
    rid = lax.broadcasted_iota(jnp.int32, x.shape, 0)
    return jnp.where(rid < n - s, pltpu.roll(x, n - s, 0), 0.0)


def _colsum(x):
    return jnp.sum(x, axis=0, keepdims=True)


def _stack_rows(rows, n):
    c = rows[0].shape[1]
    rid = lax.broadcasted_iota(jnp.int32, (n, c), 0)
    out = jnp.zeros((n, c), f32)
    for i, r in enumerate(rows):
        out = jnp.where(rid == i, jnp.broadcast_to(r, (n, c)), out)
    return out


def _dwconv(x, w, kw):
    acc = None
    for k in range(kw):
        term = w[k:k + 1, :] * _shift_dn(x, kw - 1 - k)
        acc = term if acc is None else acc + term
    return acc


def _dwconv_bwd(x, w, dy, kw, pad_rows):
    dx = None
    rows = []
    for k in range(kw):
        s = kw - 1 - k
        rows.append(_colsum(dy * _shift_dn(x, s)))
        term = w[k:k + 1, :] * _shift_up(dy, s)
        dx = term if dx is None else dx + term
    return dx, _stack_rows(rows, pad_rows)


def _mix_fn(h, mu):
    xx = _shift_dn(h, 1) - h
    return tuple(h + xx * mu[i:i + 1, :] for i in range(6))


def _mix_bwd_fn(h, mu, *ds):
    xx = _shift_dn(h, 1) - h
    s1 = ds[0]
    s2 = ds[0] * mu[0:1, :]
    rows = [_colsum(ds[0] * xx)]
    for i in range(1, 6):
        s1 = s1 + ds[i]
        s2 = s2 + ds[i] * mu[i:i + 1, :]
        rows.append(_colsum(ds[i] * xx))
    return s1 - s2 + _shift_up(s2, 1), _stack_rows(rows, 8)


def _glu_conv_fn(kw, u1, u2, w, b):
    return (_dwconv(u1 * _sigmoid(u2), w, kw) + b,)


def _glu_conv_bwd_fn(kw, pad_rows, u1, u2, w, dc):
    sig = _sigmoid(u2)
    g = u1 * sig
    dg, dw = _dwconv_bwd(g, w, dc, kw, pad_rows)
    return dg * sig, dg * g * (1.0 - sig), dw, _colsum(dc)


def _ffn_act_fn(kw, ug, uv, wg, wv):
    gc = _dwconv(ug, wg, kw)
    vc = _dwconv(uv, wv, kw)
    return (gc * _sigmoid(gc) * vc,)


def _ffn_act_bwd_fn(kw, pad_rows, ug, uv, wg, wv, dact):
    gc = _dwconv(ug, wg, kw)
    vc = _dwconv(uv, wv, kw)
    sg = _sigmoid(gc)
    dvc = dact * gc * sg
    dgc = dact * vc * (sg * (1.0 + gc * (1.0 - sg)))
    dug, dwg = _dwconv_bwd(ug, wg, dgc, kw, pad_rows)
    duv, dwv = _dwconv_bwd(uv, wv, dvc, kw, pad_rows)
    return dug, duv, dwg, dwv


def _rms_fn(x, g):
    return x * lax.rsqrt(jnp.mean(x * x, axis=-1, keepdims=True) + NORM_EPS) * g


def _hsum(x, e, et):
    s = jnp.dot(x, e, precision=lax.Precision.HIGH, preferred_element_type=f32)
    return jnp.dot(s, et, precision=lax.Precision.HIGH, preferred_element_type=f32)


def _mid_fn(vres, k, v, lw, aa, *rest):
    if vres:
        vv, vf, w0, a0, k_k, k_a, v0, e, et = rest
    else:
        w0, a0, k_k, k_a, e, et = rest
    logw = -_softplus(-(w0 + lw)) - 0.5
    decay = jnp.exp(-jnp.exp(logw))
    a = _sigmoid(a0 + aa)
    kk = k * k_k
    kk = kk / jnp.maximum(jnp.sqrt(_hsum(kk * kk, e, et)), 1e-12)
    k2 = k * (1.0 + (a - 1.0) * k_a)
    v2 = v + (vf - v) * _sigmoid(v0 + vv) if vres else v
    return decay, a, kk, k2, v2


def _post_fn(y, r, k2, v2, gg, ln_g, ln_b, rk, e, et):
    inv = 1.0 / HEAD
    yc = y - _hsum(y, e, et) * inv
    var = _hsum(yc * yc, e, et) * inv
    yn = yc * lax.rsqrt(var + GN_EPS) * ln_g + ln_b
    bonus = _hsum(r * k2 * rk, e, et) * v2
    return ((yn + bonus) * gg,)


def _ln_silu_fn(c, g, b):
    mu = jnp.mean(c, axis=-1, keepdims=True)
    var = jnp.mean(jnp.square(c - mu), axis=-1, keepdims=True)
    ln = (c - mu) * lax.rsqrt(var + LN_EPS) * g + b
    return (ln * _sigmoid(ln),)


def _bias_fn(x, b):
    return (x + b,)


def _dtanh_fn(d, th):
    return (d * (1.0 - th * th),)


def _dsig_fn(d, sg):
    return (d * sg * (1.0 - sg),)


def _add_fn(a, b):
    return (a + b,)


def _seg(blocks, bd, coarse=()):
    def side_by_side(parts):
        h = len(parts) // 2
        return jnp.concatenate([jnp.concatenate(parts[:h], axis=0), jnp.concatenate(parts[h:], axis=0)], axis=1)

    def apart(res, count):
        h = count // 2
        return ([res[i * HEAD:(i + 1) * HEAD, :LANE] for i in range(h)]
                + [res[i * HEAD:(i + 1) * HEAD, LANE:] for i in range(h)])

    x = side_by_side(blocks)
    n = x.shape[0]
    h0 = x.astype(bf16)
    h1 = (x - h0.astype(f32)).astype(bf16)
    lhs = [h0, h1] + ([side_by_side(coarse).astype(bf16)] if coarse else [])
    out = jnp.dot(jnp.concatenate(lhs, axis=0), bd, preferred_element_type=f32)
    fine = apart(out[n:2 * n] + out[0:n], len(blocks))
    return fine + (apart(out[2 * n:], len(coarse)) if coarse else [])


def _scan_consts():
    li = lax.broadcasted_iota(jnp.int32, (MXU_DIM, MXU_DIM), 0) // HEAD
    lj = lax.broadcasted_iota(jnp.int32, (MXU_DIM, MXU_DIM), 1) // HEAD
    bd = (li == lj).astype(bf16)
    si = lax.broadcasted_iota(jnp.int32, (HEAD, LANE), 0)
    sj = lax.broadcasted_iota(jnp.int32, (HEAD, LANE), 1) % HEAD
    dg = (si == sj).astype(f32)
    return bd, dg


def _scan_dims(T, D):
    return D // LANE, min(SCAN_CHUNK, T)


def _scan_groups(G):
    n = -(-G // SCAN_SPLIT)
    return [list(range(i, min(i + n, G))) for i in range(0, G, n)]


def _head_dots_fn(r, k, kk, a, e, et):
    return _hsum(kk * a * r, e, et), _hsum(k * r, e, et)


def scan_fwd(r, w, k, v, kk, a, br, kr, *, name, gathers=()):
    T, D = r.shape
    G, tc = _scan_dims(T, D)
    nch = T // tc
    ng = len(gathers)
    bd, dg = _scan_consts()

    def body(*refs):
        r_ref, w_ref, k_ref, v_ref, kk_ref, a_ref, br_ref, kr_ref, bd_ref, dg_ref = refs[:10]
        y_ref, st_ref, sa_ref = refs[10 + ng:13 + ng]
        s_ref = refs[13 + 2 * ng]
        jobs = [(refs[10 + i], refs[13 + ng + i], *refs[14 + 2 * ng + 3 * i:17 + 2 * ng + 3 * i]) for i in range(ng)]

        @pl.when(pl.program_id(0) == 0)
        def _():
            s_ref[...] = jnp.zeros_like(s_ref)
            for job in jobs:
                _ag_start(*job)

        bdv, dgv = bd_ref[...], dg_ref[...]

        def step(t, carry):
            row = pl.ds(t, 1)
            rr, ww, kr_, vr, kkr, ar, brr, krr = (x[row, :] for x in (r_ref, w_ref, k_ref, v_ref, kk_ref, a_ref,
                                                                    br_ref, kr_ref))
            bb = kkr * ar
            wr = ww * rr
            sl = [slice(g * LANE, (g + 1) * LANE) for g in range(G)]
            ps = [s_ref[g] for g in range(G)]
            results = []
            for grp in _scan_groups(G):
                blocks = [ps[g] * (-kkr[:, sl[g]]) for g in grp]
                vds = [ps[g] * wr[:, sl[g]] for g in grp]
                vds += [jnp.broadcast_to(vr[:, sl[g]], (HEAD, LANE)) * dgv for g in grp]
                results.append(_seg(blocks, bdv, vds))
            yrows = []
            for grp, res in zip(_scan_groups(G), results):
                n = len(grp)
                for i, g in enumerate(grp):
                    sab, ub, vb = res[i], res[n + i], res[2 * n + i]
                    sn = ps[g] * ww[:, sl[g]] + sab * bb[:, sl[g]] + vb * kr_[:, sl[g]]
                    s_ref[g] = sn
                    st_ref[t, g] = sn
                    sa_ref[t, g] = sab
                    yb = ub + sab * brr[:, sl[g]] + vb * krr[:, sl[g]]
                    yrows.append(_colsum(yb * dgv))
            y_ref[row, :] = jnp.concatenate(yrows, axis=1)
            return carry

        lax.fori_loop(0, tc, step, 0)

        @pl.when(pl.program_id(0) == nch - 1)
        def _():
            for job in jobs:
                _ag_finish(*job)

    vec = pl.BlockSpec((tc, D), lambda c: (c, 0))
    big = pl.BlockSpec((tc, G, HEAD, LANE), lambda c: (c, 0, 0, 0))
    hbm = pl.BlockSpec(memory_space=pl.ANY)
    return pl.pallas_call(
        body, name=name, grid=(nch,),
        in_specs=[vec] * 8 + [pl.BlockSpec((MXU_DIM, MXU_DIM), lambda c: (0, 0)), pl.BlockSpec((HEAD, LANE), lambda c: (0, 0))]
        + [hbm] * ng,
        out_specs=[vec, big, big] + [hbm] * ng,
        out_shape=[jax.ShapeDtypeStruct((T, D), f32)] + [jax.ShapeDtypeStruct((T, G, HEAD, LANE), f32)] * 2
        + [jax.ShapeDtypeStruct((N_DEV,) + x.shape, x.dtype) for x in gathers],
        scratch_shapes=[pltpu.VMEM((G, HEAD, LANE), f32)] + _comm_scratch(AG_SEMS) * ng,
        compiler_params=_cparams(("arbitrary",)),
    )(r, w, k, v, kk, a, br, kr, bd, dg, *gathers)


def scan_bwd(r, w, k, v, kk, a, br, kr, dy, states, sas, dr0, dk0, dv0, *, name, exchanges=()):
    T, D = r.shape
    G, tc = _scan_dims(T, D)
    nch = T // tc
    ne = len(exchanges)
    bd, dg = _scan_consts()

    def body(*refs):
        (r_ref, w_ref, k_ref, v_ref, kk_ref, a_ref, br_ref, kr_ref, dy_ref, st_ref, prev_ref, sa_ref,
         dr0_ref, dk0_ref, dv0_ref, bd_ref, dg_ref) = refs[:17]
        dr_ref, dw_ref, dk_ref, dv_ref, dkk_ref, da_ref = refs[17 + ne:23 + ne]
        ds_ref = refs[23 + 2 * ne]
        jobs = [(refs[17 + i], refs[23 + ne + i], *refs[24 + 2 * ne + 3 * i:27 + 2 * ne + 3 * i]) for i in range(ne)]

        @pl.when(pl.program_id(0) == 0)
        def _():
            ds_ref[...] = jnp.zeros_like(ds_ref)
            for job in jobs:
                _cx_start(*job)

        bdv, dgv = bd_ref[...], dg_ref[...]

        def step_at(t, ps):
            row = pl.ds(t, 1)
            rr, ww, kr_, vr, kkr, ar, brr, krr, dyr = (x[row, :] for x in (r_ref, w_ref, k_ref, v_ref, kk_ref, a_ref,
                                                                         br_ref, kr_ref, dy_ref))
            bb = kkr * ar
            sl = [slice(g * LANE, (g + 1) * LANE) for g in range(G)]
            dr_rows, dw_rows, dk_rows, dv_rows, dkk_rows, da_rows = [], [], [], [], [], []
            dss = [ds_ref[g] for g in range(G)]
            sabs = [sa_ref[t, g] for g in range(G)]
            sts = [st_ref[t, g] for g in range(G)]
            results = []
            for grp in _scan_groups(G):
                blocks = [dss[g] * bb[:, sl[g]] for g in grp]
                diag = [dss[g] * kr_[:, sl[g]] for g in grp]
                diag += [jnp.broadcast_to(vr[:, sl[g]], (HEAD, LANE)) * dgv for g in grp]
                diag += [jnp.broadcast_to(dyr[:, sl[g]], (HEAD, LANE)) * dgv for g in grp]
                results.append(_seg(blocks, bdv, diag))
            for grp, res in zip(_scan_groups(G), results):
                n = len(grp)
                for i, g in enumerate(grp):
                    sab, vb, dyb = sabs[g], res[2 * n + i], res[3 * n + i]
                    dsab = res[i] + dyb * brr[:, sl[g]]
                    dvb = res[n + i] + dyb * krr[:, sl[g]]
                    dst = dss[g] + dyb * rr[:, sl[g]]
                    dr_rows.append(_colsum(sts[g] * dyb))
                    dw_rows.append(_colsum(dst * ps[g]))
                    db_row = _colsum(dst * sab)
                    dk_rows.append(_colsum(dst * vb))
                    dv_rows.append(_colsum(dvb * dgv))
                    ds_ref[g] = dst * ww[:, sl[g]] - dsab * kkr[:, sl[g]]
                    dkk_rows.append(db_row * ar[:, sl[g]] - _colsum(ps[g] * dsab))
                    da_rows.append(db_row * kkr[:, sl[g]])
            cat = lambda rows: jnp.concatenate(rows, axis=1)
            dr_ref[row, :] = cat(dr_rows) + dr0_ref[row, :]
            dw_ref[row, :] = cat(dw_rows)
            dk_ref[row, :] = cat(dk_rows) + dk0_ref[row, :]
            dv_ref[row, :] = cat(dv_rows) + dv0_ref[row, :]
            dkk_ref[row, :] = cat(dkk_rows)
            da_ref[row, :] = cat(da_rows)

        def step(i, carry):
            t = tc - 1 - i
            step_at(t, [st_ref[t - 1, g] for g in range(G)])
            return carry

        lax.fori_loop(0, tc - 1, step, 0)
        first = (pl.program_id(0) < nch - 1).astype(f32)
        step_at(0, [prev_ref[0, g] * first for g in range(G)])

        @pl.when(pl.program_id(0) == nch - 1)
        def _():
            for job in jobs:
                _cx_finish(*job)

    vec = pl.BlockSpec((tc, D), lambda c: (nch - 1 - c, 0))
    big = pl.BlockSpec((tc, G, HEAD, LANE), lambda c: (nch - 1 - c, 0, 0, 0))
    prev = pl.BlockSpec((1, G, HEAD, LANE), lambda c: (jnp.maximum((nch - 1 - c) * tc - 1, 0), 0, 0, 0))
    hbm = pl.BlockSpec(memory_space=pl.ANY)
    return pl.pallas_call(
        body, name=name, grid=(nch,),
        in_specs=[vec] * 9 + [big, prev, big] + [vec] * 3
        + [pl.BlockSpec((MXU_DIM, MXU_DIM), lambda c: (0, 0)), pl.BlockSpec((HEAD, LANE), lambda c: (0, 0))] + [hbm] * ne,
        out_specs=[vec] * 6 + [hbm] * ne,
        out_shape=[jax.ShapeDtypeStruct((T, D), f32)] * 6 + [jax.ShapeDtypeStruct(x.shape, x.dtype) for x in exchanges],
        scratch_shapes=[pltpu.VMEM((G, HEAD, LANE), f32)] + _comm_scratch(CX_SEMS) * ne,
        compiler_params=_cparams(("arbitrary",)),
    )(r, w, k, v, kk, a, br, kr, dy, states, states, sas, dr0, dk0, dv0, bd, dg, *exchanges)


def _attn_p(q, k, scale):
    s = lax.dot_general(q.astype(bf16), k.astype(bf16), (((1,), (1,)), ((), ())), preferred_element_type=f32) * scale
    s = s - jnp.max(s, axis=-1, keepdims=True)
    p = jnp.exp(s)
    return p / jnp.sum(p, axis=-1, keepdims=True)


def attn_fwd(q, kv, *, name):
    T, D = q.shape
    M = kv.shape[0]
    hd = D // XATTN_HEADS
    scale = hd ** -0.5
    tq = _tile(T, (512, 256, 128))

    def body(q_ref, k_ref, v_ref, o_ref):
        p = _attn_p(q_ref[...], k_ref[...], scale)
        o_ref[...] = jnp.dot(p.astype(bf16), v_ref[...].astype(bf16), preferred_element_type=f32).astype(o_ref.dtype)

    return pl.pallas_call(
        body, name=name, grid=(XATTN_HEADS, T // tq),
        in_specs=[pl.BlockSpec((tq, hd), lambda h, i: (i, h)), pl.BlockSpec((M, hd), lambda h, i: (0, h)),
                  pl.BlockSpec((M, hd), lambda h, i: (0, XATTN_HEADS + h))],
        out_specs=pl.BlockSpec((tq, hd), lambda h, i: (i, h)),
        out_shape=jax.ShapeDtypeStruct((T, D), bf16),
        compiler_params=_cparams(("parallel", "parallel")),
    )(q, kv, kv)


def attn_bwd(q, kv, do, *, name):
    T, D = q.shape
    M = kv.shape[0]
    hd = D // XATTN_HEADS
    scale = hd ** -0.5
    tq = _tile(T, (512, 256, 128))

    def body(q_ref, k_ref, v_ref, do_ref, dq_ref, dk_ref, dv_ref):
        qv, kvv, vv, dov = q_ref[...], k_ref[...], v_ref[...], do_ref[...]
        p = _attn_p(qv, kvv, scale)
        dob = dov.astype(bf16)
        dp = lax.dot_general(dob, vv.astype(bf16), (((1,), (1,)), ((), ())), preferred_element_type=f32)
        ds = p * (dp - jnp.sum(dp * p, axis=-1, keepdims=True)) * scale
        dsb = ds.astype(bf16)
        dq_ref[...] = jnp.dot(dsb, kvv.astype(bf16), preferred_element_type=f32)

        @pl.when(pl.program_id(1) == 0)
        def _():
            dk_ref[...] = jnp.zeros_like(dk_ref)
            dv_ref[...] = jnp.zeros_like(dv_ref)

        dk_ref[...] += lax.dot_general(dsb, qv.astype(bf16), (((0,), (0,)), ((), ())), preferred_element_type=f32)
        dv_ref[...] += lax.dot_general(p.astype(bf16), dob, (((0,), (0,)), ((), ())), preferred_element_type=f32)

    qspec = pl.BlockSpec((tq, hd), lambda h, i: (i, h))
    mspec = pl.BlockSpec((M, hd), lambda h, i: (0, h))
    return pl.pallas_call(
        body, name=name, grid=(XATTN_HEADS, T // tq),
        in_specs=[qspec, mspec, pl.BlockSpec((M, hd), lambda h, i: (0, XATTN_HEADS + h)), qspec],
        out_specs=[qspec, mspec, mspec],
        out_shape=[jax.ShapeDtypeStruct((T, D), f32), jax.ShapeDtypeStruct((M, D), f32),
                   jax.ShapeDtypeStruct((M, D), f32)],
        compiler_params=_cparams(("parallel", "arbitrary")),
    )(q, kv, kv, do)


def final_loss(x, tgt, g, *, name):
    T, D = x.shape
    tt = min(256, T)

    def body(x_ref, t_ref, g_ref, dx_ref, dg_ref, loss_ref):
        tv = t_ref[...]

        def f(xv, gv):
            e = _rms_fn(xv, gv) - tv
            return 0.5 * jnp.sum(jnp.mean(e * e, axis=-1))

        val, vjp = jax.vjp(f, x_ref[...], g_ref[...])
        dx, dgv = vjp(jnp.ones((), f32))
        dx_ref[...] = dx

        @pl.when(pl.program_id(0) == 0)
        def _():
            dg_ref[...] = jnp.zeros_like(dg_ref)
            loss_ref[...] = jnp.zeros_like(loss_ref)

        dg_ref[...] += dgv
        loss_ref[...] += jnp.full(loss_ref.shape, val, f32)

    row = pl.BlockSpec((tt, D), lambda i: (i, 0))
    return pl.pallas_call(
        body, name=name, grid=(T // tt,),
        in_specs=[row, row, pl.BlockSpec((1, D), lambda i: (0, 0))],
        out_specs=[row, pl.BlockSpec((1, D), lambda i: (0, 0)), pl.BlockSpec((8, LANE), lambda i: (0, 0))],
        out_shape=[jax.ShapeDtypeStruct((T, D), f32), jax.ShapeDtypeStruct((1, D), f32),
                   jax.ShapeDtypeStruct((8, LANE), f32)],
        compiler_params=_cparams(("arbitrary",)),
    )(x, tgt, g)


def _place():
    x, y, c = lax.axis_index("x"), lax.axis_index("y"), lax.axis_index("c")
    chips = [(1 - x, y), (x, 1 - y), (1 - x, 1 - y)]
    return x, y, c, chips


def _rcopy(src, dst, send_sems, recv_sems, k, dev):
    return pltpu.make_async_remote_copy(src_ref=src, dst_ref=dst, send_sem=send_sems.at[k], recv_sem=recv_sems.at[k],
                                        device_id=dev, device_id_type=pl.DeviceIdType.MESH)


def _comm_call(body, name, x, out_shape, n_sems):
    return pl.pallas_call(
        body, name=name, out_shape=out_shape,
        in_specs=[pl.BlockSpec(memory_space=pl.ANY)], out_specs=pl.BlockSpec(memory_space=pl.ANY),
        scratch_shapes=_comm_scratch(n_sems),
    )(x)


AG_SEMS = 7
CX_SEMS = 3


def _comm_scratch(n_sems):
    return [pltpu.SemaphoreType.DMA((n_sems,)), pltpu.SemaphoreType.DMA((n_sems,)), pltpu.SemaphoreType.DMA]


def _ag_first(x_ref, o_ref, send_sems, recv_sems, local_sem):
    x_, y_, c_, chips = _place()
    me = o_ref.at[4 * x_ + 2 * y_ + c_]
    copies = [pltpu.make_async_copy(x_ref, me, local_sem), _rcopy(x_ref, me, send_sems, recv_sems, 0, (x_, y_, 1 - c_))]
    copies += [_rcopy(x_ref, me, send_sems, recv_sems, 1 + j, (*chip, c_)) for j, chip in enumerate(chips)]
    return copies


def _ag_start(x_ref, o_ref, send_sems, recv_sems, local_sem):
    for cp in _ag_first(x_ref, o_ref, send_sems, recv_sems, local_sem):
        cp.start()


def _ag_finish(x_ref, o_ref, send_sems, recv_sems, local_sem):
    x_, y_, c_, chips = _place()
    sibling = (x_, y_, 1 - c_)
    slot = lambda px, py, pc: o_ref.at[4 * px + 2 * py + pc]
    passed = [_rcopy(slot(*chip, c_), slot(*chip, c_), send_sems, recv_sems, 4 + j, sibling)
              for j, chip in enumerate(chips)]
    for j, chip in enumerate(chips):
        _rcopy(x_ref, slot(*chip, c_), send_sems, recv_sems, 1 + j, (*chip, c_)).wait_recv()
        passed[j].start()
    _rcopy(x_ref, slot(x_, y_, 1 - c_), send_sems, recv_sems, 0, sibling).wait_recv()
    for j, chip in enumerate(chips):
        _rcopy(x_ref, slot(*chip, 1 - c_), send_sems, recv_sems, 4 + j, sibling).wait_recv()
    first = _ag_first(x_ref, o_ref, send_sems, recv_sems, local_sem)
    for cp in first[1:] + passed:
        cp.wait_send()
    first[0].wait()


def all_gather(x, *, name):
    def body(*refs):
        _ag_start(*refs)
        _ag_finish(*refs)

    return _comm_call(body, name, x, jax.ShapeDtypeStruct((N_DEV,) + x.shape, x.dtype), AG_SEMS)


def pair_exchange(x, *, name):
    n = x.shape[0]

    def body(x_ref, o_ref, send_sems, recv_sems, local_sem):
        x_, y_, c_, _ = _place()
        copies = [_rcopy(x_ref.at[q, 1 - c_], o_ref.at[q], send_sems, recv_sems, q, (x_, y_, 1 - c_)) for q in range(n)]
        for cp in copies:
            cp.start()
        for cp in copies:
            cp.wait()

    return _comm_call(body, name, x, jax.ShapeDtypeStruct((n,) + x.shape[2:], x.dtype), n)


def _cx_copies(x_ref, o_ref, send_sems, recv_sems, local_sem):
    x_, y_, c_, chips = _place()
    myq = 2 * x_ + y_
    copies = [pltpu.make_async_copy(x_ref.at[myq], o_ref.at[myq], local_sem)]
    copies += [_rcopy(x_ref.at[2 * px + py], o_ref.at[myq], send_sems, recv_sems, j, (px, py, c_))
               for j, (px, py) in enumerate(chips)]
    return copies


def _cx_start(x_ref, o_ref, send_sems, recv_sems, local_sem):
    for cp in _cx_copies(x_ref, o_ref, send_sems, recv_sems, local_sem):
        cp.start()


def _cx_finish(x_ref, o_ref, send_sems, recv_sems, local_sem):
    x_, y_, c_, chips = _place()
    myq = 2 * x_ + y_
    for j, (px, py) in enumerate(chips):
        _rcopy(x_ref.at[myq], o_ref.at[2 * px + py], send_sems, recv_sems, j, (px, py, c_)).wait_recv()
    copies = _cx_copies(x_ref, o_ref, send_sems, recv_sems, local_sem)
    for cp in copies[1:]:
        cp.wait_send()
    copies[0].wait()


def chip_exchange(x, *, name):
    def body(*refs):
        _cx_start(*refs)
        _cx_finish(*refs)

    return _comm_call(body, name, x, jax.ShapeDtypeStruct(x.shape, x.dtype), CX_SEMS)


def _add_cast(a, b, *, name):
    n, _, R, C = a.shape
    tr = min(PACK_ROWS // 2, R)

    def body(a_ref, b_ref, o_ref):
        c = lax.axis_index("c")
        for q in range(n):
            o_ref[q] = (a_ref[q, c] + b_ref[q]).astype(bf16)

    return pl.pallas_call(
        body, name=name, grid=(R // tr,),
        in_specs=[pl.BlockSpec((n, 2, tr, C), lambda i: (0, 0, i, 0)), pl.BlockSpec((n, tr, C), lambda i: (0, i, 0))],
        out_specs=pl.BlockSpec((n, tr, C), lambda i: (0, i, 0)),
        out_shape=jax.ShapeDtypeStruct(b.shape, bf16), compiler_params=_cparams(("parallel",)))(a, b)


def chip_partials(pieces, *, name):
    n, R, C = pieces.shape
    by_core = pieces.reshape(n // 2, 2, R, C)
    return _add_cast(by_core, pair_exchange(by_core, name=name + "_pair"), name=name + "_add")


def reduce_scatter(pieces, *, name):
    return chip_exchange(chip_partials(pieces, name=name), name=name + "_chip")


def adamw(gparts, w, m, v, *, name):
    R, C = w.shape
    n_parts = gparts.shape[0]
    tr = min(PACK_ROWS, R)
    c1 = 1.0 / (1.0 - ADAM_B1 ** ADAM_STEP)
    c2 = 1.0 / (1.0 - ADAM_B2 ** ADAM_STEP)

    def body(g_ref, w_ref, m_ref, v_ref, go_ref, d_ref, mo_ref, vo_ref):
        g = g_ref[0].astype(f32)
        for i in range(1, n_parts):
            g = g + g_ref[i].astype(f32)
        mn = ADAM_B1 * m_ref[...] + (1.0 - ADAM_B1) * g
        vn = ADAM_B2 * v_ref[...] + (1.0 - ADAM_B2) * (g * g)
        go_ref[...] = g
        mo_ref[...] = mn
        vo_ref[...] = vn
        d_ref[...] = -ADAM_LR * ((mn * c1) / (jnp.sqrt(vn * c2) + ADAM_EPS) + ADAM_WD * w_ref[...])

    blk = pl.BlockSpec((tr, C), lambda i: (i, 0))
    return pl.pallas_call(
        body, name=name, grid=(R // tr,),
        in_specs=[pl.BlockSpec((n_parts, tr, C), lambda i: (0, i, 0)), blk, blk, blk],
        out_specs=[blk] * 4, out_shape=[jax.ShapeDtypeStruct((R, C), f32)] * 4,
        compiler_params=_cparams(("parallel",)),
    )(gparts, w, m, v)


def _pack(arrs, dtype, lead=None):
    nl = 1 if lead is None else lead
    blocks = []
    for a in arrs:
        n = a.size // nl
        r = -(-n // PACK_COLS)
        a = a.astype(dtype)
        if n != r * PACK_COLS:
            a = jnp.pad(a.reshape(nl, n), ((0, 0), (0, r * PACK_COLS - n)))
        blocks.append(a.reshape(nl, r, PACK_COLS))
    rows = sum(b.shape[1] for b in blocks)
    tot = -(-rows // PACK_ROWS) * PACK_ROWS
    if tot != rows:
        blocks.append(jnp.zeros((nl, tot - rows, PACK_COLS), dtype))
    buf = jnp.concatenate(blocks, axis=1)
    return buf[0] if lead is None else buf


def _split_shards(full, ax):
    shp = full.shape
    t = full.reshape(shp[:ax] + (N_DEV, shp[ax] // N_DEV) + shp[ax + 1:])
    return jnp.moveaxis(t, ax, 0)


def _owner_pieces(layers, ax):
    per_dev = layers[0].size // N_DEV
    if per_dev % PACK_COLS == 0:
        return [_split_shards(g[None], ax) for g in layers]
    return [_split_shards(jnp.stack(layers), ax)]


def _join_shards(parts, ax):
    t = jnp.moveaxis(parts, 0, ax)
    shp = t.shape
    return t.reshape(shp[:ax] + (shp[ax] * shp[ax + 1],) + shp[ax + 2:])


def _unpack(buf, shapes, lead=None):
    out, off = [], 0
    nl = 1 if lead is None else lead
    buf = buf.reshape(nl, -1, PACK_COLS)
    for s in shapes:
        n = math.prod(s)
        r = -(-n // PACK_COLS)
        blk = buf[:, off:off + r]
        if n != r * PACK_COLS:
            blk = blk.reshape(nl, r * PACK_COLS)[:, :n]
        out.append(blk.reshape(tuple(s) if lead is None else (lead,) + tuple(s)))
        off += r
    return out


def _row(v):
    return v.reshape(1, -1)


def _head_mats(D):
    e = (lax.broadcasted_iota(jnp.int32, (D, D // HEAD), 0) // HEAD
         == lax.broadcasted_iota(jnp.int32, (D, D // HEAD), 1)).astype(f32)
    return e, e.T


def rms_fwd(x, g, name, out_dtype=f32):
    return rowwise(lambda xv, gv: (_rms_fn(xv, gv),), [x], [g], name=name, out_dtype=out_dtype)[0]


def rms_bwd(x, g, dh, add, name):
    return rowwise_bwd(lambda xv, gv: (_rms_fn(xv, gv),), [x], [g], [dh], name=name, n_drow=1, n_dpar=1, add0=add)


def rwkv_fwd(x, p, vf, tag, gathers=()):
    vres = vf is not None
    D = x.shape[1]
    e, et = _head_mats(D)
    h = rms_fwd(x, p['norm_g'], tag + "_norm")
    xr, xw, xk, xv, xa, xg = colwise(_mix_fn, [(h, 0), (p['mu'], 0)], [h.shape[0]] * 6, name=tag + "_mix",
                                     nblk=D // LANE, out_dtype=bf16)
    r = mm(xr, p['w_r'], name=tag + "_r")
    k = mm(xk, p['w_k'], name=tag + "_k")
    v = mm(xv, p['w_v'], name=tag + "_v")
    th = mm(xw, p['w1'], name=tag + "_w1", act='tanh')
    lw = mm(th, p['w2'], name=tag + "_w2")
    t2 = mm(xa, p['a1'], name=tag + "_a1", out_dtype=bf16)
    aa = mm(t2, p['a2'], name=tag + "_a2")
    sg = mm(xg, p['g1'], name=tag + "_g1", act='sigmoid')
    gg = mm(sg, p['g2'], name=tag + "_g2")
    rows = [k, v, lw, aa]
    pars = [p['w0'], p['a0'], p['k_k'], p['k_a']]
    t4 = None
    if vres:
        t4 = mm(xv, p['v1'], name=tag + "_v1", out_dtype=bf16)
        vv = mm(t4, p['v2'], name=tag + "_v2")
        rows += [vv, vf]
        pars += [p['v0']]
    pars += [e, et]
    mid = functools.partial(_mid_fn, vres)
    decay, a, kk, k2, v2 = rowwise(mid, rows, pars, name=tag + "_mid")
    br, kr = rowwise(_head_dots_fn, [r, k2, kk, a], [e, et], name=tag + "_hdots")
    y, states, sas, *gathered = scan_fwd(r, decay, k2, v2, kk, a, br, kr, name=tag + "_scan", gathers=gathers)
    post_rows = [y, r, k2, v2, gg]
    post_pars = [p['ln_g'], p['ln_b'], p['r_k'], e, et]
    z = rowwise(_post_fn, post_rows, post_pars, name=tag + "_post", out_dtype=bf16)[0]
    xo = mm(z, p['w_o'], name=tag + "_o", res=x)
    saved = dict(x=x, h=h, xs=(xr, xw, xk, xv, xa, xg), r=r, th=th, t2=t2, sg=sg, t4=t4, mid_rows=rows, mid_pars=pars,
                 mid=mid, scan_in=(r, decay, k2, v2, kk, a, br, kr), states=(states, sas), post_rows=post_rows, post_pars=post_pars,
                 z=z, vres=vres)
    return xo, v2, saved, gathered


def rwkv_bwd(dxo, dvf_in, p, s, tag, exchanges=()):
    D = dxo.shape[1]
    g = {}
    xr, xw, xk, xv, xa, xg = s['xs']
    dz = mm(dxo, p['w_o'], name=tag + "_bo", tb=True)
    g['w_o'] = mm(s['z'], dxo, name=tag + "_bwo", ta=True)
    dy, dr1, dk1, dv1, dgg, g['ln_g'], g['ln_b'], g['r_k'] = rowwise_bwd(
        _post_fn, s['post_rows'], s['post_pars'], [dz], name=tag + "_bpost", n_drow=5, n_dpar=3)
    if dvf_in is not None:
        dv1 = rowwise(_add_fn, [dv1, dvf_in], [], name=tag + "_bvadd")[0]
    dsg = mm(dgg, p['g2'], name=tag + "_bg2", tb=True)
    g['g2'] = mm(s['sg'], dgg, name=tag + "_bwg2", ta=True)
    dt3 = rowwise(_dsig_fn, [dsg, s['sg']], [], name=tag + "_bdsig")[0]
    dxg = mm(dt3, p['g1'], name=tag + "_bg1", tb=True)
    g['g1'] = mm(xg, dt3, name=tag + "_bwg1", ta=True)
    dr, dw, dk2, dv2, dkk, da, *exchanged = scan_bwd(*s['scan_in'], dy, *s['states'], dr1, dk1, dv1,
                                                     name=tag + "_bscan", exchanges=exchanges)
    vres = s['vres']
    n_drow = 6 if vres else 4
    n_dpar = 5 if vres else 4
    outs = rowwise_bwd(s['mid'], s['mid_rows'], s['mid_pars'], [dw, da, dkk, dk2, dv2], name=tag + "_bmid",
                       n_drow=n_drow, n_dpar=n_dpar)
    dk, dv, dlw, daa = outs[:4]
    dvf = None
    if vres:
        dvv, dvf = outs[4:6]
        g['w0'], g['a0'], g['k_k'], g['k_a'], g['v0'] = outs[6:]
    else:
        g['w0'], g['a0'], g['k_k'], g['k_a'] = outs[4:]
    dth = mm(dlw, p['w2'], name=tag + "_bw2", tb=True)
    g['w2'] = mm(s['th'], dlw, name=tag + "_bww2", ta=True)
    dt1 = rowwise(_dtanh_fn, [dth, s['th']], [], name=tag + "_bdtanh")[0]
    dxw = mm(dt1, p['w1'], name=tag + "_bw1", tb=True)
    g['w1'] = mm(xw, dt1, name=tag + "_bww1", ta=True)
    dt2 = mm(daa, p['a2'], name=tag + "_ba2", tb=True)
    g['a2'] = mm(s['t2'], daa, name=tag + "_bwa2", ta=True)
    dxa = mm(dt2, p['a1'], name=tag + "_ba1", tb=True)
    g['a1'] = mm(xa, dt2, name=tag + "_bwa1", ta=True)
    dxv = mm(dv, p['w_v'], name=tag + "_bv", tb=True)
    g['w_v'] = mm(xv, dv, name=tag + "_bwv", ta=True)
    if vres:
        dt4 = mm(dvv, p['v2'], name=tag + "_bv2", tb=True)
        g['v2'] = mm(s['t4'], dvv, name=tag + "_bwv2", ta=True)
        dxv = mm(dt4, p['v1'], name=tag + "_bv1", tb=True, res=dxv)
        g['v1'] = mm(xv, dt4, name=tag + "_bwv1", ta=True)
    dxr = mm(dr, p['w_r'], name=tag + "_br", tb=True)
    g['w_r'] = mm(xr, dr, name=tag + "_bwr", ta=True)
    dxk = mm(dk, p['w_k'], name=tag + "_bk", tb=True)
    g['w_k'] = mm(xk, dk, name=tag + "_bwk", ta=True)
    T = dxo.shape[0]
    dh, dmu = colwise(_mix_bwd_fn, [(s['h'], 0), (p['mu'], 0), (dxr, 0), (dxw, 0), (dxk, 0), (dxv, 0), (dxa, 0),
                                    (dxg, 0)], [T, 8], name=tag + "_bmix", nblk=D // LANE)
    g['mu'] = dmu[:6]
    dx, g['norm_g'] = rms_bwd(s['x'], p['norm_g'], dh, dxo, tag + "_bnorm")
    return dx, dvf, g, exchanged


def conv_fwd(x, p, tag):
    T, D = x.shape
    nb = D // LANE
    kw = p['dw'].shape[0]
    h = rms_fwd(x, p['norm_g'], tag + "_norm", bf16)
    u = mm(h, p['w_in'], name=tag + "_in", bias=p['b_in'])
    c = colwise(functools.partial(_glu_conv_fn, kw), [(u, 0), (u, nb), (p['dw'], 0), (p['dw_b'], 0)], [T],
                name=tag + "_dw", nblk=nb)[0]
    sl = rowwise(_ln_silu_fn, [c], [p['ln_g'], p['ln_b']], name=tag + "_ln", out_dtype=bf16)[0]
    xo = mm(sl, p['w_out'], name=tag + "_out", bias=p['b_out'], res=x)
    return xo, dict(x=x, h=h, u=u, c=c, sl=sl)


def conv_bwd(dxo, p, s, tag):
    T, D = dxo.shape
    nb = D // LANE
    kw = p['dw'].shape[0]
    kpad = -(-kw // 8) * 8
    g = {}
    dsl = mm(dxo, p['w_out'], name=tag + "_bout", tb=True)
    g['w_out'] = mm(s['sl'], dxo, name=tag + "_bwout", ta=True)
    g['b_out'] = rowwise_bwd(_bias_fn, [dxo], [p['b_out']], [dxo], name=tag + "_bbout", n_drow=0, n_dpar=1)[0]
    dc, g['ln_g'], g['ln_b'] = rowwise_bwd(_ln_silu_fn, [s['c']], [p['ln_g'], p['ln_b']], [dsl], name=tag + "_bln",
                                           n_drow=1, n_dpar=2)
    u = s['u']
    du1, du2, ddw, g['dw_b'] = colwise(functools.partial(_glu_conv_bwd_fn, kw, kpad),
                                       [(u, 0), (u, nb), (p['dw'], 0), (dc, 0)], [T, T, kpad, 1],
                                       name=tag + "_bdw", nblk=nb)
    g['dw'] = ddw[:kw]
    du = jnp.concatenate([du1, du2], axis=1)
    g['b_in'] = rowwise_bwd(_bias_fn, [du], [p['b_in']], [du], name=tag + "_bbin", n_drow=0, n_dpar=1)[0]
    dh = mm(du, p['w_in'], name=tag + "_bin", tb=True)
    g['w_in'] = mm(s['h'], du, name=tag + "_bwin", ta=True)
    dx, g['norm_g'] = rms_bwd(s['x'], p['norm_g'], dh, dxo, tag + "_bnorm")
    return dx, g


def xattn_fwd(x, memn, p, tag):
    hn = rms_fwd(x, p['norm_g'], tag + "_norm", bf16)
    q = mm(hn, p['w_q'], name=tag + "_q", out_dtype=bf16)
    kv = mm(memn, p['w_kv'], name=tag + "_kv", out_dtype=bf16)
    o = attn_fwd(q, kv, name=tag + "_attn")
    xo = mm(o, p['w_o'], name=tag + "_o", res=x)
    return xo, dict(x=x, hn=hn, q=q, kv=kv, o=o)


def xattn_bwd(dxo, dmemn, memn, p, s, tag):
    g = {}
    do = mm(dxo, p['w_o'], name=tag + "_bo", tb=True)
    g['w_o'] = mm(s['o'], dxo, name=tag + "_bwo", ta=True)
    dq, dk, dv = attn_bwd(s['q'], s['kv'], do, name=tag + "_battn")
    dkv = jnp.concatenate([dk, dv], axis=1)
    dmemn = mm(dkv, p['w_kv'], name=tag + "_bkv", tb=True, res=dmemn)
    g['w_kv'] = mm(memn, dkv, name=tag + "_bwkv", ta=True)
    dhn = mm(dq, p['w_q'], name=tag + "_bq", tb=True)
    g['w_q'] = mm(s['hn'], dq, name=tag + "_bwq", ta=True)
    dx, g['norm_g'] = rms_bwd(s['x'], p['norm_g'], dhn, dxo, tag + "_bnorm")
    return dx, dmemn, g


def ffn_fwd(x, p, tag):
    T, D = x.shape
    w_dev, layer = p['w_in']
    nb = (N_DEV // 2) * w_dev.shape[2] // LANE
    kw = p['dw'].shape[0]
    hn = rms_fwd(x, p['norm_g'], tag + "_norm", bf16)
    u = mm(hn, w_dev, name=tag + "_in", b_dev=(layer * D, D))
    act = colwise(functools.partial(_ffn_act_fn, kw), [(u, 0), (u, nb), (p['dw'], 0), (p['dw'], nb)], [T],
                  name=tag + "_act", nblk=nb, out_dtype=bf16)[0]
    xo = mm(act, p['w_out'], name=tag + "_out", res=x)
    return xo, dict(x=x, hn=hn, u=u, act=act)


def ffn_bwd(dxo, p, s, tag):
    T, D = dxo.shape
    w_dev, layer = p['w_in']
    nb = (N_DEV // 2) * w_dev.shape[2] // LANE
    kw = p['dw'].shape[0]
    g = {}
    dact = mm(dxo, p['w_out'], name=tag + "_bout", tb=True)
    g['w_out'] = mm(s['act'], dxo, name=tag + "_bwout", ta=True)
    u = s['u']
    dug, duv, dwg, dwv = colwise(functools.partial(_ffn_act_bwd_fn, kw, 8),
                                 [(u, 0), (u, nb), (p['dw'], 0), (p['dw'], nb), (dact, 0)], [T, T, 8, 8],
                                 name=tag + "_bact", nblk=nb)
    g['dw'] = jnp.concatenate([dwg[:kw], dwv[:kw]], axis=1)
    du = jnp.concatenate([dug, duv], axis=1)
    dhn = mm(du, w_dev, name=tag + "_bin", tb=True, b_dev=(layer * D, D))
    g['w_in'] = mm(s['hn'], du, name=tag + "_bwin", ta=True, out_dev=True)
    dx, g['norm_g'] = rms_bwd(s['x'], p['norm_g'], dhn, dxo, tag + "_bnorm")
    return dx, g


def _lane_pad(n):
    return -(-n // LANE) * LANE


def _pad_blocks(a, axis, nblk):
    shp = a.shape
    n = shp[axis] // nblk
    t = a.reshape(shp[:axis] + (nblk, n) + shp[axis + 1:])
    pad = [(0, 0)] * t.ndim
    pad[axis + 1] = (0, _lane_pad(n) - n)
    t = jnp.pad(t, pad)
    return t.reshape(shp[:axis] + (nblk * _lane_pad(n),) + shp[axis + 1:])


def _unpad_blocks(a, axis, nblk, n):
    shp = a.shape
    t = a.reshape(shp[:axis] + (nblk, shp[axis] // nblk) + shp[axis + 1:])
    t = lax.slice_in_dim(t, 0, n, axis=axis + 1)
    return t.reshape(shp[:axis] + (nblk * n,) + shp[axis + 1:])


def _layer_params(W, layer):
    ia = ib = layer // 2
    mixer = {}
    if layer % 2 == 0:
        mixer = dict(norm_g=_row(W['norm_mix_g'][layer]), mu=W['rwkv_mu'][ia], w_r=W['rwkv_w_r'][ia],
                     w_k=W['rwkv_w_k'][ia], w_v=W['rwkv_w_v'][ia], w_o=W['rwkv_w_o'][ia], w0=_row(W['rwkv_w0'][ia]),
                     w1=W['rwkv_w1'][ia], w2=W['rwkv_w2'][ia], a0=_row(W['rwkv_a0'][ia]), a1=W['rwkv_a1'][ia],
                     a2=W['rwkv_a2'][ia], g1=W['rwkv_g1'][ia], g2=W['rwkv_g2'][ia], k_k=_row(W['rwkv_k_k'][ia]),
                     k_a=_row(W['rwkv_k_a'][ia]), r_k=_row(W['rwkv_r_k'][ia]), ln_g=_row(W['rwkv_ln_g'][ia]),
                     ln_b=_row(W['rwkv_ln_b'][ia]))
        if ia > 0:
            mixer.update(v0=_row(W['rwkv_v0'][ia - 1]), v1=W['rwkv_v1'][ia - 1], v2=W['rwkv_v2'][ia - 1])
    else:
        mixer = dict(norm_g=_row(W['norm_mix_g'][layer]), w_in=W['conv_w_in'][ib], b_in=_row(W['conv_b_in'][ib]),
                     dw=W['conv_dw'][ib], dw_b=_row(W['conv_dw_b'][ib]), ln_g=_row(W['conv_ln_g'][ib]),
                     ln_b=_row(W['conv_ln_b'][ib]), w_out=W['conv_w_out'][ib], b_out=_row(W['conv_b_out'][ib]))
    return mixer


def _rest_params(W, layer):
    xat = dict(norm_g=_row(W['norm_xattn_g'][layer]), w_q=W['xattn_w_q'][layer], w_kv=W['xattn_w_kv'][layer],
               w_o=W['xattn_w_o'][layer])
    ffn = dict(norm_g=_row(W['norm_ffn_g'][layer]), w_in=(W['ffn_w_in'], layer),
               dw=_pad_blocks(W['ffn_dw'][layer], 1, N_DEV), w_out=_pad_blocks(W['ffn_w_out'][layer], 0, N_DEV // 2))
    return xat, ffn


NATIVE = 'ffn_w_in'
EARLY = [n for n in W_NAMES if W_SPEC[n][0] is not None and (n.startswith('rwkv_') or not W_SPEC[n][1])]
LATE = [n for n in W_NAMES if W_SPEC[n][0] is not None and n not in EARLY and n != NATIVE]


def _native_rows(a, dtype):
    L, D, n = a.shape
    return jnp.pad(a.astype(dtype), ((0, 0), (0, 0), (0, _lane_pad(n) - n))).reshape(L * D, _lane_pad(n))


def _unpack_full(got, local, names):
    parts = _unpack(got, [local[n].shape for n in names], lead=N_DEV)
    return {n: _join_shards(part, W_SPEC[n][0]) for n, part in zip(names, parts)}


def _gather_early(local):
    full = {n: local[n] for n in W_NAMES if W_SPEC[n][0] is None}
    for as_bf16, dtype, tag in ((True, bf16, "ag_mat"), (False, f32, "ag_vec")):
        names = [n for n in EARLY if W_SPEC[n][1] == as_bf16]
        full.update(_unpack_full(all_gather(_pack([local[n] for n in names], dtype), name=tag), local, names))
    return full


def _step(local, x, mem, tgt):
    W = _gather_early(local)
    late_bufs = (_pack([local[n] for n in LATE], bf16), _native_rows(local[NATIVE], bf16))
    depth = W['norm_mix_g'].shape[0]
    g_mem = _row(W['mem_norm_g'])
    memn = rms_fwd(mem, g_mem, "mem_norm", bf16)
    layers, saved = [], []
    vf = None
    for l in range(depth):
        if l % 2 == 0:
            pm = _layer_params(W, l)
            x, v, sm, gathered = rwkv_fwd(x, pm, vf, f"rw{l}", gathers=late_bufs if l == 0 else ())
            if l == 0:
                W.update(_unpack_full(gathered[0], local, LATE))
                W[NATIVE] = gathered[1]
            if vf is None:
                vf = v
        else:
            pm = _layer_params(W, l)
            x, sm = conv_fwd(x, pm, f"cv{l}")
        px, pf = _rest_params(W, l)
        x, sx = xattn_fwd(x, memn, px, f"xa{l}")
        x, sf = ffn_fwd(x, pf, f"ff{l}")
        layers.append((pm, px, pf))
        saved.append((sm, sx, sf))
    g_fin = _row(W['final_norm_g'])
    dx, dg_fin, loss_blk = final_loss(x, tgt, g_fin, name="final_loss")

    grads = {n: [None] * local[n].shape[0] for n in W_NAMES if local[n].ndim >= 2}
    grads['final_norm_g'] = dg_fin.reshape(-1)
    n_in = local[NATIVE].shape[2]
    dmemn = jnp.zeros(memn.shape, f32)
    dvf = None
    for l in reversed(range(depth)):
        pm, px, pf = layers[l]
        sm, sx, sf = saved[l]
        dx, gf = ffn_bwd(dx, pf, sf, f"ff{l}")
        dx, dmemn, gx = xattn_bwd(dx, dmemn, memn, px, sx, f"xa{l}")
        grads['norm_ffn_g'][l] = gf['norm_g'].reshape(-1)
        grads['ffn_w_in'][l] = gf['w_in']
        grads['ffn_dw'][l] = _unpad_blocks(gf['dw'], 1, N_DEV, n_in)
        grads['ffn_w_out'][l] = _unpad_blocks(gf['w_out'], 0, N_DEV // 2, n_in)
        grads['norm_xattn_g'][l] = gx['norm_g'].reshape(-1)
        grads['xattn_w_q'][l], grads['xattn_w_kv'][l], grads['xattn_w_o'][l] = gx['w_q'], gx['w_kv'], gx['w_o']
        i = l // 2
        if l % 2 == 0:
            pre = ()
            if l == 0:
                late = _pack([p for n in LATE for p in _owner_pieces(grads[n], W_SPEC[n][0])], f32, lead=N_DEV)
                native = jnp.concatenate(grads[NATIVE], axis=1)
                pre = (chip_partials(late, name="rs_late"), chip_partials(native, name="rs_ffn_in"))
            dx, dvf_l, gm, exchanged = rwkv_bwd(dx, dvf if i == 0 else None, pm, sm, f"rw{l}", exchanges=pre)
            if l == 0:
                late_parts, native_parts = exchanged
            if dvf_l is not None:
                dvf = dvf_l if dvf is None else rowwise(_add_fn, [dvf, dvf_l], [], name=f"rw{l}_dvfadd")[0]
            for short in ('mu', 'w_r', 'w_k', 'w_v', 'w_o', 'w1', 'w2', 'a1', 'a2', 'g1', 'g2'):
                grads['rwkv_' + short][i] = gm[short]
            for short in ('w0', 'a0', 'k_k', 'k_a', 'ln_g', 'ln_b'):
                grads['rwkv_' + short][i] = gm[short].reshape(-1)
            grads['rwkv_r_k'][i] = gm['r_k'].reshape(W['rwkv_r_k'].shape[1:])
            if i > 0:
                grads['rwkv_v0'][i - 1] = gm['v0'].reshape(-1)
                grads['rwkv_v1'][i - 1], grads['rwkv_v2'][i - 1] = gm['v1'], gm['v2']
        else:
            dx, gm = conv_bwd(dx, pm, sm, f"cv{l}")
            for short in ('w_in', 'dw', 'w_out'):
                grads['conv_' + short][i] = gm[short]
            for short in ('b_in', 'dw_b', 'ln_g', 'ln_b', 'b_out'):
                grads['conv_' + short][i] = gm[short].reshape(-1)
        grads['norm_mix_g'][l] = gm['norm_g'].reshape(-1)
    _, dg_mem = rowwise_bwd(lambda xv, gv: (_rms_fn(xv, gv),), [mem], [g_mem], [dmemn], name="mem_norm_b",
                            n_drow=1, n_dpar=1)
    grads['mem_norm_g'] = dg_mem.reshape(-1)
    early_grads = {n: grads[n] for n in EARLY}
    repl_grads = {n: (jnp.stack(gv) if isinstance(gv, list) else gv) for n, gv in grads.items() if W_SPEC[n][0] is None}
    return loss_blk[0, 0], dx, early_grads, repl_grads, late_parts, native_parts


def kernel(x, mem, mem_norm_g, norm_mix_g, norm_xattn_g, norm_ffn_g, final_norm_g, rwkv_mu, rwkv_w_r, rwkv_w_k, rwkv_w_v, rwkv_w_o, rwkv_w0, rwkv_w1, rwkv_w2, rwkv_a0, rwkv_a1, rwkv_a2, rwkv_g1, rwkv_g2, rwkv_k_k, rwkv_k_a, rwkv_r_k, rwkv_ln_g, rwkv_ln_b, rwkv_v0, rwkv_v1, rwkv_v2, conv_w_in, conv_b_in, conv_dw, conv_dw_b, conv_ln_g, conv_ln_b, conv_w_out, conv_b_out, xattn_w_q, xattn_w_kv, xattn_w_o, ffn_w_in, ffn_dw, ffn_w_out, loss_target, m_mem_norm_g, m_norm_mix_g, m_norm_xattn_g, m_norm_ffn_g, m_final_norm_g, m_rwkv_mu, m_rwkv_w_r, m_rwkv_w_k, m_rwkv_w_v, m_rwkv_w_o, m_rwkv_w0, m_rwkv_w1, m_rwkv_w2, m_rwkv_a0, m_rwkv_a1, m_rwkv_a2, m_rwkv_g1, m_rwkv_g2, m_rwkv_k_k, m_rwkv_k_a, m_rwkv_r_k, m_rwkv_ln_g, m_rwkv_ln_b, m_rwkv_v0, m_rwkv_v1, m_rwkv_v2, m_conv_w_in, m_conv_b_in, m_conv_dw, m_conv_dw_b, m_conv_ln_g, m_conv_ln_b, m_conv_w_out, m_conv_b_out, m_xattn_w_q, m_xattn_w_kv, m_xattn_w_o, m_ffn_w_in, m_ffn_dw, m_ffn_w_out, v_mem_norm_g, v_norm_mix_g, v_norm_xattn_g, v_norm_ffn_g, v_final_norm_g, v_rwkv_mu, v_rwkv_w_r, v_rwkv_w_k, v_rwkv_w_v, v_rwkv_w_o, v_rwkv_w0, v_rwkv_w1, v_rwkv_w2, v_rwkv_a0, v_rwkv_a1, v_rwkv_a2, v_rwkv_g1, v_rwkv_g2, v_rwkv_k_k, v_rwkv_k_a, v_rwkv_r_k, v_rwkv_ln_g, v_rwkv_ln_b, v_rwkv_v0, v_rwkv_v1, v_rwkv_v2, v_conv_w_in, v_conv_b_in, v_conv_dw, v_conv_dw_b, v_conv_ln_g, v_conv_ln_b, v_conv_w_out, v_conv_b_out, v_xattn_w_q, v_xattn_w_kv, v_xattn_w_o, v_ffn_w_in, v_ffn_dw, v_ffn_w_out):
    given = dict(locals())
    local = {n: given[n] for n in W_NAMES}
    loss_local, dx, early_grads, grads, late_parts, native_parts = _step(local, x[0], mem[0], loss_target[0])
    loss = lax.psum(loss_local, ("x", "y", "c"))

    repl = [n for n in W_NAMES if W_SPEC[n][0] is None]
    out = {}
    kinds = ("grad_", "delta_", "new_m_", "new_v_")

    res = adamw(native_parts, *[_native_rows(given[pre + NATIVE], f32) for pre in ("", "m_", "v_")],
                name="adamw_ffn_in")
    shp = given[NATIVE].shape
    for kind, buf in zip(kinds, res):
        out[kind + NATIVE] = buf.reshape(shp[0], shp[1], -1)[:, :, :shp[2]]

    early_parts = reduce_scatter(
        _pack([p for n in EARLY for p in _owner_pieces(early_grads[n], W_SPEC[n][0])], f32, lead=N_DEV), name="rs_early")
    for names, parts, tag in ((LATE, late_parts, "adamw_late"), (EARLY, early_parts, "adamw_early")):
        res = adamw(parts, *[_pack([given[pre + n] for n in names], f32) for pre in ("", "m_", "v_")], name=tag)
        for kind, buf in zip(kinds, res):
            for n, arr in zip(names, _unpack(buf, [given[n].shape for n in names])):
                out[kind + n] = arr

    parts = all_gather(_pack([grads[n] for n in repl], f32), name="grad_gather_repl")
    res = adamw(parts, *[_pack([given[pre + n] for n in repl], f32) for pre in ("", "m_", "v_")],
                name="adamw_repl")
    for kind, buf in zip(("grad_", "delta_", "new_m_", "new_v_"), res):
        for n, arr in zip(repl, _unpack(buf, [given[n].shape for n in repl])):
            out[kind + n] = arr

    return (loss, dx[None], *[out[kind + n] for kind in ("grad_", "delta_", "new_m_", "new_v_") for n in W_NAMES])
```

```python
import functools
import math

import jax
import jax.numpy as jnp
from jax import lax
from jax.experimental import pallas as pl
from jax.experimental.pallas import tpu as pltpu

f32 = jnp.float32
bf16 = jnp.bfloat16

N_DEV = 8
HEAD = 64
XATTN_HEADS = 4
NORM_EPS = 1e-6
LN_EPS = 1e-5
GN_EPS = 64e-5
ADAM_LR, ADAM_B1, ADAM_B2, ADAM_EPS, ADAM_WD, ADAM_STEP = 0.001, 0.9, 0.999, 1e-08, 0.01, 10
LANE = 128
PACK_COLS = 1024
PACK_ROWS = 256
VMEM_LIMIT = 48 * 1024 * 1024
SCAN_CHUNK = 16
SCAN_SPLIT = 1
MXU_DIM = 256

W_SPEC = {
    'mem_norm_g': (None, False), 'norm_mix_g': (None, False), 'norm_xattn_g': (None, False),
    'norm_ffn_g': (None, False), 'final_norm_g': (None, False),
    'rwkv_mu': (2, False), 'rwkv_w_r': (1, True), 'rwkv_w_k': (1, True), 'rwkv_w_v': (1, True),
    'rwkv_w_o': (1, True), 'rwkv_w0': (None, False), 'rwkv_w1': (1, True), 'rwkv_w2': (2, True),
    'rwkv_a0': (None, False), 'rwkv_a1': (1, True), 'rwkv_a2': (2, True), 'rwkv_g1': (1, True),
    'rwkv_g2': (2, True), 'rwkv_k_k': (None, False), 'rwkv_k_a': (None, False), 'rwkv_r_k': (None, False),
    'rwkv_ln_g': (None, False), 'rwkv_ln_b': (None, False), 'rwkv_v0': (None, False),
    'rwkv_v1': (1, True), 'rwkv_v2': (2, True),
    'conv_w_in': (2, True), 'conv_b_in': (1, False), 'conv_dw': (2, False), 'conv_dw_b': (1, False),
    'conv_ln_g': (1, False), 'conv_ln_b': (1, False), 'conv_w_out': (1, True), 'conv_b_out': (1, False),
    'xattn_w_q': (1, True), 'xattn_w_kv': (2, True), 'xattn_w_o': (1, True),
    'ffn_w_in': (2, True), 'ffn_dw': (2, False), 'ffn_w_out': (1, True),
}
W_NAMES = list(W_SPEC)


def _tile(n, prefs):
    for p in prefs:
        if n % p == 0:
            return p
    return n


def _cparams(sem):
    return pltpu.CompilerParams(dimension_semantics=sem, vmem_limit_bytes=VMEM_LIMIT)


def _sigmoid(x):
    return 1.0 / (1.0 + jnp.exp(-x))


def _softplus(x):
    return jnp.maximum(x, 0.0) + jnp.log(1.0 + jnp.exp(-jnp.abs(x)))


def mm(a, b, *, name, ta=False, tb=False, bias=None, res=None, act=None, b_dev=None, out_dev=False, out_dtype=f32):
    M, K = (a.shape[1], a.shape[0]) if ta else a.shape
    tm = _tile(M, (1024, 512, 256, 128))
    if b_dev is None:
        N = b.shape[0] if tb else b.shape[1]
        assert (b.shape[1] if tb else b.shape[0]) == K, (name, a.shape, b.shape)
        tn = _tile(N, (1024, 512, 256, 128))
        tk = _tile(K, (1024, 512, 256, 128))
    else:
        b_off, b_rows = b_dev
        width = b.shape[2]
        if tb:
            N, tk = b_rows, width
            tn = _tile(N, (1024, 512, 256, 128))
            assert K == N_DEV * width and b_off % tn == 0, (name, a.shape, b.shape)
        else:
            N, tn = N_DEV * width, width
            tk = _tile(K, (1024, 512, 256, 128))
            assert K == b_rows and b_off % tk == 0, (name, a.shape, b.shape)
    if out_dev:
        tn = N // N_DEV
    nk = K // tk
    dims = (((0 if ta else 1,), (1 if tb else 0,)), ((), ()))
    has_bias, has_res = bias is not None, res is not None

    def body(*refs):
        a_ref, b_ref = refs[0], refs[1]
        pos = 2
        bias_ref = res_ref = None
        if has_bias:
            bias_ref = refs[pos]; pos += 1
        if has_res:
            res_ref = refs[pos]; pos += 1
        o_ref, acc_ref = refs[pos], refs[pos + 1]
        kstep = pl.program_id(2)

        @pl.when(kstep == 0)
        def _():
            acc_ref[...] = jnp.zeros_like(acc_ref)

        acc_ref[...] += lax.dot_general(a_ref[...].astype(bf16), b_ref[...].astype(bf16), dims,
                                        preferred_element_type=f32)

        @pl.when(kstep == nk - 1)
        def _():
            out = acc_ref[...]
            if has_bias:
                out = out + bias_ref[...]
            if act == 'tanh':
                out = jnp.tanh(out)
            elif act == 'sigmoid':
                out = _sigmoid(out)
            if has_res:
                out = out + res_ref[...]
            o_ref[...] = out.astype(o_ref.dtype)

    a_spec = pl.BlockSpec((tk, tm), lambda i, j, k: (k, i)) if ta else pl.BlockSpec((tm, tk), lambda i, j, k: (i, k))
    if b_dev is None:
        b_spec = pl.BlockSpec((tn, tk), lambda i, j, k: (j, k)) if tb else pl.BlockSpec((tk, tn), lambda i, j, k: (k, j))
    elif tb:
        b_spec = pl.BlockSpec((None, tn, tk), lambda i, j, k: (k, b_off // tn + j, 0))
    else:
        b_spec = pl.BlockSpec((None, tk, tn), lambda i, j, k: (j, b_off // tk + k, 0))
    in_specs, args = [a_spec, b_spec], [a, b]
    if has_bias:
        in_specs.append(pl.BlockSpec((1, tn), lambda i, j, k: (0, j))); args.append(bias)
    if has_res:
        in_specs.append(pl.BlockSpec((tm, tn), lambda i, j, k: (i, j))); args.append(res)
    if out_dev:
        out_spec = pl.BlockSpec((None, tm, tn), lambda i, j, k: (j, i, 0))
        out_shape = jax.ShapeDtypeStruct((N_DEV, M, tn), out_dtype)
    else:
        out_spec = pl.BlockSpec((tm, tn), lambda i, j, k: (i, j))
        out_shape = jax.ShapeDtypeStruct((M, N), out_dtype)
    return pl.pallas_call(
        body, name=name, grid=(M // tm, N // tn, nk), in_specs=in_specs,
        out_specs=out_spec, out_shape=out_shape,
        scratch_shapes=[pltpu.VMEM((tm, tn), f32)],
        compiler_params=_cparams(("parallel", "parallel", "arbitrary")),
    )(*args)


def rowwise(fn, rows, pars, *, name, tt=256, out_dtype=f32):
    T = rows[0].shape[0]
    tt = min(tt, T)
    nr, npar = len(rows), len(pars)
    outs = jax.eval_shape(fn, *[jax.ShapeDtypeStruct((tt, r.shape[1]), r.dtype) for r in rows],
                          *[jax.ShapeDtypeStruct(p.shape, p.dtype) for p in pars])

    def body(*refs):
        res = fn(*[r[...] for r in refs[:nr + npar]])
        for o_ref, o in zip(refs[nr + npar:], res):
            o_ref[...] = o.astype(o_ref.dtype)

    return pl.pallas_call(
        body, name=name, grid=(T // tt,),
        in_specs=[pl.BlockSpec((tt, r.shape[1]), lambda i: (i, 0)) for r in rows]
        + [pl.BlockSpec(p.shape, lambda i: (0, 0)) for p in pars],
        out_specs=[pl.BlockSpec((tt, o.shape[1]), lambda i: (i, 0)) for o in outs],
        out_shape=[jax.ShapeDtypeStruct((T, o.shape[1]), out_dtype) for o in outs],
        compiler_params=_cparams(("parallel",)),
    )(*rows, *pars)


def rowwise_bwd(fn, rows, pars, cots, *, name, n_drow, n_dpar, add0=None, tt=128):
    T = rows[0].shape[0]
    tt = min(tt, T)
    nr, npar, nc = len(rows), len(pars), len(cots)
    has_add = add0 is not None

    def body(*refs):
        rv = [r[...] for r in refs[:nr]]
        pv = [r[...] for r in refs[nr:nr + npar]]
        cv = [r[...] for r in refs[nr + npar:nr + npar + nc]]
        pos = nr + npar + nc
        add_ref = None
        if has_add:
            add_ref = refs[pos]; pos += 1
        drow_refs = refs[pos:pos + n_drow]
        dpar_refs = refs[pos + n_drow:pos + n_drow + n_dpar]

        def f(*d):
            return fn(*d[:n_drow], *rv[n_drow:], *d[n_drow:], *pv[n_dpar:])

        _, vjp = jax.vjp(f, *rv[:n_drow], *pv[:n_dpar])
        g = vjp(tuple(cv))
        for k in range(n_drow):
            gk = g[k]
            if k == 0 and has_add:
                gk = gk + add_ref[...]
            drow_refs[k][...] = gk

        @pl.when(pl.program_id(0) == 0)
        def _():
            for k in range(n_dpar):
                dpar_refs[k][...] = jnp.zeros_like(dpar_refs[k])

        for k in range(n_dpar):
            dpar_refs[k][...] += g[n_drow + k]

    row_spec = lambda r: pl.BlockSpec((tt, r.shape[1]), lambda i: (i, 0))
    par_spec = lambda p: pl.BlockSpec(p.shape, lambda i: (0, 0))
    in_specs = [row_spec(r) for r in rows] + [par_spec(p) for p in pars] + [row_spec(c) for c in cots]
    args = [*rows, *pars, *cots]
    if has_add:
        in_specs.append(row_spec(add0)); args.append(add0)
    return pl.pallas_call(
        body, name=name, grid=(T // tt,), in_specs=in_specs,
        out_specs=[row_spec(r) for r in rows[:n_drow]] + [par_spec(p) for p in pars[:n_dpar]],
        out_shape=[jax.ShapeDtypeStruct(r.shape, f32) for r in rows[:n_drow]]
        + [jax.ShapeDtypeStruct(p.shape, f32) for p in pars[:n_dpar]],
        compiler_params=_cparams(("arbitrary",)),
    )(*args)


def colwise(fn, cols, out_rows, *, name, nblk, out_dtype=f32):
    def body(*refs):
        res = fn(*[r[...] for r in refs[:len(cols)]])
        for o_ref, o in zip(refs[len(cols):], res):
            o_ref[...] = o.astype(o_ref.dtype)

    def spec(rows, off):
        return pl.BlockSpec((rows, LANE), lambda j: (0, j + off))

    return pl.pallas_call(
        body, name=name, grid=(nblk,),
        in_specs=[spec(a.shape[0], off) for a, off in cols],
        out_specs=[spec(r, 0) for r in out_rows],
        out_shape=[jax.ShapeDtypeStruct((r, nblk * LANE), out_dtype) for r in out_rows],
        compiler_params=_cparams(("parallel",)),
    )(*[a for a, _ in cols])


def _shift_dn(x, s):
    if s == 0:
        return x
    rid = lax.broadcasted_iota(jnp.int32, x.shape, 0)
    return jnp.where(rid >= s, pltpu.roll(x, s, 0), 0.0)


def _shift_up(x, s):
    if s == 0:
        return x
    n = x.shape[0]
    rid = lax.broadcasted_iota(jnp.int32, x.shape, 0)
    return jnp.where(rid < n - s, pltpu.roll(x, n - s, 0), 0.0)


def _colsum(x):
    return jnp.sum(x, axis=0, keepdims=True)


def _stack_rows(rows, n):
    c = rows[0].shape[1]
    rid = lax.broadcasted_iota(jnp.int32, (n, c), 0)
    out = jnp.zeros((n, c), f32)
    for i, r in enumerate(rows):
        out = jnp.where(rid == i, jnp.broadcast_to(r, (n, c)), out)
    return out


def _dwconv(x, w, kw):
    acc = None
    for k in range(kw):
        term = w[k:k + 1, :] * _shift_dn(x, kw - 1 - k)
        acc = term if acc is None else acc + term
    return acc


def _dwconv_bwd(x, w, dy, kw, pad_rows):
    dx = None
    rows = []
    for k in range(kw):
        s = kw - 1 - k
        rows.append(_colsum(dy * _shift_dn(x, s)))
        term = w[k:k + 1, :] * _shift_up(dy, s)
        dx = term if dx is None else dx + term
    return dx, _stack_rows(rows, pad_rows)


def _mix_fn(h, mu):
    xx = _shift_dn(h, 1) - h
    return tuple(h + xx * mu[i:i + 1, :] for i in range(6))


def _mix_bwd_fn(h, mu, *ds):
    xx = _shift_dn(h, 1) - h
    s1 = ds[0]
    s2 = ds[0] * mu[0:1, :]
    rows = [_colsum(ds[0] * xx)]
    for i in range(1, 6):
        s1 = s1 + ds[i]
        s2 = s2 + ds[i] * mu[i:i + 1, :]
        rows.append(_colsum(ds[i] * xx))
    return s1 - s2 + _shift_up(s2, 1), _stack_rows(rows, 8)


def _glu_conv_fn(kw, u1, u2, w, b):
    return (_dwconv(u1 * _sigmoid(u2), w, kw) + b,)


def _glu_conv_bwd_fn(kw, pad_rows, u1, u2, w, dc):
    sig = _sigmoid(u2)
    g = u1 * sig
    dg, dw = _dwconv_bwd(g, w, dc, kw, pad_rows)
    return dg * sig, dg * g * (1.0 - sig), dw, _colsum(dc)


def _ffn_act_fn(kw, ug, uv, wg, wv):
    gc = _dwconv(ug, wg, kw)
    vc = _dwconv(uv, wv, kw)
    return (gc * _sigmoid(gc) * vc,)


def _ffn_act_bwd_fn(kw, pad_rows, ug, uv, wg, wv, dact):
    gc = _dwconv(ug, wg, kw)
    vc = _dwconv(uv, wv, kw)
    sg = _sigmoid(gc)
    dvc = dact * gc * sg
    dgc = dact * vc * (sg * (1.0 + gc * (1.0 - sg)))
    dug, dwg = _dwconv_bwd(ug, wg, dgc, kw, pad_rows)
    duv, dwv = _dwconv_bwd(uv, wv, dvc, kw, pad_rows)
    return dug, duv, dwg, dwv


def _rms_fn(x, g):
    return x * lax.rsqrt(jnp.mean(x * x, axis=-1, keepdims=True) + NORM_EPS) * g


def _hsum(x, e, et):
    s = jnp.dot(x, e, precision=lax.Precision.HIGH, preferred_element_type=f32)
    return jnp.dot(s, et, precision=lax.Precision.HIGH, preferred_element_type=f32)


def _mid_fn(vres, k, v, lw, aa, *rest):
    if vres:
        vv, vf, w0, a0, k_k, k_a, v0, e, et = rest
    else:
        w0, a0, k_k, k_a, e, et = rest
    logw = -_softplus(-(w0 + lw)) - 0.5
    decay = jnp.exp(-jnp.exp(logw))
    a = _sigmoid(a0 + aa)
    kk = k * k_k
    kk = kk / jnp.maximum(jnp.sqrt(_hsum(kk * kk, e, et)), 1e-12)
    k2 = k * (1.0 + (a - 1.0) * k_a)
    v2 = v + (vf - v) * _sigmoid(v0 + vv) if vres else v
    return decay, a, kk, k2, v2


def _post_fn(y, r, k2, v2, gg, ln_g, ln_b, rk, e, et):
    inv = 1.0 / HEAD
    yc = y - _hsum(y, e, et) * inv
    var = _hsum(yc * yc, e, et) * inv
    yn = yc * lax.rsqrt(var + GN_EPS) * ln_g + ln_b
    bonus = _hsum(r * k2 * rk, e, et) * v2
    return ((yn + bonus) * gg,)


def _ln_silu_fn(c, g, b):
    mu = jnp.mean(c, axis=-1, keepdims=True)
    var = jnp.mean(jnp.square(c - mu), axis=-1, keepdims=True)
    ln = (c - mu) * lax.rsqrt(var + LN_EPS) * g + b
    return (ln * _sigmoid(ln),)


def _bias_fn(x, b):
    return (x + b,)


def _dtanh_fn(d, th):
    return (d * (1.0 - th * th),)


def _dsig_fn(d, sg):
    return (d * sg * (1.0 - sg),)


def _add_fn(a, b):
    return (a + b,)


def _seg(blocks, bd, coarse=()):
    def side_by_side(parts):
        h = len(parts) // 2
        return jnp.concatenate([jnp.concatenate(parts[:h], axis=0), jnp.concatenate(parts[h:], axis=0)], axis=1)

    def apart(res, count):
        h = count // 2
        return ([res[i * HEAD:(i + 1) * HEAD, :LANE] for i in range(h)]
                + [res[i * HEAD:(i + 1) * HEAD, LANE:] for i in range(h)])

    x = side_by_side(blocks)
    n = x.shape[0]
    h0 = x.astype(bf16)
    h1 = (x - h0.astype(f32)).astype(bf16)
    lhs = [h0, h1] + ([side_by_side(coarse).astype(bf16)] if coarse else [])
    out = jnp.dot(jnp.concatenate(lhs, axis=0), bd, preferred_element_type=f32)
    fine = apart(out[n:2 * n] + out[0:n], len(blocks))
    return fine + (apart(out[2 * n:], len(coarse)) if coarse else [])


def _scan_consts():
    li = lax.broadcasted_iota(jnp.int32, (MXU_DIM, MXU_DIM), 0) // HEAD
    lj = lax.broadcasted_iota(jnp.int32, (MXU_DIM, MXU_DIM), 1) // HEAD
    bd = (li == lj).astype(bf16)
    si = lax.broadcasted_iota(jnp.int32, (HEAD, LANE), 0)
    sj = lax.broadcasted_iota(jnp.int32, (HEAD, LANE), 1) % HEAD
    dg = (si == sj).astype(f32)
    return bd, dg


def _scan_dims(T, D):
    return D // LANE, min(SCAN_CHUNK, T)


def _scan_groups(G):
    n = -(-G // SCAN_SPLIT)
    return [list(range(i, min(i + n, G))) for i in range(0, G, n)]


def _head_dots_fn(r, k, kk, a, e, et):
    return _hsum(kk * a * r, e, et), _hsum(k * r, e, et)


def scan_fwd(r, w, k, v, kk, a, br, kr, *, name, gathers=()):
    T, D = r.shape
    G, tc = _scan_dims(T, D)
    nch = T // tc
    ng = len(gathers)
    bd, dg = _scan_consts()

    def body(*refs):
        r_ref, w_ref, k_ref, v_ref, kk_ref, a_ref, br_ref, kr_ref, bd_ref, dg_ref = refs[:10]
        y_ref, st_ref, sa_ref = refs[10 + ng:13 + ng]
        s_ref = refs[13 + 2 * ng]
        jobs = [(refs[10 + i], refs[13 + ng + i], *refs[14 + 2 * ng + 3 * i:17 + 2 * ng + 3 * i]) for i in range(ng)]

        @pl.when(pl.program_id(0) == 0)
        def _():
            s_ref[...] = jnp.zeros_like(s_ref)
            for job in jobs:
                _ag_start(*job)

        bdv, dgv = bd_ref[...], dg_ref[...]

        def step(t, carry):
            row = pl.ds(t, 1)
            rr, ww, kr_, vr, kkr, ar, brr, krr = (x[row, :] for x in (r_ref, w_ref, k_ref, v_ref, kk_ref, a_ref,
                                                                    br_ref, kr_ref))
            bb = kkr * ar
            wr = ww * rr
            sl = [slice(g * LANE, (g + 1) * LANE) for g in range(G)]
            ps = [s_ref[g] for g in range(G)]
            results = []
            for grp in _scan_groups(G):
                blocks = [ps[g] * (-kkr[:, sl[g]]) for g in grp]
                vds = [ps[g] * wr[:, sl[g]] for g in grp]
                vds += [jnp.broadcast_to(vr[:, sl[g]], (HEAD, LANE)) * dgv for g in grp]
                results.append(_seg(blocks, bdv, vds))
            yrows = []
            for grp, res in zip(_scan_groups(G), results):
                n = len(grp)
                for i, g in enumerate(grp):
                    sab, ub, vb = res[i], res[n + i], res[2 * n + i]
                    sn = ps[g] * ww[:, sl[g]] + sab * bb[:, sl[g]] + vb * kr_[:, sl[g]]
                    s_ref[g] = sn
                    st_ref[t, g] = sn
                    sa_ref[t, g] = sab
                    yb = ub + sab * brr[:, sl[g]] + vb * krr[:, sl[g]]
                    yrows.append(_colsum(yb * dgv))
            y_ref[row, :] = jnp.concatenate(yrows, axis=1)
            return carry

        lax.fori_loop(0, tc, step, 0)

        @pl.when(pl.program_id(0) == nch - 1)
        def _():
            for job in jobs:
                _ag_finish(*job)

    vec = pl.BlockSpec((tc, D), lambda c: (c, 0))
    big = pl.BlockSpec((tc, G, HEAD, LANE), lambda c: (c, 0, 0, 0))
    hbm = pl.BlockSpec(memory_space=pl.ANY)
    return pl.pallas_call(
        body, name=name, grid=(nch,),
        in_specs=[vec] * 8 + [pl.BlockSpec((MXU_DIM, MXU_DIM), lambda c: (0, 0)), pl.BlockSpec((HEAD, LANE), lambda c: (0, 0))]
        + [hbm] * ng,
        out_specs=[vec, big, big] + [hbm] * ng,
        out_shape=[jax.ShapeDtypeStruct((T, D), f32)] + [jax.ShapeDtypeStruct((T, G, HEAD, LANE), f32)] * 2
        + [jax.ShapeDtypeStruct((N_DEV,) + x.shape, x.dtype) for x in gathers],
        scratch_shapes=[pltpu.VMEM((G, HEAD, LANE), f32)] + _comm_scratch(AG_SEMS) * ng,
        compiler_params=_cparams(("arbitrary",)),
    )(r, w, k, v, kk, a, br, kr, bd, dg, *gathers)


def scan_bwd(r, w, k, v, kk, a, br, kr, dy, states, sas, dr0, dk0, dv0, *, name, exchanges=()):
    T, D = r.shape
    G, tc = _scan_dims(T, D)
    nch = T // tc
    ne = len(exchanges)
    bd, dg = _scan_consts()

    def body(*refs):
        (r_ref, w_ref, k_ref, v_ref, kk_ref, a_ref, br_ref, kr_ref, dy_ref, st_ref, prev_ref, sa_ref,
         dr0_ref, dk0_ref, dv0_ref, bd_ref, dg_ref) = refs[:17]
        dr_ref, dw_ref, dk_ref, dv_ref, dkk_ref, da_ref = refs[17 + ne:23 + ne]
        ds_ref = refs[23 + 2 * ne]
        jobs = [(refs[17 + i], refs[23 + ne + i], *refs[24 + 2 * ne + 3 * i:27 + 2 * ne + 3 * i]) for i in range(ne)]

        @pl.when(pl.program_id(0) == 0)
        def _():
            ds_ref[...] = jnp.zeros_like(ds_ref)
            for job in jobs:
                _cx_start(*job)

        bdv, dgv = bd_ref[...], dg_ref[...]

        def step_at(t, ps):
            row = pl.ds(t, 1)
            rr, ww, kr_, vr, kkr, ar, brr, krr, dyr = (x[row, :] for x in (r_ref, w_ref, k_ref, v_ref, kk_ref, a_ref,
                                                                         br_ref, kr_ref, dy_ref))
            bb = kkr * ar
            sl = [slice(g * LANE, (g + 1) * LANE) for g in range(G)]
            dr_rows, dw_rows, dk_rows, dv_rows, dkk_rows, da_rows = [], [], [], [], [], []
            dss = [ds_ref[g] for g in range(G)]
            sabs = [sa_ref[t, g] for g in range(G)]
            sts = [st_ref[t, g] for g in range(G)]
            results = []
            for grp in _scan_groups(G):
                blocks = [dss[g] * bb[:, sl[g]] for g in grp]
                diag = [dss[g] * kr_[:, sl[g]] for g in grp]
                diag += [jnp.broadcast_to(vr[:, sl[g]], (HEAD, LANE)) * dgv for g in grp]
                diag += [jnp.broadcast_to(dyr[:, sl[g]], (HEAD, LANE)) * dgv for g in grp]
                results.append(_seg(blocks, bdv, diag))
            for grp, res in zip(_scan_groups(G), results):
                n = len(grp)
                for i, g in enumerate(grp):
                    sab, vb, dyb = sabs[g], res[2 * n + i], res[3 * n + i]
                    dsab = res[i] + dyb * brr[:, sl[g]]
                    dvb = res[n + i] + dyb * krr[:, sl[g]]
                    dst = dss[g] + dyb * rr[:, sl[g]]
                    dr_rows.append(_colsum(sts[g] * dyb))
                    dw_rows.append(_colsum(dst * ps[g]))
                    db_row = _colsum(dst * sab)
                    dk_rows.append(_colsum(dst * vb))
                    dv_rows.append(_colsum(dvb * dgv))
                    ds_ref[g] = dst * ww[:, sl[g]] - dsab * kkr[:, sl[g]]
                    dkk_rows.append(db_row * ar[:, sl[g]] - _colsum(ps[g] * dsab))
                    da_rows.append(db_row * kkr[:, sl[g]])
            cat = lambda rows: jnp.concatenate(rows, axis=1)
            dr_ref[row, :] = cat(dr_rows) + dr0_ref[row, :]
            dw_ref[row, :] = cat(dw_rows)
            dk_ref[row, :] = cat(dk_rows) + dk0_ref[row, :]
            dv_ref[row, :] = cat(dv_rows) + dv0_ref[row, :]
            dkk_ref[row, :] = cat(dkk_rows)
            da_ref[row, :] = cat(da_rows)

        def step(i, carry):
            t = tc - 1 - i
            step_at(t, [st_ref[t - 1, g] for g in range(G)])
            return carry

        lax.fori_loop(0, tc - 1, step, 0)
        first = (pl.program_id(0) < nch - 1).astype(f32)
        step_at(0, [prev_ref[0, g] * first for g in range(G)])

        @pl.when(pl.program_id(0) == nch - 1)
        def _():
            for job in jobs:
                _cx_finish(*job)

    vec = pl.BlockSpec((tc, D), lambda c: (nch - 1 - c, 0))
    big = pl.BlockSpec((tc, G, HEAD, LANE), lambda c: (nch - 1 - c, 0, 0, 0))
    prev = pl.BlockSpec((1, G, HEAD, LANE), lambda c: (jnp.maximum((nch - 1 - c) * tc - 1, 0), 0, 0, 0))
    hbm = pl.BlockSpec(memory_space=pl.ANY)
    return pl.pallas_call(
        body, name=name, grid=(nch,),
        in_specs=[vec] * 9 + [big, prev, big] + [vec] * 3
        + [pl.BlockSpec((MXU_DIM, MXU_DIM), lambda c: (0, 0)), pl.BlockSpec((HEAD, LANE), lambda c: (0, 0))] + [hbm] * ne,
        out_specs=[vec] * 6 + [hbm] * ne,
        out_shape=[jax.ShapeDtypeStruct((T, D), f32)] * 6 + [jax.ShapeDtypeStruct(x.shape, x.dtype) for x in exchanges],
        scratch_shapes=[pltpu.VMEM((G, HEAD, LANE), f32)] + _comm_scratch(CX_SEMS) * ne,
        compiler_params=_cparams(("arbitrary",)),
    )(r, w, k, v, kk, a, br, kr, dy, states, states, sas, dr0, dk0, dv0, bd, dg, *exchanges)


def _attn_p(q, k, scale):
    s = lax.dot_general(q.astype(bf16), k.astype(bf16), (((1,), (1,)), ((), ())), preferred_element_type=f32) * scale
    s = s - jnp.max(s, axis=-1, keepdims=True)
    p = jnp.exp(s)
    return p / jnp.sum(p, axis=-1, keepdims=True)


def attn_fwd(q, kv, *, name):
    T, D = q.shape
    M = kv.shape[0]
    hd = D // XATTN_HEADS
    scale = hd ** -0.5
    tq = _tile(T, (512, 256, 128))

    def body(q_ref, k_ref, v_ref, o_ref):
        p = _attn_p(q_ref[...], k_ref[...], scale)
        o_ref[...] = jnp.dot(p.astype(bf16), v_ref[...].astype(bf16), preferred_element_type=f32).astype(o_ref.dtype)

    return pl.pallas_call(
        body, name=name, grid=(XATTN_HEADS, T // tq),
        in_specs=[pl.BlockSpec((tq, hd), lambda h, i: (i, h)), pl.BlockSpec((M, hd), lambda h, i: (0, h)),
                  pl.BlockSpec((M, hd), lambda h, i: (0, XATTN_HEADS + h))],
        out_specs=pl.BlockSpec((tq, hd), lambda h, i: (i, h)),
        out_shape=jax.ShapeDtypeStruct((T, D), bf16),
        compiler_params=_cparams(("parallel", "parallel")),
    )(q, kv, kv)


def attn_bwd(q, kv, do, *, name):
    T, D = q.shape
    M = kv.shape[0]
    hd = D // XATTN_HEADS
    scale = hd ** -0.5
    tq = _tile(T, (512, 256, 128))

    def body(q_ref, k_ref, v_ref, do_ref, dq_ref, dk_ref, dv_ref):
        qv, kvv, vv, dov = q_ref[...], k_ref[...], v_ref[...], do_ref[...]
        p = _attn_p(qv, kvv, scale)
        dob = dov.astype(bf16)
        dp = lax.dot_general(dob, vv.astype(bf16), (((1,), (1,)), ((), ())), preferred_element_type=f32)
        ds = p * (dp - jnp.sum(dp * p, axis=-1, keepdims=True)) * scale
        dsb = ds.astype(bf16)
        dq_ref[...] = jnp.dot(dsb, kvv.astype(bf16), preferred_element_type=f32)

        @pl.when(pl.program_id(1) == 0)
        def _():
            dk_ref[...] = jnp.zeros_like(dk_ref)
            dv_ref[...] = jnp.zeros_like(dv_ref)

        dk_ref[...] += lax.dot_general(dsb, qv.astype(bf16), (((0,), (0,)), ((), ())), preferred_element_type=f32)
        dv_ref[...] += lax.dot_general(p.astype(bf16), dob, (((0,), (0,)), ((), ())), preferred_element_type=f32)

    qspec = pl.BlockSpec((tq, hd), lambda h, i: (i, h))
    mspec = pl.BlockSpec((M, hd), lambda h, i: (0, h))
    return pl.pallas_call(
        body, name=name, grid=(XATTN_HEADS, T // tq),
        in_specs=[qspec, mspec, pl.BlockSpec((M, hd), lambda h, i: (0, XATTN_HEADS + h)), qspec],
        out_specs=[qspec, mspec, mspec],
        out_shape=[jax.ShapeDtypeStruct((T, D), f32), jax.ShapeDtypeStruct((M, D), f32),
                   jax.ShapeDtypeStruct((M, D), f32)],
        compiler_params=_cparams(("parallel", "arbitrary")),
    )(q, kv, kv, do)


def final_loss(x, tgt, g, *, name):
    T, D = x.shape
    tt = min(256, T)

    def body(x_ref, t_ref, g_ref, dx_ref, dg_ref, loss_ref):
        tv = t_ref[...]

        def f(xv, gv):
            e = _rms_fn(xv, gv) - tv
            return 0.5 * jnp.sum(jnp.mean(e * e, axis=-1))

        val, vjp = jax.vjp(f, x_ref[...], g_ref[...])
        dx, dgv = vjp(jnp.ones((), f32))
        dx_ref[...] = dx

        @pl.when(pl.program_id(0) == 0)
        def _():
            dg_ref[...] = jnp.zeros_like(dg_ref)
            loss_ref[...] = jnp.zeros_like(loss_ref)

        dg_ref[...] += dgv
        loss_ref[...] += jnp.full(loss_ref.shape, val, f32)

    row = pl.BlockSpec((tt, D), lambda i: (i, 0))
    return pl.pallas_call(
        body, name=name, grid=(T // tt,),
        in_specs=[row, row, pl.BlockSpec((1, D), lambda i: (0, 0))],
        out_specs=[row, pl.BlockSpec((1, D), lambda i: (0, 0)), pl.BlockSpec((8, LANE), lambda i: (0, 0))],
        out_shape=[jax.ShapeDtypeStruct((T, D), f32), jax.ShapeDtypeStruct((1, D), f32),
                   jax.ShapeDtypeStruct((8, LANE), f32)],
        compiler_params=_cparams(("arbitrary",)),
    )(x, tgt, g)


def _place():
    x, y, c = lax.axis_index("x"), lax.axis_index("y"), lax.axis_index("c")
    chips = [(1 - x, y), (x, 1 - y), (1 - x, 1 - y)]
    return x, y, c, chips


def _rcopy(src, dst, send_sems, recv_sems, k, dev):
    return pltpu.make_async_remote_copy(src_ref=src, dst_ref=dst, send_sem=send_sems.at[k], recv_sem=recv_sems.at[k],
                                        device_id=dev, device_id_type=pl.DeviceIdType.MESH)


def _comm_call(body, name, x, out_shape, n_sems):
    return pl.pallas_call(
        body, name=name, out_shape=out_shape,
        in_specs=[pl.BlockSpec(memory_space=pl.ANY)], out_specs=pl.BlockSpec(memory_space=pl.ANY),
        scratch_shapes=_comm_scratch(n_sems),
    )(x)


AG_SEMS = 7
CX_SEMS = 3


def _comm_scratch(n_sems):
    return [pltpu.SemaphoreType.DMA((n_sems,)), pltpu.SemaphoreType.DMA((n_sems,)), pltpu.SemaphoreType.DMA]


def _ag_first(x_ref, o_ref, send_sems, recv_sems, local_sem):
    x_, y_, c_, chips = _place()
    me = o_ref.at[4 * x_ + 2 * y_ + c_]
    copies = [pltpu.make_async_copy(x_ref, me, local_sem), _rcopy(x_ref, me, send_sems, recv_sems, 0, (x_, y_, 1 - c_))]
    copies += [_rcopy(x_ref, me, send_sems, recv_sems, 1 + j, (*chip, c_)) for j, chip in enumerate(chips)]
    return copies


def _ag_start(x_ref, o_ref, send_sems, recv_sems, local_sem):
    for cp in _ag_first(x_ref, o_ref, send_sems, recv_sems, local_sem):
        cp.start()


def _ag_finish(x_ref, o_ref, send_sems, recv_sems, local_sem):
    x_, y_, c_, chips = _place()
    sibling = (x_, y_, 1 - c_)
    slot = lambda px, py, pc: o_ref.at[4 * px + 2 * py + pc]
    passed = [_rcopy(slot(*chip, c_), slot(*chip, c_), send_sems, recv_sems, 4 + j, sibling)
              for j, chip in enumerate(chips)]
    for j, chip in enumerate(chips):
        _rcopy(x_ref, slot(*chip, c_), send_sems, recv_sems, 1 + j, (*chip, c_)).wait_recv()
        passed[j].start()
    _rcopy(x_ref, slot(x_, y_, 1 - c_), send_sems, recv_sems, 0, sibling).wait_recv()
    for j, chip in enumerate(chips):
        _rcopy(x_ref, slot(*chip, 1 - c_), send_sems, recv_sems, 4 + j, sibling).wait_recv()
    first = _ag_first(x_ref, o_ref, send_sems, recv_sems, local_sem)
    for cp in first[1:] + passed:
        cp.wait_send()
    first[0].wait()


def all_gather(x, *, name):
    def body(*refs):
        _ag_start(*refs)
        _ag_finish(*refs)

    return _comm_call(body, name, x, jax.ShapeDtypeStruct((N_DEV,) + x.shape, x.dtype), AG_SEMS)


def pair_exchange(x, *, name):
    n = x.shape[0]

    def body(x_ref, o_ref, send_sems, recv_sems, local_sem):
        x_, y_, c_, _ = _place()
        copies = [_rcopy(x_ref.at[q, 1 - c_], o_ref.at[q], send_sems, recv_sems, q, (x_, y_, 1 - c_)) for q in range(n)]
        for cp in copies:
            cp.start()
        for cp in copies:
            cp.wait()

    return _comm_call(body, name, x, jax.ShapeDtypeStruct((n,) + x.shape[2:], x.dtype), n)


def _cx_copies(x_ref, o_ref, send_sems, recv_sems, local_sem):
    x_, y_, c_, chips = _place()
    myq = 2 * x_ + y_
    copies = [pltpu.make_async_copy(x_ref.at[myq], o_ref.at[myq], local_sem)]
    copies += [_rcopy(x_ref.at[2 * px + py], o_ref.at[myq], send_sems, recv_sems, j, (px, py, c_))
               for j, (px, py) in enumerate(chips)]
    return copies


def _cx_start(x_ref, o_ref, send_sems, recv_sems, local_sem):
    for cp in _cx_copies(x_ref, o_ref, send_sems, recv_sems, local_sem):
        cp.start()


def _cx_finish(x_ref, o_ref, send_sems, recv_sems, local_sem):
    x_, y_, c_, chips = _place()
    myq = 2 * x_ + y_
    for j, (px, py) in enumerate(chips):
        _rcopy(x_ref.at[myq], o_ref.at[2 * px + py], send_sems, recv_sems, j, (px, py, c_)).wait_recv()
    copies = _cx_copies(x_ref, o_ref, send_sems, recv_sems, local_sem)
    for cp in copies[1:]:
        cp.wait_send()
    copies[0].wait()


def chip_exchange(x, *, name):
    def body(*refs):
        _cx_start(*refs)
        _cx_finish(*refs)

    return _comm_call(body, name, x, jax.ShapeDtypeStruct(x.shape, x.dtype), CX_SEMS)


def _add_cast(a, b, *, name):
    n, _, R, C = a.shape
    tr = min(PACK_ROWS // 2, R)

    def body(a_ref, b_ref, o_ref):
        c = lax.axis_index("c")
        for q in range(n):
            o_ref[q] = (a_ref[q, c] + b_ref[q]).astype(bf16)

    return pl.pallas_call(
        body, name=name, grid=(R // tr,),
        in_specs=[pl.BlockSpec((n, 2, tr, C), lambda i: (0, 0, i, 0)), pl.BlockSpec((n, tr, C), lambda i: (0, i, 0))],
        out_specs=pl.BlockSpec((n, tr, C), lambda i: (0, i, 0)),
        out_shape=jax.ShapeDtypeStruct(b.shape, bf16), compiler_params=_cparams(("parallel",)))(a, b)


def chip_partials(pieces, *, name):
    n, R, C = pieces.shape
    by_core = pieces.reshape(n // 2, 2, R, C)
    return _add_cast(by_core, pair_exchange(by_core, name=name + "_pair"), name=name + "_add")


def reduce_scatter(pieces, *, name):
    return chip_exchange(chip_partials(pieces, name=name), name=name + "_chip")


def adamw(gparts, w, m, v, *, name, row_off=0):
    R, C = w.shape
    n_parts = gparts.shape[0]
    tr = math.gcd(math.gcd(R, row_off), PACK_ROWS)
    assert tr % 16 == 0, (name, R, row_off)
    c1 = 1.0 / (1.0 - ADAM_B1 ** ADAM_STEP)
    c2 = 1.0 / (1.0 - ADAM_B2 ** ADAM_STEP)

    def body(g_ref, w_ref, m_ref, v_ref, go_ref, d_ref, mo_ref, vo_ref):
        g = g_ref[0].astype(f32)
        for i in range(1, n_parts):
            g = g + g_ref[i].astype(f32)
        mn = ADAM_B1 * m_ref[...] + (1.0 - ADAM_B1) * g
        vn = ADAM_B2 * v_ref[...] + (1.0 - ADAM_B2) * (g * g)
        go_ref[...] = g
        mo_ref[...] = mn
        vo_ref[...] = vn
        d_ref[...] = -ADAM_LR * ((mn * c1) / (jnp.sqrt(vn * c2) + ADAM_EPS) + ADAM_WD * w_ref[...])

    blk = pl.BlockSpec((tr, C), lambda i: (i, 0))
    return pl.pallas_call(
        body, name=name, grid=(R // tr,),
        in_specs=[pl.BlockSpec((n_parts, tr, C), lambda i: (0, row_off // tr + i, 0)), blk, blk, blk],
        out_specs=[blk] * 4, out_shape=[jax.ShapeDtypeStruct((R, C), f32)] * 4,
        compiler_params=_cparams(("parallel",)),
    )(gparts, w, m, v)


def _pack(arrs, dtype, lead=None):
    nl = 1 if lead is None else lead
    blocks = []
    for a in arrs:
        n = a.size // nl
        r = -(-n // PACK_COLS)
        a = a.astype(dtype)
        if n != r * PACK_COLS:
            a = jnp.pad(a.reshape(nl, n), ((0, 0), (0, r * PACK_COLS - n)))
        blocks.append(a.reshape(nl, r, PACK_COLS))
    rows = sum(b.shape[1] for b in blocks)
    tot = -(-rows // PACK_ROWS) * PACK_ROWS
    if tot != rows:
        blocks.append(jnp.zeros((nl, tot - rows, PACK_COLS), dtype))
    buf = jnp.concatenate(blocks, axis=1)
    return buf[0] if lead is None else buf


def _split_shards(full, ax):
    shp = full.shape
    t = full.reshape(shp[:ax] + (N_DEV, shp[ax] // N_DEV) + shp[ax + 1:])
    return jnp.moveaxis(t, ax, 0)


def _owner_pieces(layers, ax):
    per_dev = layers[0].size // N_DEV
    if per_dev % PACK_COLS == 0:
        return [_split_shards(g[None], ax) for g in layers]
    return [_split_shards(jnp.stack(layers), ax)]


def _join_shards(parts, ax):
    t = jnp.moveaxis(parts, 0, ax)
    shp = t.shape
    return t.reshape(shp[:ax] + (shp[ax] * shp[ax + 1],) + shp[ax + 2:])


def _unpack(buf, shapes, lead=None):
    out, off = [], 0
    nl = 1 if lead is None else lead
    buf = buf.reshape(nl, -1, PACK_COLS)
    for s in shapes:
        n = math.prod(s)
        r = -(-n // PACK_COLS)
        blk = buf[:, off:off + r]
        if n != r * PACK_COLS:
            blk = blk.reshape(nl, r * PACK_COLS)[:, :n]
        out.append(blk.reshape(tuple(s) if lead is None else (lead,) + tuple(s)))
        off += r
    return out


def _row(v):
    return v.reshape(1, -1)


def _head_mats(D):
    e = (lax.broadcasted_iota(jnp.int32, (D, D // HEAD), 0) // HEAD
         == lax.broadcasted_iota(jnp.int32, (D, D // HEAD), 1)).astype(f32)
    return e, e.T


def rms_fwd(x, g, name, out_dtype=f32):
    return rowwise(lambda xv, gv: (_rms_fn(xv, gv),), [x], [g], name=name, out_dtype=out_dtype)[0]


def rms_bwd(x, g, dh, add, name):
    return rowwise_bwd(lambda xv, gv: (_rms_fn(xv, gv),), [x], [g], [dh], name=name, n_drow=1, n_dpar=1, add0=add)


def rwkv_fwd(x, p, vf, tag, gathers=()):
    vres = vf is not None
    D = x.shape[1]
    e, et = _head_mats(D)
    h = rms_fwd(x, p['norm_g'], tag + "_norm")
    xr, xw, xk, xv, xa, xg = colwise(_mix_fn, [(h, 0), (p['mu'], 0)], [h.shape[0]] * 6, name=tag + "_mix",
                                     nblk=D // LANE, out_dtype=bf16)
    r = mm(xr, p['w_r'], name=tag + "_r")
    k = mm(xk, p['w_k'], name=tag + "_k")
    v = mm(xv, p['w_v'], name=tag + "_v")
    th = mm(xw, p['w1'], name=tag + "_w1", act='tanh')
    lw = mm(th, p['w2'], name=tag + "_w2")
    t2 = mm(xa, p['a1'], name=tag + "_a1", out_dtype=bf16)
    aa = mm(t2, p['a2'], name=tag + "_a2")
    sg = mm(xg, p['g1'], name=tag + "_g1", act='sigmoid')
    gg = mm(sg, p['g2'], name=tag + "_g2")
    rows = [k, v, lw, aa]
    pars = [p['w0'], p['a0'], p['k_k'], p['k_a']]
    t4 = None
    if vres:
        t4 = mm(xv, p['v1'], name=tag + "_v1", out_dtype=bf16)
        vv = mm(t4, p['v2'], name=tag + "_v2")
        rows += [vv, vf]
        pars += [p['v0']]
    pars += [e, et]
    mid = functools.partial(_mid_fn, vres)
    decay, a, kk, k2, v2 = rowwise(mid, rows, pars, name=tag + "_mid")
    br, kr = rowwise(_head_dots_fn, [r, k2, kk, a], [e, et], name=tag + "_hdots")
    y, states, sas, *gathered = scan_fwd(r, decay, k2, v2, kk, a, br, kr, name=tag + "_scan", gathers=gathers)
    post_rows = [y, r, k2, v2, gg]
    post_pars = [p['ln_g'], p['ln_b'], p['r_k'], e, et]
    z = rowwise(_post_fn, post_rows, post_pars, name=tag + "_post", out_dtype=bf16)[0]
    xo = mm(z, p['w_o'], name=tag + "_o", res=x)
    saved = dict(x=x, h=h, xs=(xr, xw, xk, xv, xa, xg), r=r, th=th, t2=t2, sg=sg, t4=t4, mid_rows=rows, mid_pars=pars,
                 mid=mid, scan_in=(r, decay, k2, v2, kk, a, br, kr), states=(states, sas), post_rows=post_rows, post_pars=post_pars,
                 z=z, vres=vres)
    return xo, v2, saved, gathered


def rwkv_bwd(dxo, dvf_in, p, s, tag, exchanges=()):
    D = dxo.shape[1]
    g = {}
    xr, xw, xk, xv, xa, xg = s['xs']
    dz = mm(dxo, p['w_o'], name=tag + "_bo", tb=True)
    g['w_o'] = mm(s['z'], dxo, name=tag + "_bwo", ta=True)
    dy, dr1, dk1, dv1, dgg, g['ln_g'], g['ln_b'], g['r_k'] = rowwise_bwd(
        _post_fn, s['post_rows'], s['post_pars'], [dz], name=tag + "_bpost", n_drow=5, n_dpar=3)
    if dvf_in is not None:
        dv1 = rowwise(_add_fn, [dv1, dvf_in], [], name=tag + "_bvadd")[0]
    dsg = mm(dgg, p['g2'], name=tag + "_bg2", tb=True)
    g['g2'] = mm(s['sg'], dgg, name=tag + "_bwg2", ta=True)
    dt3 = rowwise(_dsig_fn, [dsg, s['sg']], [], name=tag + "_bdsig")[0]
    dxg = mm(dt3, p['g1'], name=tag + "_bg1", tb=True)
    g['g1'] = mm(xg, dt3, name=tag + "_bwg1", ta=True)
    dr, dw, dk2, dv2, dkk, da, *exchanged = scan_bwd(*s['scan_in'], dy, *s['states'], dr1, dk1, dv1,
                                                     name=tag + "_bscan", exchanges=exchanges)
    vres = s['vres']
    n_drow = 6 if vres else 4
    n_dpar = 5 if vres else 4
    outs = rowwise_bwd(s['mid'], s['mid_rows'], s['mid_pars'], [dw, da, dkk, dk2, dv2], name=tag + "_bmid",
                       n_drow=n_drow, n_dpar=n_dpar)
    dk, dv, dlw, daa = outs[:4]
    dvf = None
    if vres:
        dvv, dvf = outs[4:6]
        g['w0'], g['a0'], g['k_k'], g['k_a'], g['v0'] = outs[6:]
    else:
        g['w0'], g['a0'], g['k_k'], g['k_a'] = outs[4:]
    dth = mm(dlw, p['w2'], name=tag + "_bw2", tb=True)
    g['w2'] = mm(s['th'], dlw, name=tag + "_bww2", ta=True)
    dt1 = rowwise(_dtanh_fn, [dth, s['th']], [], name=tag + "_bdtanh")[0]
    dxw = mm(dt1, p['w1'], name=tag + "_bw1", tb=True)
    g['w1'] = mm(xw, dt1, name=tag + "_bww1", ta=True)
    dt2 = mm(daa, p['a2'], name=tag + "_ba2", tb=True)
    g['a2'] = mm(s['t2'], daa, name=tag + "_bwa2", ta=True)
    dxa = mm(dt2, p['a1'], name=tag + "_ba1", tb=True)
    g['a1'] = mm(xa, dt2, name=tag + "_bwa1", ta=True)
    dxv = mm(dv, p['w_v'], name=tag + "_bv", tb=True)
    g['w_v'] = mm(xv, dv, name=tag + "_bwv", ta=True)
    if vres:
        dt4 = mm(dvv, p['v2'], name=tag + "_bv2", tb=True)
        g['v2'] = mm(s['t4'], dvv, name=tag + "_bwv2", ta=True)
        dxv = mm(dt4, p['v1'], name=tag + "_bv1", tb=True, res=dxv)
        g['v1'] = mm(xv, dt4, name=tag + "_bwv1", ta=True)
    dxr = mm(dr, p['w_r'], name=tag + "_br", tb=True)
    g['w_r'] = mm(xr, dr, name=tag + "_bwr", ta=True)
    dxk = mm(dk, p['w_k'], name=tag + "_bk", tb=True)
    g['w_k'] = mm(xk, dk, name=tag + "_bwk", ta=True)
    T = dxo.shape[0]
    dh, dmu = colwise(_mix_bwd_fn, [(s['h'], 0), (p['mu'], 0), (dxr, 0), (dxw, 0), (dxk, 0), (dxv, 0), (dxa, 0),
                                    (dxg, 0)], [T, 8], name=tag + "_bmix", nblk=D // LANE)
    g['mu'] = dmu[:6]
    dx, g['norm_g'] = rms_bwd(s['x'], p['norm_g'], dh, dxo, tag + "_bnorm")
    return dx, dvf, g, exchanged


def conv_fwd(x, p, tag):
    T, D = x.shape
    nb = D // LANE
    kw = p['dw'].shape[0]
    h = rms_fwd(x, p['norm_g'], tag + "_norm", bf16)
    u = mm(h, p['w_in'], name=tag + "_in", bias=p['b_in'])
    c = colwise(functools.partial(_glu_conv_fn, kw), [(u, 0), (u, nb), (p['dw'], 0), (p['dw_b'], 0)], [T],
                name=tag + "_dw", nblk=nb)[0]
    sl = rowwise(_ln_silu_fn, [c], [p['ln_g'], p['ln_b']], name=tag + "_ln", out_dtype=bf16)[0]
    xo = mm(sl, p['w_out'], name=tag + "_out", bias=p['b_out'], res=x)
    return xo, dict(x=x, h=h, u=u, c=c, sl=sl)


def conv_bwd(dxo, p, s, tag):
    T, D = dxo.shape
    nb = D // LANE
    kw = p['dw'].shape[0]
    kpad = -(-kw // 8) * 8
    g = {}
    dsl = mm(dxo, p['w_out'], name=tag + "_bout", tb=True)
    g['w_out'] = mm(s['sl'], dxo, name=tag + "_bwout", ta=True)
    g['b_out'] = rowwise_bwd(_bias_fn, [dxo], [p['b_out']], [dxo], name=tag + "_bbout", n_drow=0, n_dpar=1)[0]
    dc, g['ln_g'], g['ln_b'] = rowwise_bwd(_ln_silu_fn, [s['c']], [p['ln_g'], p['ln_b']], [dsl], name=tag + "_bln",
                                           n_drow=1, n_dpar=2)
    u = s['u']
    du1, du2, ddw, g['dw_b'] = colwise(functools.partial(_glu_conv_bwd_fn, kw, kpad),
                                       [(u, 0), (u, nb), (p['dw'], 0), (dc, 0)], [T, T, kpad, 1],
                                       name=tag + "_bdw", nblk=nb)
    g['dw'] = ddw[:kw]
    du = jnp.concatenate([du1, du2], axis=1)
    g['b_in'] = rowwise_bwd(_bias_fn, [du], [p['b_in']], [du], name=tag + "_bbin", n_drow=0, n_dpar=1)[0]
    dh = mm(du, p['w_in'], name=tag + "_bin", tb=True)
    g['w_in'] = mm(s['h'], du, name=tag + "_bwin", ta=True)
    dx, g['norm_g'] = rms_bwd(s['x'], p['norm_g'], dh, dxo, tag + "_bnorm")
    return dx, g


def xattn_fwd(x, memn, p, tag):
    hn = rms_fwd(x, p['norm_g'], tag + "_norm", bf16)
    q = mm(hn, p['w_q'], name=tag + "_q", out_dtype=bf16)
    kv = mm(memn, p['w_kv'], name=tag + "_kv", out_dtype=bf16)
    o = attn_fwd(q, kv, name=tag + "_attn")
    xo = mm(o, p['w_o'], name=tag + "_o", res=x)
    return xo, dict(x=x, hn=hn, q=q, kv=kv, o=o)


def xattn_bwd(dxo, dmemn, memn, p, s, tag):
    g = {}
    do = mm(dxo, p['w_o'], name=tag + "_bo", tb=True)
    g['w_o'] = mm(s['o'], dxo, name=tag + "_bwo", ta=True)
    dq, dk, dv = attn_bwd(s['q'], s['kv'], do, name=tag + "_battn")
    dkv = jnp.concatenate([dk, dv], axis=1)
    dmemn = mm(dkv, p['w_kv'], name=tag + "_bkv", tb=True, res=dmemn)
    g['w_kv'] = mm(memn, dkv, name=tag + "_bwkv", ta=True)
    dhn = mm(dq, p['w_q'], name=tag + "_bq", tb=True)
    g['w_q'] = mm(s['hn'], dq, name=tag + "_bwq", ta=True)
    dx, g['norm_g'] = rms_bwd(s['x'], p['norm_g'], dhn, dxo, tag + "_bnorm")
    return dx, dmemn, g


def ffn_fwd(x, p, tag):
    T, D = x.shape
    w_dev, layer = p['w_in']
    nb = (N_DEV // 2) * w_dev.shape[2] // LANE
    kw = p['dw'].shape[0]
    hn = rms_fwd(x, p['norm_g'], tag + "_norm", bf16)
    u = mm(hn, w_dev, name=tag + "_in", b_dev=(layer * D, D))
    act = colwise(functools.partial(_ffn_act_fn, kw), [(u, 0), (u, nb), (p['dw'], 0), (p['dw'], nb)], [T],
                  name=tag + "_act", nblk=nb, out_dtype=bf16)[0]
    xo = mm(act, p['w_out'], name=tag + "_out", res=x)
    return xo, dict(x=x, hn=hn, u=u, act=act)


def ffn_bwd(dxo, p, s, tag):
    T, D = dxo.shape
    w_dev, layer = p['w_in']
    nb = (N_DEV // 2) * w_dev.shape[2] // LANE
    kw = p['dw'].shape[0]
    g = {}
    dact = mm(dxo, p['w_out'], name=tag + "_bout", tb=True)
    g['w_out'] = mm(s['act'], dxo, name=tag + "_bwout", ta=True)
    u = s['u']
    dug, duv, dwg, dwv = colwise(functools.partial(_ffn_act_bwd_fn, kw, 8),
                                 [(u, 0), (u, nb), (p['dw'], 0), (p['dw'], nb), (dact, 0)], [T, T, 8, 8],
                                 name=tag + "_bact", nblk=nb)
    g['dw'] = jnp.concatenate([dwg[:kw], dwv[:kw]], axis=1)
    du = jnp.concatenate([dug, duv], axis=1)
    dhn = mm(du, w_dev, name=tag + "_bin", tb=True, b_dev=(layer * D, D))
    g['w_in'] = mm(s['hn'], du, name=tag + "_bwin", ta=True, out_dev=True)
    dx, g['norm_g'] = rms_bwd(s['x'], p['norm_g'], dhn, dxo, tag + "_bnorm")
    return dx, g


def _lane_pad(n):
    return -(-n // LANE) * LANE


def _pad_blocks(a, axis, nblk):
    shp = a.shape
    n = shp[axis] // nblk
    t = a.reshape(shp[:axis] + (nblk, n) + shp[axis + 1:])
    pad = [(0, 0)] * t.ndim
    pad[axis + 1] = (0, _lane_pad(n) - n)
    t = jnp.pad(t, pad)
    return t.reshape(shp[:axis] + (nblk * _lane_pad(n),) + shp[axis + 1:])


def _unpad_blocks(a, axis, nblk, n):
    shp = a.shape
    t = a.reshape(shp[:axis] + (nblk, shp[axis] // nblk) + shp[axis + 1:])
    t = lax.slice_in_dim(t, 0, n, axis=axis + 1)
    return t.reshape(shp[:axis] + (nblk * n,) + shp[axis + 1:])


def _layer_params(W, layer):
    ia = ib = layer // 2
    mixer = {}
    if layer % 2 == 0:
        mixer = dict(norm_g=_row(W['norm_mix_g'][layer]), mu=W['rwkv_mu'][ia], w_r=W['rwkv_w_r'][ia],
                     w_k=W['rwkv_w_k'][ia], w_v=W['rwkv_w_v'][ia], w_o=W['rwkv_w_o'][ia], w0=_row(W['rwkv_w0'][ia]),
                     w1=W['rwkv_w1'][ia], w2=W['rwkv_w2'][ia], a0=_row(W['rwkv_a0'][ia]), a1=W['rwkv_a1'][ia],
                     a2=W['rwkv_a2'][ia], g1=W['rwkv_g1'][ia], g2=W['rwkv_g2'][ia], k_k=_row(W['rwkv_k_k'][ia]),
                     k_a=_row(W['rwkv_k_a'][ia]), r_k=_row(W['rwkv_r_k'][ia]), ln_g=_row(W['rwkv_ln_g'][ia]),
                     ln_b=_row(W['rwkv_ln_b'][ia]))
        if ia > 0:
            mixer.update(v0=_row(W['rwkv_v0'][ia - 1]), v1=W['rwkv_v1'][ia - 1], v2=W['rwkv_v2'][ia - 1])
    else:
        mixer = dict(norm_g=_row(W['norm_mix_g'][layer]), w_in=W['conv_w_in'][ib], b_in=_row(W['conv_b_in'][ib]),
                     dw=W['conv_dw'][ib], dw_b=_row(W['conv_dw_b'][ib]), ln_g=_row(W['conv_ln_g'][ib]),
                     ln_b=_row(W['conv_ln_b'][ib]), w_out=W['conv_w_out'][ib], b_out=_row(W['conv_b_out'][ib]))
    return mixer


def _rest_params(W, layer):
    xat = dict(norm_g=_row(W['norm_xattn_g'][layer]), w_q=W['xattn_w_q'][layer], w_kv=W['xattn_w_kv'][layer],
               w_o=W['xattn_w_o'][layer])
    ffn = dict(norm_g=_row(W['norm_ffn_g'][layer]), w_in=(W['ffn_w_in'], layer),
               dw=_pad_blocks(W['ffn_dw'][layer], 1, N_DEV), w_out=_pad_blocks(W['ffn_w_out'][layer], 0, N_DEV // 2))
    return xat, ffn


NATIVE = 'ffn_w_in'
EARLY = [n for n in W_NAMES if W_SPEC[n][0] is not None and (n.startswith('rwkv_') or not W_SPEC[n][1])]
LATE = [n for n in W_NAMES if W_SPEC[n][0] is not None and n not in EARLY and n != NATIVE]


def _native_rows(a, dtype):
    L, D, n = a.shape
    return jnp.pad(a.astype(dtype), ((0, 0), (0, 0), (0, _lane_pad(n) - n))).reshape(L * D, _lane_pad(n))


def _unpack_full(got, local, names):
    parts = _unpack(got, [local[n].shape for n in names], lead=N_DEV)
    return {n: _join_shards(part, W_SPEC[n][0]) for n, part in zip(names, parts)}


def _gather_early(local):
    full = {n: local[n] for n in W_NAMES if W_SPEC[n][0] is None}
    for as_bf16, dtype, tag in ((True, bf16, "ag_mat"), (False, f32, "ag_vec")):
        names = [n for n in EARLY if W_SPEC[n][1] == as_bf16]
        full.update(_unpack_full(all_gather(_pack([local[n] for n in names], dtype), name=tag), local, names))
    return full


def _step(local, x, mem, tgt):
    W = _gather_early(local)
    late_bufs = (_pack([local[n] for n in LATE], bf16), _native_rows(local[NATIVE], bf16))
    depth = W['norm_mix_g'].shape[0]
    g_mem = _row(W['mem_norm_g'])
    memn = rms_fwd(mem, g_mem, "mem_norm", bf16)
    layers, saved = [], []
    vf = None
    for l in range(depth):
        if l % 2 == 0:
            pm = _layer_params(W, l)
            x, v, sm, gathered = rwkv_fwd(x, pm, vf, f"rw{l}", gathers=late_bufs if l == 0 else ())
            if l == 0:
                W.update(_unpack_full(gathered[0], local, LATE))
                W[NATIVE] = gathered[1]
            if vf is None:
                vf = v
        else:
            pm = _layer_params(W, l)
            x, sm = conv_fwd(x, pm, f"cv{l}")
        px, pf = _rest_params(W, l)
        x, sx = xattn_fwd(x, memn, px, f"xa{l}")
        x, sf = ffn_fwd(x, pf, f"ff{l}")
        layers.append((pm, px, pf))
        saved.append((sm, sx, sf))
    g_fin = _row(W['final_norm_g'])
    dx, dg_fin, loss_blk = final_loss(x, tgt, g_fin, name="final_loss")

    grads = {n: [None] * local[n].shape[0] for n in W_NAMES if local[n].ndim >= 2}
    grads['final_norm_g'] = dg_fin.reshape(-1)
    n_in = local[NATIVE].shape[2]
    dmemn = jnp.zeros(memn.shape, f32)
    dvf = None
    for l in reversed(range(depth)):
        pm, px, pf = layers[l]
        sm, sx, sf = saved[l]
        dx, gf = ffn_bwd(dx, pf, sf, f"ff{l}")
        dx, dmemn, gx = xattn_bwd(dx, dmemn, memn, px, sx, f"xa{l}")
        grads['norm_ffn_g'][l] = gf['norm_g'].reshape(-1)
        grads['ffn_w_in'][l] = gf['w_in']
        grads['ffn_dw'][l] = _unpad_blocks(gf['dw'], 1, N_DEV, n_in)
        grads['ffn_w_out'][l] = _unpad_blocks(gf['w_out'], 0, N_DEV // 2, n_in)
        grads['norm_xattn_g'][l] = gx['norm_g'].reshape(-1)
        grads['xattn_w_q'][l], grads['xattn_w_kv'][l], grads['xattn_w_o'][l] = gx['w_q'], gx['w_kv'], gx['w_o']
        i = l // 2
        if l % 2 == 0:
            pre = ()
            if l == 0:
                late = _pack([p for n in LATE for p in _owner_pieces(grads[n], W_SPEC[n][0])], f32, lead=N_DEV)
                native = jnp.concatenate(grads[NATIVE], axis=1)
                pre = (chip_partials(late, name="rs_late"), chip_partials(native, name="rs_ffn_in"))
            dx, dvf_l, gm, exchanged = rwkv_bwd(dx, dvf if i == 0 else None, pm, sm, f"rw{l}", exchanges=pre)
            if l == 0:
                late_parts, native_parts = exchanged
            if dvf_l is not None:
                dvf = dvf_l if dvf is None else rowwise(_add_fn, [dvf, dvf_l], [], name=f"rw{l}_dvfadd")[0]
            for short in ('mu', 'w_r', 'w_k', 'w_v', 'w_o', 'w1', 'w2', 'a1', 'a2', 'g1', 'g2'):
                grads['rwkv_' + short][i] = gm[short]
            for short in ('w0', 'a0', 'k_k', 'k_a', 'ln_g', 'ln_b'):
                grads['rwkv_' + short][i] = gm[short].reshape(-1)
            grads['rwkv_r_k'][i] = gm['r_k'].reshape(W['rwkv_r_k'].shape[1:])
            if i > 0:
                grads['rwkv_v0'][i - 1] = gm['v0'].reshape(-1)
                grads['rwkv_v1'][i - 1], grads['rwkv_v2'][i - 1] = gm['v1'], gm['v2']
        else:
            dx, gm = conv_bwd(dx, pm, sm, f"cv{l}")
            for short in ('w_in', 'dw', 'w_out'):
                grads['conv_' + short][i] = gm[short]
            for short in ('b_in', 'dw_b', 'ln_g', 'ln_b', 'b_out'):
                grads['conv_' + short][i] = gm[short].reshape(-1)
        grads['norm_mix_g'][l] = gm['norm_g'].reshape(-1)
    _, dg_mem = rowwise_bwd(lambda xv, gv: (_rms_fn(xv, gv),), [mem], [g_mem], [dmemn], name="mem_norm_b",
                            n_drow=1, n_dpar=1)
    grads['mem_norm_g'] = dg_mem.reshape(-1)
    early_grads = {n: grads[n] for n in EARLY}
    repl_grads = {n: (jnp.stack(gv) if isinstance(gv, list) else gv) for n, gv in grads.items() if W_SPEC[n][0] is None}
    return loss_blk[0, 0], dx, early_grads, repl_grads, late_parts, native_parts


def kernel(x, mem, mem_norm_g, norm_mix_g, norm_xattn_g, norm_ffn_g, final_norm_g, rwkv_mu, rwkv_w_r, rwkv_w_k, rwkv_w_v, rwkv_w_o, rwkv_w0, rwkv_w1, rwkv_w2, rwkv_a0, rwkv_a1, rwkv_a2, rwkv_g1, rwkv_g2, rwkv_k_k, rwkv_k_a, rwkv_r_k, rwkv_ln_g, rwkv_ln_b, rwkv_v0, rwkv_v1, rwkv_v2, conv_w_in, conv_b_in, conv_dw, conv_dw_b, conv_ln_g, conv_ln_b, conv_w_out, conv_b_out, xattn_w_q, xattn_w_kv, xattn_w_o, ffn_w_in, ffn_dw, ffn_w_out, loss_target, m_mem_norm_g, m_norm_mix_g, m_norm_xattn_g, m_norm_ffn_g, m_final_norm_g, m_rwkv_mu, m_rwkv_w_r, m_rwkv_w_k, m_rwkv_w_v, m_rwkv_w_o, m_rwkv_w0, m_rwkv_w1, m_rwkv_w2, m_rwkv_a0, m_rwkv_a1, m_rwkv_a2, m_rwkv_g1, m_rwkv_g2, m_rwkv_k_k, m_rwkv_k_a, m_rwkv_r_k, m_rwkv_ln_g, m_rwkv_ln_b, m_rwkv_v0, m_rwkv_v1, m_rwkv_v2, m_conv_w_in, m_conv_b_in, m_conv_dw, m_conv_dw_b, m_conv_ln_g, m_conv_ln_b, m_conv_w_out, m_conv_b_out, m_xattn_w_q, m_xattn_w_kv, m_xattn_w_o, m_ffn_w_in, m_ffn_dw, m_ffn_w_out, v_mem_norm_g, v_norm_mix_g, v_norm_xattn_g, v_norm_ffn_g, v_final_norm_g, v_rwkv_mu, v_rwkv_w_r, v_rwkv_w_k, v_rwkv_w_v, v_rwkv_w_o, v_rwkv_w0, v_rwkv_w1, v_rwkv_w2, v_rwkv_a0, v_rwkv_a1, v_rwkv_a2, v_rwkv_g1, v_rwkv_g2, v_rwkv_k_k, v_rwkv_k_a, v_rwkv_r_k, v_rwkv_ln_g, v_rwkv_ln_b, v_rwkv_v0, v_rwkv_v1, v_rwkv_v2, v_conv_w_in, v_conv_b_in, v_conv_dw, v_conv_dw_b, v_conv_ln_g, v_conv_ln_b, v_conv_w_out, v_conv_b_out, v_xattn_w_q, v_xattn_w_kv, v_xattn_w_o, v_ffn_w_in, v_ffn_dw, v_ffn_w_out):
    given = dict(locals())
    local = {n: given[n] for n in W_NAMES}
    loss_local, dx, early_grads, grads, late_parts, native_parts = _step(local, x[0], mem[0], loss_target[0])
    loss = lax.psum(loss_local, ("x", "y", "c"))

    repl = [n for n in W_NAMES if W_SPEC[n][0] is None]
    out = {}
    kinds = ("grad_", "delta_", "new_m_", "new_v_")

    res = adamw(native_parts, *[_native_rows(given[pre + NATIVE], f32) for pre in ("", "m_", "v_")],
                name="adamw_ffn_in")
    shp = given[NATIVE].shape
    for kind, buf in zip(kinds, res):
        out[kind + NATIVE] = buf.reshape(shp[0], shp[1], -1)[:, :, :shp[2]]

    early_parts = reduce_scatter(
        _pack([p for n in EARLY for p in _owner_pieces(early_grads[n], W_SPEC[n][0])], f32, lead=N_DEV), name="rs_early")
    res = adamw(early_parts, *[_pack([given[pre + n] for n in EARLY], f32) for pre in ("", "m_", "v_")],
                name="adamw_early")
    for kind, buf in zip(kinds, res):
        for n, arr in zip(EARLY, _unpack(buf, [given[n].shape for n in EARLY])):
            out[kind + n] = arr
    row_off = 0
    for n in LATE:
        shp = given[n].shape
        res = adamw(late_parts, *[given[pre + n].reshape(-1, PACK_COLS) for pre in ("", "m_", "v_")],
                    name="adamw_" + n, row_off=row_off)
        for kind, buf in zip(kinds, res):
            out[kind + n] = buf.reshape(shp)
        row_off += given[n].size // PACK_COLS

    parts = all_gather(_pack([grads[n] for n in repl], f32), name="grad_gather_repl")
    res = adamw(parts, *[_pack([given[pre + n] for n in repl], f32) for pre in ("", "m_", "v_")],
                name="adamw_repl")
    for kind, buf in zip(("grad_", "delta_", "new_m_", "new_v_"), res):
        for n, arr in zip(repl, _unpack(buf, [given[n].shape for n in repl])):
            out[kind + n] = arr

    return (loss, dx[None], *[out[kind + n] for kind in ("grad_", "delta_", "new_m_", "new_v_") for n in W_NAMES])
```

```python
import functools
import math

import jax
import jax.numpy as jnp
from jax import lax
from jax.experimental import pallas as pl
from jax.experimental.pallas import tpu as pltpu

f32 = jnp.float32
bf16 = jnp.bfloat16

N_DEV = 8
HEAD = 64
XATTN_HEADS = 4
NORM_EPS = 1e-6
LN_EPS = 1e-5
GN_EPS = 64e-5
ADAM_LR, ADAM_B1, ADAM_B2, ADAM_EPS, ADAM_WD, ADAM_STEP = 0.001, 0.9, 0.999, 1e-08, 0.01, 10
LANE = 128
PACK_COLS = 1024
PACK_ROWS = 256
VMEM_LIMIT = 48 * 1024 * 1024
SCAN_CHUNK = 16
MXU_DIM = 256

W_SPEC = {
    'mem_norm_g': (None, False), 'norm_mix_g': (None, False), 'norm_xattn_g': (None, False),
    'norm_ffn_g': (None, False), 'final_norm_g': (None, False),
    'rwkv_mu': (2, False), 'rwkv_w_r': (1, True), 'rwkv_w_k': (1, True), 'rwkv_w_v': (1, True),
    'rwkv_w_o': (1, True), 'rwkv_w0': (None, False), 'rwkv_w1': (1, True), 'rwkv_w2': (2, True),
    'rwkv_a0': (None, False), 'rwkv_a1': (1, True), 'rwkv_a2': (2, True), 'rwkv_g1': (1, True),
    'rwkv_g2': (2, True), 'rwkv_k_k': (None, False), 'rwkv_k_a': (None, False), 'rwkv_r_k': (None, False),
    'rwkv_ln_g': (None, False), 'rwkv_ln_b': (None, False), 'rwkv_v0': (None, False),
    'rwkv_v1': (1, True), 'rwkv_v2': (2, True),
    'conv_w_in': (2, True), 'conv_b_in': (1, False), 'conv_dw': (2, False), 'conv_dw_b': (1, False),
    'conv_ln_g': (1, False), 'conv_ln_b': (1, False), 'conv_w_out': (1, True), 'conv_b_out': (1, False),
    'xattn_w_q': (1, True), 'xattn_w_kv': (2, True), 'xattn_w_o': (1, True),
    'ffn_w_in': (2, True), 'ffn_dw': (2, False), 'ffn_w_out': (1, True),
}
W_NAMES = list(W_SPEC)


def _tile(n, prefs):
    for p in prefs:
        if n % p == 0:
            return p
    return n


def _cparams(sem):
    return pltpu.CompilerParams(dimension_semantics=sem, vmem_limit_bytes=VMEM_LIMIT)


def _sigmoid(x):
    return 1.0 / (1.0 + jnp.exp(-x))


def _softplus(x):
    return jnp.maximum(x, 0.0) + jnp.log(1.0 + jnp.exp(-jnp.abs(x)))


def mm(a, b, *, name, ta=False, tb=False, bias=None, res=None, act=None, b_dev=None, out_dev=False, out_dtype=f32):
    M, K = (a.shape[1], a.shape[0]) if ta else a.shape
    tm = _tile(M, (1024, 512, 256, 128))
    if b_dev is None:
        N = b.shape[0] if tb else b.shape[1]
        assert (b.shape[1] if tb else b.shape[0]) == K, (name, a.shape, b.shape)
        tn = _tile(N, (1024, 512, 256, 128))
        tk = _tile(K, (1024, 512, 256, 128))
    else:
        b_off, b_rows = b_dev
        width = b.shape[2]
        if tb:
            N, tk = b_rows, width
            tn = _tile(N, (1024, 512, 256, 128))
            assert K == N_DEV * width and b_off % tn == 0, (name, a.shape, b.shape)
        else:
            N, tn = N_DEV * width, width
            tk = _tile(K, (1024, 512, 256, 128))
            assert K == b_rows and b_off % tk == 0, (name, a.shape, b.shape)
    if out_dev:
        tn = N // N_DEV
    nk = K // tk
    dims = (((0 if ta else 1,), (1 if tb else 0,)), ((), ()))
    has_bias, has_res = bias is not None, res is not None

    def body(*refs):
        a_ref, b_ref = refs[0], refs[1]
        pos = 2
        bias_ref = res_ref = None
        if has_bias:
            bias_ref = refs[pos]; pos += 1
        if has_res:
            res_ref = refs[pos]; pos += 1
        o_ref, acc_ref = refs[pos], refs[pos + 1]
        kstep = pl.program_id(2)

        @pl.when(kstep == 0)
        def _():
            acc_ref[...] = jnp.zeros_like(acc_ref)

        acc_ref[...] += lax.dot_general(a_ref[...].astype(bf16), b_ref[...].astype(bf16), dims,
                                        preferred_element_type=f32)

        @pl.when(kstep == nk - 1)
        def _():
            out = acc_ref[...]
            if has_bias:
                out = out + bias_ref[...]
            if act == 'tanh':
                out = jnp.tanh(out)
            elif act == 'sigmoid':
                out = _sigmoid(out)
            if has_res:
                out = out + res_ref[...]
            o_ref[...] = out.astype(o_ref.dtype)

    a_spec = pl.BlockSpec((tk, tm), lambda i, j, k: (k, i)) if ta else pl.BlockSpec((tm, tk), lambda i, j, k: (i, k))
    if b_dev is None:
        b_spec = pl.BlockSpec((tn, tk), lambda i, j, k: (j, k)) if tb else pl.BlockSpec((tk, tn), lambda i, j, k: (k, j))
    elif tb:
        b_spec = pl.BlockSpec((None, tn, tk), lambda i, j, k: (k, b_off // tn + j, 0))
    else:
        b_spec = pl.BlockSpec((None, tk, tn), lambda i, j, k: (j, b_off // tk + k, 0))
    in_specs, args = [a_spec, b_spec], [a, b]
    if has_bias:
        in_specs.append(pl.BlockSpec((1, tn), lambda i, j, k: (0, j))); args.append(bias)
    if has_res:
        in_specs.append(pl.BlockSpec((tm, tn), lambda i, j, k: (i, j))); args.append(res)
    if out_dev:
        out_spec = pl.BlockSpec((None, tm, tn), lambda i, j, k: (j, i, 0))
        out_shape = jax.ShapeDtypeStruct((N_DEV, M, tn), out_dtype)
    else:
        out_spec = pl.BlockSpec((tm, tn), lambda i, j, k: (i, j))
        out_shape = jax.ShapeDtypeStruct((M, N), out_dtype)
    return pl.pallas_call(
        body, name=name, grid=(M // tm, N // tn, nk), in_specs=in_specs,
        out_specs=out_spec, out_shape=out_shape,
        scratch_shapes=[pltpu.VMEM((tm, tn), f32)],
        compiler_params=_cparams(("parallel", "parallel", "arbitrary")),
    )(*args)


def rowwise(fn, rows, pars, *, name, tt=256, out_dtype=f32):
    T = rows[0].shape[0]
    tt = min(tt, T)
    nr, npar = len(rows), len(pars)
    outs = jax.eval_shape(fn, *[jax.ShapeDtypeStruct((tt, r.shape[1]), r.dtype) for r in rows],
                          *[jax.ShapeDtypeStruct(p.shape, p.dtype) for p in pars])

    def body(*refs):
        res = fn(*[r[...] for r in refs[:nr + npar]])
        for o_ref, o in zip(refs[nr + npar:], res):
            o_ref[...] = o.astype(o_ref.dtype)

    return pl.pallas_call(
        body, name=name, grid=(T // tt,),
        in_specs=[pl.BlockSpec((tt, r.shape[1]), lambda i: (i, 0)) for r in rows]
        + [pl.BlockSpec(p.shape, lambda i: (0, 0)) for p in pars],
        out_specs=[pl.BlockSpec((tt, o.shape[1]), lambda i: (i, 0)) for o in outs],
        out_shape=[jax.ShapeDtypeStruct((T, o.shape[1]), out_dtype) for o in outs],
        compiler_params=_cparams(("parallel",)),
    )(*rows, *pars)


def rowwise_bwd(fn, rows, pars, cots, *, name, n_drow, n_dpar, add0=None, tt=128):
    T = rows[0].shape[0]
    tt = min(tt, T)
    nr, npar, nc = len(rows), len(pars), len(cots)
    has_add = add0 is not None

    def body(*refs):
        rv = [r[...] for r in refs[:nr]]
        pv = [r[...] for r in refs[nr:nr + npar]]
        cv = [r[...] for r in refs[nr + npar:nr + npar + nc]]
        pos = nr + npar + nc
        add_ref = None
        if has_add:
            add_ref = refs[pos]; pos += 1
        drow_refs = refs[pos:pos + n_drow]
        dpar_refs = refs[pos + n_drow:pos + n_drow + n_dpar]

        def f(*d):
            return fn(*d[:n_drow], *rv[n_drow:], *d[n_drow:], *pv[n_dpar:])

        _, vjp = jax.vjp(f, *rv[:n_drow], *pv[:n_dpar])
        g = vjp(tuple(cv))
        for k in range(n_drow):
            gk = g[k]
            if k == 0 and has_add:
                gk = gk + add_ref[...]
            drow_refs[k][...] = gk

        @pl.when(pl.program_id(0) == 0)
        def _():
            for k in range(n_dpar):
                dpar_refs[k][...] = jnp.zeros_like(dpar_refs[k])

        for k in range(n_dpar):
            dpar_refs[k][...] += g[n_drow + k]

    row_spec = lambda r: pl.BlockSpec((tt, r.shape[1]), lambda i: (i, 0))
    par_spec = lambda p: pl.BlockSpec(p.shape, lambda i: (0, 0))
    in_specs = [row_spec(r) for r in rows] + [par_spec(p) for p in pars] + [row_spec(c) for c in cots]
    args = [*rows, *pars, *cots]
    if has_add:
        in_specs.append(row_spec(add0)); args.append(add0)
    return pl.pallas_call(
        body, name=name, grid=(T // tt,), in_specs=in_specs,
        out_specs=[row_spec(r) for r in rows[:n_drow]] + [par_spec(p) for p in pars[:n_dpar]],
        out_shape=[jax.ShapeDtypeStruct(r.shape, f32) for r in rows[:n_drow]]
        + [jax.ShapeDtypeStruct(p.shape, f32) for p in pars[:n_dpar]],
        compiler_params=_cparams(("arbitrary",)),
    )(*args)


def colwise(fn, cols, out_rows, *, name, nblk, out_dtype=f32):
    def body(*refs):
        res = fn(*[r[...] for r in refs[:len(cols)]])
        for o_ref, o in zip(refs[len(cols):], res):
            o_ref[...] = o.astype(o_ref.dtype)

    def spec(rows, off):
        return pl.BlockSpec((rows, LANE), lambda j: (0, j + off))

    return pl.pallas_call(
        body, name=name, grid=(nblk,),
        in_specs=[spec(a.shape[0], off) for a, off in cols],
        out_specs=[spec(r, 0) for r in out_rows],
        out_shape=[jax.ShapeDtypeStruct((r, nblk * LANE), out_dtype) for r in out_rows],
        compiler_params=_cparams(("parallel",)),
    )(*[a for a, _ in cols])


def _shift_dn(x, s):
    if s == 0:
        return x
    rid = lax.broadcasted_iota(jnp.int32, x.shape, 0)
    return jnp.where(rid >= s, pltpu.roll(x, s, 0), 0.0)


def _shift_up(x, s):
    if s == 0:
        return x
    n = x.shape[0]
    rid = lax.broadcasted_iota(jnp.int32, x.shape, 0)
    return jnp.where(rid < n - s, pltpu.roll(x, n - s, 0), 0.0)


def _colsum(x):
    return jnp.sum(x, axis=0, keepdims=True)


def _stack_rows(rows, n):
    c = rows[0].shape[1]
    rid = lax.broadcasted_iota(jnp.int32, (n, c), 0)
    out = jnp.zeros((n, c), f32)
    for i, r in enumerate(rows):
        out = jnp.where(rid == i, jnp.broadcast_to(r, (n, c)), out)
    return out


def _dwconv(x, w, kw):
    acc = None
    for k in range(kw):
        term = w[k:k + 1, :] * _shift_dn(x, kw - 1 - k)
        acc = term if acc is None else acc + term
    return acc


def _dwconv_bwd(x, w, dy, kw, pad_rows):
    dx = None
    rows = []
    for k in range(kw):
        s = kw - 1 - k
        rows.append(_colsum(dy * _shift_dn(x, s)))
        term = w[k:k + 1, :] * _shift_up(dy, s)
        dx = term if dx is None else dx + term
    return dx, _stack_rows(rows, pad_rows)


def _mix_fn(h, mu):
    xx = _shift_dn(h, 1) - h
    return tuple(h + xx * mu[i:i + 1, :] for i in range(6))


def _mix_bwd_fn(h, mu, *ds):
    xx = _shift_dn(h, 1) - h
    s1 = ds[0]
    s2 = ds[0] * mu[0:1, :]
    rows = [_colsum(ds[0] * xx)]
    for i in range(1, 6):
        s1 = s1 + ds[i]
        s2 = s2 + ds[i] * mu[i:i + 1, :]
        rows.append(_colsum(ds[i] * xx))
    return s1 - s2 + _shift_up(s2, 1), _stack_rows(rows, 8)


def _glu_conv_fn(kw, u1, u2, w, b):
    return (_dwconv(u1 * _sigmoid(u2), w, kw) + b,)


def _glu_conv_bwd_fn(kw, pad_rows, u1, u2, w, dc):
    sig = _sigmoid(u2)
    g = u1 * sig
    dg, dw = _dwconv_bwd(g, w, dc, kw, pad_rows)
    return dg * sig, dg * g * (1.0 - sig), dw, _colsum(dc)


def _ffn_act_fn(kw, ug, uv, wg, wv):
    gc = _dwconv(ug, wg, kw)
    vc = _dwconv(uv, wv, kw)
    return (gc * _sigmoid(gc) * vc,)


def _ffn_act_bwd_fn(kw, pad_rows, ug, uv, wg, wv, dact):
    gc = _dwconv(ug, wg, kw)
    vc = _dwconv(uv, wv, kw)
    sg = _sigmoid(gc)
    dvc = dact * gc * sg
    dgc = dact * vc * (sg * (1.0 + gc * (1.0 - sg)))
    dug, dwg = _dwconv_bwd(ug, wg, dgc, kw, pad_rows)
    duv, dwv = _dwconv_bwd(uv, wv, dvc, kw, pad_rows)
    return dug, duv, dwg, dwv


def _rms_fn(x, g):
    return x * lax.rsqrt(jnp.mean(x * x, axis=-1, keepdims=True) + NORM_EPS) * g


def _hsum(x, e, et):
    s = jnp.dot(x, e, precision=lax.Precision.HIGH, preferred_element_type=f32)
    return jnp.dot(s, et, precision=lax.Precision.HIGH, preferred_element_type=f32)


def _mid_fn(vres, k, v, lw, aa, *rest):
    if vres:
        vv, vf, w0, a0, k_k, k_a, v0, e, et = rest
    else:
        w0, a0, k_k, k_a, e, et = rest
    logw = -_softplus(-(w0 + lw)) - 0.5
    decay = jnp.exp(-jnp.exp(logw))
    a = _sigmoid(a0 + aa)
    kk = k * k_k
    kk = kk / jnp.maximum(jnp.sqrt(_hsum(kk * kk, e, et)), 1e-12)
    k2 = k * (1.0 + (a - 1.0) * k_a)
    v2 = v + (vf - v) * _sigmoid(v0 + vv) if vres else v
    return decay, a, kk, k2, v2


def _post_fn(y, r, k2, v2, gg, ln_g, ln_b, rk, e, et):
    inv = 1.0 / HEAD
    yc = y - _hsum(y, e, et) * inv
    var = _hsum(yc * yc, e, et) * inv
    yn = yc * lax.rsqrt(var + GN_EPS) * ln_g + ln_b
    bonus = _hsum(r * k2 * rk, e, et) * v2
    return ((yn + bonus) * gg,)


def _ln_silu_fn(c, g, b):
    mu = jnp.mean(c, axis=-1, keepdims=True)
    var = jnp.mean(jnp.square(c - mu), axis=-1, keepdims=True)
    ln = (c - mu) * lax.rsqrt(var + LN_EPS) * g + b
    return (ln * _sigmoid(ln),)


def _bias_fn(x, b):
    return (x + b,)


def _dtanh_fn(d, th):
    return (d * (1.0 - th * th),)


def _dsig_fn(d, sg):
    return (d * sg * (1.0 - sg),)


def _add_fn(a, b):
    return (a + b,)


def _seg(blocks, bd, coarse=()):
    def side_by_side(parts):
        h = len(parts) // 2
        return jnp.concatenate([jnp.concatenate(parts[:h], axis=0), jnp.concatenate(parts[h:], axis=0)], axis=1)

    def apart(res, count):
        h = count // 2
        return ([res[i * HEAD:(i + 1) * HEAD, :LANE] for i in range(h)]
                + [res[i * HEAD:(i + 1) * HEAD, LANE:] for i in range(h)])

    x = side_by_side(blocks)
    n = x.shape[0]
    h0 = x.astype(bf16)
    h1 = (x - h0.astype(f32)).astype(bf16)
    lhs = [h0, h1] + ([side_by_side(coarse).astype(bf16)] if coarse else [])
    out = jnp.dot(jnp.concatenate(lhs, axis=0), bd, preferred_element_type=f32)
    fine = apart(out[n:2 * n] + out[0:n], len(blocks))
    return fine + (apart(out[2 * n:], len(coarse)) if coarse else [])


def _scan_consts():
    li = lax.broadcasted_iota(jnp.int32, (MXU_DIM, MXU_DIM), 0) // HEAD
    lj = lax.broadcasted_iota(jnp.int32, (MXU_DIM, MXU_DIM), 1) // HEAD
    bd = (li == lj).astype(bf16)
    si = lax.broadcasted_iota(jnp.int32, (HEAD, LANE), 0)
    sj = lax.broadcasted_iota(jnp.int32, (HEAD, LANE), 1) % HEAD
    dg = (si == sj).astype(f32)
    return bd, dg


def _scan_dims(T, D):
    return D // LANE, min(SCAN_CHUNK, T)


def _head_dots_fn(r, k, kk, a, e, et):
    return _hsum(kk * a * r, e, et), _hsum(k * r, e, et)


def scan_fwd(r, w, k, v, kk, a, br, kr, *, name, gathers=()):
    T, D = r.shape
    G, tc = _scan_dims(T, D)
    nch = T // tc
    ng = len(gathers)
    bd, dg = _scan_consts()

    def body(*refs):
        r_ref, w_ref, k_ref, v_ref, kk_ref, a_ref, br_ref, kr_ref, bd_ref, dg_ref = refs[:10]
        y_ref, st_ref, sa_ref = refs[10 + ng:13 + ng]
        s_ref = refs[13 + 2 * ng]
        jobs = [(refs[10 + i], refs[13 + ng + i], *refs[14 + 2 * ng + 3 * i:17 + 2 * ng + 3 * i]) for i in range(ng)]

        @pl.when(pl.program_id(0) == 0)
        def _():
            s_ref[...] = jnp.zeros_like(s_ref)
            for job in jobs:
                _ag_start(*job)

        bdv, dgv = bd_ref[...], dg_ref[...]

        def step(t, carry):
            row = pl.ds(t, 1)
            rr, ww, kr_, vr, kkr, ar, brr, krr = (x[row, :] for x in (r_ref, w_ref, k_ref, v_ref, kk_ref, a_ref,
                                                                    br_ref, kr_ref))
            bb = kkr * ar
            wr = ww * rr
            sl = [slice(g * LANE, (g + 1) * LANE) for g in range(G)]
            ps = [s_ref[g] for g in range(G)]
            blocks = [ps[g] * (-kkr[:, sl[g]]) for g in range(G)]
            vds = [ps[g] * wr[:, sl[g]] for g in range(G)]
            vds += [jnp.broadcast_to(vr[:, sl[g]], (HEAD, LANE)) * dgv for g in range(G)]
            res = _seg(blocks, bdv, vds)
            yrows = []
            for g in range(G):
                sab, ub, vb = res[g], res[G + g], res[2 * G + g]
                sn = ps[g] * ww[:, sl[g]] + sab * bb[:, sl[g]] + vb * kr_[:, sl[g]]
                s_ref[g] = sn
                st_ref[t, g] = sn
                sa_ref[t, g] = sab
                yb = ub + sab * brr[:, sl[g]] + vb * krr[:, sl[g]]
                yrows.append(_colsum(yb * dgv))
            y_ref[row, :] = jnp.concatenate(yrows, axis=1)
            return carry

        lax.fori_loop(0, tc, step, 0)

        @pl.when(pl.program_id(0) == nch - 1)
        def _():
            for job in jobs:
                _ag_finish(*job)

    vec = pl.BlockSpec((tc, D), lambda c: (c, 0))
    big = pl.BlockSpec((tc, G, HEAD, LANE), lambda c: (c, 0, 0, 0))
    hbm = pl.BlockSpec(memory_space=pl.ANY)
    return pl.pallas_call(
        body, name=name, grid=(nch,),
        in_specs=[vec] * 8 + [pl.BlockSpec((MXU_DIM, MXU_DIM), lambda c: (0, 0)), pl.BlockSpec((HEAD, LANE), lambda c: (0, 0))]
        + [hbm] * ng,
        out_specs=[vec, big, big] + [hbm] * ng,
        out_shape=[jax.ShapeDtypeStruct((T, D), f32)] + [jax.ShapeDtypeStruct((T, G, HEAD, LANE), f32)] * 2
        + [jax.ShapeDtypeStruct((N_DEV,) + x.shape, x.dtype) for x in gathers],
        scratch_shapes=[pltpu.VMEM((G, HEAD, LANE), f32)] + _comm_scratch(AG_SEMS) * ng,
        compiler_params=_cparams(("arbitrary",)),
    )(r, w, k, v, kk, a, br, kr, bd, dg, *gathers)


def scan_bwd(r, w, k, v, kk, a, br, kr, dy, states, sas, dr0, dk0, dv0, *, name, exchanges=()):
    T, D = r.shape
    G, tc = _scan_dims(T, D)
    nch = T // tc
    ne = len(exchanges)
    bd, dg = _scan_consts()

    def body(*refs):
        (r_ref, w_ref, k_ref, v_ref, kk_ref, a_ref, br_ref, kr_ref, dy_ref, st_ref, prev_ref, sa_ref,
         dr0_ref, dk0_ref, dv0_ref, bd_ref, dg_ref) = refs[:17]
        dr_ref, dw_ref, dk_ref, dv_ref, dkk_ref, da_ref = refs[17 + ne:23 + ne]
        ds_ref = refs[23 + 2 * ne]
        jobs = [(refs[17 + i], refs[23 + ne + i], *refs[24 + 2 * ne + 3 * i:27 + 2 * ne + 3 * i]) for i in range(ne)]

        @pl.when(pl.program_id(0) == 0)
        def _():
            ds_ref[...] = jnp.zeros_like(ds_ref)
            for job in jobs:
                _cx_start(*job)

        bdv, dgv = bd_ref[...], dg_ref[...]

        def step_at(t, ps):
            row = pl.ds(t, 1)
            rr, ww, kr_, vr, kkr, ar, brr, krr, dyr = (x[row, :] for x in (r_ref, w_ref, k_ref, v_ref, kk_ref, a_ref,
                                                                         br_ref, kr_ref, dy_ref))
            bb = kkr * ar
            sl = [slice(g * LANE, (g + 1) * LANE) for g in range(G)]
            dr_rows, dw_rows, dk_rows, dv_rows, dkk_rows, da_rows = [], [], [], [], [], []
            dss = [ds_ref[g] for g in range(G)]
            sabs = [sa_ref[t, g] for g in range(G)]
            sts = [st_ref[t, g] for g in range(G)]
            blocks = [dss[g] * bb[:, sl[g]] for g in range(G)]
            diag = [dss[g] * kr_[:, sl[g]] for g in range(G)]
            diag += [jnp.broadcast_to(vr[:, sl[g]], (HEAD, LANE)) * dgv for g in range(G)]
            diag += [jnp.broadcast_to(dyr[:, sl[g]], (HEAD, LANE)) * dgv for g in range(G)]
            res = _seg(blocks, bdv, diag)
            for g in range(G):
                sab, vb, dyb = sabs[g], res[2 * G + g], res[3 * G + g]
                dsab = res[g] + dyb * brr[:, sl[g]]
                dvb = res[G + g] + dyb * krr[:, sl[g]]
                dst = dss[g] + dyb * rr[:, sl[g]]
                dr_rows.append(_colsum(sts[g] * dyb))
                dw_rows.append(_colsum(dst * ps[g]))
                db_row = _colsum(dst * sab)
                dk_rows.append(_colsum(dst * vb))
                dv_rows.append(_colsum(dvb * dgv))
                ds_ref[g] = dst * ww[:, sl[g]] - dsab * kkr[:, sl[g]]
                dkk_rows.append(db_row * ar[:, sl[g]] - _colsum(ps[g] * dsab))
                da_rows.append(db_row * kkr[:, sl[g]])
            cat = lambda rows: jnp.concatenate(rows, axis=1)
            dr_ref[row, :] = cat(dr_rows) + dr0_ref[row, :]
            dw_ref[row, :] = cat(dw_rows)
            dk_ref[row, :] = cat(dk_rows) + dk0_ref[row, :]
            dv_ref[row, :] = cat(dv_rows) + dv0_ref[row, :]
            dkk_ref[row, :] = cat(dkk_rows)
            da_ref[row, :] = cat(da_rows)

        def step(i, carry):
            t = tc - 1 - i
            step_at(t, [st_ref[t - 1, g] for g in range(G)])
            return carry

        lax.fori_loop(0, tc - 1, step, 0)
        first = (pl.program_id(0) < nch - 1).astype(f32)
        step_at(0, [prev_ref[0, g] * first for g in range(G)])

        @pl.when(pl.program_id(0) == nch - 1)
        def _():
            for job in jobs:
                _cx_finish(*job)

    vec = pl.BlockSpec((tc, D), lambda c: (nch - 1 - c, 0))
    big = pl.BlockSpec((tc, G, HEAD, LANE), lambda c: (nch - 1 - c, 0, 0, 0))
    prev = pl.BlockSpec((1, G, HEAD, LANE), lambda c: (jnp.maximum((nch - 1 - c) * tc - 1, 0), 0, 0, 0))
    hbm = pl.BlockSpec(memory_space=pl.ANY)
    return pl.pallas_call(
        body, name=name, grid=(nch,),
        in_specs=[vec] * 9 + [big, prev, big] + [vec] * 3
        + [pl.BlockSpec((MXU_DIM, MXU_DIM), lambda c: (0, 0)), pl.BlockSpec((HEAD, LANE), lambda c: (0, 0))] + [hbm] * ne,
        out_specs=[vec] * 6 + [hbm] * ne,
        out_shape=[jax.ShapeDtypeStruct((T, D), f32)] * 6 + [jax.ShapeDtypeStruct(x.shape, x.dtype) for x in exchanges],
        scratch_shapes=[pltpu.VMEM((G, HEAD, LANE), f32)] + _comm_scratch(CX_SEMS) * ne,
        compiler_params=_cparams(("arbitrary",)),
    )(r, w, k, v, kk, a, br, kr, dy, states, states, sas, dr0, dk0, dv0, bd, dg, *exchanges)


def _attn_p(q, k, scale):
    s = lax.dot_general(q.astype(bf16), k.astype(bf16), (((1,), (1,)), ((), ())), preferred_element_type=f32) * scale
    s = s - jnp.max(s, axis=-1, keepdims=True)
    p = jnp.exp(s)
    return p / jnp.sum(p, axis=-1, keepdims=True)


def attn_fwd(q, kv, *, name):
    T, D = q.shape
    M = kv.shape[0]
    hd = D // XATTN_HEADS
    scale = hd ** -0.5
    tq = _tile(T, (512, 256, 128))

    def body(q_ref, k_ref, v_ref, o_ref):
        p = _attn_p(q_ref[...], k_ref[...], scale)
        o_ref[...] = jnp.dot(p.astype(bf16), v_ref[...].astype(bf16), preferred_element_type=f32).astype(o_ref.dtype)

    return pl.pallas_call(
        body, name=name, grid=(XATTN_HEADS, T // tq),
        in_specs=[pl.BlockSpec((tq, hd), lambda h, i: (i, h)), pl.BlockSpec((M, hd), lambda h, i: (0, h)),
                  pl.BlockSpec((M, hd), lambda h, i: (0, XATTN_HEADS + h))],
        out_specs=pl.BlockSpec((tq, hd), lambda h, i: (i, h)),
        out_shape=jax.ShapeDtypeStruct((T, D), bf16),
        compiler_params=_cparams(("parallel", "parallel")),
    )(q, kv, kv)


def attn_bwd(q, kv, do, *, name):
    T, D = q.shape
    M = kv.shape[0]
    hd = D // XATTN_HEADS
    scale = hd ** -0.5
    tq = _tile(T, (512, 256, 128))

    def body(q_ref, k_ref, v_ref, do_ref, dq_ref, dk_ref, dv_ref):
        qv, kvv, vv, dov = q_ref[...], k_ref[...], v_ref[...], do_ref[...]
        p = _attn_p(qv, kvv, scale)
        dob = dov.astype(bf16)
        dp = lax.dot_general(dob, vv.astype(bf16), (((1,), (1,)), ((), ())), preferred_element_type=f32)
        ds = p * (dp - jnp.sum(dp * p, axis=-1, keepdims=True)) * scale
        dsb = ds.astype(bf16)
        dq_ref[...] = jnp.dot(dsb, kvv.astype(bf16), preferred_element_type=f32)

        @pl.when(pl.program_id(1) == 0)
        def _():
            dk_ref[...] = jnp.zeros_like(dk_ref)
            dv_ref[...] = jnp.zeros_like(dv_ref)

        dk_ref[...] += lax.dot_general(dsb, qv.astype(bf16), (((0,), (0,)), ((), ())), preferred_element_type=f32)
        dv_ref[...] += lax.dot_general(p.astype(bf16), dob, (((0,), (0,)), ((), ())), preferred_element_type=f32)

    qspec = pl.BlockSpec((tq, hd), lambda h, i: (i, h))
    mspec = pl.BlockSpec((M, hd), lambda h, i: (0, h))
    return pl.pallas_call(
        body, name=name, grid=(XATTN_HEADS, T // tq),
        in_specs=[qspec, mspec, pl.BlockSpec((M, hd), lambda h, i: (0, XATTN_HEADS + h)), qspec],
        out_specs=[qspec, mspec, mspec],
        out_shape=[jax.ShapeDtypeStruct((T, D), f32), jax.ShapeDtypeStruct((M, D), f32),
                   jax.ShapeDtypeStruct((M, D), f32)],
        compiler_params=_cparams(("parallel", "arbitrary")),
    )(q, kv, kv, do)


def final_loss(x, tgt, g, *, name):
    T, D = x.shape
    tt = min(256, T)

    def body(x_ref, t_ref, g_ref, dx_ref, dg_ref, loss_ref):
        tv = t_ref[...]

        def f(xv, gv):
            e = _rms_fn(xv, gv) - tv
            return 0.5 * jnp.sum(jnp.mean(e * e, axis=-1))

        val, vjp = jax.vjp(f, x_ref[...], g_ref[...])
        dx, dgv = vjp(jnp.ones((), f32))
        dx_ref[...] = dx

        @pl.when(pl.program_id(0) == 0)
        def _():
            dg_ref[...] = jnp.zeros_like(dg_ref)
            loss_ref[...] = jnp.zeros_like(loss_ref)

        dg_ref[...] += dgv
        loss_ref[...] += jnp.full(loss_ref.shape, val, f32)

    row = pl.BlockSpec((tt, D), lambda i: (i, 0))
    return pl.pallas_call(
        body, name=name, grid=(T // tt,),
        in_specs=[row, row, pl.BlockSpec((1, D), lambda i: (0, 0))],
        out_specs=[row, pl.BlockSpec((1, D), lambda i: (0, 0)), pl.BlockSpec((8, LANE), lambda i: (0, 0))],
        out_shape=[jax.ShapeDtypeStruct((T, D), f32), jax.ShapeDtypeStruct((1, D), f32),
                   jax.ShapeDtypeStruct((8, LANE), f32)],
        compiler_params=_cparams(("arbitrary",)),
    )(x, tgt, g)


def _place():
    x, y, c = lax.axis_index("x"), lax.axis_index("y"), lax.axis_index("c")
    chips = [(1 - x, y), (x, 1 - y), (1 - x, 1 - y)]
    return x, y, c, chips


def _rcopy(src, dst, send_sems, recv_sems, k, dev):
    return pltpu.make_async_remote_copy(src_ref=src, dst_ref=dst, send_sem=send_sems.at[k], recv_sem=recv_sems.at[k],
                                        device_id=dev, device_id_type=pl.DeviceIdType.MESH)


def _comm_call(body, name, x, out_shape, n_sems):
    return pl.pallas_call(
        body, name=name, out_shape=out_shape,
        in_specs=[pl.BlockSpec(memory_space=pl.ANY)], out_specs=pl.BlockSpec(memory_space=pl.ANY),
        scratch_shapes=_comm_scratch(n_sems),
    )(x)


AG_SEMS = 7
CX_SEMS = 3


def _comm_scratch(n_sems):
    return [pltpu.SemaphoreType.DMA((n_sems,)), pltpu.SemaphoreType.DMA((n_sems,)), pltpu.SemaphoreType.DMA]


def _ag_first(x_ref, o_ref, send_sems, recv_sems, local_sem):
    x_, y_, c_, chips = _place()
    me = o_ref.at[4 * x_ + 2 * y_ + c_]
    copies = [pltpu.make_async_copy(x_ref, me, local_sem), _rcopy(x_ref, me, send_sems, recv_sems, 0, (x_, y_, 1 - c_))]
    copies += [_rcopy(x_ref, me, send_sems, recv_sems, 1 + j, (*chip, c_)) for j, chip in enumerate(chips)]
    return copies


def _ag_start(x_ref, o_ref, send_sems, recv_sems, local_sem):
    for cp in _ag_first(x_ref, o_ref, send_sems, recv_sems, local_sem):
        cp.start()


def _ag_finish(x_ref, o_ref, send_sems, recv_sems, local_sem):
    x_, y_, c_, chips = _place()
    sibling = (x_, y_, 1 - c_)
    slot = lambda px, py, pc: o_ref.at[4 * px + 2 * py + pc]
    passed = [_rcopy(slot(*chip, c_), slot(*chip, c_), send_sems, recv_sems, 4 + j, sibling)
              for j, chip in enumerate(chips)]
    for j, chip in enumerate(chips):
        _rcopy(x_ref, slot(*chip, c_), send_sems, recv_sems, 1 + j, (*chip, c_)).wait_recv()
        passed[j].start()
    _rcopy(x_ref, slot(x_, y_, 1 - c_), send_sems, recv_sems, 0, sibling).wait_recv()
    for j, chip in enumerate(chips):
        _rcopy(x_ref, slot(*chip, 1 - c_), send_sems, recv_sems, 4 + j, sibling).wait_recv()
    first = _ag_first(x_ref, o_ref, send_sems, recv_sems, local_sem)
    for cp in first[1:] + passed:
        cp.wait_send()
    first[0].wait()


def all_gather(x, *, name):
    def body(*refs):
        _ag_start(*refs)
        _ag_finish(*refs)

    return _comm_call(body, name, x, jax.ShapeDtypeStruct((N_DEV,) + x.shape, x.dtype), AG_SEMS)


def pair_exchange(x, *, name):
    n = x.shape[0]

    def body(x_ref, o_ref, send_sems, recv_sems, local_sem):
        x_, y_, c_, _ = _place()
        copies = [_rcopy(x_ref.at[q, 1 - c_], o_ref.at[q], send_sems, recv_sems, q, (x_, y_, 1 - c_)) for q in range(n)]
        for cp in copies:
            cp.start()
        for cp in copies:
            cp.wait()

    return _comm_call(body, name, x, jax.ShapeDtypeStruct((n,) + x.shape[2:], x.dtype), n)


def _cx_copies(x_ref, o_ref, send_sems, recv_sems, local_sem):
    x_, y_, c_, chips = _place()
    myq = 2 * x_ + y_
    copies = [pltpu.make_async_copy(x_ref.at[myq], o_ref.at[myq], local_sem)]
    copies += [_rcopy(x_ref.at[2 * px + py], o_ref.at[myq], send_sems, recv_sems, j, (px, py, c_))
               for j, (px, py) in enumerate(chips)]
    return copies


def _cx_start(x_ref, o_ref, send_sems, recv_sems, local_sem):
    for cp in _cx_copies(x_ref, o_ref, send_sems, recv_sems, local_sem):
        cp.start()


def _cx_finish(x_ref, o_ref, send_sems, recv_sems, local_sem):
    x_, y_, c_, chips = _place()
    myq = 2 * x_ + y_
    for j, (px, py) in enumerate(chips):
        _rcopy(x_ref.at[myq], o_ref.at[2 * px + py], send_sems, recv_sems, j, (px, py, c_)).wait_recv()
    copies = _cx_copies(x_ref, o_ref, send_sems, recv_sems, local_sem)
    for cp in copies[1:]:
        cp.wait_send()
    copies[0].wait()


def chip_exchange(x, *, name):
    def body(*refs):
        _cx_start(*refs)
        _cx_finish(*refs)

    return _comm_call(body, name, x, jax.ShapeDtypeStruct(x.shape, x.dtype), CX_SEMS)


def _add_cast(a, b, *, name):
    n, _, R, C = a.shape
    tr = min(PACK_ROWS // 2, R)

    def body(a_ref, b_ref, o_ref):
        c = lax.axis_index("c")
        for q in range(n):
            o_ref[q] = (a_ref[q, c] + b_ref[q]).astype(bf16)

    return pl.pallas_call(
        body, name=name, grid=(R // tr,),
        in_specs=[pl.BlockSpec((n, 2, tr, C), lambda i: (0, 0, i, 0)), pl.BlockSpec((n, tr, C), lambda i: (0, i, 0))],
        out_specs=pl.BlockSpec((n, tr, C), lambda i: (0, i, 0)),
        out_shape=jax.ShapeDtypeStruct(b.shape, bf16), compiler_params=_cparams(("parallel",)))(a, b)


def chip_partials(pieces, *, name):
    n, R, C = pieces.shape
    by_core = pieces.reshape(n // 2, 2, R, C)
    return _add_cast(by_core, pair_exchange(by_core, name=name + "_pair"), name=name + "_add")


def reduce_scatter(pieces, *, name):
    return chip_exchange(chip_partials(pieces, name=name), name=name + "_chip")


def adamw(gparts, w, m, v, *, name, row_off=0):
    R, C = w.shape
    n_parts = gparts.shape[0]
    tr = math.gcd(math.gcd(R, row_off), PACK_ROWS)
    assert tr % 16 == 0, (name, R, row_off)
    c1 = 1.0 / (1.0 - ADAM_B1 ** ADAM_STEP)
    c2 = 1.0 / (1.0 - ADAM_B2 ** ADAM_STEP)

    def body(g_ref, w_ref, m_ref, v_ref, go_ref, d_ref, mo_ref, vo_ref):
        g = g_ref[0].astype(f32)
        for i in range(1, n_parts):
            g = g + g_ref[i].astype(f32)
        mn = ADAM_B1 * m_ref[...] + (1.0 - ADAM_B1) * g
        vn = ADAM_B2 * v_ref[...] + (1.0 - ADAM_B2) * (g * g)
        go_ref[...] = g
        mo_ref[...] = mn
        vo_ref[...] = vn
        d_ref[...] = -ADAM_LR * ((mn * c1) / (jnp.sqrt(vn * c2) + ADAM_EPS) + ADAM_WD * w_ref[...])

    blk = pl.BlockSpec((tr, C), lambda i: (i, 0))
    return pl.pallas_call(
        body, name=name, grid=(R // tr,),
        in_specs=[pl.BlockSpec((n_parts, tr, C), lambda i: (0, row_off // tr + i, 0)), blk, blk, blk],
        out_specs=[blk] * 4, out_shape=[jax.ShapeDtypeStruct((R, C), f32)] * 4,
        compiler_params=_cparams(("parallel",)),
    )(gparts, w, m, v)


def _pack(arrs, dtype, lead=None):
    nl = 1 if lead is None else lead
    blocks = []
    for a in arrs:
        n = a.size // nl
        r = -(-n // PACK_COLS)
        a = a.astype(dtype)
        if n != r * PACK_COLS:
            a = jnp.pad(a.reshape(nl, n), ((0, 0), (0, r * PACK_COLS - n)))
        blocks.append(a.reshape(nl, r, PACK_COLS))
    rows = sum(b.shape[1] for b in blocks)
    tot = -(-rows // PACK_ROWS) * PACK_ROWS
    if tot != rows:
        blocks.append(jnp.zeros((nl, tot - rows, PACK_COLS), dtype))
    buf = jnp.concatenate(blocks, axis=1)
    return buf[0] if lead is None else buf


def _split_shards(full, ax):
    shp = full.shape
    t = full.reshape(shp[:ax] + (N_DEV, shp[ax] // N_DEV) + shp[ax + 1:])
    return jnp.moveaxis(t, ax, 0)


def _owner_pieces(layers, ax):
    per_dev = layers[0].size // N_DEV
    if per_dev % PACK_COLS == 0:
        return [_split_shards(g[None], ax) for g in layers]
    return [_split_shards(jnp.stack(layers), ax)]


def _join_shards(parts, ax):
    t = jnp.moveaxis(parts, 0, ax)
    shp = t.shape
    return t.reshape(shp[:ax] + (shp[ax] * shp[ax + 1],) + shp[ax + 2:])


def _unpack(buf, shapes, lead=None):
    out, off = [], 0
    nl = 1 if lead is None else lead
    buf = buf.reshape(nl, -1, PACK_COLS)
    for s in shapes:
        n = math.prod(s)
        r = -(-n // PACK_COLS)
        blk = buf[:, off:off + r]
        if n != r * PACK_COLS:
            blk = blk.reshape(nl, r * PACK_COLS)[:, :n]
        out.append(blk.reshape(tuple(s) if lead is None else (lead,) + tuple(s)))
        off += r
    return out


def _row(v):
    return v.reshape(1, -1)


def _head_mats(D):
    e = (lax.broadcasted_iota(jnp.int32, (D, D // HEAD), 0) // HEAD
         == lax.broadcasted_iota(jnp.int32, (D, D // HEAD), 1)).astype(f32)
    return e, e.T


def rms_fwd(x, g, name, out_dtype=f32):
    return rowwise(lambda xv, gv: (_rms_fn(xv, gv),), [x], [g], name=name, out_dtype=out_dtype)[0]


def rms_bwd(x, g, dh, add, name):
    return rowwise_bwd(lambda xv, gv: (_rms_fn(xv, gv),), [x], [g], [dh], name=name, n_drow=1, n_dpar=1, add0=add,
                       tt=256)


def rwkv_fwd(x, p, vf, tag, gathers=()):
    vres = vf is not None
    D = x.shape[1]
    e, et = _head_mats(D)
    h = rms_fwd(x, p['norm_g'], tag + "_norm")
    xr, xw, xk, xv, xa, xg = colwise(_mix_fn, [(h, 0), (p['mu'], 0)], [h.shape[0]] * 6, name=tag + "_mix",
                                     nblk=D // LANE, out_dtype=bf16)
    r = mm(xr, p['w_r'], name=tag + "_r")
    k = mm(xk, p['w_k'], name=tag + "_k")
    v = mm(xv, p['w_v'], name=tag + "_v")
    th = mm(xw, p['w1'], name=tag + "_w1", act='tanh')
    lw = mm(th, p['w2'], name=tag + "_w2")
    t2 = mm(xa, p['a1'], name=tag + "_a1", out_dtype=bf16)
    aa = mm(t2, p['a2'], name=tag + "_a2")
    sg = mm(xg, p['g1'], name=tag + "_g1", act='sigmoid')
    gg = mm(sg, p['g2'], name=tag + "_g2")
    rows = [k, v, lw, aa]
    pars = [p['w0'], p['a0'], p['k_k'], p['k_a']]
    t4 = None
    if vres:
        t4 = mm(xv, p['v1'], name=tag + "_v1", out_dtype=bf16)
        vv = mm(t4, p['v2'], name=tag + "_v2")
        rows += [vv, vf]
        pars += [p['v0']]
    pars += [e, et]
    mid = functools.partial(_mid_fn, vres)
    decay, a, kk, k2, v2 = rowwise(mid, rows, pars, name=tag + "_mid")
    br, kr = rowwise(_head_dots_fn, [r, k2, kk, a], [e, et], name=tag + "_hdots")
    y, states, sas, *gathered = scan_fwd(r, decay, k2, v2, kk, a, br, kr, name=tag + "_scan", gathers=gathers)
    post_rows = [y, r, k2, v2, gg]
    post_pars = [p['ln_g'], p['ln_b'], p['r_k'], e, et]
    z = rowwise(_post_fn, post_rows, post_pars, name=tag + "_post", out_dtype=bf16)[0]
    xo = mm(z, p['w_o'], name=tag + "_o", res=x)
    saved = dict(x=x, h=h, xs=(xr, xw, xk, xv, xa, xg), r=r, th=th, t2=t2, sg=sg, t4=t4, mid_rows=rows, mid_pars=pars,
                 mid=mid, scan_in=(r, decay, k2, v2, kk, a, br, kr), states=(states, sas), post_rows=post_rows, post_pars=post_pars,
                 z=z, vres=vres)
    return xo, v2, saved, gathered


def rwkv_bwd(dxo, dvf_in, p, s, tag, exchanges=()):
    D = dxo.shape[1]
    g = {}
    xr, xw, xk, xv, xa, xg = s['xs']
    dz = mm(dxo, p['w_o'], name=tag + "_bo", tb=True)
    g['w_o'] = mm(s['z'], dxo, name=tag + "_bwo", ta=True)
    dy, dr1, dk1, dv1, dgg, g['ln_g'], g['ln_b'], g['r_k'] = rowwise_bwd(
        _post_fn, s['post_rows'], s['post_pars'], [dz], name=tag + "_bpost", n_drow=5, n_dpar=3)
    if dvf_in is not None:
        dv1 = rowwise(_add_fn, [dv1, dvf_in], [], name=tag + "_bvadd")[0]
    dsg = mm(dgg, p['g2'], name=tag + "_bg2", tb=True)
    g['g2'] = mm(s['sg'], dgg, name=tag + "_bwg2", ta=True)
    dt3 = rowwise(_dsig_fn, [dsg, s['sg']], [], name=tag + "_bdsig")[0]
    dxg = mm(dt3, p['g1'], name=tag + "_bg1", tb=True)
    g['g1'] = mm(xg, dt3, name=tag + "_bwg1", ta=True)
    dr, dw, dk2, dv2, dkk, da, *exchanged = scan_bwd(*s['scan_in'], dy, *s['states'], dr1, dk1, dv1,
                                                     name=tag + "_bscan", exchanges=exchanges)
    vres = s['vres']
    n_drow = 6 if vres else 4
    n_dpar = 5 if vres else 4
    outs = rowwise_bwd(s['mid'], s['mid_rows'], s['mid_pars'], [dw, da, dkk, dk2, dv2], name=tag + "_bmid",
                       n_drow=n_drow, n_dpar=n_dpar)
    dk, dv, dlw, daa = outs[:4]
    dvf = None
    if vres:
        dvv, dvf = outs[4:6]
        g['w0'], g['a0'], g['k_k'], g['k_a'], g['v0'] = outs[6:]
    else:
        g['w0'], g['a0'], g['k_k'], g['k_a'] = outs[4:]
    dth = mm(dlw, p['w2'], name=tag + "_bw2", tb=True)
    g['w2'] = mm(s['th'], dlw, name=tag + "_bww2", ta=True)
    dt1 = rowwise(_dtanh_fn, [dth, s['th']], [], name=tag + "_bdtanh")[0]
    dxw = mm(dt1, p['w1'], name=tag + "_bw1", tb=True)
    g['w1'] = mm(xw, dt1, name=tag + "_bww1", ta=True)
    dt2 = mm(daa, p['a2'], name=tag + "_ba2", tb=True)
    g['a2'] = mm(s['t2'], daa, name=tag + "_bwa2", ta=True)
    dxa = mm(dt2, p['a1'], name=tag + "_ba1", tb=True)
    g['a1'] = mm(xa, dt2, name=tag + "_bwa1", ta=True)
    dxv = mm(dv, p['w_v'], name=tag + "_bv", tb=True)
    g['w_v'] = mm(xv, dv, name=tag + "_bwv", ta=True)
    if vres:
        dt4 = mm(dvv, p['v2'], name=tag + "_bv2", tb=True)
        g['v2'] = mm(s['t4'], dvv, name=tag + "_bwv2", ta=True)
        dxv = mm(dt4, p['v1'], name=tag + "_bv1", tb=True, res=dxv)
        g['v1'] = mm(xv, dt4, name=tag + "_bwv1", ta=True)
    dxr = mm(dr, p['w_r'], name=tag + "_br", tb=True)
    g['w_r'] = mm(xr, dr, name=tag + "_bwr", ta=True)
    dxk = mm(dk, p['w_k'], name=tag + "_bk", tb=True)
    g['w_k'] = mm(xk, dk, name=tag + "_bwk", ta=True)
    T = dxo.shape[0]
    dh, dmu = colwise(_mix_bwd_fn, [(s['h'], 0), (p['mu'], 0), (dxr, 0), (dxw, 0), (dxk, 0), (dxv, 0), (dxa, 0),
                                    (dxg, 0)], [T, 8], name=tag + "_bmix", nblk=D // LANE)
    g['mu'] = dmu[:6]
    dx, g['norm_g'] = rms_bwd(s['x'], p['norm_g'], dh, dxo, tag + "_bnorm")
    return dx, dvf, g, exchanged


def conv_fwd(x, p, tag):
    T, D = x.shape
    nb = D // LANE
    kw = p['dw'].shape[0]
    h = rms_fwd(x, p['norm_g'], tag + "_norm", bf16)
    u = mm(h, p['w_in'], name=tag + "_in", bias=p['b_in'])
    c = colwise(functools.partial(_glu_conv_fn, kw), [(u, 0), (u, nb), (p['dw'], 0), (p['dw_b'], 0)], [T],
                name=tag + "_dw", nblk=nb)[0]
    sl = rowwise(_ln_silu_fn, [c], [p['ln_g'], p['ln_b']], name=tag + "_ln", out_dtype=bf16)[0]
    xo = mm(sl, p['w_out'], name=tag + "_out", bias=p['b_out'], res=x)
    return xo, dict(x=x, h=h, u=u, c=c, sl=sl)


def conv_bwd(dxo, p, s, tag):
    T, D = dxo.shape
    nb = D // LANE
    kw = p['dw'].shape[0]
    kpad = -(-kw // 8) * 8
    g = {}
    dsl = mm(dxo, p['w_out'], name=tag + "_bout", tb=True)
    g['w_out'] = mm(s['sl'], dxo, name=tag + "_bwout", ta=True)
    g['b_out'] = rowwise_bwd(_bias_fn, [dxo], [p['b_out']], [dxo], name=tag + "_bbout", n_drow=0, n_dpar=1)[0]
    dc, g['ln_g'], g['ln_b'] = rowwise_bwd(_ln_silu_fn, [s['c']], [p['ln_g'], p['ln_b']], [dsl], name=tag + "_bln",
                                           n_drow=1, n_dpar=2)
    u = s['u']
    du1, du2, ddw, g['dw_b'] = colwise(functools.partial(_glu_conv_bwd_fn, kw, kpad),
                                       [(u, 0), (u, nb), (p['dw'], 0), (dc, 0)], [T, T, kpad, 1],
                                       name=tag + "_bdw", nblk=nb)
    g['dw'] = ddw[:kw]
    du = jnp.concatenate([du1, du2], axis=1)
    g['b_in'] = rowwise_bwd(_bias_fn, [du], [p['b_in']], [du], name=tag + "_bbin", n_drow=0, n_dpar=1)[0]
    dh = mm(du, p['w_in'], name=tag + "_bin", tb=True)
    g['w_in'] = mm(s['h'], du, name=tag + "_bwin", ta=True)
    dx, g['norm_g'] = rms_bwd(s['x'], p['norm_g'], dh, dxo, tag + "_bnorm")
    return dx, g


def xattn_fwd(x, memn, p, tag):
    hn = rms_fwd(x, p['norm_g'], tag + "_norm", bf16)
    q = mm(hn, p['w_q'], name=tag + "_q", out_dtype=bf16)
    kv = mm(memn, p['w_kv'], name=tag + "_kv", out_dtype=bf16)
    o = attn_fwd(q, kv, name=tag + "_attn")
    xo = mm(o, p['w_o'], name=tag + "_o", res=x)
    return xo, dict(x=x, hn=hn, q=q, kv=kv, o=o)


def xattn_bwd(dxo, dmemn, memn, p, s, tag):
    g = {}
    do = mm(dxo, p['w_o'], name=tag + "_bo", tb=True)
    g['w_o'] = mm(s['o'], dxo, name=tag + "_bwo", ta=True)
    dq, dk, dv = attn_bwd(s['q'], s['kv'], do, name=tag + "_battn")
    dkv = jnp.concatenate([dk, dv], axis=1)
    dmemn = mm(dkv, p['w_kv'], name=tag + "_bkv", tb=True, res=dmemn)
    g['w_kv'] = mm(memn, dkv, name=tag + "_bwkv", ta=True)
    dhn = mm(dq, p['w_q'], name=tag + "_bq", tb=True)
    g['w_q'] = mm(s['hn'], dq, name=tag + "_bwq", ta=True)
    dx, g['norm_g'] = rms_bwd(s['x'], p['norm_g'], dhn, dxo, tag + "_bnorm")
    return dx, dmemn, g


def ffn_fwd(x, p, tag):
    T, D = x.shape
    w_dev, layer = p['w_in']
    nb = (N_DEV // 2) * w_dev.shape[2] // LANE
    kw = p['dw'].shape[0]
    hn = rms_fwd(x, p['norm_g'], tag + "_norm", bf16)
    u = mm(hn, w_dev, name=tag + "_in", b_dev=(layer * D, D))
    act = colwise(functools.partial(_ffn_act_fn, kw), [(u, 0), (u, nb), (p['dw'], 0), (p['dw'], nb)], [T],
                  name=tag + "_act", nblk=nb, out_dtype=bf16)[0]
    xo = mm(act, p['w_out'], name=tag + "_out", res=x)
    return xo, dict(x=x, hn=hn, u=u, act=act)


def ffn_bwd(dxo, p, s, tag):
    T, D = dxo.shape
    w_dev, layer = p['w_in']
    nb = (N_DEV // 2) * w_dev.shape[2] // LANE
    kw = p['dw'].shape[0]
    g = {}
    dact = mm(dxo, p['w_out'], name=tag + "_bout", tb=True)
    g['w_out'] = mm(s['act'], dxo, name=tag + "_bwout", ta=True)
    u = s['u']
    dug, duv, dwg, dwv = colwise(functools.partial(_ffn_act_bwd_fn, kw, 8),
                                 [(u, 0), (u, nb), (p['dw'], 0), (p['dw'], nb), (dact, 0)], [T, T, 8, 8],
                                 name=tag + "_bact", nblk=nb)
    g['dw'] = jnp.concatenate([dwg[:kw], dwv[:kw]], axis=1)
    du = jnp.concatenate([dug, duv], axis=1)
    dhn = mm(du, w_dev, name=tag + "_bin", tb=True, b_dev=(layer * D, D))
    g['w_in'] = mm(s['hn'], du, name=tag + "_bwin", ta=True, out_dev=True)
    dx, g['norm_g'] = rms_bwd(s['x'], p['norm_g'], dhn, dxo, tag + "_bnorm")
    return dx, g


def _lane_pad(n):
    return -(-n // LANE) * LANE


def _pad_blocks(a, axis, nblk):
    shp = a.shape
    n = shp[axis] // nblk
    t = a.reshape(shp[:axis] + (nblk, n) + shp[axis + 1:])
    pad = [(0, 0)] * t.ndim
    pad[axis + 1] = (0, _lane_pad(n) - n)
    t = jnp.pad(t, pad)
    return t.reshape(shp[:axis] + (nblk * _lane_pad(n),) + shp[axis + 1:])


def _unpad_blocks(a, axis, nblk, n):
    shp = a.shape
    t = a.reshape(shp[:axis] + (nblk, shp[axis] // nblk) + shp[axis + 1:])
    t = lax.slice_in_dim(t, 0, n, axis=axis + 1)
    return t.reshape(shp[:axis] + (nblk * n,) + shp[axis + 1:])


def _layer_params(W, layer):
    ia = ib = layer // 2
    mixer = {}
    if layer % 2 == 0:
        mixer = dict(norm_g=_row(W['norm_mix_g'][layer]), mu=W['rwkv_mu'][ia], w_r=W['rwkv_w_r'][ia],
                     w_k=W['rwkv_w_k'][ia], w_v=W['rwkv_w_v'][ia], w_o=W['rwkv_w_o'][ia], w0=_row(W['rwkv_w0'][ia]),
                     w1=W['rwkv_w1'][ia], w2=W['rwkv_w2'][ia], a0=_row(W['rwkv_a0'][ia]), a1=W['rwkv_a1'][ia],
                     a2=W['rwkv_a2'][ia], g1=W['rwkv_g1'][ia], g2=W['rwkv_g2'][ia], k_k=_row(W['rwkv_k_k'][ia]),
                     k_a=_row(W['rwkv_k_a'][ia]), r_k=_row(W['rwkv_r_k'][ia]), ln_g=_row(W['rwkv_ln_g'][ia]),
                     ln_b=_row(W['rwkv_ln_b'][ia]))
        if ia > 0:
            mixer.update(v0=_row(W['rwkv_v0'][ia - 1]), v1=W['rwkv_v1'][ia - 1], v2=W['rwkv_v2'][ia - 1])
    else:
        mixer = dict(norm_g=_row(W['norm_mix_g'][layer]), w_in=W['conv_w_in'][ib], b_in=_row(W['conv_b_in'][ib]),
                     dw=W['conv_dw'][ib], dw_b=_row(W['conv_dw_b'][ib]), ln_g=_row(W['conv_ln_g'][ib]),
                     ln_b=_row(W['conv_ln_b'][ib]), w_out=W['conv_w_out'][ib], b_out=_row(W['conv_b_out'][ib]))
    return mixer


def _rest_params(W, layer):
    xat = dict(norm_g=_row(W['norm_xattn_g'][layer]), w_q=W['xattn_w_q'][layer], w_kv=W['xattn_w_kv'][layer],
               w_o=W['xattn_w_o'][layer])
    ffn = dict(norm_g=_row(W['norm_ffn_g'][layer]), w_in=(W['ffn_w_in'], layer),
               dw=_pad_blocks(W['ffn_dw'][layer], 1, N_DEV), w_out=_pad_blocks(W['ffn_w_out'][layer], 0, N_DEV // 2))
    return xat, ffn


NATIVE = 'ffn_w_in'
EARLY = [n for n in W_NAMES if W_SPEC[n][0] is not None and (n.startswith('rwkv_') or not W_SPEC[n][1])]
LATE = [n for n in W_NAMES if W_SPEC[n][0] is not None and n not in EARLY and n != NATIVE]


def _native_rows(a, dtype):
    L, D, n = a.shape
    return jnp.pad(a.astype(dtype), ((0, 0), (0, 0), (0, _lane_pad(n) - n))).reshape(L * D, _lane_pad(n))


def _unpack_full(got, local, names):
    parts = _unpack(got, [local[n].shape for n in names], lead=N_DEV)
    return {n: _join_shards(part, W_SPEC[n][0]) for n, part in zip(names, parts)}


def _gather_early(local):
    full = {n: local[n] for n in W_NAMES if W_SPEC[n][0] is None}
    for as_bf16, dtype, tag in ((True, bf16, "ag_mat"), (False, f32, "ag_vec")):
        names = [n for n in EARLY if W_SPEC[n][1] == as_bf16]
        full.update(_unpack_full(all_gather(_pack([local[n] for n in names], dtype), name=tag), local, names))
    return full


def _step(local, x, mem, tgt):
    W = _gather_early(local)
    late_bufs = (_pack([local[n] for n in LATE], bf16), _native_rows(local[NATIVE], bf16))
    depth = W['norm_mix_g'].shape[0]
    g_mem = _row(W['mem_norm_g'])
    memn = rms_fwd(mem, g_mem, "mem_norm", bf16)
    layers, saved = [], []
    vf = None
    for l in range(depth):
        if l % 2 == 0:
            pm = _layer_params(W, l)
            x, v, sm, gathered = rwkv_fwd(x, pm, vf, f"rw{l}", gathers=late_bufs if l == 0 else ())
            if l == 0:
                W.update(_unpack_full(gathered[0], local, LATE))
                W[NATIVE] = gathered[1]
            if vf is None:
                vf = v
        else:
            pm = _layer_params(W, l)
            x, sm = conv_fwd(x, pm, f"cv{l}")
        px, pf = _rest_params(W, l)
        x, sx = xattn_fwd(x, memn, px, f"xa{l}")
        x, sf = ffn_fwd(x, pf, f"ff{l}")
        layers.append((pm, px, pf))
        saved.append((sm, sx, sf))
    g_fin = _row(W['final_norm_g'])
    dx, dg_fin, loss_blk = final_loss(x, tgt, g_fin, name="final_loss")

    grads = {n: [None] * local[n].shape[0] for n in W_NAMES if local[n].ndim >= 2}
    grads['final_norm_g'] = dg_fin.reshape(-1)
    n_in = local[NATIVE].shape[2]
    dmemn = jnp.zeros(memn.shape, f32)
    dvf = None
    for l in reversed(range(depth)):
        pm, px, pf = layers[l]
        sm, sx, sf = saved[l]
        dx, gf = ffn_bwd(dx, pf, sf, f"ff{l}")
        dx, dmemn, gx = xattn_bwd(dx, dmemn, memn, px, sx, f"xa{l}")
        grads['norm_ffn_g'][l] = gf['norm_g'].reshape(-1)
        grads['ffn_w_in'][l] = gf['w_in']
        grads['ffn_dw'][l] = _unpad_blocks(gf['dw'], 1, N_DEV, n_in)
        grads['ffn_w_out'][l] = _unpad_blocks(gf['w_out'], 0, N_DEV // 2, n_in)
        grads['norm_xattn_g'][l] = gx['norm_g'].reshape(-1)
        grads['xattn_w_q'][l], grads['xattn_w_kv'][l], grads['xattn_w_o'][l] = gx['w_q'], gx['w_kv'], gx['w_o']
        i = l // 2
        if l % 2 == 0:
            pre = ()
            if l == 0:
                late = _pack([p for n in LATE for p in _owner_pieces(grads[n], W_SPEC[n][0])], f32, lead=N_DEV)
                native = jnp.concatenate(grads[NATIVE], axis=1)
                pre = (chip_partials(late, name="rs_late"), chip_partials(native, name="rs_ffn_in"))
            dx, dvf_l, gm, exchanged = rwkv_bwd(dx, dvf if i == 0 else None, pm, sm, f"rw{l}", exchanges=pre)
            if l == 0:
                late_parts, native_parts = exchanged
            if dvf_l is not None:
                dvf = dvf_l if dvf is None else rowwise(_add_fn, [dvf, dvf_l], [], name=f"rw{l}_dvfadd")[0]
            for short in ('mu', 'w_r', 'w_k', 'w_v', 'w_o', 'w1', 'w2', 'a1', 'a2', 'g1', 'g2'):
                grads['rwkv_' + short][i] = gm[short]
            for short in ('w0', 'a0', 'k_k', 'k_a', 'ln_g', 'ln_b'):
                grads['rwkv_' + short][i] = gm[short].reshape(-1)
            grads['rwkv_r_k'][i] = gm['r_k'].reshape(W['rwkv_r_k'].shape[1:])
            if i > 0:
                grads['rwkv_v0'][i - 1] = gm['v0'].reshape(-1)
                grads['rwkv_v1'][i - 1], grads['rwkv_v2'][i - 1] = gm['v1'], gm['v2']
        else:
            dx, gm = conv_bwd(dx, pm, sm, f"cv{l}")
            for short in ('w_in', 'dw', 'w_out'):
                grads['conv_' + short][i] = gm[short]
            for short in ('b_in', 'dw_b', 'ln_g', 'ln_b', 'b_out'):
                grads['conv_' + short][i] = gm[short].reshape(-1)
        grads['norm_mix_g'][l] = gm['norm_g'].reshape(-1)
    _, dg_mem = rowwise_bwd(lambda xv, gv: (_rms_fn(xv, gv),), [mem], [g_mem], [dmemn], name="mem_norm_b",
                            n_drow=1, n_dpar=1)
    grads['mem_norm_g'] = dg_mem.reshape(-1)
    early_grads = {n: grads[n] for n in EARLY}
    repl_grads = {n: (jnp.stack(gv) if isinstance(gv, list) else gv) for n, gv in grads.items() if W_SPEC[n][0] is None}
    return loss_blk[0, 0], dx, early_grads, repl_grads, late_parts, native_parts


def kernel(x, mem, mem_norm_g, norm_mix_g, norm_xattn_g, norm_ffn_g, final_norm_g, rwkv_mu, rwkv_w_r, rwkv_w_k, rwkv_w_v, rwkv_w_o, rwkv_w0, rwkv_w1, rwkv_w2, rwkv_a0, rwkv_a1, rwkv_a2, rwkv_g1, rwkv_g2, rwkv_k_k, rwkv_k_a, rwkv_r_k, rwkv_ln_g, rwkv_ln_b, rwkv_v0, rwkv_v1, rwkv_v2, conv_w_in, conv_b_in, conv_dw, conv_dw_b, conv_ln_g, conv_ln_b, conv_w_out, conv_b_out, xattn_w_q, xattn_w_kv, xattn_w_o, ffn_w_in, ffn_dw, ffn_w_out, loss_target, m_mem_norm_g, m_norm_mix_g, m_norm_xattn_g, m_norm_ffn_g, m_final_norm_g, m_rwkv_mu, m_rwkv_w_r, m_rwkv_w_k, m_rwkv_w_v, m_rwkv_w_o, m_rwkv_w0, m_rwkv_w1, m_rwkv_w2, m_rwkv_a0, m_rwkv_a1, m_rwkv_a2, m_rwkv_g1, m_rwkv_g2, m_rwkv_k_k, m_rwkv_k_a, m_rwkv_r_k, m_rwkv_ln_g, m_rwkv_ln_b, m_rwkv_v0, m_rwkv_v1, m_rwkv_v2, m_conv_w_in, m_conv_b_in, m_conv_dw, m_conv_dw_b, m_conv_ln_g, m_conv_ln_b, m_conv_w_out, m_conv_b_out, m_xattn_w_q, m_xattn_w_kv, m_xattn_w_o, m_ffn_w_in, m_ffn_dw, m_ffn_w_out, v_mem_norm_g, v_norm_mix_g, v_norm_xattn_g, v_norm_ffn_g, v_final_norm_g, v_rwkv_mu, v_rwkv_w_r, v_rwkv_w_k, v_rwkv_w_v, v_rwkv_w_o, v_rwkv_w0, v_rwkv_w1, v_rwkv_w2, v_rwkv_a0, v_rwkv_a1, v_rwkv_a2, v_rwkv_g1, v_rwkv_g2, v_rwkv_k_k, v_rwkv_k_a, v_rwkv_r_k, v_rwkv_ln_g, v_rwkv_ln_b, v_rwkv_v0, v_rwkv_v1, v_rwkv_v2, v_conv_w_in, v_conv_b_in, v_conv_dw, v_conv_dw_b, v_conv_ln_g, v_conv_ln_b, v_conv_w_out, v_conv_b_out, v_xattn_w_q, v_xattn_w_kv, v_xattn_w_o, v_ffn_w_in, v_ffn_dw, v_ffn_w_out):
    given = dict(locals())
    local = {n: given[n] for n in W_NAMES}
    loss_local, dx, early_grads, grads, late_parts, native_parts = _step(local, x[0], mem[0], loss_target[0])
    loss = lax.psum(loss_local, ("x", "y", "c"))

    repl = [n for n in W_NAMES if W_SPEC[n][0] is None]
    out = {}
    kinds = ("grad_", "delta_", "new_m_", "new_v_")

    res = adamw(native_parts, *[_native_rows(given[pre + NATIVE], f32) for pre in ("", "m_", "v_")],
                name="adamw_ffn_in")
    shp = given[NATIVE].shape
    for kind, buf in zip(kinds, res):
        out[kind + NATIVE] = buf.reshape(shp[0], shp[1], -1)[:, :, :shp[2]]

    early_parts = reduce_scatter(
        _pack([p for n in EARLY for p in _owner_pieces(early_grads[n], W_SPEC[n][0])], f32, lead=N_DEV), name="rs_early")
    res = adamw(early_parts, *[_pack([given[pre + n] for n in EARLY], f32) for pre in ("", "m_", "v_")],
                name="adamw_early")
    for kind, buf in zip(kinds, res):
        for n, arr in zip(EARLY, _unpack(buf, [given[n].shape for n in EARLY])):
            out[kind + n] = arr
    row_off = 0
    for n in LATE:
        shp = given[n].shape
        res = adamw(late_parts, *[given[pre + n].reshape(-1, PACK_COLS) for pre in ("", "m_", "v_")],
                    name="adamw_" + n, row_off=row_off)
        for kind, buf in zip(kinds, res):
            out[kind + n] = buf.reshape(shp)
        row_off += given[n].size // PACK_COLS

    parts = all_gather(_pack([grads[n] for n in repl], f32), name="grad_gather_repl")
    res = adamw(parts, *[_pack([given[pre + n] for n in repl], f32) for pre in ("", "m_", "v_")],
                name="adamw_repl")
    for kind, buf in zip(("grad_", "delta_", "new_m_", "new_v_"), res):
        for n, arr in zip(repl, _unpack(buf, [given[n].shape for n in repl])):
            out[kind + n] = arr

    return (loss, dx[None], *[out[kind + n] for kind in ("grad_", "delta_", "new_m_", "new_v_") for n in W_NAMES])
```

```python
import functools
import math

import jax
import jax.numpy as jnp
from jax import lax
from jax.experimental import pallas as pl
from jax.experimental.pallas import tpu as pltpu

f32 = jnp.float32
bf16 = jnp.bfloat16

N_DEV = 8
HEAD = 64
XATTN_HEADS = 4
NORM_EPS = 1e-6
LN_EPS = 1e-5
GN_EPS = 64e-5
ADAM_LR, ADAM_B1, ADAM_B2, ADAM_EPS, ADAM_WD, ADAM_STEP = 0.001, 0.9, 0.999, 1e-08, 0.01, 10
LANE = 128
PACK_COLS = 1024
PACK_ROWS = 256
VMEM_LIMIT = 48 * 1024 * 1024
SCAN_CHUNK = 16
MXU_DIM = 256

W_SPEC = {
    'mem_norm_g': (None, False), 'norm_mix_g': (None, False), 'norm_xattn_g': (None, False),
    'norm_ffn_g': (None, False), 'final_norm_g': (None, False),
    'rwkv_mu': (2, False), 'rwkv_w_r': (1, True), 'rwkv_w_k': (1, True), 'rwkv_w_v': (1, True),
    'rwkv_w_o': (1, True), 'rwkv_w0': (None, False), 'rwkv_w1': (1, True), 'rwkv_w2': (2, True),
    'rwkv_a0': (None, False), 'rwkv_a1': (1, True), 'rwkv_a2': (2, True), 'rwkv_g1': (1, True),
    'rwkv_g2': (2, True), 'rwkv_k_k': (None, False), 'rwkv_k_a': (None, False), 'rwkv_r_k': (None, False),
    'rwkv_ln_g': (None, False), 'rwkv_ln_b': (None, False), 'rwkv_v0': (None, False),
    'rwkv_v1': (1, True), 'rwkv_v2': (2, True),
    'conv_w_in': (2, True), 'conv_b_in': (1, False), 'conv_dw': (2, False), 'conv_dw_b': (1, False),
    'conv_ln_g': (1, False), 'conv_ln_b': (1, False), 'conv_w_out': (1, True), 'conv_b_out': (1, False),
    'xattn_w_q': (1, True), 'xattn_w_kv': (2, True), 'xattn_w_o': (1, True),
    'ffn_w_in': (2, True), 'ffn_dw': (2, False), 'ffn_w_out': (1, True),
}
W_NAMES = list(W_SPEC)


def _tile(n, prefs):
    for p in prefs:
        if n % p == 0:
            return p
    return n


def _cparams(sem):
    return pltpu.CompilerParams(dimension_semantics=sem, vmem_limit_bytes=VMEM_LIMIT)


def _sigmoid(x):
    return 1.0 / (1.0 + jnp.exp(-x))


def _softplus(x):
    return jnp.maximum(x, 0.0) + jnp.log(1.0 + jnp.exp(-jnp.abs(x)))


def mm(a, b, *, name, ta=False, tb=False, bias=None, res=None, act=None, b_dev=None, out_dev=False, out_dtype=f32):
    M, K = (a.shape[1], a.shape[0]) if ta else a.shape
    tm = _tile(M, (1024, 512, 256, 128))
    if b_dev is None:
        N = b.shape[0] if tb else b.shape[1]
        assert (b.shape[1] if tb else b.shape[0]) == K, (name, a.shape, b.shape)
        tn = _tile(N, (1024, 512, 256, 128))
        tk = _tile(K, (1024, 512, 256, 128))
    else:
        b_off, b_rows = b_dev
        width = b.shape[2]
        if tb:
            N, tk = b_rows, width
            tn = _tile(N, (1024, 512, 256, 128))
            assert K == N_DEV * width and b_off % tn == 0, (name, a.shape, b.shape)
        else:
            N, tn = N_DEV * width, width
            tk = _tile(K, (1024, 512, 256, 128))
            assert K == b_rows and b_off % tk == 0, (name, a.shape, b.shape)
    if out_dev:
        tn = N // N_DEV
    nk = K // tk
    dims = (((0 if ta else 1,), (1 if tb else 0,)), ((), ()))
    has_bias, has_res = bias is not None, res is not None

    def body(*refs):
        a_ref, b_ref = refs[0], refs[1]
        pos = 2
        bias_ref = res_ref = None
        if has_bias:
            bias_ref = refs[pos]; pos += 1
        if has_res:
            res_ref = refs[pos]; pos += 1
        o_ref, acc_ref = refs[pos], refs[pos + 1]
        kstep = pl.program_id(2)

        @pl.when(kstep == 0)
        def _():
            acc_ref[...] = jnp.zeros_like(acc_ref)

        acc_ref[...] += lax.dot_general(a_ref[...].astype(bf16), b_ref[...].astype(bf16), dims,
                                        preferred_element_type=f32)

        @pl.when(kstep == nk - 1)
        def _():
            out = acc_ref[...]
            if has_bias:
                out = out + bias_ref[...]
            if act == 'tanh':
                out = jnp.tanh(out)
            elif act == 'sigmoid':
                out = _sigmoid(out)
            if has_res:
                out = out + res_ref[...]
            o_ref[...] = out.astype(o_ref.dtype)

    a_spec = pl.BlockSpec((tk, tm), lambda i, j, k: (k, i)) if ta else pl.BlockSpec((tm, tk), lambda i, j, k: (i, k))
    if b_dev is None:
        b_spec = pl.BlockSpec((tn, tk), lambda i, j, k: (j, k)) if tb else pl.BlockSpec((tk, tn), lambda i, j, k: (k, j))
    elif tb:
        b_spec = pl.BlockSpec((None, tn, tk), lambda i, j, k: (k, b_off // tn + j, 0))
    else:
        b_spec = pl.BlockSpec((None, tk, tn), lambda i, j, k: (j, b_off // tk + k, 0))
    in_specs, args = [a_spec, b_spec], [a, b]
    if has_bias:
        in_specs.append(pl.BlockSpec((1, tn), lambda i, j, k: (0, j))); args.append(bias)
    if has_res:
        in_specs.append(pl.BlockSpec((tm, tn), lambda i, j, k: (i, j))); args.append(res)
    if out_dev:
        out_spec = pl.BlockSpec((None, tm, tn), lambda i, j, k: (j, i, 0))
        out_shape = jax.ShapeDtypeStruct((N_DEV, M, tn), out_dtype)
    else:
        out_spec = pl.BlockSpec((tm, tn), lambda i, j, k: (i, j))
        out_shape = jax.ShapeDtypeStruct((M, N), out_dtype)
    return pl.pallas_call(
        body, name=name, grid=(M // tm, N // tn, nk), in_specs=in_specs,
        out_specs=out_spec, out_shape=out_shape,
        scratch_shapes=[pltpu.VMEM((tm, tn), f32)],
        compiler_params=_cparams(("parallel", "parallel", "arbitrary")),
    )(*args)


def rowwise(fn, rows, pars, *, name, tt=256, out_dtype=f32):
    T = rows[0].shape[0]
    tt = min(tt, T)
    nr, npar = len(rows), len(pars)
    outs = jax.eval_shape(fn, *[jax.ShapeDtypeStruct((tt, r.shape[1]), r.dtype) for r in rows],
                          *[jax.ShapeDtypeStruct(p.shape, p.dtype) for p in pars])

    def body(*refs):
        res = fn(*[r[...] for r in refs[:nr + npar]])
        for o_ref, o in zip(refs[nr + npar:], res):
            o_ref[...] = o.astype(o_ref.dtype)

    return pl.pallas_call(
        body, name=name, grid=(T // tt,),
        in_specs=[pl.BlockSpec((tt, r.shape[1]), lambda i: (i, 0)) for r in rows]
        + [pl.BlockSpec(p.shape, lambda i: (0, 0)) for p in pars],
        out_specs=[pl.BlockSpec((tt, o.shape[1]), lambda i: (i, 0)) for o in outs],
        out_shape=[jax.ShapeDtypeStruct((T, o.shape[1]), out_dtype) for o in outs],
        compiler_params=_cparams(("parallel",)),
    )(*rows, *pars)


def rowwise_bwd(fn, rows, pars, cots, *, name, n_drow, n_dpar, add0=None, tt=128):
    T = rows[0].shape[0]
    tt = min(tt, T)
    nr, npar, nc = len(rows), len(pars), len(cots)
    has_add = add0 is not None

    def body(*refs):
        rv = [r[...] for r in refs[:nr]]
        pv = [r[...] for r in refs[nr:nr + npar]]
        cv = [r[...] for r in refs[nr + npar:nr + npar + nc]]
        pos = nr + npar + nc
        add_ref = None
        if has_add:
            add_ref = refs[pos]; pos += 1
        drow_refs = refs[pos:pos + n_drow]
        dpar_refs = refs[pos + n_drow:pos + n_drow + n_dpar]

        def f(*d):
            return fn(*d[:n_drow], *rv[n_drow:], *d[n_drow:], *pv[n_dpar:])

        _, vjp = jax.vjp(f, *rv[:n_drow], *pv[:n_dpar])
        g = vjp(tuple(cv))
        for k in range(n_drow):
            gk = g[k]
            if k == 0 and has_add:
                gk = gk + add_ref[...]
            drow_refs[k][...] = gk

        @pl.when(pl.program_id(0) == 0)
        def _():
            for k in range(n_dpar):
                dpar_refs[k][...] = jnp.zeros_like(dpar_refs[k])

        for k in range(n_dpar):
            dpar_refs[k][...] += g[n_drow + k]

    row_spec = lambda r: pl.BlockSpec((tt, r.shape[1]), lambda i: (i, 0))
    par_spec = lambda p: pl.BlockSpec(p.shape, lambda i: (0, 0))
    in_specs = [row_spec(r) for r in rows] + [par_spec(p) for p in pars] + [row_spec(c) for c in cots]
    args = [*rows, *pars, *cots]
    if has_add:
        in_specs.append(row_spec(add0)); args.append(add0)
    return pl.pallas_call(
        body, name=name, grid=(T // tt,), in_specs=in_specs,
        out_specs=[row_spec(r) for r in rows[:n_drow]] + [par_spec(p) for p in pars[:n_dpar]],
        out_shape=[jax.ShapeDtypeStruct(r.shape, f32) for r in rows[:n_drow]]
        + [jax.ShapeDtypeStruct(p.shape, f32) for p in pars[:n_dpar]],
        compiler_params=_cparams(("arbitrary",)),
    )(*args)


def colwise(fn, cols, out_rows, *, name, nblk, out_dtype=f32):
    def body(*refs):
        res = fn(*[r[...] for r in refs[:len(cols)]])
        for o_ref, o in zip(refs[len(cols):], res):
            o_ref[...] = o.astype(o_ref.dtype)

    def spec(rows, off):
        return pl.BlockSpec((rows, LANE), lambda j: (0, j + off))

    return pl.pallas_call(
        body, name=name, grid=(nblk,),
        in_specs=[spec(a.shape[0], off) for a, off in cols],
        out_specs=[spec(r, 0) for r in out_rows],
        out_shape=[jax.ShapeDtypeStruct((r, nblk * LANE), out_dtype) for r in out_rows],
        compiler_params=_cparams(("parallel",)),
    )(*[a for a, _ in cols])


def _shift_dn(x, s):
    if s == 0:
        return x
    rid = lax.broadcasted_iota(jnp.int32, x.shape, 0)
    return jnp.where(rid >= s, pltpu.roll(x, s, 0), 0.0)


def _shift_up(x, s):
    if s == 0:
        return x
    n = x.shape[0]
    rid = lax.broadcasted_iota(jnp.int32, x.shape, 0)
    return jnp.where(rid < n - s, pltpu.roll(x, n - s, 0), 0.0)


def _colsum(x):
    return jnp.sum(x, axis=0, keepdims=True)


def _stack_rows(rows, n):
    c = rows[0].shape[1]
    rid = lax.broadcasted_iota(jnp.int32, (n, c), 0)
    out = jnp.zeros((n, c), f32)
    for i, r in enumerate(rows):
        out = jnp.where(rid == i, jnp.broadcast_to(r, (n, c)), out)
    return out


def _dwconv(x, w, kw):
    acc = None
    for k in range(kw):
        term = w[k:k + 1, :] * _shift_dn(x, kw - 1 - k)
        acc = term if acc is None else acc + term
    return acc


def _dwconv_bwd(x, w, dy, kw, pad_rows):
    dx = None
    rows = []
    for k in range(kw):
        s = kw - 1 - k
        rows.append(_colsum(dy * _shift_dn(x, s)))
        term = w[k:k + 1, :] * _shift_up(dy, s)
        dx = term if dx is None else dx + term
    return dx, _stack_rows(rows, pad_rows)


def _mix_fn(h, mu):
    xx = _shift_dn(h, 1) - h
    return tuple(h + xx * mu[i:i + 1, :] for i in range(6))


def _mix_bwd_fn(h, mu, *ds):
    xx = _shift_dn(h, 1) - h
    s1 = ds[0]
    s2 = ds[0] * mu[0:1, :]
    rows = [_colsum(ds[0] * xx)]
    for i in range(1, 6):
        s1 = s1 + ds[i]
        s2 = s2 + ds[i] * mu[i:i + 1, :]
        rows.append(_colsum(ds[i] * xx))
    return s1 - s2 + _shift_up(s2, 1), _stack_rows(rows, 8)


def _glu_conv_fn(kw, u1, u2, w, b):
    return (_dwconv(u1 * _sigmoid(u2), w, kw) + b,)


def _glu_conv_bwd_fn(kw, pad_rows, u1, u2, w, dc):
    sig = _sigmoid(u2)
    g = u1 * sig
    dg, dw = _dwconv_bwd(g, w, dc, kw, pad_rows)
    return dg * sig, dg * g * (1.0 - sig), dw, _colsum(dc)


def _ffn_act_fn(kw, ug, uv, wg, wv):
    gc = _dwconv(ug, wg, kw)
    vc = _dwconv(uv, wv, kw)
    return (gc * _sigmoid(gc) * vc,)


def _ffn_act_bwd_fn(kw, pad_rows, ug, uv, wg, wv, dact):
    gc = _dwconv(ug, wg, kw)
    vc = _dwconv(uv, wv, kw)
    sg = _sigmoid(gc)
    dvc = dact * gc * sg
    dgc = dact * vc * (sg * (1.0 + gc * (1.0 - sg)))
    dug, dwg = _dwconv_bwd(ug, wg, dgc, kw, pad_rows)
    duv, dwv = _dwconv_bwd(uv, wv, dvc, kw, pad_rows)
    return dug, duv, dwg, dwv


def _rms_fn(x, g):
    return x * lax.rsqrt(jnp.mean(x * x, axis=-1, keepdims=True) + NORM_EPS) * g


def _hsum(x, e, et):
    s = jnp.dot(x, e, precision=lax.Precision.HIGH, preferred_element_type=f32)
    return jnp.dot(s, et, precision=lax.Precision.HIGH, preferred_element_type=f32)


def _mid_fn(vres, k, v, lw, aa, *rest):
    if vres:
        vv, vf, w0, a0, k_k, k_a, v0, e, et = rest
    else:
        w0, a0, k_k, k_a, e, et = rest
    logw = -_softplus(-(w0 + lw)) - 0.5
    decay = jnp.exp(-jnp.exp(logw))
    a = _sigmoid(a0 + aa)
    kk = k * k_k
    kk = kk / jnp.maximum(jnp.sqrt(_hsum(kk * kk, e, et)), 1e-12)
    k2 = k * (1.0 + (a - 1.0) * k_a)
    v2 = v + (vf - v) * _sigmoid(v0 + vv) if vres else v
    return decay, a, kk, k2, v2


def _post_fn(y, r, k2, v2, gg, ln_g, ln_b, rk, e, et):
    inv = 1.0 / HEAD
    yc = y - _hsum(y, e, et) * inv
    var = _hsum(yc * yc, e, et) * inv
    yn = yc * lax.rsqrt(var + GN_EPS) * ln_g + ln_b
    bonus = _hsum(r * k2 * rk, e, et) * v2
    return ((yn + bonus) * gg,)


def _ln_silu_fn(c, g, b):
    mu = jnp.mean(c, axis=-1, keepdims=True)
    var = jnp.mean(jnp.square(c - mu), axis=-1, keepdims=True)
    ln = (c - mu) * lax.rsqrt(var + LN_EPS) * g + b
    return (ln * _sigmoid(ln),)


def _bias_fn(x, b):
    return (x + b,)


def _dtanh_fn(d, th):
    return (d * (1.0 - th * th),)


def _dsig_fn(d, sg):
    return (d * sg * (1.0 - sg),)


def _add_fn(a, b):
    return (a + b,)


def _seg(blocks, bd, coarse=()):
    def side_by_side(parts):
        h = len(parts) // 2
        return jnp.concatenate([jnp.concatenate(parts[:h], axis=0), jnp.concatenate(parts[h:], axis=0)], axis=1)

    def apart(res, count):
        h = count // 2
        return ([res[i * HEAD:(i + 1) * HEAD, :LANE] for i in range(h)]
                + [res[i * HEAD:(i + 1) * HEAD, LANE:] for i in range(h)])

    x = side_by_side(blocks)
    n = x.shape[0]
    h0 = x.astype(bf16)
    h1 = (x - h0.astype(f32)).astype(bf16)
    lhs = [h0, h1] + ([side_by_side(coarse).astype(bf16)] if coarse else [])
    out = jnp.dot(jnp.concatenate(lhs, axis=0), bd, preferred_element_type=f32)
    fine = apart(out[n:2 * n] + out[0:n], len(blocks))
    return fine + (apart(out[2 * n:], len(coarse)) if coarse else [])


def _scan_consts():
    li = lax.broadcasted_iota(jnp.int32, (MXU_DIM, MXU_DIM), 0) // HEAD
    lj = lax.broadcasted_iota(jnp.int32, (MXU_DIM, MXU_DIM), 1) // HEAD
    bd = (li == lj).astype(bf16)
    si = lax.broadcasted_iota(jnp.int32, (HEAD, LANE), 0)
    sj = lax.broadcasted_iota(jnp.int32, (HEAD, LANE), 1) % HEAD
    dg = (si == sj).astype(f32)
    return bd, dg


def _scan_dims(T, D):
    return D // LANE, min(SCAN_CHUNK, T)


def _head_dots_fn(r, k, kk, a, e, et):
    return _hsum(kk * a * r, e, et), _hsum(k * r, e, et)


def scan_fwd(r, w, k, v, kk, a, br, kr, *, name, gathers=()):
    T, D = r.shape
    G, tc = _scan_dims(T, D)
    nch = T // tc
    ng = len(gathers)
    bd, dg = _scan_consts()

    def body(*refs):
        r_ref, w_ref, k_ref, v_ref, kk_ref, a_ref, br_ref, kr_ref, bd_ref, dg_ref = refs[:10]
        y_ref, st_ref, sa_ref = refs[10 + ng:13 + ng]
        s_ref = refs[13 + 2 * ng]
        jobs = [(refs[10 + i], refs[13 + ng + i], *refs[14 + 2 * ng + 3 * i:17 + 2 * ng + 3 * i]) for i in range(ng)]

        @pl.when(pl.program_id(0) == 0)
        def _():
            s_ref[...] = jnp.zeros_like(s_ref)
            for job in jobs:
                _ag_start(*job)

        bdv, dgv = bd_ref[...], dg_ref[...]

        def step(t, carry):
            row = pl.ds(t, 1)
            rr, ww, kr_, vr, kkr, ar, brr, krr = (x[row, :] for x in (r_ref, w_ref, k_ref, v_ref, kk_ref, a_ref,
                                                                    br_ref, kr_ref))
            bb = kkr * ar
            wr = ww * rr
            sl = [slice(g * LANE, (g + 1) * LANE) for g in range(G)]
            ps = [s_ref[g] for g in range(G)]
            blocks = [ps[g] * (-kkr[:, sl[g]]) for g in range(G)]
            vds = [ps[g] * wr[:, sl[g]] for g in range(G)]
            vds += [jnp.broadcast_to(vr[:, sl[g]], (HEAD, LANE)) * dgv for g in range(G)]
            res = _seg(blocks, bdv, vds)
            yrows = []
            for g in range(G):
                sab, ub, vb = res[g], res[G + g], res[2 * G + g]
                sn = ps[g] * ww[:, sl[g]] + sab * bb[:, sl[g]] + vb * kr_[:, sl[g]]
                s_ref[g] = sn
                st_ref[t, g] = sn
                sa_ref[t, g] = sab
                yb = ub + sab * brr[:, sl[g]] + vb * krr[:, sl[g]]
                yrows.append(_colsum(yb * dgv))
            y_ref[row, :] = jnp.concatenate(yrows, axis=1)
            return carry

        lax.fori_loop(0, tc, step, 0)

        @pl.when(pl.program_id(0) == nch - 1)
        def _():
            for job in jobs:
                _ag_finish(*job)

    vec = pl.BlockSpec((tc, D), lambda c: (c, 0))
    big = pl.BlockSpec((tc, G, HEAD, LANE), lambda c: (c, 0, 0, 0))
    hbm = pl.BlockSpec(memory_space=pl.ANY)
    return pl.pallas_call(
        body, name=name, grid=(nch,),
        in_specs=[vec] * 8 + [pl.BlockSpec((MXU_DIM, MXU_DIM), lambda c: (0, 0)), pl.BlockSpec((HEAD, LANE), lambda c: (0, 0))]
        + [hbm] * ng,
        out_specs=[vec, big, big] + [hbm] * ng,
        out_shape=[jax.ShapeDtypeStruct((T, D), f32)] + [jax.ShapeDtypeStruct((T, G, HEAD, LANE), f32)] * 2
        + [jax.ShapeDtypeStruct((N_DEV,) + x.shape, x.dtype) for x in gathers],
        scratch_shapes=[pltpu.VMEM((G, HEAD, LANE), f32)] + _comm_scratch(AG_SEMS) * ng,
        compiler_params=_cparams(("arbitrary",)),
    )(r, w, k, v, kk, a, br, kr, bd, dg, *gathers)


def scan_bwd(r, w, k, v, kk, a, br, kr, dy, states, sas, dr0, dk0, dv0, *, name, exchanges=()):
    T, D = r.shape
    G, tc = _scan_dims(T, D)
    nch = T // tc
    ne = len(exchanges)
    bd, dg = _scan_consts()

    def body(*refs):
        (r_ref, w_ref, k_ref, v_ref, kk_ref, a_ref, br_ref, kr_ref, dy_ref, st_ref, prev_ref, sa_ref,
         dr0_ref, dk0_ref, dv0_ref, bd_ref, dg_ref) = refs[:17]
        dr_ref, dw_ref, dk_ref, dv_ref, dkk_ref, da_ref = refs[17 + ne:23 + ne]
        ds_ref = refs[23 + 2 * ne]
        jobs = [(refs[17 + i], refs[23 + ne + i], *refs[24 + 2 * ne + 3 * i:27 + 2 * ne + 3 * i]) for i in range(ne)]

        @pl.when(pl.program_id(0) == 0)
        def _():
            ds_ref[...] = jnp.zeros_like(ds_ref)
            for job in jobs:
                _cx_start(*job)

        bdv, dgv = bd_ref[...], dg_ref[...]

        def step_at(t, ps):
            row = pl.ds(t, 1)
            rr, ww, kr_, vr, kkr, ar, brr, krr, dyr = (x[row, :] for x in (r_ref, w_ref, k_ref, v_ref, kk_ref, a_ref,
                                                                         br_ref, kr_ref, dy_ref))
            bb = kkr * ar
            sl = [slice(g * LANE, (g + 1) * LANE) for g in range(G)]
            dr_rows, dw_rows, dk_rows, dv_rows, dkk_rows, da_rows = [], [], [], [], [], []
            dss = [ds_ref[g] for g in range(G)]
            sabs = [sa_ref[t, g] for g in range(G)]
            sts = [st_ref[t, g] for g in range(G)]
            blocks = [dss[g] * bb[:, sl[g]] for g in range(G)]
            diag = [dss[g] * kr_[:, sl[g]] for g in range(G)]
            diag += [jnp.broadcast_to(vr[:, sl[g]], (HEAD, LANE)) * dgv for g in range(G)]
            diag += [jnp.broadcast_to(dyr[:, sl[g]], (HEAD, LANE)) * dgv for g in range(G)]
            res = _seg(blocks, bdv, diag)
            for g in range(G):
                sab, vb, dyb = sabs[g], res[2 * G + g], res[3 * G + g]
                dsab = res[g] + dyb * brr[:, sl[g]]
                dvb = res[G + g] + dyb * krr[:, sl[g]]
                dst = dss[g] + dyb * rr[:, sl[g]]
                dr_rows.append(_colsum(sts[g] * dyb))
                dw_rows.append(_colsum(dst * ps[g]))
                db_row = _colsum(dst * sab)
                dk_rows.append(_colsum(dst * vb))
                dv_rows.append(_colsum(dvb * dgv))
                ds_ref[g] = dst * ww[:, sl[g]] - dsab * kkr[:, sl[g]]
                dkk_rows.append(db_row * ar[:, sl[g]] - _colsum(ps[g] * dsab))
                da_rows.append(db_row * kkr[:, sl[g]])
            cat = lambda rows: jnp.concatenate(rows, axis=1)
            dr_ref[row, :] = cat(dr_rows) + dr0_ref[row, :]
            dw_ref[row, :] = cat(dw_rows)
            dk_ref[row, :] = cat(dk_rows) + dk0_ref[row, :]
            dv_ref[row, :] = cat(dv_rows) + dv0_ref[row, :]
            dkk_ref[row, :] = cat(dkk_rows)
            da_ref[row, :] = cat(da_rows)

        def step(i, carry):
            t = tc - 1 - i
            step_at(t, [st_ref[t - 1, g] for g in range(G)])
            return carry

        lax.fori_loop(0, tc - 1, step, 0)
        first = (pl.program_id(0) < nch - 1).astype(f32)
        step_at(0, [prev_ref[0, g] * first for g in range(G)])

        @pl.when(pl.program_id(0) == nch - 1)
        def _():
            for job in jobs:
                _cx_finish(*job)

    vec = pl.BlockSpec((tc, D), lambda c: (nch - 1 - c, 0))
    big = pl.BlockSpec((tc, G, HEAD, LANE), lambda c: (nch - 1 - c, 0, 0, 0))
    prev = pl.BlockSpec((1, G, HEAD, LANE), lambda c: (jnp.maximum((nch - 1 - c) * tc - 1, 0), 0, 0, 0))
    hbm = pl.BlockSpec(memory_space=pl.ANY)
    return pl.pallas_call(
        body, name=name, grid=(nch,),
        in_specs=[vec] * 9 + [big, prev, big] + [vec] * 3
        + [pl.BlockSpec((MXU_DIM, MXU_DIM), lambda c: (0, 0)), pl.BlockSpec((HEAD, LANE), lambda c: (0, 0))] + [hbm] * ne,
        out_specs=[vec] * 6 + [hbm] * ne,
        out_shape=[jax.ShapeDtypeStruct((T, D), f32)] * 6 + [jax.ShapeDtypeStruct(x.shape, x.dtype) for x in exchanges],
        scratch_shapes=[pltpu.VMEM((G, HEAD, LANE), f32)] + _comm_scratch(CX_SEMS) * ne,
        compiler_params=_cparams(("arbitrary",)),
    )(r, w, k, v, kk, a, br, kr, dy, states, states, sas, dr0, dk0, dv0, bd, dg, *exchanges)


def _attn_p(q, k, scale):
    s = lax.dot_general(q.astype(bf16), k.astype(bf16), (((1,), (1,)), ((), ())), preferred_element_type=f32) * scale
    s = s - jnp.max(s, axis=-1, keepdims=True)
    p = jnp.exp(s)
    return p / jnp.sum(p, axis=-1, keepdims=True)


def attn_fwd(q, kv, *, name):
    T, D = q.shape
    M = kv.shape[0]
    hd = D // XATTN_HEADS
    scale = hd ** -0.5
    tq = _tile(T, (512, 256, 128))

    def body(q_ref, k_ref, v_ref, o_ref):
        p = _attn_p(q_ref[...], k_ref[...], scale)
        o_ref[...] = jnp.dot(p.astype(bf16), v_ref[...].astype(bf16), preferred_element_type=f32).astype(o_ref.dtype)

    return pl.pallas_call(
        body, name=name, grid=(XATTN_HEADS, T // tq),
        in_specs=[pl.BlockSpec((tq, hd), lambda h, i: (i, h)), pl.BlockSpec((M, hd), lambda h, i: (0, h)),
                  pl.BlockSpec((M, hd), lambda h, i: (0, XATTN_HEADS + h))],
        out_specs=pl.BlockSpec((tq, hd), lambda h, i: (i, h)),
        out_shape=jax.ShapeDtypeStruct((T, D), bf16),
        compiler_params=_cparams(("parallel", "parallel")),
    )(q, kv, kv)


def attn_bwd(q, kv, do, *, name):
    T, D = q.shape
    M = kv.shape[0]
    hd = D // XATTN_HEADS
    scale = hd ** -0.5
    tq = _tile(T, (512, 256, 128))

    def body(q_ref, k_ref, v_ref, do_ref, dq_ref, dk_ref, dv_ref):
        qv, kvv, vv, dov = q_ref[...], k_ref[...], v_ref[...], do_ref[...]
        p = _attn_p(qv, kvv, scale)
        dob = dov.astype(bf16)
        dp = lax.dot_general(dob, vv.astype(bf16), (((1,), (1,)), ((), ())), preferred_element_type=f32)
        ds = p * (dp - jnp.sum(dp * p, axis=-1, keepdims=True)) * scale
        dsb = ds.astype(bf16)
        dq_ref[...] = jnp.dot(dsb, kvv.astype(bf16), preferred_element_type=f32)

        @pl.when(pl.program_id(1) == 0)
        def _():
            dk_ref[...] = jnp.zeros_like(dk_ref)
            dv_ref[...] = jnp.zeros_like(dv_ref)

        dk_ref[...] += lax.dot_general(dsb, qv.astype(bf16), (((0,), (0,)), ((), ())), preferred_element_type=f32)
        dv_ref[...] += lax.dot_general(p.astype(bf16), dob, (((0,), (0,)), ((), ())), preferred_element_type=f32)

    qspec = pl.BlockSpec((tq, hd), lambda h, i: (i, h))
    mspec = pl.BlockSpec((M, hd), lambda h, i: (0, h))
    return pl.pallas_call(
        body, name=name, grid=(XATTN_HEADS, T // tq),
        in_specs=[qspec, mspec, pl.BlockSpec((M, hd), lambda h, i: (0, XATTN_HEADS + h)), qspec],
        out_specs=[qspec, mspec, mspec],
        out_shape=[jax.ShapeDtypeStruct((T, D), f32), jax.ShapeDtypeStruct((M, D), f32),
                   jax.ShapeDtypeStruct((M, D), f32)],
        compiler_params=_cparams(("parallel", "arbitrary")),
    )(q, kv, kv, do)


def final_loss(x, tgt, g, *, name):
    T, D = x.shape
    tt = min(256, T)

    def body(x_ref, t_ref, g_ref, dx_ref, dg_ref, loss_ref):
        tv = t_ref[...]

        def f(xv, gv):
            e = _rms_fn(xv, gv) - tv
            return 0.5 * jnp.sum(jnp.mean(e * e, axis=-1))

        val, vjp = jax.vjp(f, x_ref[...], g_ref[...])
        dx, dgv = vjp(jnp.ones((), f32))
        dx_ref[...] = dx

        @pl.when(pl.program_id(0) == 0)
        def _():
            dg_ref[...] = jnp.zeros_like(dg_ref)
            loss_ref[...] = jnp.zeros_like(loss_ref)

        dg_ref[...] += dgv
        loss_ref[...] += jnp.full(loss_ref.shape, val, f32)

    row = pl.BlockSpec((tt, D), lambda i: (i, 0))
    return pl.pallas_call(
        body, name=name, grid=(T // tt,),
        in_specs=[row, row, pl.BlockSpec((1, D), lambda i: (0, 0))],
        out_specs=[row, pl.BlockSpec((1, D), lambda i: (0, 0)), pl.BlockSpec((8, LANE), lambda i: (0, 0))],
        out_shape=[jax.ShapeDtypeStruct((T, D), f32), jax.ShapeDtypeStruct((1, D), f32),
                   jax.ShapeDtypeStruct((8, LANE), f32)],
        compiler_params=_cparams(("arbitrary",)),
    )(x, tgt, g)


def _place():
    x, y, c = lax.axis_index("x"), lax.axis_index("y"), lax.axis_index("c")
    chips = [(1 - x, y), (x, 1 - y), (1 - x, 1 - y)]
    return x, y, c, chips


def _rcopy(src, dst, send_sems, recv_sems, k, dev):
    return pltpu.make_async_remote_copy(src_ref=src, dst_ref=dst, send_sem=send_sems.at[k], recv_sem=recv_sems.at[k],
                                        device_id=dev, device_id_type=pl.DeviceIdType.MESH)


def _comm_call(body, name, x, out_shape, n_sems):
    return pl.pallas_call(
        body, name=name, out_shape=out_shape,
        in_specs=[pl.BlockSpec(memory_space=pl.ANY)], out_specs=pl.BlockSpec(memory_space=pl.ANY),
        scratch_shapes=_comm_scratch(n_sems),
    )(x)


AG_SEMS = 7
CX_SEMS = 3


def _comm_scratch(n_sems):
    return [pltpu.SemaphoreType.DMA((n_sems,)), pltpu.SemaphoreType.DMA((n_sems,)), pltpu.SemaphoreType.DMA]


def _ag_first(x_ref, o_ref, send_sems, recv_sems, local_sem):
    x_, y_, c_, chips = _place()
    me = o_ref.at[4 * x_ + 2 * y_ + c_]
    copies = [pltpu.make_async_copy(x_ref, me, local_sem), _rcopy(x_ref, me, send_sems, recv_sems, 0, (x_, y_, 1 - c_))]
    copies += [_rcopy(x_ref, me, send_sems, recv_sems, 1 + j, (*chip, c_)) for j, chip in enumerate(chips)]
    return copies


def _ag_start(x_ref, o_ref, send_sems, recv_sems, local_sem):
    for cp in _ag_first(x_ref, o_ref, send_sems, recv_sems, local_sem):
        cp.start()


def _ag_finish(x_ref, o_ref, send_sems, recv_sems, local_sem):
    x_, y_, c_, chips = _place()
    sibling = (x_, y_, 1 - c_)
    slot = lambda px, py, pc: o_ref.at[4 * px + 2 * py + pc]
    passed = [_rcopy(slot(*chip, c_), slot(*chip, c_), send_sems, recv_sems, 4 + j, sibling)
              for j, chip in enumerate(chips)]
    for j, chip in enumerate(chips):
        _rcopy(x_ref, slot(*chip, c_), send_sems, recv_sems, 1 + j, (*chip, c_)).wait_recv()
        passed[j].start()
    _rcopy(x_ref, slot(x_, y_, 1 - c_), send_sems, recv_sems, 0, sibling).wait_recv()
    for j, chip in enumerate(chips):
        _rcopy(x_ref, slot(*chip, 1 - c_), send_sems, recv_sems, 4 + j, sibling).wait_recv()
    first = _ag_first(x_ref, o_ref, send_sems, recv_sems, local_sem)
    for cp in first[1:] + passed:
        cp.wait_send()
    first[0].wait()


def all_gather(x, *, name):
    def body(*refs):
        _ag_start(*refs)
        _ag_finish(*refs)

    return _comm_call(body, name, x, jax.ShapeDtypeStruct((N_DEV,) + x.shape, x.dtype), AG_SEMS)


def pair_exchange(x, *, name):
    n = x.shape[0]

    def body(x_ref, o_ref, send_sems, recv_sems, local_sem):
        x_, y_, c_, _ = _place()
        copies = [_rcopy(x_ref.at[q, 1 - c_], o_ref.at[q], send_sems, recv_sems, q, (x_, y_, 1 - c_)) for q in range(n)]
        for cp in copies:
            cp.start()
        for cp in copies:
            cp.wait()

    return _comm_call(body, name, x, jax.ShapeDtypeStruct((n,) + x.shape[2:], x.dtype), n)


def _cx_copies(x_ref, o_ref, send_sems, recv_sems, local_sem):
    x_, y_, c_, chips = _place()
    myq = 2 * x_ + y_
    copies = [pltpu.make_async_copy(x_ref.at[myq], o_ref.at[myq], local_sem)]
    copies += [_rcopy(x_ref.at[2 * px + py], o_ref.at[myq], send_sems, recv_sems, j, (px, py, c_))
               for j, (px, py) in enumerate(chips)]
    return copies


def _cx_start(x_ref, o_ref, send_sems, recv_sems, local_sem):
    for cp in _cx_copies(x_ref, o_ref, send_sems, recv_sems, local_sem):
        cp.start()


def _cx_finish(x_ref, o_ref, send_sems, recv_sems, local_sem):
    x_, y_, c_, chips = _place()
    myq = 2 * x_ + y_
    for j, (px, py) in enumerate(chips):
        _rcopy(x_ref.at[myq], o_ref.at[2 * px + py], send_sems, recv_sems, j, (px, py, c_)).wait_recv()
    copies = _cx_copies(x_ref, o_ref, send_sems, recv_sems, local_sem)
    for cp in copies[1:]:
        cp.wait_send()
    copies[0].wait()


def chip_exchange(x, *, name):
    def body(*refs):
        _cx_start(*refs)
        _cx_finish(*refs)

    return _comm_call(body, name, x, jax.ShapeDtypeStruct(x.shape, x.dtype), CX_SEMS)


def _add_cast(a, b, *, name):
    n, _, R, C = a.shape
    tr = min(PACK_ROWS // 2, R)

    def body(a_ref, b_ref, o_ref):
        c = lax.axis_index("c")
        for q in range(n):
            o_ref[q] = (a_ref[q, c] + b_ref[q]).astype(bf16)

    return pl.pallas_call(
        body, name=name, grid=(R // tr,),
        in_specs=[pl.BlockSpec((n, 2, tr, C), lambda i: (0, 0, i, 0)), pl.BlockSpec((n, tr, C), lambda i: (0, i, 0))],
        out_specs=pl.BlockSpec((n, tr, C), lambda i: (0, i, 0)),
        out_shape=jax.ShapeDtypeStruct(b.shape, bf16), compiler_params=_cparams(("parallel",)))(a, b)


def chip_partials(pieces, *, name):
    n, R, C = pieces.shape
    by_core = pieces.reshape(n // 2, 2, R, C)
    return _add_cast(by_core, pair_exchange(by_core, name=name + "_pair"), name=name + "_add")


def reduce_scatter(pieces, *, name):
    return chip_exchange(chip_partials(pieces, name=name), name=name + "_chip")


def adamw(gparts, w, m, v, *, name, row_off=0):
    R, C = w.shape
    n_parts = gparts.shape[0]
    tr = math.gcd(math.gcd(R, row_off), PACK_ROWS)
    assert tr % 16 == 0, (name, R, row_off)
    c1 = 1.0 / (1.0 - ADAM_B1 ** ADAM_STEP)
    c2 = 1.0 / (1.0 - ADAM_B2 ** ADAM_STEP)

    def body(g_ref, w_ref, m_ref, v_ref, go_ref, d_ref, mo_ref, vo_ref):
        g = g_ref[0].astype(f32)
        for i in range(1, n_parts):
            g = g + g_ref[i].astype(f32)
        mn = ADAM_B1 * m_ref[...] + (1.0 - ADAM_B1) * g
        vn = ADAM_B2 * v_ref[...] + (1.0 - ADAM_B2) * (g * g)
        go_ref[...] = g
        mo_ref[...] = mn
        vo_ref[...] = vn
        d_ref[...] = -ADAM_LR * ((mn * c1) / (jnp.sqrt(vn * c2) + ADAM_EPS) + ADAM_WD * w_ref[...])

    blk = pl.BlockSpec((tr, C), lambda i: (i, 0))
    return pl.pallas_call(
        body, name=name, grid=(R // tr,),
        in_specs=[pl.BlockSpec((n_parts, tr, C), lambda i: (0, row_off // tr + i, 0)), blk, blk, blk],
        out_specs=[blk] * 4, out_shape=[jax.ShapeDtypeStruct((R, C), f32)] * 4,
        compiler_params=_cparams(("parallel",)),
    )(gparts, w, m, v)


def _pack(arrs, dtype, lead=None):
    nl = 1 if lead is None else lead
    blocks = []
    for a in arrs:
        n = a.size // nl
        r = -(-n // PACK_COLS)
        a = a.astype(dtype)
        if n != r * PACK_COLS:
            a = jnp.pad(a.reshape(nl, n), ((0, 0), (0, r * PACK_COLS - n)))
        blocks.append(a.reshape(nl, r, PACK_COLS))
    rows = sum(b.shape[1] for b in blocks)
    tot = -(-rows // PACK_ROWS) * PACK_ROWS
    if tot != rows:
        blocks.append(jnp.zeros((nl, tot - rows, PACK_COLS), dtype))
    buf = jnp.concatenate(blocks, axis=1)
    return buf[0] if lead is None else buf


def _split_shards(full, ax):
    shp = full.shape
    t = full.reshape(shp[:ax] + (N_DEV, shp[ax] // N_DEV) + shp[ax + 1:])
    return jnp.moveaxis(t, ax, 0)


def _owner_pieces(layers, ax):
    per_dev = layers[0].size // N_DEV
    if per_dev % PACK_COLS == 0:
        return [_split_shards(g[None], ax) for g in layers]
    return [_split_shards(jnp.stack(layers), ax)]


def _join_shards(parts, ax):
    t = jnp.moveaxis(parts, 0, ax)
    shp = t.shape
    return t.reshape(shp[:ax] + (shp[ax] * shp[ax + 1],) + shp[ax + 2:])


def _unpack(buf, shapes, lead=None):
    out, off = [], 0
    nl = 1 if lead is None else lead
    buf = buf.reshape(nl, -1, PACK_COLS)
    for s in shapes:
        n = math.prod(s)
        r = -(-n // PACK_COLS)
        blk = buf[:, off:off + r]
        if n != r * PACK_COLS:
            blk = blk.reshape(nl, r * PACK_COLS)[:, :n]
        out.append(blk.reshape(tuple(s) if lead is None else (lead,) + tuple(s)))
        off += r
    return out


def _row(v):
    return v.reshape(1, -1)


def _head_mats(D):
    e = (lax.broadcasted_iota(jnp.int32, (D, D // HEAD), 0) // HEAD
         == lax.broadcasted_iota(jnp.int32, (D, D // HEAD), 1)).astype(f32)
    return e, e.T


def rms_fwd(x, g, name, out_dtype=f32):
    return rowwise(lambda xv, gv: (_rms_fn(xv, gv),), [x], [g], name=name, out_dtype=out_dtype, tt=512)[0]


def rms_bwd(x, g, dh, add, name):
    return rowwise_bwd(lambda xv, gv: (_rms_fn(xv, gv),), [x], [g], [dh], name=name, n_drow=1, n_dpar=1, add0=add,
                       tt=256)


def rwkv_fwd(x, p, vf, tag, gathers=()):
    vres = vf is not None
    D = x.shape[1]
    e, et = _head_mats(D)
    h = rms_fwd(x, p['norm_g'], tag + "_norm")
    xr, xw, xk, xv, xa, xg = colwise(_mix_fn, [(h, 0), (p['mu'], 0)], [h.shape[0]] * 6, name=tag + "_mix",
                                     nblk=D // LANE, out_dtype=bf16)
    r = mm(xr, p['w_r'], name=tag + "_r")
    k = mm(xk, p['w_k'], name=tag + "_k")
    v = mm(xv, p['w_v'], name=tag + "_v")
    th = mm(xw, p['w1'], name=tag + "_w1", act='tanh')
    lw = mm(th, p['w2'], name=tag + "_w2")
    t2 = mm(xa, p['a1'], name=tag + "_a1", out_dtype=bf16)
    aa = mm(t2, p['a2'], name=tag + "_a2")
    sg = mm(xg, p['g1'], name=tag + "_g1", act='sigmoid')
    gg = mm(sg, p['g2'], name=tag + "_g2")
    rows = [k, v, lw, aa]
    pars = [p['w0'], p['a0'], p['k_k'], p['k_a']]
    t4 = None
    if vres:
        t4 = mm(xv, p['v1'], name=tag + "_v1", out_dtype=bf16)
        vv = mm(t4, p['v2'], name=tag + "_v2")
        rows += [vv, vf]
        pars += [p['v0']]
    pars += [e, et]
    mid = functools.partial(_mid_fn, vres)
    decay, a, kk, k2, v2 = rowwise(mid, rows, pars, name=tag + "_mid")
    br, kr = rowwise(_head_dots_fn, [r, k2, kk, a], [e, et], name=tag + "_hdots")
    y, states, sas, *gathered = scan_fwd(r, decay, k2, v2, kk, a, br, kr, name=tag + "_scan", gathers=gathers)
    post_rows = [y, r, k2, v2, gg]
    post_pars = [p['ln_g'], p['ln_b'], p['r_k'], e, et]
    z = rowwise(_post_fn, post_rows, post_pars, name=tag + "_post", out_dtype=bf16)[0]
    xo = mm(z, p['w_o'], name=tag + "_o", res=x)
    saved = dict(x=x, h=h, xs=(xr, xw, xk, xv, xa, xg), r=r, th=th, t2=t2, sg=sg, t4=t4, mid_rows=rows, mid_pars=pars,
                 mid=mid, scan_in=(r, decay, k2, v2, kk, a, br, kr), states=(states, sas), post_rows=post_rows, post_pars=post_pars,
                 z=z, vres=vres)
    return xo, v2, saved, gathered


def rwkv_bwd(dxo, dvf_in, p, s, tag, exchanges=()):
    D = dxo.shape[1]
    g = {}
    xr, xw, xk, xv, xa, xg = s['xs']
    dz = mm(dxo, p['w_o'], name=tag + "_bo", tb=True)
    g['w_o'] = mm(s['z'], dxo, name=tag + "_bwo", ta=True)
    dy, dr1, dk1, dv1, dgg, g['ln_g'], g['ln_b'], g['r_k'] = rowwise_bwd(
        _post_fn, s['post_rows'], s['post_pars'], [dz], name=tag + "_bpost", n_drow=5, n_dpar=3)
    if dvf_in is not None:
        dv1 = rowwise(_add_fn, [dv1, dvf_in], [], name=tag + "_bvadd")[0]
    dsg = mm(dgg, p['g2'], name=tag + "_bg2", tb=True)
    g['g2'] = mm(s['sg'], dgg, name=tag + "_bwg2", ta=True)
    dt3 = rowwise(_dsig_fn, [dsg, s['sg']], [], name=tag + "_bdsig")[0]
    dxg = mm(dt3, p['g1'], name=tag + "_bg1", tb=True)
    g['g1'] = mm(xg, dt3, name=tag + "_bwg1", ta=True)
    dr, dw, dk2, dv2, dkk, da, *exchanged = scan_bwd(*s['scan_in'], dy, *s['states'], dr1, dk1, dv1,
                                                     name=tag + "_bscan", exchanges=exchanges)
    vres = s['vres']
    n_drow = 6 if vres else 4
    n_dpar = 5 if vres else 4
    outs = rowwise_bwd(s['mid'], s['mid_rows'], s['mid_pars'], [dw, da, dkk, dk2, dv2], name=tag + "_bmid",
                       n_drow=n_drow, n_dpar=n_dpar)
    dk, dv, dlw, daa = outs[:4]
    dvf = None
    if vres:
        dvv, dvf = outs[4:6]
        g['w0'], g['a0'], g['k_k'], g['k_a'], g['v0'] = outs[6:]
    else:
        g['w0'], g['a0'], g['k_k'], g['k_a'] = outs[4:]
    dth = mm(dlw, p['w2'], name=tag + "_bw2", tb=True)
    g['w2'] = mm(s['th'], dlw, name=tag + "_bww2", ta=True)
    dt1 = rowwise(_dtanh_fn, [dth, s['th']], [], name=tag + "_bdtanh")[0]
    dxw = mm(dt1, p['w1'], name=tag + "_bw1", tb=True)
    g['w1'] = mm(xw, dt1, name=tag + "_bww1", ta=True)
    dt2 = mm(daa, p['a2'], name=tag + "_ba2", tb=True)
    g['a2'] = mm(s['t2'], daa, name=tag + "_bwa2", ta=True)
    dxa = mm(dt2, p['a1'], name=tag + "_ba1", tb=True)
    g['a1'] = mm(xa, dt2, name=tag + "_bwa1", ta=True)
    dxv = mm(dv, p['w_v'], name=tag + "_bv", tb=True)
    g['w_v'] = mm(xv, dv, name=tag + "_bwv", ta=True)
    if vres:
        dt4 = mm(dvv, p['v2'], name=tag + "_bv2", tb=True)
        g['v2'] = mm(s['t4'], dvv, name=tag + "_bwv2", ta=True)
        dxv = mm(dt4, p['v1'], name=tag + "_bv1", tb=True, res=dxv)
        g['v1'] = mm(xv, dt4, name=tag + "_bwv1", ta=True)
    dxr = mm(dr, p['w_r'], name=tag + "_br", tb=True)
    g['w_r'] = mm(xr, dr, name=tag + "_bwr", ta=True)
    dxk = mm(dk, p['w_k'], name=tag + "_bk", tb=True)
    g['w_k'] = mm(xk, dk, name=tag + "_bwk", ta=True)
    T = dxo.shape[0]
    dh, dmu = colwise(_mix_bwd_fn, [(s['h'], 0), (p['mu'], 0), (dxr, 0), (dxw, 0), (dxk, 0), (dxv, 0), (dxa, 0),
                                    (dxg, 0)], [T, 8], name=tag + "_bmix", nblk=D // LANE)
    g['mu'] = dmu[:6]
    dx, g['norm_g'] = rms_bwd(s['x'], p['norm_g'], dh, dxo, tag + "_bnorm")
    return dx, dvf, g, exchanged


def conv_fwd(x, p, tag):
    T, D = x.shape
    nb = D // LANE
    kw = p['dw'].shape[0]
    h = rms_fwd(x, p['norm_g'], tag + "_norm", bf16)
    u = mm(h, p['w_in'], name=tag + "_in", bias=p['b_in'])
    c = colwise(functools.partial(_glu_conv_fn, kw), [(u, 0), (u, nb), (p['dw'], 0), (p['dw_b'], 0)], [T],
                name=tag + "_dw", nblk=nb)[0]
    sl = rowwise(_ln_silu_fn, [c], [p['ln_g'], p['ln_b']], name=tag + "_ln", out_dtype=bf16)[0]
    xo = mm(sl, p['w_out'], name=tag + "_out", bias=p['b_out'], res=x)
    return xo, dict(x=x, h=h, u=u, c=c, sl=sl)


def conv_bwd(dxo, p, s, tag):
    T, D = dxo.shape
    nb = D // LANE
    kw = p['dw'].shape[0]
    kpad = -(-kw // 8) * 8
    g = {}
    dsl = mm(dxo, p['w_out'], name=tag + "_bout", tb=True)
    g['w_out'] = mm(s['sl'], dxo, name=tag + "_bwout", ta=True)
    g['b_out'] = rowwise_bwd(_bias_fn, [dxo], [p['b_out']], [dxo], name=tag + "_bbout", n_drow=0, n_dpar=1,
                             tt=256)[0]
    dc, g['ln_g'], g['ln_b'] = rowwise_bwd(_ln_silu_fn, [s['c']], [p['ln_g'], p['ln_b']], [dsl], name=tag + "_bln",
                                           n_drow=1, n_dpar=2, tt=256)
    u = s['u']
    du1, du2, ddw, g['dw_b'] = colwise(functools.partial(_glu_conv_bwd_fn, kw, kpad),
                                       [(u, 0), (u, nb), (p['dw'], 0), (dc, 0)], [T, T, kpad, 1],
                                       name=tag + "_bdw", nblk=nb)
    g['dw'] = ddw[:kw]
    du = jnp.concatenate([du1, du2], axis=1)
    g['b_in'] = rowwise_bwd(_bias_fn, [du], [p['b_in']], [du], name=tag + "_bbin", n_drow=0, n_dpar=1, tt=256)[0]
    dh = mm(du, p['w_in'], name=tag + "_bin", tb=True)
    g['w_in'] = mm(s['h'], du, name=tag + "_bwin", ta=True)
    dx, g['norm_g'] = rms_bwd(s['x'], p['norm_g'], dh, dxo, tag + "_bnorm")
    return dx, g


def xattn_fwd(x, memn, p, tag):
    hn = rms_fwd(x, p['norm_g'], tag + "_norm", bf16)
    q = mm(hn, p['w_q'], name=tag + "_q", out_dtype=bf16)
    kv = mm(memn, p['w_kv'], name=tag + "_kv", out_dtype=bf16)
    o = attn_fwd(q, kv, name=tag + "_attn")
    xo = mm(o, p['w_o'], name=tag + "_o", res=x)
    return xo, dict(x=x, hn=hn, q=q, kv=kv, o=o)


def xattn_bwd(dxo, dmemn, memn, p, s, tag):
    g = {}
    do = mm(dxo, p['w_o'], name=tag + "_bo", tb=True)
    g['w_o'] = mm(s['o'], dxo, name=tag + "_bwo", ta=True)
    dq, dk, dv = attn_bwd(s['q'], s['kv'], do, name=tag + "_battn")
    dkv = jnp.concatenate([dk, dv], axis=1)
    dmemn = mm(dkv, p['w_kv'], name=tag + "_bkv", tb=True, res=dmemn)
    g['w_kv'] = mm(memn, dkv, name=tag + "_bwkv", ta=True)
    dhn = mm(dq, p['w_q'], name=tag + "_bq", tb=True)
    g['w_q'] = mm(s['hn'], dq, name=tag + "_bwq", ta=True)
    dx, g['norm_g'] = rms_bwd(s['x'], p['norm_g'], dhn, dxo, tag + "_bnorm")
    return dx, dmemn, g


def ffn_fwd(x, p, tag):
    T, D = x.shape
    w_dev, layer = p['w_in']
    nb = (N_DEV // 2) * w_dev.shape[2] // LANE
    kw = p['dw'].shape[0]
    hn = rms_fwd(x, p['norm_g'], tag + "_norm", bf16)
    u = mm(hn, w_dev, name=tag + "_in", b_dev=(layer * D, D))
    act = colwise(functools.partial(_ffn_act_fn, kw), [(u, 0), (u, nb), (p['dw'], 0), (p['dw'], nb)], [T],
                  name=tag + "_act", nblk=nb, out_dtype=bf16)[0]
    xo = mm(act, p['w_out'], name=tag + "_out", res=x)
    return xo, dict(x=x, hn=hn, u=u, act=act)


def ffn_bwd(dxo, p, s, tag):
    T, D = dxo.shape
    w_dev, layer = p['w_in']
    nb = (N_DEV // 2) * w_dev.shape[2] // LANE
    kw = p['dw'].shape[0]
    g = {}
    dact = mm(dxo, p['w_out'], name=tag + "_bout", tb=True)
    g['w_out'] = mm(s['act'], dxo, name=tag + "_bwout", ta=True)
    u = s['u']
    dug, duv, dwg, dwv = colwise(functools.partial(_ffn_act_bwd_fn, kw, 8),
                                 [(u, 0), (u, nb), (p['dw'], 0), (p['dw'], nb), (dact, 0)], [T, T, 8, 8],
                                 name=tag + "_bact", nblk=nb)
    g['dw'] = jnp.concatenate([dwg[:kw], dwv[:kw]], axis=1)
    du = jnp.concatenate([dug, duv], axis=1)
    dhn = mm(du, w_dev, name=tag + "_bin", tb=True, b_dev=(layer * D, D))
    g['w_in'] = mm(s['hn'], du, name=tag + "_bwin", ta=True, out_dev=True)
    dx, g['norm_g'] = rms_bwd(s['x'], p['norm_g'], dhn, dxo, tag + "_bnorm")
    return dx, g


def _lane_pad(n):
    return -(-n // LANE) * LANE


def _pad_blocks(a, axis, nblk):
    shp = a.shape
    n = shp[axis] // nblk
    t = a.reshape(shp[:axis] + (nblk, n) + shp[axis + 1:])
    pad = [(0, 0)] * t.ndim
    pad[axis + 1] = (0, _lane_pad(n) - n)
    t = jnp.pad(t, pad)
    return t.reshape(shp[:axis] + (nblk * _lane_pad(n),) + shp[axis + 1:])


def _unpad_blocks(a, axis, nblk, n):
    shp = a.shape
    t = a.reshape(shp[:axis] + (nblk, shp[axis] // nblk) + shp[axis + 1:])
    t = lax.slice_in_dim(t, 0, n, axis=axis + 1)
    return t.reshape(shp[:axis] + (nblk * n,) + shp[axis + 1:])


def _layer_params(W, layer):
    ia = ib = layer // 2
    mixer = {}
    if layer % 2 == 0:
        mixer = dict(norm_g=_row(W['norm_mix_g'][layer]), mu=W['rwkv_mu'][ia], w_r=W['rwkv_w_r'][ia],
                     w_k=W['rwkv_w_k'][ia], w_v=W['rwkv_w_v'][ia], w_o=W['rwkv_w_o'][ia], w0=_row(W['rwkv_w0'][ia]),
                     w1=W['rwkv_w1'][ia], w2=W['rwkv_w2'][ia], a0=_row(W['rwkv_a0'][ia]), a1=W['rwkv_a1'][ia],
                     a2=W['rwkv_a2'][ia], g1=W['rwkv_g1'][ia], g2=W['rwkv_g2'][ia], k_k=_row(W['rwkv_k_k'][ia]),
                     k_a=_row(W['rwkv_k_a'][ia]), r_k=_row(W['rwkv_r_k'][ia]), ln_g=_row(W['rwkv_ln_g'][ia]),
                     ln_b=_row(W['rwkv_ln_b'][ia]))
        if ia > 0:
            mixer.update(v0=_row(W['rwkv_v0'][ia - 1]), v1=W['rwkv_v1'][ia - 1], v2=W['rwkv_v2'][ia - 1])
    else:
        mixer = dict(norm_g=_row(W['norm_mix_g'][layer]), w_in=W['conv_w_in'][ib], b_in=_row(W['conv_b_in'][ib]),
                     dw=W['conv_dw'][ib], dw_b=_row(W['conv_dw_b'][ib]), ln_g=_row(W['conv_ln_g'][ib]),
                     ln_b=_row(W['conv_ln_b'][ib]), w_out=W['conv_w_out'][ib], b_out=_row(W['conv_b_out'][ib]))
    return mixer


def _rest_params(W, layer):
    xat = dict(norm_g=_row(W['norm_xattn_g'][layer]), w_q=W['xattn_w_q'][layer], w_kv=W['xattn_w_kv'][layer],
               w_o=W['xattn_w_o'][layer])
    ffn = dict(norm_g=_row(W['norm_ffn_g'][layer]), w_in=(W['ffn_w_in'], layer),
               dw=_pad_blocks(W['ffn_dw'][layer], 1, N_DEV), w_out=_pad_blocks(W['ffn_w_out'][layer], 0, N_DEV // 2))
    return xat, ffn


NATIVE = 'ffn_w_in'
EARLY = [n for n in W_NAMES if W_SPEC[n][0] is not None and (n.startswith('rwkv_') or not W_SPEC[n][1])]
LATE = [n for n in W_NAMES if W_SPEC[n][0] is not None and n not in EARLY and n != NATIVE]


def _native_rows(a, dtype):
    L, D, n = a.shape
    return jnp.pad(a.astype(dtype), ((0, 0), (0, 0), (0, _lane_pad(n) - n))).reshape(L * D, _lane_pad(n))


def _unpack_full(got, local, names):
    parts = _unpack(got, [local[n].shape for n in names], lead=N_DEV)
    return {n: _join_shards(part, W_SPEC[n][0]) for n, part in zip(names, parts)}


def _gather_early(local):
    full = {n: local[n] for n in W_NAMES if W_SPEC[n][0] is None}
    for as_bf16, dtype, tag in ((True, bf16, "ag_mat"), (False, f32, "ag_vec")):
        names = [n for n in EARLY if W_SPEC[n][1] == as_bf16]
        full.update(_unpack_full(all_gather(_pack([local[n] for n in names], dtype), name=tag), local, names))
    return full


def _step(local, x, mem, tgt):
    W = _gather_early(local)
    late_bufs = (_pack([local[n] for n in LATE], bf16), _native_rows(local[NATIVE], bf16))
    depth = W['norm_mix_g'].shape[0]
    g_mem = _row(W['mem_norm_g'])
    memn = rms_fwd(mem, g_mem, "mem_norm", bf16)
    layers, saved = [], []
    vf = None
    for l in range(depth):
        if l % 2 == 0:
            pm = _layer_params(W, l)
            x, v, sm, gathered = rwkv_fwd(x, pm, vf, f"rw{l}", gathers=late_bufs if l == 0 else ())
            if l == 0:
                W.update(_unpack_full(gathered[0], local, LATE))
                W[NATIVE] = gathered[1]
            if vf is None:
                vf = v
        else:
            pm = _layer_params(W, l)
            x, sm = conv_fwd(x, pm, f"cv{l}")
        px, pf = _rest_params(W, l)
        x, sx = xattn_fwd(x, memn, px, f"xa{l}")
        x, sf = ffn_fwd(x, pf, f"ff{l}")
        layers.append((pm, px, pf))
        saved.append((sm, sx, sf))
    g_fin = _row(W['final_norm_g'])
    dx, dg_fin, loss_blk = final_loss(x, tgt, g_fin, name="final_loss")

    grads = {n: [None] * local[n].shape[0] for n in W_NAMES if local[n].ndim >= 2}
    grads['final_norm_g'] = dg_fin.reshape(-1)
    n_in = local[NATIVE].shape[2]
    dmemn = jnp.zeros(memn.shape, f32)
    dvf = None
    for l in reversed(range(depth)):
        pm, px, pf = layers[l]
        sm, sx, sf = saved[l]
        dx, gf = ffn_bwd(dx, pf, sf, f"ff{l}")
        dx, dmemn, gx = xattn_bwd(dx, dmemn, memn, px, sx, f"xa{l}")
        grads['norm_ffn_g'][l] = gf['norm_g'].reshape(-1)
        grads['ffn_w_in'][l] = gf['w_in']
        grads['ffn_dw'][l] = _unpad_blocks(gf['dw'], 1, N_DEV, n_in)
        grads['ffn_w_out'][l] = _unpad_blocks(gf['w_out'], 0, N_DEV // 2, n_in)
        grads['norm_xattn_g'][l] = gx['norm_g'].reshape(-1)
        grads['xattn_w_q'][l], grads['xattn_w_kv'][l], grads['xattn_w_o'][l] = gx['w_q'], gx['w_kv'], gx['w_o']
        i = l // 2
        if l % 2 == 0:
            pre = ()
            if l == 0:
                late = _pack([p for n in LATE for p in _owner_pieces(grads[n], W_SPEC[n][0])], f32, lead=N_DEV)
                native = jnp.concatenate(grads[NATIVE], axis=1)
                pre = (chip_partials(late, name="rs_late"), chip_partials(native, name="rs_ffn_in"))
            dx, dvf_l, gm, exchanged = rwkv_bwd(dx, dvf if i == 0 else None, pm, sm, f"rw{l}", exchanges=pre)
            if l == 0:
                late_parts, native_parts = exchanged
            if dvf_l is not None:
                dvf = dvf_l if dvf is None else rowwise(_add_fn, [dvf, dvf_l], [], name=f"rw{l}_dvfadd")[0]
            for short in ('mu', 'w_r', 'w_k', 'w_v', 'w_o', 'w1', 'w2', 'a1', 'a2', 'g1', 'g2'):
                grads['rwkv_' + short][i] = gm[short]
            for short in ('w0', 'a0', 'k_k', 'k_a', 'ln_g', 'ln_b'):
                grads['rwkv_' + short][i] = gm[short].reshape(-1)
            grads['rwkv_r_k'][i] = gm['r_k'].reshape(W['rwkv_r_k'].shape[1:])
            if i > 0:
                grads['rwkv_v0'][i - 1] = gm['v0'].reshape(-1)
                grads['rwkv_v1'][i - 1], grads['rwkv_v2'][i - 1] = gm['v1'], gm['v2']
        else:
            dx, gm = conv_bwd(dx, pm, sm, f"cv{l}")
            for short in ('w_in', 'dw', 'w_out'):
                grads['conv_' + short][i] = gm[short]
            for short in ('b_in', 'dw_b', 'ln_g', 'ln_b', 'b_out'):
                grads['conv_' + short][i] = gm[short].reshape(-1)
        grads['norm_mix_g'][l] = gm['norm_g'].reshape(-1)
    _, dg_mem = rowwise_bwd(lambda xv, gv: (_rms_fn(xv, gv),), [mem], [g_mem], [dmemn], name="mem_norm_b",
                            n_drow=1, n_dpar=1)
    grads['mem_norm_g'] = dg_mem.reshape(-1)
    early_grads = {n: grads[n] for n in EARLY}
    repl_grads = {n: (jnp.stack(gv) if isinstance(gv, list) else gv) for n, gv in grads.items() if W_SPEC[n][0] is None}
    return loss_blk[0, 0], dx, early_grads, repl_grads, late_parts, native_parts


def kernel(x, mem, mem_norm_g, norm_mix_g, norm_xattn_g, norm_ffn_g, final_norm_g, rwkv_mu, rwkv_w_r, rwkv_w_k, rwkv_w_v, rwkv_w_o, rwkv_w0, rwkv_w1, rwkv_w2, rwkv_a0, rwkv_a1, rwkv_a2, rwkv_g1, rwkv_g2, rwkv_k_k, rwkv_k_a, rwkv_r_k, rwkv_ln_g, rwkv_ln_b, rwkv_v0, rwkv_v1, rwkv_v2, conv_w_in, conv_b_in, conv_dw, conv_dw_b, conv_ln_g, conv_ln_b, conv_w_out, conv_b_out, xattn_w_q, xattn_w_kv, xattn_w_o, ffn_w_in, ffn_dw, ffn_w_out, loss_target, m_mem_norm_g, m_norm_mix_g, m_norm_xattn_g, m_norm_ffn_g, m_final_norm_g, m_rwkv_mu, m_rwkv_w_r, m_rwkv_w_k, m_rwkv_w_v, m_rwkv_w_o, m_rwkv_w0, m_rwkv_w1, m_rwkv_w2, m_rwkv_a0, m_rwkv_a1, m_rwkv_a2, m_rwkv_g1, m_rwkv_g2, m_rwkv_k_k, m_rwkv_k_a, m_rwkv_r_k, m_rwkv_ln_g, m_rwkv_ln_b, m_rwkv_v0, m_rwkv_v1, m_rwkv_v2, m_conv_w_in, m_conv_b_in, m_conv_dw, m_conv_dw_b, m_conv_ln_g, m_conv_ln_b, m_conv_w_out, m_conv_b_out, m_xattn_w_q, m_xattn_w_kv, m_xattn_w_o, m_ffn_w_in, m_ffn_dw, m_ffn_w_out, v_mem_norm_g, v_norm_mix_g, v_norm_xattn_g, v_norm_ffn_g, v_final_norm_g, v_rwkv_mu, v_rwkv_w_r, v_rwkv_w_k, v_rwkv_w_v, v_rwkv_w_o, v_rwkv_w0, v_rwkv_w1, v_rwkv_w2, v_rwkv_a0, v_rwkv_a1, v_rwkv_a2, v_rwkv_g1, v_rwkv_g2, v_rwkv_k_k, v_rwkv_k_a, v_rwkv_r_k, v_rwkv_ln_g, v_rwkv_ln_b, v_rwkv_v0, v_rwkv_v1, v_rwkv_v2, v_conv_w_in, v_conv_b_in, v_conv_dw, v_conv_dw_b, v_conv_ln_g, v_conv_ln_b, v_conv_w_out, v_conv_b_out, v_xattn_w_q, v_xattn_w_kv, v_xattn_w_o, v_ffn_w_in, v_ffn_dw, v_ffn_w_out):
    given = dict(locals())
    local = {n: given[n] for n in W_NAMES}
    loss_local, dx, early_grads, grads, late_parts, native_parts = _step(local, x[0], mem[0], loss_target[0])
    loss = lax.psum(loss_local, ("x", "y", "c"))

    repl = [n for n in W_NAMES if W_SPEC[n][0] is None]
    out = {}
    kinds = ("grad_", "delta_", "new_m_", "new_v_")

    res = adamw(native_parts, *[_native_rows(given[pre + NATIVE], f32) for pre in ("", "m_", "v_")],
                name="adamw_ffn_in")
    shp = given[NATIVE].shape
    for kind, buf in zip(kinds, res):
        out[kind + NATIVE] = buf.reshape(shp[0], shp[1], -1)[:, :, :shp[2]]

    early_parts = reduce_scatter(
        _pack([p for n in EARLY for p in _owner_pieces(early_grads[n], W_SPEC[n][0])], f32, lead=N_DEV), name="rs_early")
    res = adamw(early_parts, *[_pack([given[pre + n] for n in EARLY], f32) for pre in ("", "m_", "v_")],
                name="adamw_early")
    for kind, buf in zip(kinds, res):
        for n, arr in zip(EARLY, _unpack(buf, [given[n].shape for n in EARLY])):
            out[kind + n] = arr
    row_off = 0
    for n in LATE:
        shp = given[n].shape
        res = adamw(late_parts, *[given[pre + n].reshape(-1, PACK_COLS) for pre in ("", "m_", "v_")],
                    name="adamw_" + n, row_off=row_off)
        for kind, buf in zip(kinds, res):
            out[kind + n] = buf.reshape(shp)
        row_off += given[n].size // PACK_COLS

    parts = all_gather(_pack([grads[n] for n in repl], f32), name="grad_gather_repl")
    res = adamw(parts, *[_pack([given[pre + n] for n in repl], f32) for pre in ("", "m_", "v_")],
                name="adamw_repl")
    for kind, buf in zip(("grad_", "delta_", "new_m_", "new_v_"), res):
        for n, arr in zip(repl, _unpack(buf, [given[n].shape for n in repl])):
            out[kind + n] = arr

    return (loss, dx[None], *[out[kind + n] for kind in ("grad_", "delta_", "new_m_", "new_v_") for n in W_NAMES])
```

```python
import functools
import math

import jax
import jax.numpy as jnp
from jax import lax
from jax.experimental import pallas as pl
from jax.experimental.pallas import tpu as pltpu

f32 = jnp.float32
bf16 = jnp.bfloat16

N_DEV = 8
HEAD = 64
XATTN_HEADS = 4
NORM_EPS = 1e-6
LN_EPS = 1e-5
GN_EPS = 64e-5
ADAM_LR, ADAM_B1, ADAM_B2, ADAM_EPS, ADAM_WD, ADAM_STEP = 0.001, 0.9, 0.999, 1e-08, 0.01, 10
LANE = 128
PACK_COLS = 1024
PACK_ROWS = 256
VMEM_LIMIT = 48 * 1024 * 1024
SCAN_CHUNK = 32
MXU_DIM = 256

W_SPEC = {
    'mem_norm_g': (None, False), 'norm_mix_g': (None, False), 'norm_xattn_g': (None, False),
    'norm_ffn_g': (None, False), 'final_norm_g': (None, False),
    'rwkv_mu': (2, False), 'rwkv_w_r': (1, True), 'rwkv_w_k': (1, True), 'rwkv_w_v': (1, True),
    'rwkv_w_o': (1, True), 'rwkv_w0': (None, False), 'rwkv_w1': (1, True), 'rwkv_w2': (2, True),
    'rwkv_a0': (None, False), 'rwkv_a1': (1, True), 'rwkv_a2': (2, True), 'rwkv_g1': (1, True),
    'rwkv_g2': (2, True), 'rwkv_k_k': (None, False), 'rwkv_k_a': (None, False), 'rwkv_r_k': (None, False),
    'rwkv_ln_g': (None, False), 'rwkv_ln_b': (None, False), 'rwkv_v0': (None, False),
    'rwkv_v1': (1, True), 'rwkv_v2': (2, True),
    'conv_w_in': (2, True), 'conv_b_in': (1, False), 'conv_dw': (2, False), 'conv_dw_b': (1, False),
    'conv_ln_g': (1, False), 'conv_ln_b': (1, False), 'conv_w_out': (1, True), 'conv_b_out': (1, False),
    'xattn_w_q': (1, True), 'xattn_w_kv': (2, True), 'xattn_w_o': (1, True),
    'ffn_w_in': (2, True), 'ffn_dw': (2, False), 'ffn_w_out': (1, True),
}
W_NAMES = list(W_SPEC)


def _tile(n, prefs):
    for p in prefs:
        if n % p == 0:
            return p
    return n


def _cparams(sem):
    return pltpu.CompilerParams(dimension_semantics=sem, vmem_limit_bytes=VMEM_LIMIT)


def _sigmoid(x):
    return 1.0 / (1.0 + jnp.exp(-x))


def _softplus(x):
    return jnp.maximum(x, 0.0) + jnp.log(1.0 + jnp.exp(-jnp.abs(x)))


def mm(a, b, *, name, ta=False, tb=False, bias=None, res=None, act=None, b_dev=None, out_dev=False, out_dtype=f32):
    M, K = (a.shape[1], a.shape[0]) if ta else a.shape
    tm = _tile(M, (1024, 512, 256, 128))
    if b_dev is None:
        N = b.shape[0] if tb else b.shape[1]
        assert (b.shape[1] if tb else b.shape[0]) == K, (name, a.shape, b.shape)
        tn = _tile(N, (1024, 512, 256, 128))
        tk = _tile(K, (1024, 512, 256, 128))
    else:
        b_off, b_rows = b_dev
        width = b.shape[2]
        if tb:
            N, tk = b_rows, width
            tn = _tile(N, (1024, 512, 256, 128))
            assert K == N_DEV * width and b_off % tn == 0, (name, a.shape, b.shape)
        else:
            N, tn = N_DEV * width, width
            tk = _tile(K, (1024, 512, 256, 128))
            assert K == b_rows and b_off % tk == 0, (name, a.shape, b.shape)
    if out_dev:
        tn = N // N_DEV
    nk = K // tk
    dims = (((0 if ta else 1,), (1 if tb else 0,)), ((), ()))
    has_bias, has_res = bias is not None, res is not None

    def body(*refs):
        a_ref, b_ref = refs[0], refs[1]
        pos = 2
        bias_ref = res_ref = None
        if has_bias:
            bias_ref = refs[pos]; pos += 1
        if has_res:
            res_ref = refs[pos]; pos += 1
        o_ref, acc_ref = refs[pos], refs[pos + 1]
        kstep = pl.program_id(2)

        @pl.when(kstep == 0)
        def _():
            acc_ref[...] = jnp.zeros_like(acc_ref)

        acc_ref[...] += lax.dot_general(a_ref[...].astype(bf16), b_ref[...].astype(bf16), dims,
                                        preferred_element_type=f32)

        @pl.when(kstep == nk - 1)
        def _():
            out = acc_ref[...]
            if has_bias:
                out = out + bias_ref[...]
            if act == 'tanh':
                out = jnp.tanh(out)
            elif act == 'sigmoid':
                out = _sigmoid(out)
            if has_res:
                out = out + res_ref[...]
            o_ref[...] = out.astype(o_ref.dtype)

    a_spec = pl.BlockSpec((tk, tm), lambda i, j, k: (k, i)) if ta else pl.BlockSpec((tm, tk), lambda i, j, k: (i, k))
    if b_dev is None:
        b_spec = pl.BlockSpec((tn, tk), lambda i, j, k: (j, k)) if tb else pl.BlockSpec((tk, tn), lambda i, j, k: (k, j))
    elif tb:
        b_spec = pl.BlockSpec((None, tn, tk), lambda i, j, k: (k, b_off // tn + j, 0))
    else:
        b_spec = pl.BlockSpec((None, tk, tn), lambda i, j, k: (j, b_off // tk + k, 0))
    in_specs, args = [a_spec, b_spec], [a, b]
    if has_bias:
        in_specs.append(pl.BlockSpec((1, tn), lambda i, j, k: (0, j))); args.append(bias)
    if has_res:
        in_specs.append(pl.BlockSpec((tm, tn), lambda i, j, k: (i, j))); args.append(res)
    if out_dev:
        out_spec = pl.BlockSpec((None, tm, tn), lambda i, j, k: (j, i, 0))
        out_shape = jax.ShapeDtypeStruct((N_DEV, M, tn), out_dtype)
    else:
        out_spec = pl.BlockSpec((tm, tn), lambda i, j, k: (i, j))
        out_shape = jax.ShapeDtypeStruct((M, N), out_dtype)
    return pl.pallas_call(
        body, name=name, grid=(M // tm, N // tn, nk), in_specs=in_specs,
        out_specs=out_spec, out_shape=out_shape,
        scratch_shapes=[pltpu.VMEM((tm, tn), f32)],
        compiler_params=_cparams(("parallel", "parallel", "arbitrary")),
    )(*args)


def rowwise(fn, rows, pars, *, name, tt=256, out_dtype=f32):
    T = rows[0].shape[0]
    tt = min(tt, T)
    nr, npar = len(rows), len(pars)
    outs = jax.eval_shape(fn, *[jax.ShapeDtypeStruct((tt, r.shape[1]), r.dtype) for r in rows],
                          *[jax.ShapeDtypeStruct(p.shape, p.dtype) for p in pars])

    def body(*refs):
        res = fn(*[r[...] for r in refs[:nr + npar]])
        for o_ref, o in zip(refs[nr + npar:], res):
            o_ref[...] = o.astype(o_ref.dtype)

    return pl.pallas_call(
        body, name=name, grid=(T // tt,),
        in_specs=[pl.BlockSpec((tt, r.shape[1]), lambda i: (i, 0)) for r in rows]
        + [pl.BlockSpec(p.shape, lambda i: (0, 0)) for p in pars],
        out_specs=[pl.BlockSpec((tt, o.shape[1]), lambda i: (i, 0)) for o in outs],
        out_shape=[jax.ShapeDtypeStruct((T, o.shape[1]), out_dtype) for o in outs],
        compiler_params=_cparams(("parallel",)),
    )(*rows, *pars)


def rowwise_bwd(fn, rows, pars, cots, *, name, n_drow, n_dpar, add0=None, tt=128):
    T = rows[0].shape[0]
    tt = min(tt, T)
    nr, npar, nc = len(rows), len(pars), len(cots)
    has_add = add0 is not None

    def body(*refs):
        rv = [r[...] for r in refs[:nr]]
        pv = [r[...] for r in refs[nr:nr + npar]]
        cv = [r[...] for r in refs[nr + npar:nr + npar + nc]]
        pos = nr + npar + nc
        add_ref = None
        if has_add:
            add_ref = refs[pos]; pos += 1
        drow_refs = refs[pos:pos + n_drow]
        dpar_refs = refs[pos + n_drow:pos + n_drow + n_dpar]

        def f(*d):
            return fn(*d[:n_drow], *rv[n_drow:], *d[n_drow:], *pv[n_dpar:])

        _, vjp = jax.vjp(f, *rv[:n_drow], *pv[:n_dpar])
        g = vjp(tuple(cv))
        for k in range(n_drow):
            gk = g[k]
            if k == 0 and has_add:
                gk = gk + add_ref[...]
            drow_refs[k][...] = gk

        @pl.when(pl.program_id(0) == 0)
        def _():
            for k in range(n_dpar):
                dpar_refs[k][...] = jnp.zeros_like(dpar_refs[k])

        for k in range(n_dpar):
            dpar_refs[k][...] += g[n_drow + k]

    row_spec = lambda r: pl.BlockSpec((tt, r.shape[1]), lambda i: (i, 0))
    par_spec = lambda p: pl.BlockSpec(p.shape, lambda i: (0, 0))
    in_specs = [row_spec(r) for r in rows] + [par_spec(p) for p in pars] + [row_spec(c) for c in cots]
    args = [*rows, *pars, *cots]
    if has_add:
        in_specs.append(row_spec(add0)); args.append(add0)
    return pl.pallas_call(
        body, name=name, grid=(T // tt,), in_specs=in_specs,
        out_specs=[row_spec(r) for r in rows[:n_drow]] + [par_spec(p) for p in pars[:n_dpar]],
        out_shape=[jax.ShapeDtypeStruct(r.shape, f32) for r in rows[:n_drow]]
        + [jax.ShapeDtypeStruct(p.shape, f32) for p in pars[:n_dpar]],
        compiler_params=_cparams(("arbitrary",)),
    )(*args)


def colwise(fn, cols, out_rows, *, name, nblk, out_dtype=f32):
    def body(*refs):
        res = fn(*[r[...] for r in refs[:len(cols)]])
        for o_ref, o in zip(refs[len(cols):], res):
            o_ref[...] = o.astype(o_ref.dtype)

    def spec(rows, off):
        return pl.BlockSpec((rows, LANE), lambda j: (0, j + off))

    return pl.pallas_call(
        body, name=name, grid=(nblk,),
        in_specs=[spec(a.shape[0], off) for a, off in cols],
        out_specs=[spec(r, 0) for r in out_rows],
        out_shape=[jax.ShapeDtypeStruct((r, nblk * LANE), out_dtype) for r in out_rows],
        compiler_params=_cparams(("parallel",)),
    )(*[a for a, _ in cols])


def _shift_dn(x, s):
    if s == 0:
        return x
    rid = lax.broadcasted_iota(jnp.int32, x.shape, 0)
    return jnp.where(rid >= s, pltpu.roll(x, s, 0), 0.0)


def _shift_up(x, s):
    if s == 0:
        return x
    n = x.shape[0]
    rid = lax.broadcasted_iota(jnp.int32, x.shape, 0)
    return jnp.where(rid < n - s, pltpu.roll(x, n - s, 0), 0.0)


def _colsum(x):
    return jnp.sum(x, axis=0, keepdims=True)


def _stack_rows(rows, n):
    c = rows[0].shape[1]
    rid = lax.broadcasted_iota(jnp.int32, (n, c), 0)
    out = jnp.zeros((n, c), f32)
    for i, r in enumerate(rows):
        out = jnp.where(rid == i, jnp.broadcast_to(r, (n, c)), out)
    return out


def _dwconv(x, w, kw):
    acc = None
    for k in range(kw):
        term = w[k:k + 1, :] * _shift_dn(x, kw - 1 - k)
        acc = term if acc is None else acc + term
    return acc


def _dwconv_bwd(x, w, dy, kw, pad_rows):
    dx = None
    rows = []
    for k in range(kw):
        s = kw - 1 - k
        rows.append(_colsum(dy * _shift_dn(x, s)))
        term = w[k:k + 1, :] * _shift_up(dy, s)
        dx = term if dx is None else dx + term
    return dx, _stack_rows(rows, pad_rows)


def _mix_fn(h, mu):
    xx = _shift_dn(h, 1) - h
    return tuple(h + xx * mu[i:i + 1, :] for i in range(6))


def _mix_bwd_fn(h, mu, *ds):
    xx = _shift_dn(h, 1) - h
    s1 = ds[0]
    s2 = ds[0] * mu[0:1, :]
    rows = [_colsum(ds[0] * xx)]
    for i in range(1, 6):
        s1 = s1 + ds[i]
        s2 = s2 + ds[i] * mu[i:i + 1, :]
        rows.append(_colsum(ds[i] * xx))
    return s1 - s2 + _shift_up(s2, 1), _stack_rows(rows, 8)


def _glu_conv_fn(kw, u1, u2, w, b):
    return (_dwconv(u1 * _sigmoid(u2), w, kw) + b,)


def _glu_conv_bwd_fn(kw, pad_rows, u1, u2, w, dc):
    sig = _sigmoid(u2)
    g = u1 * sig
    dg, dw = _dwconv_bwd(g, w, dc, kw, pad_rows)
    return dg * sig, dg * g * (1.0 - sig), dw, _colsum(dc)


def _ffn_act_fn(kw, ug, uv, wg, wv):
    gc = _dwconv(ug, wg, kw)
    vc = _dwconv(uv, wv, kw)
    return (gc * _sigmoid(gc) * vc,)


def _ffn_act_bwd_fn(kw, pad_rows, ug, uv, wg, wv, dact):
    gc = _dwconv(ug, wg, kw)
    vc = _dwconv(uv, wv, kw)
    sg = _sigmoid(gc)
    dvc = dact * gc * sg
    dgc = dact * vc * (sg * (1.0 + gc * (1.0 - sg)))
    dug, dwg = _dwconv_bwd(ug, wg, dgc, kw, pad_rows)
    duv, dwv = _dwconv_bwd(uv, wv, dvc, kw, pad_rows)
    return dug, duv, dwg, dwv


def _rms_fn(x, g):
    return x * lax.rsqrt(jnp.mean(x * x, axis=-1, keepdims=True) + NORM_EPS) * g


def _hsum(x, e, et):
    s = jnp.dot(x, e, precision=lax.Precision.HIGH, preferred_element_type=f32)
    return jnp.dot(s, et, precision=lax.Precision.HIGH, preferred_element_type=f32)


def _mid_fn(vres, k, v, lw, aa, *rest):
    if vres:
        vv, vf, w0, a0, k_k, k_a, v0, e, et = rest
    else:
        w0, a0, k_k, k_a, e, et = rest
    logw = -_softplus(-(w0 + lw)) - 0.5
    decay = jnp.exp(-jnp.exp(logw))
    a = _sigmoid(a0 + aa)
    kk = k * k_k
    kk = kk / jnp.maximum(jnp.sqrt(_hsum(kk * kk, e, et)), 1e-12)
    k2 = k * (1.0 + (a - 1.0) * k_a)
    v2 = v + (vf - v) * _sigmoid(v0 + vv) if vres else v
    return decay, a, kk, k2, v2


def _post_fn(y, r, k2, v2, gg, ln_g, ln_b, rk, e, et):
    inv = 1.0 / HEAD
    yc = y - _hsum(y, e, et) * inv
    var = _hsum(yc * yc, e, et) * inv
    yn = yc * lax.rsqrt(var + GN_EPS) * ln_g + ln_b
    bonus = _hsum(r * k2 * rk, e, et) * v2
    return ((yn + bonus) * gg,)


def _ln_silu_fn(c, g, b):
    mu = jnp.mean(c, axis=-1, keepdims=True)
    var = jnp.mean(jnp.square(c - mu), axis=-1, keepdims=True)
    ln = (c - mu) * lax.rsqrt(var + LN_EPS) * g + b
    return (ln * _sigmoid(ln),)


def _bias_fn(x, b):
    return (x + b,)


def _dtanh_fn(d, th):
    return (d * (1.0 - th * th),)


def _dsig_fn(d, sg):
    return (d * sg * (1.0 - sg),)


def _add_fn(a, b):
    return (a + b,)


def _seg(blocks, bd, coarse=()):
    def side_by_side(parts):
        h = len(parts) // 2
        return jnp.concatenate([jnp.concatenate(parts[:h], axis=0), jnp.concatenate(parts[h:], axis=0)], axis=1)

    def apart(res, count):
        h = count // 2
        return ([res[i * HEAD:(i + 1) * HEAD, :LANE] for i in range(h)]
                + [res[i * HEAD:(i + 1) * HEAD, LANE:] for i in range(h)])

    x = side_by_side(blocks)
    n = x.shape[0]
    h0 = x.astype(bf16)
    h1 = (x - h0.astype(f32)).astype(bf16)
    lhs = [h0, h1] + ([side_by_side(coarse).astype(bf16)] if coarse else [])
    out = jnp.dot(jnp.concatenate(lhs, axis=0), bd, preferred_element_type=f32)
    fine = apart(out[n:2 * n] + out[0:n], len(blocks))
    return fine + (apart(out[2 * n:], len(coarse)) if coarse else [])


def _scan_consts():
    li = lax.broadcasted_iota(jnp.int32, (MXU_DIM, MXU_DIM), 0) // HEAD
    lj = lax.broadcasted_iota(jnp.int32, (MXU_DIM, MXU_DIM), 1) // HEAD
    bd = (li == lj).astype(bf16)
    si = lax.broadcasted_iota(jnp.int32, (HEAD, LANE), 0)
    sj = lax.broadcasted_iota(jnp.int32, (HEAD, LANE), 1) % HEAD
    dg = (si == sj).astype(f32)
    return bd, dg


def _scan_dims(T, D):
    return D // LANE, min(SCAN_CHUNK, T)


def _head_dots_fn(r, k, kk, a, e, et):
    return _hsum(kk * a * r, e, et), _hsum(k * r, e, et)


def scan_fwd(r, w, k, v, kk, a, br, kr, *, name, gathers=()):
    T, D = r.shape
    G, tc = _scan_dims(T, D)
    nch = T // tc
    ng = len(gathers)
    bd, dg = _scan_consts()

    def body(*refs):
        r_ref, w_ref, k_ref, v_ref, kk_ref, a_ref, br_ref, kr_ref, bd_ref, dg_ref = refs[:10]
        y_ref, st_ref, sa_ref = refs[10 + ng:13 + ng]
        s_ref = refs[13 + 2 * ng]
        jobs = [(refs[10 + i], refs[13 + ng + i], *refs[14 + 2 * ng + 3 * i:17 + 2 * ng + 3 * i]) for i in range(ng)]

        @pl.when(pl.program_id(0) == 0)
        def _():
            s_ref[...] = jnp.zeros_like(s_ref)
            for job in jobs:
                _ag_start(*job)

        bdv, dgv = bd_ref[...], dg_ref[...]

        def step(t, carry):
            row = pl.ds(t, 1)
            rr, ww, kr_, vr, kkr, ar, brr, krr = (x[row, :] for x in (r_ref, w_ref, k_ref, v_ref, kk_ref, a_ref,
                                                                    br_ref, kr_ref))
            bb = kkr * ar
            wr = ww * rr
            sl = [slice(g * LANE, (g + 1) * LANE) for g in range(G)]
            ps = [s_ref[g] for g in range(G)]
            blocks = [ps[g] * (-kkr[:, sl[g]]) for g in range(G)]
            vds = [ps[g] * wr[:, sl[g]] for g in range(G)]
            vds += [jnp.broadcast_to(vr[:, sl[g]], (HEAD, LANE)) * dgv for g in range(G)]
            res = _seg(blocks, bdv, vds)
            yrows = []
            for g in range(G):
                sab, ub, vb = res[g], res[G + g], res[2 * G + g]
                sn = ps[g] * ww[:, sl[g]] + sab * bb[:, sl[g]] + vb * kr_[:, sl[g]]
                s_ref[g] = sn
                st_ref[t, g] = sn
                sa_ref[t, g] = sab
                yb = ub + sab * brr[:, sl[g]] + vb * krr[:, sl[g]]
                yrows.append(_colsum(yb * dgv))
            y_ref[row, :] = jnp.concatenate(yrows, axis=1)
            return carry

        lax.fori_loop(0, tc, step, 0)

        @pl.when(pl.program_id(0) == nch - 1)
        def _():
            for job in jobs:
                _ag_finish(*job)

    vec = pl.BlockSpec((tc, D), lambda c: (c, 0))
    big = pl.BlockSpec((tc, G, HEAD, LANE), lambda c: (c, 0, 0, 0))
    hbm = pl.BlockSpec(memory_space=pl.ANY)
    return pl.pallas_call(
        body, name=name, grid=(nch,),
        in_specs=[vec] * 8 + [pl.BlockSpec((MXU_DIM, MXU_DIM), lambda c: (0, 0)), pl.BlockSpec((HEAD, LANE), lambda c: (0, 0))]
        + [hbm] * ng,
        out_specs=[vec, big, big] + [hbm] * ng,
        out_shape=[jax.ShapeDtypeStruct((T, D), f32)] + [jax.ShapeDtypeStruct((T, G, HEAD, LANE), f32)] * 2
        + [jax.ShapeDtypeStruct((N_DEV,) + x.shape, x.dtype) for x in gathers],
        scratch_shapes=[pltpu.VMEM((G, HEAD, LANE), f32)] + _comm_scratch(AG_SEMS) * ng,
        compiler_params=_cparams(("arbitrary",)),
    )(r, w, k, v, kk, a, br, kr, bd, dg, *gathers)


def scan_bwd(r, w, k, v, kk, a, br, kr, dy, states, sas, dr0, dk0, dv0, *, name, exchanges=()):
    T, D = r.shape
    G, tc = _scan_dims(T, D)
    nch = T // tc
    ne = len(exchanges)
    bd, dg = _scan_consts()

    def body(*refs):
        (r_ref, w_ref, k_ref, v_ref, kk_ref, a_ref, br_ref, kr_ref, dy_ref, st_ref, prev_ref, sa_ref,
         dr0_ref, dk0_ref, dv0_ref, bd_ref, dg_ref) = refs[:17]
        dr_ref, dw_ref, dk_ref, dv_ref, dkk_ref, da_ref = refs[17 + ne:23 + ne]
        ds_ref = refs[23 + 2 * ne]
        jobs = [(refs[17 + i], refs[23 + ne + i], *refs[24 + 2 * ne + 3 * i:27 + 2 * ne + 3 * i]) for i in range(ne)]

        @pl.when(pl.program_id(0) == 0)
        def _():
            ds_ref[...] = jnp.zeros_like(ds_ref)
            for job in jobs:
                _cx_start(*job)

        bdv, dgv = bd_ref[...], dg_ref[...]

        def step_at(t, ps):
            row = pl.ds(t, 1)
            rr, ww, kr_, vr, kkr, ar, brr, krr, dyr = (x[row, :] for x in (r_ref, w_ref, k_ref, v_ref, kk_ref, a_ref,
                                                                         br_ref, kr_ref, dy_ref))
            bb = kkr * ar
            sl = [slice(g * LANE, (g + 1) * LANE) for g in range(G)]
            dr_rows, dw_rows, dk_rows, dv_rows, dkk_rows, da_rows = [], [], [], [], [], []
            dss = [ds_ref[g] for g in range(G)]
            sabs = [sa_ref[t, g] for g in range(G)]
            sts = [st_ref[t, g] for g in range(G)]
            blocks = [dss[g] * bb[:, sl[g]] for g in range(G)]
            diag = [dss[g] * kr_[:, sl[g]] for g in range(G)]
            diag += [jnp.broadcast_to(vr[:, sl[g]], (HEAD, LANE)) * dgv for g in range(G)]
            diag += [jnp.broadcast_to(dyr[:, sl[g]], (HEAD, LANE)) * dgv for g in range(G)]
            res = _seg(blocks, bdv, diag)
            for g in range(G):
                sab, vb, dyb = sabs[g], res[2 * G + g], res[3 * G + g]
                dsab = res[g] + dyb * brr[:, sl[g]]
                dvb = res[G + g] + dyb * krr[:, sl[g]]
                dst = dss[g] + dyb * rr[:, sl[g]]
                dr_rows.append(_colsum(sts[g] * dyb))
                dw_rows.append(_colsum(dst * ps[g]))
                db_row = _colsum(dst * sab)
                dk_rows.append(_colsum(dst * vb))
                dv_rows.append(_colsum(dvb * dgv))
                ds_ref[g] = dst * ww[:, sl[g]] - dsab * kkr[:, sl[g]]
                dkk_rows.append(db_row * ar[:, sl[g]] - _colsum(ps[g] * dsab))
                da_rows.append(db_row * kkr[:, sl[g]])
            cat = lambda rows: jnp.concatenate(rows, axis=1)
            dr_ref[row, :] = cat(dr_rows) + dr0_ref[row, :]
            dw_ref[row, :] = cat(dw_rows)
            dk_ref[row, :] = cat(dk_rows) + dk0_ref[row, :]
            dv_ref[row, :] = cat(dv_rows) + dv0_ref[row, :]
            dkk_ref[row, :] = cat(dkk_rows)
            da_ref[row, :] = cat(da_rows)

        def step(i, carry):
            t = tc - 1 - i
            step_at(t, [st_ref[t - 1, g] for g in range(G)])
            return carry

        lax.fori_loop(0, tc - 1, step, 0)
        first = (pl.program_id(0) < nch - 1).astype(f32)
        step_at(0, [prev_ref[0, g] * first for g in range(G)])

        @pl.when(pl.program_id(0) == nch - 1)
        def _():
            for job in jobs:
                _cx_finish(*job)

    vec = pl.BlockSpec((tc, D), lambda c: (nch - 1 - c, 0))
    big = pl.BlockSpec((tc, G, HEAD, LANE), lambda c: (nch - 1 - c, 0, 0, 0))
    prev = pl.BlockSpec((1, G, HEAD, LANE), lambda c: (jnp.maximum((nch - 1 - c) * tc - 1, 0), 0, 0, 0))
    hbm = pl.BlockSpec(memory_space=pl.ANY)
    return pl.pallas_call(
        body, name=name, grid=(nch,),
        in_specs=[vec] * 9 + [big, prev, big] + [vec] * 3
        + [pl.BlockSpec((MXU_DIM, MXU_DIM), lambda c: (0, 0)), pl.BlockSpec((HEAD, LANE), lambda c: (0, 0))] + [hbm] * ne,
        out_specs=[vec] * 6 + [hbm] * ne,
        out_shape=[jax.ShapeDtypeStruct((T, D), f32)] * 6 + [jax.ShapeDtypeStruct(x.shape, x.dtype) for x in exchanges],
        scratch_shapes=[pltpu.VMEM((G, HEAD, LANE), f32)] + _comm_scratch(CX_SEMS) * ne,
        compiler_params=_cparams(("arbitrary",)),
    )(r, w, k, v, kk, a, br, kr, dy, states, states, sas, dr0, dk0, dv0, bd, dg, *exchanges)


def _attn_p(q, k, scale):
    s = lax.dot_general(q.astype(bf16), k.astype(bf16), (((1,), (1,)), ((), ())), preferred_element_type=f32) * scale
    s = s - jnp.max(s, axis=-1, keepdims=True)
    p = jnp.exp(s)
    return p / jnp.sum(p, axis=-1, keepdims=True)


def attn_fwd(q, kv, *, name):
    T, D = q.shape
    M = kv.shape[0]
    hd = D // XATTN_HEADS
    scale = hd ** -0.5
    tq = _tile(T, (512, 256, 128))

    def body(q_ref, k_ref, v_ref, o_ref):
        p = _attn_p(q_ref[...], k_ref[...], scale)
        o_ref[...] = jnp.dot(p.astype(bf16), v_ref[...].astype(bf16), preferred_element_type=f32).astype(o_ref.dtype)

    return pl.pallas_call(
        body, name=name, grid=(XATTN_HEADS, T // tq),
        in_specs=[pl.BlockSpec((tq, hd), lambda h, i: (i, h)), pl.BlockSpec((M, hd), lambda h, i: (0, h)),
                  pl.BlockSpec((M, hd), lambda h, i: (0, XATTN_HEADS + h))],
        out_specs=pl.BlockSpec((tq, hd), lambda h, i: (i, h)),
        out_shape=jax.ShapeDtypeStruct((T, D), bf16),
        compiler_params=_cparams(("parallel", "parallel")),
    )(q, kv, kv)


def attn_bwd(q, kv, do, *, name):
    T, D = q.shape
    M = kv.shape[0]
    hd = D // XATTN_HEADS
    scale = hd ** -0.5
    tq = _tile(T, (512, 256, 128))

    def body(q_ref, k_ref, v_ref, do_ref, dq_ref, dk_ref, dv_ref):
        qv, kvv, vv, dov = q_ref[...], k_ref[...], v_ref[...], do_ref[...]
        p = _attn_p(qv, kvv, scale)
        dob = dov.astype(bf16)
        dp = lax.dot_general(dob, vv.astype(bf16), (((1,), (1,)), ((), ())), preferred_element_type=f32)
        ds = p * (dp - jnp.sum(dp * p, axis=-1, keepdims=True)) * scale
        dsb = ds.astype(bf16)
        dq_ref[...] = jnp.dot(dsb, kvv.astype(bf16), preferred_element_type=f32)

        @pl.when(pl.program_id(1) == 0)
        def _():
            dk_ref[...] = jnp.zeros_like(dk_ref)
            dv_ref[...] = jnp.zeros_like(dv_ref)

        dk_ref[...] += lax.dot_general(dsb, qv.astype(bf16), (((0,), (0,)), ((), ())), preferred_element_type=f32)
        dv_ref[...] += lax.dot_general(p.astype(bf16), dob, (((0,), (0,)), ((), ())), preferred_element_type=f32)

    qspec = pl.BlockSpec((tq, hd), lambda h, i: (i, h))
    mspec = pl.BlockSpec((M, hd), lambda h, i: (0, h))
    return pl.pallas_call(
        body, name=name, grid=(XATTN_HEADS, T // tq),
        in_specs=[qspec, mspec, pl.BlockSpec((M, hd), lambda h, i: (0, XATTN_HEADS + h)), qspec],
        out_specs=[qspec, mspec, mspec],
        out_shape=[jax.ShapeDtypeStruct((T, D), f32), jax.ShapeDtypeStruct((M, D), f32),
                   jax.ShapeDtypeStruct((M, D), f32)],
        compiler_params=_cparams(("parallel", "arbitrary")),
    )(q, kv, kv, do)


def final_loss(x, tgt, g, *, name):
    T, D = x.shape
    tt = min(256, T)

    def body(x_ref, t_ref, g_ref, dx_ref, dg_ref, loss_ref):
        tv = t_ref[...]

        def f(xv, gv):
            e = _rms_fn(xv, gv) - tv
            return 0.5 * jnp.sum(jnp.mean(e * e, axis=-1))

        val, vjp = jax.vjp(f, x_ref[...], g_ref[...])
        dx, dgv = vjp(jnp.ones((), f32))
        dx_ref[...] = dx

        @pl.when(pl.program_id(0) == 0)
        def _():
            dg_ref[...] = jnp.zeros_like(dg_ref)
            loss_ref[...] = jnp.zeros_like(loss_ref)

        dg_ref[...] += dgv
        loss_ref[...] += jnp.full(loss_ref.shape, val, f32)

    row = pl.BlockSpec((tt, D), lambda i: (i, 0))
    return pl.pallas_call(
        body, name=name, grid=(T // tt,),
        in_specs=[row, row, pl.BlockSpec((1, D), lambda i: (0, 0))],
        out_specs=[row, pl.BlockSpec((1, D), lambda i: (0, 0)), pl.BlockSpec((8, LANE), lambda i: (0, 0))],
        out_shape=[jax.ShapeDtypeStruct((T, D), f32), jax.ShapeDtypeStruct((1, D), f32),
                   jax.ShapeDtypeStruct((8, LANE), f32)],
        compiler_params=_cparams(("arbitrary",)),
    )(x, tgt, g)


def _place():
    x, y, c = lax.axis_index("x"), lax.axis_index("y"), lax.axis_index("c")
    chips = [(1 - x, y), (x, 1 - y), (1 - x, 1 - y)]
    return x, y, c, chips


def _rcopy(src, dst, send_sems, recv_sems, k, dev):
    return pltpu.make_async_remote_copy(src_ref=src, dst_ref=dst, send_sem=send_sems.at[k], recv_sem=recv_sems.at[k],
                                        device_id=dev, device_id_type=pl.DeviceIdType.MESH)


def _comm_call(body, name, x, out_shape, n_sems):
    return pl.pallas_call(
        body, name=name, out_shape=out_shape,
        in_specs=[pl.BlockSpec(memory_space=pl.ANY)], out_specs=pl.BlockSpec(memory_space=pl.ANY),
        scratch_shapes=_comm_scratch(n_sems),
    )(x)


AG_SEMS = 7
CX_SEMS = 3


def _comm_scratch(n_sems):
    return [pltpu.SemaphoreType.DMA((n_sems,)), pltpu.SemaphoreType.DMA((n_sems,)), pltpu.SemaphoreType.DMA]


def _ag_first(x_ref, o_ref, send_sems, recv_sems, local_sem):
    x_, y_, c_, chips = _place()
    me = o_ref.at[4 * x_ + 2 * y_ + c_]
    copies = [pltpu.make_async_copy(x_ref, me, local_sem), _rcopy(x_ref, me, send_sems, recv_sems, 0, (x_, y_, 1 - c_))]
    copies += [_rcopy(x_ref, me, send_sems, recv_sems, 1 + j, (*chip, c_)) for j, chip in enumerate(chips)]
    return copies


def _ag_start(x_ref, o_ref, send_sems, recv_sems, local_sem):
    for cp in _ag_first(x_ref, o_ref, send_sems, recv_sems, local_sem):
        cp.start()


def _ag_finish(x_ref, o_ref, send_sems, recv_sems, local_sem):
    x_, y_, c_, chips = _place()
    sibling = (x_, y_, 1 - c_)
    slot = lambda px, py, pc: o_ref.at[4 * px + 2 * py + pc]
    passed = [_rcopy(slot(*chip, c_), slot(*chip, c_), send_sems, recv_sems, 4 + j, sibling)
              for j, chip in enumerate(chips)]
    for j, chip in enumerate(chips):
        _rcopy(x_ref, slot(*chip, c_), send_sems, recv_sems, 1 + j, (*chip, c_)).wait_recv()
        passed[j].start()
    _rcopy(x_ref, slot(x_, y_, 1 - c_), send_sems, recv_sems, 0, sibling).wait_recv()
    for j, chip in enumerate(chips):
        _rcopy(x_ref, slot(*chip, 1 - c_), send_sems, recv_sems, 4 + j, sibling).wait_recv()
    first = _ag_first(x_ref, o_ref, send_sems, recv_sems, local_sem)
    for cp in first[1:] + passed:
        cp.wait_send()
    first[0].wait()


def all_gather(x, *, name):
    def body(*refs):
        _ag_start(*refs)
        _ag_finish(*refs)

    return _comm_call(body, name, x, jax.ShapeDtypeStruct((N_DEV,) + x.shape, x.dtype), AG_SEMS)


def pair_exchange(x, *, name):
    n = x.shape[0]

    def body(x_ref, o_ref, send_sems, recv_sems, local_sem):
        x_, y_, c_, _ = _place()
        copies = [_rcopy(x_ref.at[q, 1 - c_], o_ref.at[q], send_sems, recv_sems, q, (x_, y_, 1 - c_)) for q in range(n)]
        for cp in copies:
            cp.start()
        for cp in copies:
            cp.wait()

    return _comm_call(body, name, x, jax.ShapeDtypeStruct((n,) + x.shape[2:], x.dtype), n)


def _cx_copies(x_ref, o_ref, send_sems, recv_sems, local_sem):
    x_, y_, c_, chips = _place()
    myq = 2 * x_ + y_
    copies = [pltpu.make_async_copy(x_ref.at[myq], o_ref.at[myq], local_sem)]
    copies += [_rcopy(x_ref.at[2 * px + py], o_ref.at[myq], send_sems, recv_sems, j, (px, py, c_))
               for j, (px, py) in enumerate(chips)]
    return copies


def _cx_start(x_ref, o_ref, send_sems, recv_sems, local_sem):
    for cp in _cx_copies(x_ref, o_ref, send_sems, recv_sems, local_sem):
        cp.start()


def _cx_finish(x_ref, o_ref, send_sems, recv_sems, local_sem):
    x_, y_, c_, chips = _place()
    myq = 2 * x_ + y_
    for j, (px, py) in enumerate(chips):
        _rcopy(x_ref.at[myq], o_ref.at[2 * px + py], send_sems, recv_sems, j, (px, py, c_)).wait_recv()
    copies = _cx_copies(x_ref, o_ref, send_sems, recv_sems, local_sem)
    for cp in copies[1:]:
        cp.wait_send()
    copies[0].wait()


def chip_exchange(x, *, name):
    def body(*refs):
        _cx_start(*refs)
        _cx_finish(*refs)

    return _comm_call(body, name, x, jax.ShapeDtypeStruct(x.shape, x.dtype), CX_SEMS)


def _add_cast(a, b, *, name):
    n, _, R, C = a.shape
    tr = min(PACK_ROWS // 2, R)

    def body(a_ref, b_ref, o_ref):
        c = lax.axis_index("c")
        for q in range(n):
            o_ref[q] = (a_ref[q, c] + b_ref[q]).astype(bf16)

    return pl.pallas_call(
        body, name=name, grid=(R // tr,),
        in_specs=[pl.BlockSpec((n, 2, tr, C), lambda i: (0, 0, i, 0)), pl.BlockSpec((n, tr, C), lambda i: (0, i, 0))],
        out_specs=pl.BlockSpec((n, tr, C), lambda i: (0, i, 0)),
        out_shape=jax.ShapeDtypeStruct(b.shape, bf16), compiler_params=_cparams(("parallel",)))(a, b)


def chip_partials(pieces, *, name):
    n, R, C = pieces.shape
    by_core = pieces.reshape(n // 2, 2, R, C)
    return _add_cast(by_core, pair_exchange(by_core, name=name + "_pair"), name=name + "_add")


def reduce_scatter(pieces, *, name):
    return chip_exchange(chip_partials(pieces, name=name), name=name + "_chip")


def adamw(gparts, w, m, v, *, name, row_off=0):
    R, C = w.shape
    n_parts = gparts.shape[0]
    tr = math.gcd(math.gcd(R, row_off), PACK_ROWS)
    assert tr % 16 == 0, (name, R, row_off)
    c1 = 1.0 / (1.0 - ADAM_B1 ** ADAM_STEP)
    c2 = 1.0 / (1.0 - ADAM_B2 ** ADAM_STEP)

    def body(g_ref, w_ref, m_ref, v_ref, go_ref, d_ref, mo_ref, vo_ref):
        g = g_ref[0].astype(f32)
        for i in range(1, n_parts):
            g = g + g_ref[i].astype(f32)
        mn = ADAM_B1 * m_ref[...] + (1.0 - ADAM_B1) * g
        vn = ADAM_B2 * v_ref[...] + (1.0 - ADAM_B2) * (g * g)
        go_ref[...] = g
        mo_ref[...] = mn
        vo_ref[...] = vn
        d_ref[...] = -ADAM_LR * ((mn * c1) / (jnp.sqrt(vn * c2) + ADAM_EPS) + ADAM_WD * w_ref[...])

    blk = pl.BlockSpec((tr, C), lambda i: (i, 0))
    return pl.pallas_call(
        body, name=name, grid=(R // tr,),
        in_specs=[pl.BlockSpec((n_parts, tr, C), lambda i: (0, row_off // tr + i, 0)), blk, blk, blk],
        out_specs=[blk] * 4, out_shape=[jax.ShapeDtypeStruct((R, C), f32)] * 4,
        compiler_params=_cparams(("parallel",)),
    )(gparts, w, m, v)


def _pack(arrs, dtype, lead=None):
    nl = 1 if lead is None else lead
    blocks = []
    for a in arrs:
        n = a.size // nl
        r = -(-n // PACK_COLS)
        a = a.astype(dtype)
        if n != r * PACK_COLS:
            a = jnp.pad(a.reshape(nl, n), ((0, 0), (0, r * PACK_COLS - n)))
        blocks.append(a.reshape(nl, r, PACK_COLS))
    rows = sum(b.shape[1] for b in blocks)
    tot = -(-rows // PACK_ROWS) * PACK_ROWS
    if tot != rows:
        blocks.append(jnp.zeros((nl, tot - rows, PACK_COLS), dtype))
    buf = jnp.concatenate(blocks, axis=1)
    return buf[0] if lead is None else buf


def _split_shards(full, ax):
    shp = full.shape
    t = full.reshape(shp[:ax] + (N_DEV, shp[ax] // N_DEV) + shp[ax + 1:])
    return jnp.moveaxis(t, ax, 0)


def _owner_pieces(layers, ax):
    per_dev = layers[0].size // N_DEV
    if per_dev % PACK_COLS == 0:
        return [_split_shards(g[None], ax) for g in layers]
    return [_split_shards(jnp.stack(layers), ax)]


def _join_shards(parts, ax):
    t = jnp.moveaxis(parts, 0, ax)
    shp = t.shape
    return t.reshape(shp[:ax] + (shp[ax] * shp[ax + 1],) + shp[ax + 2:])


def _unpack(buf, shapes, lead=None):
    out, off = [], 0
    nl = 1 if lead is None else lead
    buf = buf.reshape(nl, -1, PACK_COLS)
    for s in shapes:
        n = math.prod(s)
        r = -(-n // PACK_COLS)
        blk = buf[:, off:off + r]
        if n != r * PACK_COLS:
            blk = blk.reshape(nl, r * PACK_COLS)[:, :n]
        out.append(blk.reshape(tuple(s) if lead is None else (lead,) + tuple(s)))
        off += r
    return out


def _row(v):
    return v.reshape(1, -1)


def _head_mats(D):
    e = (lax.broadcasted_iota(jnp.int32, (D, D // HEAD), 0) // HEAD
         == lax.broadcasted_iota(jnp.int32, (D, D // HEAD), 1)).astype(f32)
    return e, e.T


def rms_fwd(x, g, name, out_dtype=f32):
    return rowwise(lambda xv, gv: (_rms_fn(xv, gv),), [x], [g], name=name, out_dtype=out_dtype, tt=512)[0]


def rms_bwd(x, g, dh, add, name):
    return rowwise_bwd(lambda xv, gv: (_rms_fn(xv, gv),), [x], [g], [dh], name=name, n_drow=1, n_dpar=1, add0=add,
                       tt=256)


def rwkv_fwd(x, p, vf, tag, gathers=()):
    vres = vf is not None
    D = x.shape[1]
    e, et = _head_mats(D)
    h = rms_fwd(x, p['norm_g'], tag + "_norm")
    xr, xw, xk, xv, xa, xg = colwise(_mix_fn, [(h, 0), (p['mu'], 0)], [h.shape[0]] * 6, name=tag + "_mix",
                                     nblk=D // LANE, out_dtype=bf16)
    r = mm(xr, p['w_r'], name=tag + "_r")
    k = mm(xk, p['w_k'], name=tag + "_k")
    v = mm(xv, p['w_v'], name=tag + "_v")
    th = mm(xw, p['w1'], name=tag + "_w1", act='tanh')
    lw = mm(th, p['w2'], name=tag + "_w2")
    t2 = mm(xa, p['a1'], name=tag + "_a1", out_dtype=bf16)
    aa = mm(t2, p['a2'], name=tag + "_a2")
    sg = mm(xg, p['g1'], name=tag + "_g1", act='sigmoid')
    gg = mm(sg, p['g2'], name=tag + "_g2")
    rows = [k, v, lw, aa]
    pars = [p['w0'], p['a0'], p['k_k'], p['k_a']]
    t4 = None
    if vres:
        t4 = mm(xv, p['v1'], name=tag + "_v1", out_dtype=bf16)
        vv = mm(t4, p['v2'], name=tag + "_v2")
        rows += [vv, vf]
        pars += [p['v0']]
    pars += [e, et]
    mid = functools.partial(_mid_fn, vres)
    decay, a, kk, k2, v2 = rowwise(mid, rows, pars, name=tag + "_mid")
    br, kr = rowwise(_head_dots_fn, [r, k2, kk, a], [e, et], name=tag + "_hdots")
    y, states, sas, *gathered = scan_fwd(r, decay, k2, v2, kk, a, br, kr, name=tag + "_scan", gathers=gathers)
    post_rows = [y, r, k2, v2, gg]
    post_pars = [p['ln_g'], p['ln_b'], p['r_k'], e, et]
    z = rowwise(_post_fn, post_rows, post_pars, name=tag + "_post", out_dtype=bf16)[0]
    xo = mm(z, p['w_o'], name=tag + "_o", res=x)
    saved = dict(x=x, h=h, xs=(xr, xw, xk, xv, xa, xg), r=r, th=th, t2=t2, sg=sg, t4=t4, mid_rows=rows, mid_pars=pars,
                 mid=mid, scan_in=(r, decay, k2, v2, kk, a, br, kr), states=(states, sas), post_rows=post_rows, post_pars=post_pars,
                 z=z, vres=vres)
    return xo, v2, saved, gathered


def rwkv_bwd(dxo, dvf_in, p, s, tag, exchanges=()):
    D = dxo.shape[1]
    g = {}
    xr, xw, xk, xv, xa, xg = s['xs']
    dz = mm(dxo, p['w_o'], name=tag + "_bo", tb=True)
    g['w_o'] = mm(s['z'], dxo, name=tag + "_bwo", ta=True)
    dy, dr1, dk1, dv1, dgg, g['ln_g'], g['ln_b'], g['r_k'] = rowwise_bwd(
        _post_fn, s['post_rows'], s['post_pars'], [dz], name=tag + "_bpost", n_drow=5, n_dpar=3)
    if dvf_in is not None:
        dv1 = rowwise(_add_fn, [dv1, dvf_in], [], name=tag + "_bvadd")[0]
    dsg = mm(dgg, p['g2'], name=tag + "_bg2", tb=True)
    g['g2'] = mm(s['sg'], dgg, name=tag + "_bwg2", ta=True)
    dt3 = rowwise(_dsig_fn, [dsg, s['sg']], [], name=tag + "_bdsig")[0]
    dxg = mm(dt3, p['g1'], name=tag + "_bg1", tb=True)
    g['g1'] = mm(xg, dt3, name=tag + "_bwg1", ta=True)
    dr, dw, dk2, dv2, dkk, da, *exchanged = scan_bwd(*s['scan_in'], dy, *s['states'], dr1, dk1, dv1,
                                                     name=tag + "_bscan", exchanges=exchanges)
    vres = s['vres']
    n_drow = 6 if vres else 4
    n_dpar = 5 if vres else 4
    outs = rowwise_bwd(s['mid'], s['mid_rows'], s['mid_pars'], [dw, da, dkk, dk2, dv2], name=tag + "_bmid",
                       n_drow=n_drow, n_dpar=n_dpar)
    dk, dv, dlw, daa = outs[:4]
    dvf = None
    if vres:
        dvv, dvf = outs[4:6]
        g['w0'], g['a0'], g['k_k'], g['k_a'], g['v0'] = outs[6:]
    else:
        g['w0'], g['a0'], g['k_k'], g['k_a'] = outs[4:]
    dth = mm(dlw, p['w2'], name=tag + "_bw2", tb=True)
    g['w2'] = mm(s['th'], dlw, name=tag + "_bww2", ta=True)
    dt1 = rowwise(_dtanh_fn, [dth, s['th']], [], name=tag + "_bdtanh")[0]
    dxw = mm(dt1, p['w1'], name=tag + "_bw1", tb=True)
    g['w1'] = mm(xw, dt1, name=tag + "_bww1", ta=True)
    dt2 = mm(daa, p['a2'], name=tag + "_ba2", tb=True)
    g['a2'] = mm(s['t2'], daa, name=tag + "_bwa2", ta=True)
    dxa = mm(dt2, p['a1'], name=tag + "_ba1", tb=True)
    g['a1'] = mm(xa, dt2, name=tag + "_bwa1", ta=True)
    dxv = mm(dv, p['w_v'], name=tag + "_bv", tb=True)
    g['w_v'] = mm(xv, dv, name=tag + "_bwv", ta=True)
    if vres:
        dt4 = mm(dvv, p['v2'], name=tag + "_bv2", tb=True)
        g['v2'] = mm(s['t4'], dvv, name=tag + "_bwv2", ta=True)
        dxv = mm(dt4, p['v1'], name=tag + "_bv1", tb=True, res=dxv)
        g['v1'] = mm(xv, dt4, name=tag + "_bwv1", ta=True)
    dxr = mm(dr, p['w_r'], name=tag + "_br", tb=True)
    g['w_r'] = mm(xr, dr, name=tag + "_bwr", ta=True)
    dxk = mm(dk, p['w_k'], name=tag + "_bk", tb=True)
    g['w_k'] = mm(xk, dk, name=tag + "_bwk", ta=True)
    T = dxo.shape[0]
    dh, dmu = colwise(_mix_bwd_fn, [(s['h'], 0), (p['mu'], 0), (dxr, 0), (dxw, 0), (dxk, 0), (dxv, 0), (dxa, 0),
                                    (dxg, 0)], [T, 8], name=tag + "_bmix", nblk=D // LANE)
    g['mu'] = dmu[:6]
    dx, g['norm_g'] = rms_bwd(s['x'], p['norm_g'], dh, dxo, tag + "_bnorm")
    return dx, dvf, g, exchanged


def conv_fwd(x, p, tag):
    T, D = x.shape
    nb = D // LANE
    kw = p['dw'].shape[0]
    h = rms_fwd(x, p['norm_g'], tag + "_norm", bf16)
    u = mm(h, p['w_in'], name=tag + "_in", bias=p['b_in'])
    c = colwise(functools.partial(_glu_conv_fn, kw), [(u, 0), (u, nb), (p['dw'], 0), (p['dw_b'], 0)], [T],
                name=tag + "_dw", nblk=nb)[0]
    sl = rowwise(_ln_silu_fn, [c], [p['ln_g'], p['ln_b']], name=tag + "_ln", out_dtype=bf16)[0]
    xo = mm(sl, p['w_out'], name=tag + "_out", bias=p['b_out'], res=x)
    return xo, dict(x=x, h=h, u=u, c=c, sl=sl)


def conv_bwd(dxo, p, s, tag):
    T, D = dxo.shape
    nb = D // LANE
    kw = p['dw'].shape[0]
    kpad = -(-kw // 8) * 8
    g = {}
    dsl = mm(dxo, p['w_out'], name=tag + "_bout", tb=True)
    g['w_out'] = mm(s['sl'], dxo, name=tag + "_bwout", ta=True)
    g['b_out'] = rowwise_bwd(_bias_fn, [dxo], [p['b_out']], [dxo], name=tag + "_bbout", n_drow=0, n_dpar=1,
                             tt=256)[0]
    dc, g['ln_g'], g['ln_b'] = rowwise_bwd(_ln_silu_fn, [s['c']], [p['ln_g'], p['ln_b']], [dsl], name=tag + "_bln",
                                           n_drow=1, n_dpar=2, tt=256)
    u = s['u']
    du1, du2, ddw, g['dw_b'] = colwise(functools.partial(_glu_conv_bwd_fn, kw, kpad),
                                       [(u, 0), (u, nb), (p['dw'], 0), (dc, 0)], [T, T, kpad, 1],
                                       name=tag + "_bdw", nblk=nb)
    g['dw'] = ddw[:kw]
    du = jnp.concatenate([du1, du2], axis=1)
    g['b_in'] = rowwise_bwd(_bias_fn, [du], [p['b_in']], [du], name=tag + "_bbin", n_drow=0, n_dpar=1, tt=256)[0]
    dh = mm(du, p['w_in'], name=tag + "_bin", tb=True)
    g['w_in'] = mm(s['h'], du, name=tag + "_bwin", ta=True)
    dx, g['norm_g'] = rms_bwd(s['x'], p['norm_g'], dh, dxo, tag + "_bnorm")
    return dx, g


def xattn_fwd(x, memn, p, tag):
    hn = rms_fwd(x, p['norm_g'], tag + "_norm", bf16)
    q = mm(hn, p['w_q'], name=tag + "_q", out_dtype=bf16)
    kv = mm(memn, p['w_kv'], name=tag + "_kv", out_dtype=bf16)
    o = attn_fwd(q, kv, name=tag + "_attn")
    xo = mm(o, p['w_o'], name=tag + "_o", res=x)
    return xo, dict(x=x, hn=hn, q=q, kv=kv, o=o)


def xattn_bwd(dxo, dmemn, memn, p, s, tag):
    g = {}
    do = mm(dxo, p['w_o'], name=tag + "_bo", tb=True)
    g['w_o'] = mm(s['o'], dxo, name=tag + "_bwo", ta=True)
    dq, dk, dv = attn_bwd(s['q'], s['kv'], do, name=tag + "_battn")
    dkv = jnp.concatenate([dk, dv], axis=1)
    dmemn = mm(dkv, p['w_kv'], name=tag + "_bkv", tb=True, res=dmemn)
    g['w_kv'] = mm(memn, dkv, name=tag + "_bwkv", ta=True)
    dhn = mm(dq, p['w_q'], name=tag + "_bq", tb=True)
    g['w_q'] = mm(s['hn'], dq, name=tag + "_bwq", ta=True)
    dx, g['norm_g'] = rms_bwd(s['x'], p['norm_g'], dhn, dxo, tag + "_bnorm")
    return dx, dmemn, g


def ffn_fwd(x, p, tag):
    T, D = x.shape
    w_dev, layer = p['w_in']
    nb = (N_DEV // 2) * w_dev.shape[2] // LANE
    kw = p['dw'].shape[0]
    hn = rms_fwd(x, p['norm_g'], tag + "_norm", bf16)
    u = mm(hn, w_dev, name=tag + "_in", b_dev=(layer * D, D))
    act = colwise(functools.partial(_ffn_act_fn, kw), [(u, 0), (u, nb), (p['dw'], 0), (p['dw'], nb)], [T],
                  name=tag + "_act", nblk=nb, out_dtype=bf16)[0]
    xo = mm(act, p['w_out'], name=tag + "_out", res=x)
    return xo, dict(x=x, hn=hn, u=u, act=act)


def ffn_bwd(dxo, p, s, tag):
    T, D = dxo.shape
    w_dev, layer = p['w_in']
    nb = (N_DEV // 2) * w_dev.shape[2] // LANE
    kw = p['dw'].shape[0]
    g = {}
    dact = mm(dxo, p['w_out'], name=tag + "_bout", tb=True)
    g['w_out'] = mm(s['act'], dxo, name=tag + "_bwout", ta=True)
    u = s['u']
    dug, duv, dwg, dwv = colwise(functools.partial(_ffn_act_bwd_fn, kw, 8),
                                 [(u, 0), (u, nb), (p['dw'], 0), (p['dw'], nb), (dact, 0)], [T, T, 8, 8],
                                 name=tag + "_bact", nblk=nb)
    g['dw'] = jnp.concatenate([dwg[:kw], dwv[:kw]], axis=1)
    du = jnp.concatenate([dug, duv], axis=1)
    dhn = mm(du, w_dev, name=tag + "_bin", tb=True, b_dev=(layer * D, D))
    g['w_in'] = mm(s['hn'], du, name=tag + "_bwin", ta=True, out_dev=True)
    dx, g['norm_g'] = rms_bwd(s['x'], p['norm_g'], dhn, dxo, tag + "_bnorm")
    return dx, g


def _lane_pad(n):
    return -(-n // LANE) * LANE


def _pad_blocks(a, axis, nblk):
    shp = a.shape
    n = shp[axis] // nblk
    t = a.reshape(shp[:axis] + (nblk, n) + shp[axis + 1:])
    pad = [(0, 0)] * t.ndim
    pad[axis + 1] = (0, _lane_pad(n) - n)
    t = jnp.pad(t, pad)
    return t.reshape(shp[:axis] + (nblk * _lane_pad(n),) + shp[axis + 1:])


def _unpad_blocks(a, axis, nblk, n):
    shp = a.shape
    t = a.reshape(shp[:axis] + (nblk, shp[axis] // nblk) + shp[axis + 1:])
    t = lax.slice_in_dim(t, 0, n, axis=axis + 1)
    return t.reshape(shp[:axis] + (nblk * n,) + shp[axis + 1:])


def _layer_params(W, layer):
    ia = ib = layer // 2
    mixer = {}
    if layer % 2 == 0:
        mixer = dict(norm_g=_row(W['norm_mix_g'][layer]), mu=W['rwkv_mu'][ia], w_r=W['rwkv_w_r'][ia],
                     w_k=W['rwkv_w_k'][ia], w_v=W['rwkv_w_v'][ia], w_o=W['rwkv_w_o'][ia], w0=_row(W['rwkv_w0'][ia]),
                     w1=W['rwkv_w1'][ia], w2=W['rwkv_w2'][ia], a0=_row(W['rwkv_a0'][ia]), a1=W['rwkv_a1'][ia],
                     a2=W['rwkv_a2'][ia], g1=W['rwkv_g1'][ia], g2=W['rwkv_g2'][ia], k_k=_row(W['rwkv_k_k'][ia]),
                     k_a=_row(W['rwkv_k_a'][ia]), r_k=_row(W['rwkv_r_k'][ia]), ln_g=_row(W['rwkv_ln_g'][ia]),
                     ln_b=_row(W['rwkv_ln_b'][ia]))
        if ia > 0:
            mixer.update(v0=_row(W['rwkv_v0'][ia - 1]), v1=W['rwkv_v1'][ia - 1], v2=W['rwkv_v2'][ia - 1])
    else:
        mixer = dict(norm_g=_row(W['norm_mix_g'][layer]), w_in=W['conv_w_in'][ib], b_in=_row(W['conv_b_in'][ib]),
                     dw=W['conv_dw'][ib], dw_b=_row(W['conv_dw_b'][ib]), ln_g=_row(W['conv_ln_g'][ib]),
                     ln_b=_row(W['conv_ln_b'][ib]), w_out=W['conv_w_out'][ib], b_out=_row(W['conv_b_out'][ib]))
    return mixer


def _rest_params(W, layer):
    xat = dict(norm_g=_row(W['norm_xattn_g'][layer]), w_q=W['xattn_w_q'][layer], w_kv=W['xattn_w_kv'][layer],
               w_o=W['xattn_w_o'][layer])
    ffn = dict(norm_g=_row(W['norm_ffn_g'][layer]), w_in=(W['ffn_w_in'], layer),
               dw=_pad_blocks(W['ffn_dw'][layer], 1, N_DEV), w_out=_pad_blocks(W['ffn_w_out'][layer], 0, N_DEV // 2))
    return xat, ffn


NATIVE = 'ffn_w_in'
EARLY = [n for n in W_NAMES if W_SPEC[n][0] is not None and (n.startswith('rwkv_') or not W_SPEC[n][1])]
LATE = [n for n in W_NAMES if W_SPEC[n][0] is not None and n not in EARLY and n != NATIVE]


def _native_rows(a, dtype):
    L, D, n = a.shape
    return jnp.pad(a.astype(dtype), ((0, 0), (0, 0), (0, _lane_pad(n) - n))).reshape(L * D, _lane_pad(n))


def _unpack_full(got, local, names):
    parts = _unpack(got, [local[n].shape for n in names], lead=N_DEV)
    return {n: _join_shards(part, W_SPEC[n][0]) for n, part in zip(names, parts)}


def _gather_early(local):
    full = {n: local[n] for n in W_NAMES if W_SPEC[n][0] is None}
    for as_bf16, dtype, tag in ((True, bf16, "ag_mat"), (False, f32, "ag_vec")):
        names = [n for n in EARLY if W_SPEC[n][1] == as_bf16]
        full.update(_unpack_full(all_gather(_pack([local[n] for n in names], dtype), name=tag), local, names))
    return full


def _step(local, x, mem, tgt):
    W = _gather_early(local)
    late_bufs = (_pack([local[n] for n in LATE], bf16), _native_rows(local[NATIVE], bf16))
    depth = W['norm_mix_g'].shape[0]
    g_mem = _row(W['mem_norm_g'])
    memn = rms_fwd(mem, g_mem, "mem_norm", bf16)
    layers, saved = [], []
    vf = None
    for l in range(depth):
        if l % 2 == 0:
            pm = _layer_params(W, l)
            x, v, sm, gathered = rwkv_fwd(x, pm, vf, f"rw{l}", gathers=late_bufs if l == 0 else ())
            if l == 0:
                W.update(_unpack_full(gathered[0], local, LATE))
                W[NATIVE] = gathered[1]
            if vf is None:
                vf = v
        else:
            pm = _layer_params(W, l)
            x, sm = conv_fwd(x, pm, f"cv{l}")
        px, pf = _rest_params(W, l)
        x, sx = xattn_fwd(x, memn, px, f"xa{l}")
        x, sf = ffn_fwd(x, pf, f"ff{l}")
        layers.append((pm, px, pf))
        saved.append((sm, sx, sf))
    g_fin = _row(W['final_norm_g'])
    dx, dg_fin, loss_blk = final_loss(x, tgt, g_fin, name="final_loss")

    grads = {n: [None] * local[n].shape[0] for n in W_NAMES if local[n].ndim >= 2}
    grads['final_norm_g'] = dg_fin.reshape(-1)
    n_in = local[NATIVE].shape[2]
    dmemn = jnp.zeros(memn.shape, f32)
    dvf = None
    for l in reversed(range(depth)):
        pm, px, pf = layers[l]
        sm, sx, sf = saved[l]
        dx, gf = ffn_bwd(dx, pf, sf, f"ff{l}")
        dx, dmemn, gx = xattn_bwd(dx, dmemn, memn, px, sx, f"xa{l}")
        grads['norm_ffn_g'][l] = gf['norm_g'].reshape(-1)
        grads['ffn_w_in'][l] = gf['w_in']
        grads['ffn_dw'][l] = _unpad_blocks(gf['dw'], 1, N_DEV, n_in)
        grads['ffn_w_out'][l] = _unpad_blocks(gf['w_out'], 0, N_DEV // 2, n_in)
        grads['norm_xattn_g'][l] = gx['norm_g'].reshape(-1)
        grads['xattn_w_q'][l], grads['xattn_w_kv'][l], grads['xattn_w_o'][l] = gx['w_q'], gx['w_kv'], gx['w_o']
        i = l // 2
        if l % 2 == 0:
            pre = ()
            if l == 0:
                late = _pack([p for n in LATE for p in _owner_pieces(grads[n], W_SPEC[n][0])], f32, lead=N_DEV)
                native = jnp.concatenate(grads[NATIVE], axis=1)
                pre = (chip_partials(late, name="rs_late"), chip_partials(native, name="rs_ffn_in"))
            dx, dvf_l, gm, exchanged = rwkv_bwd(dx, dvf if i == 0 else None, pm, sm, f"rw{l}", exchanges=pre)
            if l == 0:
                late_parts, native_parts = exchanged
            if dvf_l is not None:
                dvf = dvf_l if dvf is None else rowwise(_add_fn, [dvf, dvf_l], [], name=f"rw{l}_dvfadd")[0]
            for short in ('mu', 'w_r', 'w_k', 'w_v', 'w_o', 'w1', 'w2', 'a1', 'a2', 'g1', 'g2'):
                grads['rwkv_' + short][i] = gm[short]
            for short in ('w0', 'a0', 'k_k', 'k_a', 'ln_g', 'ln_b'):
                grads['rwkv_' + short][i] = gm[short].reshape(-1)
            grads['rwkv_r_k'][i] = gm['r_k'].reshape(W['rwkv_r_k'].shape[1:])
            if i > 0:
                grads['rwkv_v0'][i - 1] = gm['v0'].reshape(-1)
                grads['rwkv_v1'][i - 1], grads['rwkv_v2'][i - 1] = gm['v1'], gm['v2']
        else:
            dx, gm = conv_bwd(dx, pm, sm, f"cv{l}")
            for short in ('w_in', 'dw', 'w_out'):
                grads['conv_' + short][i] = gm[short]
            for short in ('b_in', 'dw_b', 'ln_g', 'ln_b', 'b_out'):
                grads['conv_' + short][i] = gm[short].reshape(-1)
        grads['norm_mix_g'][l] = gm['norm_g'].reshape(-1)
    _, dg_mem = rowwise_bwd(lambda xv, gv: (_rms_fn(xv, gv),), [mem], [g_mem], [dmemn], name="mem_norm_b",
                            n_drow=1, n_dpar=1)
    grads['mem_norm_g'] = dg_mem.reshape(-1)
    early_grads = {n: grads[n] for n in EARLY}
    repl_grads = {n: (jnp.stack(gv) if isinstance(gv, list) else gv) for n, gv in grads.items() if W_SPEC[n][0] is None}
    return loss_blk[0, 0], dx, early_grads, repl_grads, late_parts, native_parts


def kernel(x, mem, mem_norm_g, norm_mix_g, norm_xattn_g, norm_ffn_g, final_norm_g, rwkv_mu, rwkv_w_r, rwkv_w_k, rwkv_w_v, rwkv_w_o, rwkv_w0, rwkv_w1, rwkv_w2, rwkv_a0, rwkv_a1, rwkv_a2, rwkv_g1, rwkv_g2, rwkv_k_k, rwkv_k_a, rwkv_r_k, rwkv_ln_g, rwkv_ln_b, rwkv_v0, rwkv_v1, rwkv_v2, conv_w_in, conv_b_in, conv_dw, conv_dw_b, conv_ln_g, conv_ln_b, conv_w_out, conv_b_out, xattn_w_q, xattn_w_kv, xattn_w_o, ffn_w_in, ffn_dw, ffn_w_out, loss_target, m_mem_norm_g, m_norm_mix_g, m_norm_xattn_g, m_norm_ffn_g, m_final_norm_g, m_rwkv_mu, m_rwkv_w_r, m_rwkv_w_k, m_rwkv_w_v, m_rwkv_w_o, m_rwkv_w0, m_rwkv_w1, m_rwkv_w2, m_rwkv_a0, m_rwkv_a1, m_rwkv_a2, m_rwkv_g1, m_rwkv_g2, m_rwkv_k_k, m_rwkv_k_a, m_rwkv_r_k, m_rwkv_ln_g, m_rwkv_ln_b, m_rwkv_v0, m_rwkv_v1, m_rwkv_v2, m_conv_w_in, m_conv_b_in, m_conv_dw, m_conv_dw_b, m_conv_ln_g, m_conv_ln_b, m_conv_w_out, m_conv_b_out, m_xattn_w_q, m_xattn_w_kv, m_xattn_w_o, m_ffn_w_in, m_ffn_dw, m_ffn_w_out, v_mem_norm_g, v_norm_mix_g, v_norm_xattn_g, v_norm_ffn_g, v_final_norm_g, v_rwkv_mu, v_rwkv_w_r, v_rwkv_w_k, v_rwkv_w_v, v_rwkv_w_o, v_rwkv_w0, v_rwkv_w1, v_rwkv_w2, v_rwkv_a0, v_rwkv_a1, v_rwkv_a2, v_rwkv_g1, v_rwkv_g2, v_rwkv_k_k, v_rwkv_k_a, v_rwkv_r_k, v_rwkv_ln_g, v_rwkv_ln_b, v_rwkv_v0, v_rwkv_v1, v_rwkv_v2, v_conv_w_in, v_conv_b_in, v_conv_dw, v_conv_dw_b, v_conv_ln_g, v_conv_ln_b, v_conv_w_out, v_conv_b_out, v_xattn_w_q, v_xattn_w_kv, v_xattn_w_o, v_ffn_w_in, v_ffn_dw, v_ffn_w_out):
    given = dict(locals())
    local = {n: given[n] for n in W_NAMES}
    loss_local, dx, early_grads, grads, late_parts, native_parts = _step(local, x[0], mem[0], loss_target[0])
    loss = lax.psum(loss_local, ("x", "y", "c"))

    repl = [n for n in W_NAMES if W_SPEC[n][0] is None]
    out = {}
    kinds = ("grad_", "delta_", "new_m_", "new_v_")

    res = adamw(native_parts, *[_native_rows(given[pre + NATIVE], f32) for pre in ("", "m_", "v_")],
                name="adamw_ffn_in")
    shp = given[NATIVE].shape
    for kind, buf in zip(kinds, res):
        out[kind + NATIVE] = buf.reshape(shp[0], shp[1], -1)[:, :, :shp[2]]

    early_parts = reduce_scatter(
        _pack([p for n in EARLY for p in _owner_pieces(early_grads[n], W_SPEC[n][0])], f32, lead=N_DEV), name="rs_early")
    res = adamw(early_parts, *[_pack([given[pre + n] for n in EARLY], f32) for pre in ("", "m_", "v_")],
                name="adamw_early")
    for kind, buf in zip(kinds, res):
        for n, arr in zip(EARLY, _unpack(buf, [given[n].shape for n in EARLY])):
            out[kind + n] = arr
    row_off = 0
    for n in LATE:
        shp = given[n].shape
        res = adamw(late_parts, *[given[pre + n].reshape(-1, PACK_COLS) for pre in ("", "m_", "v_")],
                    name="adamw_" + n, row_off=row_off)
        for kind, buf in zip(kinds, res):
            out[kind + n] = buf.reshape(shp)
        row_off += given[n].size // PACK_COLS

    parts = all_gather(_pack([grads[n] for n in repl], f32), name="grad_gather_repl")
    res = adamw(parts, *[_pack([given[pre + n] for n in repl], f32) for pre in ("", "m_", "v_")],
                name="adamw_repl")
    for kind, buf in zip(("grad_", "delta_", "new_m_", "new_v_"), res):
        for n, arr in zip(repl, _unpack(buf, [given[n].shape for n in repl])):
            out[kind + n] = arr

    return (loss, dx[None], *[out[kind + n] for kind in ("grad_", "delta_", "new_m_", "new_v_") for n in W_NAMES])
```
